```python
import jax, jax.numpy as jnp
from jax import lax
import numpy as np

D_MODEL = 1024
BATCH = 1
SEQ = 16384
DEPTH = 1

GRID_W = 64
MEM_LEN = 256
HEAD_DIM = 64
NA_HEADS = 8
NA_WIDTH = NA_HEADS * HEAD_DIM
CONV_WIDTH = D_MODEL - NA_WIDTH
MIX_WIDTH = NA_WIDTH + CONV_WIDTH
CONV_K = 3
NA_WIN_ROWS = 8
NA_WIN_COLS = 16
MEM_HEADS = 4
MEM_HEAD_DIM = 128
MEM_WIDTH = MEM_HEADS * MEM_HEAD_DIM
N_EXPERTS = 16
EC_CAPACITY_FACTOR = 2
EXPERT_FF = 2048
LN_EPS = 1e-5
DN_ALPHA = (2 * DEPTH) ** 0.25
DN_BETA = (8 * DEPTH) ** -0.25

kernel_name = "hybrid_na2d_shortconv_memxattn_ecmoe"


def layer_norm(x, g, b):
    xf = x.astype(jnp.float32)
    mu = jnp.mean(xf, axis=-1, keepdims=True)
    var = jnp.mean(jnp.square(xf - mu), axis=-1, keepdims=True)
    return ((xf - mu) * lax.rsqrt(var + LN_EPS)).astype(x.dtype) * g + b


def window_starts(n, win):
    return jnp.clip(jnp.arange(n) - win // 2, 0, n - win)


def neighbourhood_attention(q, k, v, rpb):
    B, S, H, Dh = q.shape
    rows = S // GRID_W
    kr = min(NA_WIN_ROWS, rows)
    kc = NA_WIN_COLS
    qg = q.reshape(B, rows, GRID_W, H, Dh)
    kg = k.reshape(B, rows, GRID_W, H, Dh)
    vg = v.reshape(B, rows, GRID_W, H, Dh)
    row_idx = window_starts(rows, kr)[:, None] + jnp.arange(kr)[None, :]
    k_rows = kg[:, row_idx]
    v_rows = vg[:, row_idx]
    scores = jnp.einsum('brqhd,brjkhd->bhrqjk', qg, k_rows,
                        preferred_element_type=jnp.float32) * (Dh ** -0.5)
    dr_idx = row_idx - jnp.arange(rows)[:, None] + (NA_WIN_ROWS - 1)
    cols = jnp.arange(GRID_W)
    dc = cols[None, :] - cols[:, None]
    dc_idx = jnp.clip(dc, -(kc - 1), kc - 1) + (kc - 1)
    c_start = window_starts(GRID_W, kc)[:, None]
    col_mask = (cols[None, :] >= c_start) & (cols[None, :] < c_start + kc)
    bias = rpb[:, dr_idx[:, None, :, None], dc_idx[None, :, None, :]]
    scores = scores + bias[None].astype(jnp.float32)
    scores = jnp.where(col_mask[:, None, :], scores, jnp.finfo(jnp.float32).min)
    probs = jax.nn.softmax(scores.reshape(B, H, rows, GRID_W, kr * GRID_W), axis=-1)
    probs = probs.reshape(B, H, rows, GRID_W, kr, GRID_W).astype(v.dtype)
    out = jnp.einsum('bhrqjk,brjkhd->brqhd', probs, v_rows)
    return out.reshape(B, S, H * Dh)


def short_conv(u, w):
    rhs = w[:, None, :].astype(u.dtype)
    return lax.conv_general_dilated(u, rhs, window_strides=(1,),
                                    padding=[(CONV_K // 2, CONV_K // 2)],
                                    dimension_numbers=('NWC', 'WIO', 'NWC'),
                                    feature_group_count=u.shape[-1])


def hybrid_mixer(x, w_in, rpb, conv_w, w_out):
    B, S, _ = x.shape
    proj = x @ w_in
    splits = [NA_WIDTH, 2 * NA_WIDTH, 3 * NA_WIDTH,
              3 * NA_WIDTH + CONV_WIDTH, 3 * NA_WIDTH + 2 * CONV_WIDTH]
    q, k, v, b_gate, c_gate, h = jnp.split(proj, splits, axis=-1)
    heads = lambda t: t.reshape(B, S, NA_HEADS, HEAD_DIM)
    y_na = neighbourhood_attention(heads(q), heads(k), heads(v), rpb)
    y_conv = b_gate * short_conv(c_gate * h, conv_w)
    return jnp.concatenate([y_na, y_conv], axis=-1) @ w_out


def memory_cross_attention(x, mem, wq, wk, wv, wo):
    B, S, _ = x.shape
    M = mem.shape[1]
    q = (x @ wq).reshape(B, S, MEM_HEADS, MEM_HEAD_DIM)
    k = (mem @ wk).reshape(B, M, MEM_HEADS, MEM_HEAD_DIM)
    v = (mem @ wv).reshape(B, M, MEM_HEADS, MEM_HEAD_DIM)
    s = jnp.einsum('bshd,bmhd->bhsm', q, k,
                   preferred_element_type=jnp.float32) * (MEM_HEAD_DIM ** -0.5)
    p = jax.nn.softmax(s, axis=-1).astype(v.dtype)
    o = jnp.einsum('bhsm,bmhd->bshd', p, v).reshape(B, S, MEM_WIDTH)
    return o @ wo


def expert_choice_moe(x, w_router, w_gate, w_up, w_down):
    B, S, D = x.shape
    cap = EC_CAPACITY_FACTOR * S // N_EXPERTS
    logits = jnp.einsum('bsd,de->bse', x, w_router, preferred_element_type=jnp.float32)
    affinity = jax.nn.softmax(logits, axis=-1)
    gates, idx = lax.top_k(jnp.swapaxes(affinity, 1, 2), cap)
    xe = jax.vmap(lambda xb, ib: xb[ib])(x, idx)
    hid = jax.nn.silu(jnp.einsum('becd,edf->becf', xe, w_gate)) * \
        jnp.einsum('becd,edf->becf', xe, w_up)
    ye = jnp.einsum('becf,efd->becd', hid, w_down) * gates[..., None].astype(x.dtype)
    scatter = lambda ib, yb: jnp.zeros((S, D), yb.dtype).at[ib.reshape(-1)].add(yb.reshape(-1, D))
    return jax.vmap(scatter)(idx, ye)


def setup_inputs(seed: int = 0) -> dict:
    key = jax.random.key(seed)
    ks = jax.random.split(key, 20)
    nrm = lambda k, shape, scale: jax.random.normal(k, shape, jnp.float32) * scale
    L = DEPTH
    return {
        "x": nrm(ks[0], (BATCH, SEQ, D_MODEL), 1.0),
        "mem": nrm(ks[1], (BATCH, MEM_LEN, D_MODEL), 1.0),
        "w_in": nrm(ks[2], (L, D_MODEL, 3 * NA_WIDTH + 3 * CONV_WIDTH), D_MODEL ** -0.5),
        "na_rpb": nrm(ks[3], (L, NA_HEADS, 2 * NA_WIN_ROWS - 1, 2 * NA_WIN_COLS - 1), 0.1),
        "conv_w": nrm(ks[4], (L, CONV_K, CONV_WIDTH), CONV_K ** -0.5),
        "w_mix_out": nrm(ks[5], (L, MIX_WIDTH, D_MODEL), MIX_WIDTH ** -0.5 * DN_BETA),
        "ln1_g": 1.0 + nrm(ks[6], (L, D_MODEL), 0.02),
        "ln1_b": nrm(ks[7], (L, D_MODEL), 0.02),
        "w_mem_q": nrm(ks[8], (L, D_MODEL, MEM_WIDTH), D_MODEL ** -0.5),
        "w_mem_k": nrm(ks[9], (L, D_MODEL, MEM_WIDTH), D_MODEL ** -0.5),
        "w_mem_v": nrm(ks[10], (L, D_MODEL, MEM_WIDTH), D_MODEL ** -0.5),
        "w_mem_out": nrm(ks[11], (L, MEM_WIDTH, D_MODEL), MEM_WIDTH ** -0.5 * DN_BETA),
        "ln2_g": 1.0 + nrm(ks[12], (L, D_MODEL), 0.02),
        "ln2_b": nrm(ks[13], (L, D_MODEL), 0.02),
        "w_router": nrm(ks[14], (L, D_MODEL, N_EXPERTS), D_MODEL ** -0.5),
        "w_exp_gate": nrm(ks[15], (L, N_EXPERTS, D_MODEL, EXPERT_FF), D_MODEL ** -0.5),
        "w_exp_up": nrm(ks[16], (L, N_EXPERTS, D_MODEL, EXPERT_FF), D_MODEL ** -0.5),
        "w_exp_down": nrm(ks[17], (L, N_EXPERTS, EXPERT_FF, D_MODEL), EXPERT_FF ** -0.5 * DN_BETA),
        "ln3_g": 1.0 + nrm(ks[18], (L, D_MODEL), 0.02),
        "ln3_b": nrm(ks[19], (L, D_MODEL), 0.02),
    }


def reference(x, mem, w_in, na_rpb, conv_w, w_mix_out, ln1_g, ln1_b,
              w_mem_q, w_mem_k, w_mem_v, w_mem_out, ln2_g, ln2_b,
              w_router, w_exp_gate, w_exp_up, w_exp_down, ln3_g, ln3_b):
    for l in range(DEPTH):
        x = layer_norm(DN_ALPHA * x + hybrid_mixer(x, w_in[l], na_rpb[l], conv_w[l], w_mix_out[l]),
                       ln1_g[l], ln1_b[l])
        x = layer_norm(DN_ALPHA * x + memory_cross_attention(x, mem, w_mem_q[l], w_mem_k[l],
                                                             w_mem_v[l], w_mem_out[l]),
                       ln2_g[l], ln2_b[l])
        x = layer_norm(DN_ALPHA * x + expert_choice_moe(x, w_router[l], w_exp_gate[l],
                                                        w_exp_up[l], w_exp_down[l]),
                       ln3_g[l], ln3_b[l])
    return x
```

```python
import functools

import jax
import jax.numpy as jnp
from jax import lax
from jax.experimental import pallas as pl
from jax.experimental.pallas import tpu as pltpu

F32 = jnp.float32
BF16 = jnp.bfloat16
I32 = jnp.int32

GRID_W = 64
HEAD_DIM = 64
NA_HEADS = 8
NA_WIDTH = NA_HEADS * HEAD_DIM
WIN_ROWS = 8
WIN_COLS = 16
MEM_HEADS = 4
MEM_HEAD_DIM = 128
N_EXPERTS = 16
CAPACITY_FACTOR = 2
LN_EPS = 1e-5
NEG = -1e30

LANES = 128
SUBLANES = 8
VMEM_LIMIT = 56 * 1024 * 1024

ROW_BLOCK = 8
TOK_BLOCK = ROW_BLOCK * GRID_W
ROUTE_TILE = 256
SLOT_CHUNK = 256
FFN_TM = 1024
FFN_TF = 512


def _layer_norm(z, g, b):
    mu = jnp.mean(z, axis=-1, keepdims=True)
    zc = z - mu
    var = jnp.mean(zc * zc, axis=-1, keepdims=True)
    return zc * lax.rsqrt(var + LN_EPS) * g + b


def _dot_nt(a, b):
    return lax.dot_general(a, b, (((1,), (1,)), ((), ())),
                           preferred_element_type=F32)


def _proj_kernel(x_ref, w_ref, q_ref, k_ref, v_ref, b_ref, u_ref):
    xb = x_ref[...].astype(BF16)
    nw = NA_WIDTH

    def mm(c0):
        return jnp.dot(xb, w_ref[:, c0:c0 + nw], preferred_element_type=F32)

    q_ref[...] = (mm(0) * (HEAD_DIM ** -0.5)).astype(BF16)
    k_ref[...] = mm(nw).astype(BF16)
    v_ref[...] = mm(2 * nw).astype(BF16)
    b_ref[...] = mm(3 * nw)
    u_ref[...] = mm(4 * nw) * mm(5 * nw)


def _proj(x, w_in_b):
    S, D = x.shape
    nw = NA_WIDTH
    tm = TOK_BLOCK
    blk = lambda i: (i, 0)
    return pl.pallas_call(
        _proj_kernel,
        grid=(S // tm,),
        in_specs=[pl.BlockSpec((tm, D), blk),
                  pl.BlockSpec((D, 6 * nw), lambda i: (0, 0))],
        out_specs=[pl.BlockSpec((tm, nw), blk)] * 5,
        out_shape=[jax.ShapeDtypeStruct((S, nw), BF16)] * 3
        + [jax.ShapeDtypeStruct((S, nw), F32)] * 2,
        compiler_params=pltpu.CompilerParams(
            dimension_semantics=("arbitrary",), vmem_limit_bytes=VMEM_LIMIT),
        name="proj",
    )(x, w_in_b)


def _build_na_bias(rpb_ref, bias_ref):
    shape = (GRID_W, LANES)
    qc = lax.broadcasted_iota(I32, shape, 0)
    ln = lax.broadcasted_iota(I32, shape, 1)
    kc = ln % GRID_W
    c_start = jnp.clip(qc - WIN_COLS // 2, 0, GRID_W - WIN_COLS)
    left = ln < GRID_W
    base = LANES - (WIN_COLS - 1)

    def canvas(h, d):
        ra = jnp.broadcast_to(rpb_ref[h, d:d + 1, :], shape)
        rb = jnp.broadcast_to(rpb_ref[h, d + 1:d + 2, :], shape)
        ta = pltpu.roll(pltpu.roll(ra, base, 1), 0, 1, stride=1, stride_axis=0)
        tb = pltpu.roll(pltpu.roll(rb, (base + GRID_W) % LANES, 1), 0, 1,
                        stride=1, stride_axis=0)
        t = jnp.where(left, ta, tb)
        return jnp.where(kc >= c_start, jnp.where(kc < c_start + WIN_COLS, t, NEG), NEG)

    for par in range(2):
        for hp in range(NA_HEADS // 2):
            for m in range(WIN_ROWS):
                d = 2 * m + par
                bias_ref[par, hp, m, 0:GRID_W] = canvas(2 * hp, d)
                bias_ref[par, hp, m, GRID_W:2 * GRID_W] = canvas(2 * hp + 1, d)


def _mixer_kernel(rows, alpha,
                  q_ref, kp_ref, kc_ref, kn_ref, vp_ref, vc_ref, vn_ref,
                  b_ref, u_ref, up_ref, un_ref, x_ref, wout_ref, rpb_ref,
                  cw_ref, g_ref, be_ref, o_ref, kbuf, vbuf, mix, bias_ref):
    i = pl.program_id(0)
    nb = pl.num_programs(0)
    tb = TOK_BLOCK
    win = WIN_ROWS * GRID_W

    @pl.when(i == 0)
    def _():
        _build_na_bias(rpb_ref, bias_ref)

    kbuf[0:tb] = kp_ref[...]
    kbuf[tb:2 * tb] = kc_ref[...]
    kbuf[2 * tb:3 * tb] = kn_ref[...]
    vbuf[0:tb] = vp_ref[...]
    vbuf[tb:2 * tb] = vc_ref[...]
    vbuf[2 * tb:3 * tb] = vn_ref[...]

    lane = lax.broadcasted_iota(I32, (1, LANES), 1)
    first_head = lane < HEAD_DIM
    m_lo = jnp.where(first_head, 1.0, 0.0).astype(BF16)
    m_hi = jnp.where(first_head, 0.0, 1.0).astype(BF16)

    def row_body(r, carry):
        grow = i * ROW_BLOCK + r
        start = jnp.clip(grow - WIN_ROWS // 2, 0, rows - WIN_ROWS)
        w0 = pl.multiple_of((start - (i - 1) * ROW_BLOCK) * GRID_W, GRID_W)
        d0 = start - grow + WIN_ROWS
        par = d0 % 2
        m0 = d0 // 2
        q0 = pl.multiple_of(r * GRID_W, GRID_W)
        for hp in range(NA_HEADS // 2):
            cs = slice(hp * LANES, (hp + 1) * LANES)
            qp = q_ref[pl.ds(q0, GRID_W), cs]
            lhs = jnp.concatenate([qp * m_lo, qp * m_hi], axis=0)
            kw = kbuf[pl.ds(w0, win), cs]
            s = _dot_nt(lhs, kw)
            bias = jnp.concatenate(
                [bias_ref[par, hp, m0 + j] for j in range(WIN_ROWS // 2)], axis=1)
            s = s + bias
            mx = jnp.max(s, axis=-1, keepdims=True)
            p = jnp.exp(s - mx)
            l = jnp.sum(p, axis=-1, keepdims=True)
            vw = vbuf[pl.ds(w0, win), cs]
            o2 = jnp.dot(p.astype(BF16), vw, preferred_element_type=F32)
            o2 = o2 / l
            y = jnp.where(first_head, o2[0:GRID_W], o2[GRID_W:2 * GRID_W])
            mix[pl.ds(q0, GRID_W), cs] = y.astype(BF16)
        return carry

    lax.fori_loop(0, ROW_BLOCK, row_body, 0)

    u = u_ref[...]
    prev_row = up_ref[SUBLANES - 1:SUBLANES, :] * jnp.where(i > 0, 1.0, 0.0)
    next_row = un_ref[0:1, :] * jnp.where(i < nb - 1, 1.0, 0.0)
    rid = lax.broadcasted_iota(I32, (tb, 1), 0)
    um1 = jnp.where(rid == 0, prev_row, pltpu.roll(u, 1, axis=0))
    up1 = jnp.where(rid == tb - 1, next_row, pltpu.roll(u, tb - 1, axis=0))
    cw = cw_ref[...]
    yc = b_ref[...] * (cw[0:1] * um1 + cw[1:2] * u + cw[2:3] * up1)
    mix[:, NA_WIDTH:] = yc.astype(BF16)

    y = jnp.dot(mix[...], wout_ref[...], preferred_element_type=F32)
    z = alpha * x_ref[...] + y
    o_ref[...] = _layer_norm(z, g_ref[...], be_ref[...])


def _mixer(x, q, k, v, bg, u, w_out_b, rpb, conv_w, g, b, alpha):
    S, D = x.shape
    rpb_pad = jnp.pad(rpb.astype(F32), ((0, 0), (1, 1), (0, LANES - rpb.shape[2])))
    nw = NA_WIDTH
    tb = TOK_BLOCK
    nb = S // tb
    rows = S // GRID_W
    cur = lambda i: (i, 0)
    prv = lambda i: (jnp.maximum(i - 1, 0), 0)
    nxt = lambda i: (jnp.minimum(i + 1, nb - 1), 0)
    halo = tb // SUBLANES
    hprev = lambda i: (jnp.maximum(i * halo - 1, 0), 0)
    hnext = lambda i: (jnp.minimum((i + 1) * halo, S // SUBLANES - 1), 0)
    const2 = lambda i: (0, 0)
    kv = lambda im: pl.BlockSpec((tb, nw), im)
    return pl.pallas_call(
        functools.partial(_mixer_kernel, rows, alpha),
        grid=(nb,),
        in_specs=[kv(cur), kv(prv), kv(cur), kv(nxt), kv(prv), kv(cur), kv(nxt),
                  kv(cur), kv(cur),
                  pl.BlockSpec((SUBLANES, nw), hprev),
                  pl.BlockSpec((SUBLANES, nw), hnext),
                  pl.BlockSpec((tb, D), cur),
                  pl.BlockSpec((D, D), const2),
                  pl.BlockSpec(rpb_pad.shape, lambda i: (0, 0, 0)),
                  pl.BlockSpec(conv_w.shape, const2),
                  pl.BlockSpec((1, D), const2),
                  pl.BlockSpec((1, D), const2)],
        out_specs=pl.BlockSpec((tb, D), cur),
        out_shape=jax.ShapeDtypeStruct((S, D), F32),
        scratch_shapes=[pltpu.VMEM((3 * tb, nw), BF16),
                        pltpu.VMEM((3 * tb, nw), BF16),
                        pltpu.VMEM((tb, D), BF16),
                        pltpu.VMEM((2, NA_HEADS // 2, WIN_ROWS, 2 * GRID_W, LANES), F32)],
        compiler_params=pltpu.CompilerParams(
            dimension_semantics=("arbitrary",), vmem_limit_bytes=VMEM_LIMIT),
        name="mixer",
    )(q, k, k, k, v, v, v, bg, u, u, u, x, w_out_b, rpb_pad, conv_w, g, b)


def _xattn_kernel(alpha, x_ref, mem_ref, wq_ref, wk_ref, wv_ref, wo_ref,
                  g_ref, be_ref, wrh_ref, wrl_ref,
                  x2_ref, x2b_ref, aff_ref, kmem, vmem):
    i = pl.program_id(0)

    @pl.when(i == 0)
    def _():
        mb = mem_ref[...].astype(BF16)
        kmem[...] = jnp.dot(mb, wk_ref[...], preferred_element_type=F32).astype(BF16)
        vmem[...] = jnp.dot(mb, wv_ref[...], preferred_element_type=F32).astype(BF16)

    x1 = x_ref[...]
    qb = jnp.dot(x1.astype(BF16), wq_ref[...], preferred_element_type=F32).astype(BF16)
    outs = []
    for h in range(MEM_HEADS):
        cs = slice(h * MEM_HEAD_DIM, (h + 1) * MEM_HEAD_DIM)
        s = _dot_nt(qb[:, cs], kmem[:, cs]) * (MEM_HEAD_DIM ** -0.5)
        mx = jnp.max(s, axis=-1, keepdims=True)
        p = jnp.exp(s - mx)
        l = jnp.sum(p, axis=-1, keepdims=True)
        oh = jnp.dot(p.astype(BF16), vmem[:, cs], preferred_element_type=F32)
        outs.append(oh / l)
    o = jnp.concatenate(outs, axis=1).astype(BF16)
    y = jnp.dot(o, wo_ref[...], preferred_element_type=F32)
    x2 = _layer_norm(alpha * x1 + y, g_ref[...], be_ref[...])
    x2_ref[...] = x2
    hi = x2.astype(BF16)
    x2b_ref[...] = hi
    lo = (x2 - hi.astype(F32)).astype(BF16)
    wrh = wrh_ref[...]
    lg = _dot_nt(wrh, hi) + _dot_nt(wrh, lo) + _dot_nt(wrl_ref[...], hi)
    mx = jnp.max(lg, axis=0, keepdims=True)
    p = jnp.exp(lg - mx)
    aff = p / jnp.sum(p, axis=0, keepdims=True)
    for j in range(aff_ref.shape[0]):
        aff_ref[j] = aff[:, j * LANES:(j + 1) * LANES]


def _xattn(x1, mem, wq_b, wk_b, wv_b, wo_b, g, b, wr_hi, wr_lo, alpha):
    S, D = x1.shape
    M = mem.shape[0]
    mw = MEM_HEADS * MEM_HEAD_DIM
    tm = TOK_BLOCK
    cur = lambda i: (i, 0)
    const2 = lambda i: (0, 0)
    full = lambda a: pl.BlockSpec(a.shape, const2)
    return pl.pallas_call(
        functools.partial(_xattn_kernel, alpha),
        grid=(S // tm,),
        in_specs=[pl.BlockSpec((tm, D), cur), full(mem), full(wq_b), full(wk_b),
                  full(wv_b), full(wo_b), full(g), full(b), full(wr_hi), full(wr_lo)],
        out_specs=[pl.BlockSpec((tm, D), cur), pl.BlockSpec((tm, D), cur),
                   pl.BlockSpec((tm // LANES, N_EXPERTS, LANES), lambda i: (i, 0, 0))],
        out_shape=[jax.ShapeDtypeStruct((S, D), F32),
                   jax.ShapeDtypeStruct((S, D), BF16),
                   jax.ShapeDtypeStruct((S // LANES, N_EXPERTS, LANES), F32)],
        scratch_shapes=[pltpu.VMEM((M, mw), BF16), pltpu.VMEM((M, mw), BF16)],
        compiler_params=pltpu.CompilerParams(
            dimension_semantics=("arbitrary",), vmem_limit_bytes=VMEM_LIMIT),
        name="xattn",
    )(x1, mem, wq_b, wk_b, wv_b, wo_b, g, b, wr_hi, wr_lo)


def _route_kernel(cap, aff_ref, loc_ref, b_ref, c_ref):
    nblk = aff_ref.shape[0]
    per_tile = ROUTE_TILE // LANES
    ntile = nblk // per_tile

    def count_ge(t):
        ge = jnp.where(aff_ref[...] >= t[None], 1.0, 0.0)
        return jnp.sum(jnp.sum(ge, axis=0), axis=1, keepdims=True)

    def coarse(it, bits):
        cand = bits | jnp.left_shift(jnp.int32(1), 30 - it)
        ok = count_ge(pltpu.bitcast(cand, F32)) >= cap
        return jnp.where(ok, cand, bits)

    def fine(it, lohi):
        lo, hi = lohi
        mid = lo + (hi - lo) * 0.5
        ok = count_ge(mid) >= cap
        return jnp.where(ok, mid, lo), jnp.where(ok, hi, mid)

    bits = lax.fori_loop(0, 31, coarse, jnp.zeros((N_EXPERTS, 1), I32))
    thr, _ = lax.fori_loop(
        0, 30, fine, (pltpu.bitcast(bits, F32), pltpu.bitcast(bits + 1, F32)))
    gt_all = jnp.where(aff_ref[...] > thr[None], 1.0, 0.0)
    n_gt = jnp.sum(jnp.sum(gt_all, axis=0), axis=1, keepdims=True)
    need = cap - n_gt

    ii = lax.broadcasted_iota(I32, (LANES, LANES), 0)
    jj = lax.broadcasted_iota(I32, (LANES, LANES), 1)
    tri = jnp.where(ii <= jj, 1.0, 0.0).astype(BF16)

    def tile_body(t, carry):
        run, run_eq = carry
        tile_run = jnp.zeros((N_EXPERTS, 1), F32)
        for j in range(per_tile):
            blk = aff_ref[t * per_tile + j]
            gt = jnp.where(blk > thr, 1.0, 0.0)
            eq = jnp.where(blk == thr, 1.0, 0.0)
            eq_incl = run_eq + jnp.dot(eq.astype(BF16), tri, preferred_element_type=F32)
            sel = gt + eq * jnp.where(eq_incl <= need, 1.0, 0.0)
            run_eq = run_eq + jnp.sum(eq, axis=1, keepdims=True)
            incl = jnp.dot(sel.astype(BF16), tri, preferred_element_type=F32)
            loc = tile_run + incl - sel
            loc_ref[t * per_tile + j] = jnp.where(sel > 0.5, loc, -1.0).astype(I32)
            tile_run = tile_run + jnp.sum(sel, axis=1, keepdims=True)
        b_ref[t] = run.astype(I32)
        c_ref[t] = tile_run.astype(I32)
        return run + tile_run, run_eq

    zero = jnp.zeros((N_EXPERTS, 1), F32)
    lax.fori_loop(0, ntile, tile_body, (zero, zero))


def _route(aff3, cap):
    nblk = aff3.shape[0]
    ntile = nblk * LANES // ROUTE_TILE
    full3 = lambda a: pl.BlockSpec(a, lambda: (0, 0, 0))
    return pl.pallas_call(
        functools.partial(_route_kernel, cap),
        in_specs=[full3(aff3.shape)],
        out_specs=[full3(aff3.shape), full3((ntile, N_EXPERTS, 1)),
                   full3((ntile, N_EXPERTS, 1))],
        out_shape=[jax.ShapeDtypeStruct(aff3.shape, I32),
                   jax.ShapeDtypeStruct((ntile, N_EXPERTS, 1), I32),
                   jax.ShapeDtypeStruct((ntile, N_EXPERTS, 1), I32)],
        compiler_params=pltpu.CompilerParams(vmem_limit_bytes=VMEM_LIMIT),
        name="route",
    )(aff3)


def _segment_copies(b_sm, c_sm, t, make_copy, action):
    off = 0
    nbits = ROUTE_TILE.bit_length()
    for e in range(N_EXPERTS):
        c = c_sm[t * N_EXPERTS + e]
        b = b_sm[t * N_EXPERTS + e]
        for kbit in range(nbits - 1, -1, -1):
            n = 1 << kbit
            r0 = (c >> (kbit + 1)) << (kbit + 1)

            @pl.when((c & n) != 0)
            def _(e=e, b=b, r0=r0, n=n, off=off):
                action(make_copy(e, b + r0, off + r0, n))
        off = off + c
    return off


def _dispatch_kernel(cap, b_sm, c_sm, x_ref, loc_ref, xe_hbm, stg, sem):
    t = pl.program_id(0)
    nt = pl.num_programs(0)
    slot = t % 2

    offs = []
    off = 0
    for e in range(N_EXPERTS):
        offs.append(off)
        off = off + c_sm[t * N_EXPERTS + e]
    offs.append(off)
    nchunk = (off + SLOT_CHUNK - 1) // SLOT_CHUNK
    sub = lax.broadcasted_iota(I32, (SLOT_CHUNK, 1), 0)
    lane = lax.broadcasted_iota(I32, (1, LANES), 1)
    seg_lo = jnp.zeros((1, LANES), I32)
    seg_hi = jnp.zeros((1, LANES), I32)
    for e in range(N_EXPERTS):
        seg_lo = jnp.where(lane == e, offs[e], seg_lo)
        seg_hi = jnp.where(lane == e, offs[e + 1], seg_hi)
    seg_lo_f = seg_lo.astype(F32)
    locf = jnp.concatenate(
        [loc_ref[j] for j in range(ROUTE_TILE // LANES)], axis=1).astype(F32)
    locb = jnp.concatenate(
        [locf, jnp.zeros((LANES - N_EXPERTS, ROUTE_TILE), F32)], axis=0).astype(BF16)

    def chunk_body(m, carry):
        base = m * SLOT_CHUNK
        srow = sub + base
        inseg = jnp.where(srow >= seg_lo, jnp.where(srow < seg_hi, 1.0, 0.0), 0.0)
        rank = srow.astype(F32) - jnp.sum(inseg * seg_lo_f, axis=1, keepdims=True)
        want = jnp.dot(inseg.astype(BF16), locb, preferred_element_type=F32)
        p = jnp.where(want == rank, 1.0, 0.0)
        res = jnp.dot(p.astype(BF16), x_ref[...], preferred_element_type=F32)
        row0 = pl.multiple_of(base * SUBLANES, SUBLANES)
        for s in range(SUBLANES):
            stg[slot, pl.ds(row0 + s, SLOT_CHUNK, stride=SUBLANES), :] = (
                res[:, s * LANES:(s + 1) * LANES])
        return carry

    lax.fori_loop(0, nchunk, chunk_body, 0)

    def copy_for(slot_):
        def make(e, dst_row, src_row, n):
            return pltpu.make_async_copy(
                stg.at[slot_, pl.ds(pl.multiple_of(src_row * SUBLANES, SUBLANES), n * SUBLANES)],
                xe_hbm.at[pl.ds(pl.multiple_of((e * cap + dst_row) * SUBLANES, SUBLANES),
                                n * SUBLANES)],
                sem.at[slot_])
        return make

    _segment_copies(b_sm, c_sm, t, copy_for(slot), lambda cp: cp.start())

    @pl.when(t > 0)
    def _():
        _segment_copies(b_sm, c_sm, t - 1, copy_for(1 - slot), lambda cp: cp.wait())

    @pl.when(t == nt - 1)
    def _():
        _segment_copies(b_sm, c_sm, t, copy_for(slot), lambda cp: cp.wait())


def _dispatch(b_flat, c_flat, x2b, loc3, cap):
    S, D = x2b.shape
    nt = S // ROUTE_TILE
    per_tile = ROUTE_TILE // LANES
    max_rows = N_EXPERTS * ROUTE_TILE
    return pl.pallas_call(
        functools.partial(_dispatch_kernel, cap),
        grid_spec=pltpu.PrefetchScalarGridSpec(
            num_scalar_prefetch=2,
            grid=(nt,),
            in_specs=[pl.BlockSpec((ROUTE_TILE, D), lambda t, b, c: (t, 0)),
                      pl.BlockSpec((per_tile, N_EXPERTS, LANES), lambda t, b, c: (t, 0, 0))],
            out_specs=pl.BlockSpec(memory_space=pl.ANY),
            scratch_shapes=[pltpu.VMEM((2, max_rows * SUBLANES, LANES), F32),
                            pltpu.SemaphoreType.DMA((2,))]),
        out_shape=jax.ShapeDtypeStruct((N_EXPERTS * cap * SUBLANES, LANES), F32),
        compiler_params=pltpu.CompilerParams(
            dimension_semantics=("arbitrary",), vmem_limit_bytes=VMEM_LIMIT),
        name="dispatch",
    )(b_flat, c_flat, x2b, loc3)


def _ffn_kernel(xe_ref, wg_ref, wu_ref, wd_ref, wr_ref, o_ref, xb, acc, gate):
    e = pl.program_id(0)
    f = pl.program_id(2)
    tm = xb.shape[0]

    @pl.when(f == 0)
    def _():
        for s in range(SUBLANES):
            xb[:, s * LANES:(s + 1) * LANES] = (
                xe_ref[pl.ds(s, tm, stride=SUBLANES), :].astype(BF16))
        lg = jnp.dot(xb[...], wr_ref[...], preferred_element_type=F32)
        lane = lax.broadcasted_iota(I32, (1, LANES), 1)
        lg = jnp.where(lane < N_EXPERTS, lg, NEG)
        p = jnp.exp(lg - jnp.max(lg, axis=-1, keepdims=True))
        mine = jnp.sum(jnp.where(lane == e, p, 0.0), axis=-1, keepdims=True)
        gate[...] = mine / jnp.sum(p, axis=-1, keepdims=True)
        acc[...] = jnp.zeros_like(acc)

    x = xb[...]
    g = jnp.dot(x, wg_ref[...].astype(BF16), preferred_element_type=F32)
    u = jnp.dot(x, wu_ref[...].astype(BF16), preferred_element_type=F32)
    h = (g * jax.nn.sigmoid(g) * u).astype(BF16)
    acc[...] += jnp.dot(h, wd_ref[...].astype(BF16), preferred_element_type=F32)

    @pl.when(f == pl.num_programs(2) - 1)
    def _():
        y = acc[...] * gate[...]
        for s in range(SUBLANES):
            o_ref[pl.ds(s, tm, stride=SUBLANES), :] = y[:, s * LANES:(s + 1) * LANES]


def _ffn(xe2d, w_gate, w_up, w_down, wr_pad, cap):
    E, D, FF = w_gate.shape
    tm, tf = FFN_TM, FFN_TF
    mt = cap // tm
    rows = lambda e, m, f: (e * mt + m, 0)
    return pl.pallas_call(
        _ffn_kernel,
        grid=(E, mt, FF // tf),
        in_specs=[pl.BlockSpec((tm * SUBLANES, LANES), rows),
                  pl.BlockSpec((None, D, tf), lambda e, m, f: (e, 0, f)),
                  pl.BlockSpec((None, D, tf), lambda e, m, f: (e, 0, f)),
                  pl.BlockSpec((None, tf, D), lambda e, m, f: (e, f, 0)),
                  pl.BlockSpec(wr_pad.shape, lambda e, m, f: (0, 0))],
        out_specs=pl.BlockSpec((tm * SUBLANES, LANES), rows),
        out_shape=jax.ShapeDtypeStruct(xe2d.shape, F32),
        scratch_shapes=[pltpu.VMEM((tm, D), BF16), pltpu.VMEM((tm, D), F32),
                        pltpu.VMEM((tm, 1), F32)],
        compiler_params=pltpu.CompilerParams(
            dimension_semantics=("arbitrary", "arbitrary", "arbitrary"),
            vmem_limit_bytes=VMEM_LIMIT),
        name="ffn",
    )(xe2d, w_gate, w_up, w_down, wr_pad)


def _combine_kernel(cap, alpha, b_sm, c_sm, x_ref, loc_ref, g_ref, be_ref, ye_hbm,
                    o_ref, ybuf, sem):
    t = pl.program_id(0)
    nt = pl.num_programs(0)
    slot = t % 2

    def copy_for(slot_):
        def make(e, src_row, dst_row, n):
            return pltpu.make_async_copy(
                ye_hbm.at[pl.ds(pl.multiple_of((e * cap + src_row) * SUBLANES, SUBLANES),
                                n * SUBLANES)],
                ybuf.at[slot_, pl.ds(pl.multiple_of(dst_row * SUBLANES, SUBLANES), n * SUBLANES)],
                sem.at[slot_])
        return make

    @pl.when(t == 0)
    def _():
        ybuf[...] = jnp.zeros_like(ybuf)
        _segment_copies(b_sm, c_sm, t, copy_for(slot), lambda cp: cp.start())

    @pl.when(t < nt - 1)
    def _():
        _segment_copies(b_sm, c_sm, t + 1, copy_for(1 - slot), lambda cp: cp.start())

    _segment_copies(b_sm, c_sm, t, copy_for(slot), lambda cp: cp.wait())

    offs = []
    off = 0
    for e in range(N_EXPERTS):
        offs.append(off)
        off = off + c_sm[t * N_EXPERTS + e]
    offs.append(off)
    total = off
    nchunk = (total + SLOT_CHUNK - 1) // SLOT_CHUNK
    lane = lax.broadcasted_iota(I32, (1, SLOT_CHUNK), 1)
    sub = lax.broadcasted_iota(I32, (SLOT_CHUNK, 1), 0)
    eid = lax.broadcasted_iota(I32, (N_EXPERTS, 1), 0)
    seg_lo = jnp.zeros((N_EXPERTS, 1), I32)
    seg_hi = jnp.zeros((N_EXPERTS, 1), I32)
    for e in range(N_EXPERTS):
        seg_lo = jnp.where(eid == e, offs[e], seg_lo)
        seg_hi = jnp.where(eid == e, offs[e + 1], seg_hi)
    seg_lo_f = seg_lo.astype(F32)
    locb = loc_ref[...].astype(F32).astype(BF16)

    def chunk_body(m, acc):
        base = m * SLOT_CHUNK
        srow = lane + base
        inseg = jnp.where(srow >= seg_lo, jnp.where(srow < seg_hi, 1.0, 0.0), 0.0)
        rank = srow.astype(F32) - jnp.sum(inseg * seg_lo_f, axis=0, keepdims=True)
        want = jnp.dot(locb, inseg.astype(BF16), preferred_element_type=F32)
        ptb = jnp.where(want == rank, 1.0, 0.0).astype(BF16)
        row0 = pl.multiple_of(base * SUBLANES, SUBLANES)
        yc = jnp.concatenate(
            [ybuf[slot, pl.ds(row0 + s, SLOT_CHUNK, stride=SUBLANES), :]
             for s in range(SUBLANES)], axis=1)
        yc = jnp.where(sub < total - base, yc, 0.0)
        hi = yc.astype(BF16)
        lo = (yc - hi.astype(F32)).astype(BF16)
        return (acc + jnp.dot(ptb, hi, preferred_element_type=F32)
                + jnp.dot(ptb, lo, preferred_element_type=F32))

    acc = lax.fori_loop(0, nchunk, chunk_body, alpha * x_ref[...])
    o_ref[...] = _layer_norm(acc, g_ref[...], be_ref[...])


def _combine(b_flat, c_flat, x2, loc_t, g, b, ye2d, cap, alpha):
    S, D = x2.shape
    nt = S // ROUTE_TILE
    max_rows = N_EXPERTS * ROUTE_TILE
    cur = lambda t, b_, c_: (t, 0)
    const2 = lambda t, b_, c_: (0, 0)
    return pl.pallas_call(
        functools.partial(_combine_kernel, cap, alpha),
        grid_spec=pltpu.PrefetchScalarGridSpec(
            num_scalar_prefetch=2,
            grid=(nt,),
            in_specs=[pl.BlockSpec((ROUTE_TILE, D), cur),
                      pl.BlockSpec((ROUTE_TILE, N_EXPERTS), cur),
                      pl.BlockSpec((1, D), const2),
                      pl.BlockSpec((1, D), const2),
                      pl.BlockSpec(memory_space=pl.ANY)],
            out_specs=pl.BlockSpec((ROUTE_TILE, D), cur),
            scratch_shapes=[pltpu.VMEM((2, max_rows * SUBLANES, LANES), F32),
                            pltpu.SemaphoreType.DMA((2,))]),
        out_shape=jax.ShapeDtypeStruct((S, D), F32),
        compiler_params=pltpu.CompilerParams(
            dimension_semantics=("arbitrary",), vmem_limit_bytes=VMEM_LIMIT),
        name="combine",
    )(b_flat, c_flat, x2, loc_t, g, b, ye2d)


def _layer(x, mem, w_in, rpb, conv_w, w_mix_out, ln1_g, ln1_b, wq, wk, wv, wo,
           ln2_g, ln2_b, w_router, w_gate, w_up, w_down, ln3_g, ln3_b, alpha):
    S, D = x.shape
    cap = CAPACITY_FACTOR * S // N_EXPERTS
    assert S % TOK_BLOCK == 0 and S // GRID_W >= 2 * ROW_BLOCK
    assert cap % FFN_TM == 0 and cap <= S
    row = lambda a: a.reshape(1, -1)

    q, k, v, bg, u = _proj(x, w_in.astype(BF16))
    x1 = _mixer(x, q, k, v, bg, u, w_mix_out.astype(BF16), rpb,
                conv_w, row(ln1_g), row(ln1_b), alpha)

    wr_t = w_router.T
    wr_hi = wr_t.astype(BF16)
    wr_lo = (wr_t - wr_hi.astype(F32)).astype(BF16)
    x2, x2b, aff3 = _xattn(x1, mem, wq.astype(BF16), wk.astype(BF16), wv.astype(BF16),
                           wo.astype(BF16), row(ln2_g), row(ln2_b), wr_hi, wr_lo, alpha)

    loc3, b3, c3 = _route(aff3, cap)
    b_flat = b3.reshape(-1)
    c_flat = c3.reshape(-1)
    loc_t = loc3.transpose(0, 2, 1).reshape(S, N_EXPERTS)

    xe2d = _dispatch(b_flat, c_flat, x2b, loc3, cap)
    wr_pad = jnp.pad(w_router.astype(BF16), ((0, 0), (0, LANES - N_EXPERTS)))
    ye2d = _ffn(xe2d, w_gate, w_up, w_down, wr_pad, cap)
    return _combine(b_flat, c_flat, x2, loc_t, row(ln3_g), row(ln3_b), ye2d, cap, alpha)


def kernel(x, mem, w_in, na_rpb, conv_w, w_mix_out, ln1_g, ln1_b, w_mem_q, w_mem_k,
           w_mem_v, w_mem_out, ln2_g, ln2_b, w_router, w_exp_gate, w_exp_up,
           w_exp_down, ln3_g, ln3_b):
    depth = w_in.shape[0]
    alpha = (2 * depth) ** 0.25
    outs = []
    for bi in range(x.shape[0]):
        xb = x[bi]
        for l in range(depth):
            xb = _layer(xb, mem[bi], w_in[l], na_rpb[l], conv_w[l], w_mix_out[l],
                        ln1_g[l], ln1_b[l], w_mem_q[l], w_mem_k[l], w_mem_v[l],
                        w_mem_out[l], ln2_g[l], ln2_b[l], w_router[l], w_exp_gate[l],
                        w_exp_up[l], w_exp_down[l], ln3_g[l], ln3_b[l], alpha)
        outs.append(xb)
    return jnp.stack(outs)
```

```python
import functools

import jax
import jax.numpy as jnp
from jax import lax
from jax.experimental import pallas as pl
from jax.experimental.pallas import tpu as pltpu

F32 = jnp.float32
BF16 = jnp.bfloat16
I32 = jnp.int32

GRID_W = 64
HEAD_DIM = 64
NA_HEADS = 8
NA_WIDTH = NA_HEADS * HEAD_DIM
WIN_ROWS = 8
WIN_COLS = 16
MEM_HEADS = 4
MEM_HEAD_DIM = 128
N_EXPERTS = 16
CAPACITY_FACTOR = 2
LN_EPS = 1e-5
NEG = -1e30

LANES = 128
SUBLANES = 8
VMEM_LIMIT = 56 * 1024 * 1024

ROW_BLOCK = 8
TOK_BLOCK = ROW_BLOCK * GRID_W
ROUTE_TILE = 256
SLOT_CHUNK = 256
SM_ROWS = 16
FFN_TM = 1024
FFN_TF = 512


def _layer_norm(z, g, b):
    mu = jnp.mean(z, axis=-1, keepdims=True)
    zc = z - mu
    var = jnp.mean(zc * zc, axis=-1, keepdims=True)
    return zc * lax.rsqrt(var + LN_EPS) * g + b


def _dot_nt(a, b):
    return lax.dot_general(a, b, (((1,), (1,)), ((), ())),
                           preferred_element_type=F32)


def _proj_kernel(x_ref, w_ref, q_ref, k_ref, v_ref, b_ref, u_ref):
    xb = x_ref[...].astype(BF16)
    nw = NA_WIDTH

    def mm(c0):
        return jnp.dot(xb, w_ref[:, c0:c0 + nw], preferred_element_type=F32)

    q_ref[...] = (mm(0) * (HEAD_DIM ** -0.5)).astype(BF16)
    k_ref[...] = mm(nw).astype(BF16)
    v_ref[...] = mm(2 * nw).astype(BF16)
    b_ref[...] = mm(3 * nw)
    u_ref[...] = mm(4 * nw) * mm(5 * nw)


def _proj(x, w_in_b):
    S, D = x.shape
    nw = NA_WIDTH
    tm = TOK_BLOCK
    blk = lambda i: (i, 0)
    return pl.pallas_call(
        _proj_kernel,
        grid=(S // tm,),
        in_specs=[pl.BlockSpec((tm, D), blk),
                  pl.BlockSpec((D, 6 * nw), lambda i: (0, 0))],
        out_specs=[pl.BlockSpec((tm, nw), blk)] * 5,
        out_shape=[jax.ShapeDtypeStruct((S, nw), BF16)] * 3
        + [jax.ShapeDtypeStruct((S, nw), F32)] * 2,
        compiler_params=pltpu.CompilerParams(
            dimension_semantics=("arbitrary",), vmem_limit_bytes=VMEM_LIMIT),
        name="proj",
    )(x, w_in_b)


def _build_na_bias(rpb_ref, bias_ref):
    shape = (GRID_W, LANES)
    qc = lax.broadcasted_iota(I32, shape, 0)
    ln = lax.broadcasted_iota(I32, shape, 1)
    kc = ln % GRID_W
    c_start = jnp.clip(qc - WIN_COLS // 2, 0, GRID_W - WIN_COLS)
    left = ln < GRID_W
    base = LANES - (WIN_COLS - 1)

    def canvas(h, d):
        ra = jnp.broadcast_to(rpb_ref[h, d:d + 1, :], shape)
        rb = jnp.broadcast_to(rpb_ref[h, d + 1:d + 2, :], shape)
        ta = pltpu.roll(pltpu.roll(ra, base, 1), 0, 1, stride=1, stride_axis=0)
        tb = pltpu.roll(pltpu.roll(rb, (base + GRID_W) % LANES, 1), 0, 1,
                        stride=1, stride_axis=0)
        t = jnp.where(left, ta, tb)
        return jnp.where(kc >= c_start, jnp.where(kc < c_start + WIN_COLS, t, NEG), NEG)

    for par in range(2):
        for hp in range(NA_HEADS // 2):
            for m in range(WIN_ROWS):
                d = 2 * m + par
                bias_ref[par, hp, m, 0:GRID_W] = canvas(2 * hp, d)
                bias_ref[par, hp, m, GRID_W:2 * GRID_W] = canvas(2 * hp + 1, d)


def _mixer_kernel(rows, alpha,
                  q_ref, kp_ref, kc_ref, kn_ref, vp_ref, vc_ref, vn_ref,
                  b_ref, u_ref, up_ref, un_ref, x_ref, wout_ref, rpb_ref,
                  cw_ref, g_ref, be_ref, o_ref, kbuf, vbuf, mix, bias_ref,
                  s_scr0, s_scr1, p_scr0, p_scr1):
    i = pl.program_id(0)
    nb = pl.num_programs(0)
    tb = TOK_BLOCK
    win = WIN_ROWS * GRID_W

    @pl.when(i == 0)
    def _():
        _build_na_bias(rpb_ref, bias_ref)

    kbuf[0:tb] = kp_ref[...]
    kbuf[tb:2 * tb] = kc_ref[...]
    kbuf[2 * tb:3 * tb] = kn_ref[...]
    vbuf[0:tb] = vp_ref[...]
    vbuf[tb:2 * tb] = vc_ref[...]
    vbuf[2 * tb:3 * tb] = vn_ref[...]

    lane = lax.broadcasted_iota(I32, (1, LANES), 1)
    first_head = lane < HEAD_DIM
    m_lo = jnp.where(first_head, 1.0, 0.0).astype(BF16)
    m_hi = jnp.where(first_head, 0.0, 1.0).astype(BF16)

    def window(r):
        grow = i * ROW_BLOCK + r
        start = jnp.clip(grow - WIN_ROWS // 2, 0, rows - WIN_ROWS)
        w0 = pl.multiple_of((start - (i - 1) * ROW_BLOCK) * GRID_W, GRID_W)
        d0 = start - grow + WIN_ROWS
        return w0, d0 % 2, d0 // 2

    def scores(r, s_out):
        w0, _, _ = window(r)
        q0 = r * GRID_W if isinstance(r, int) else pl.multiple_of(r * GRID_W, GRID_W)
        for hp in range(NA_HEADS // 2):
            cs = slice(hp * LANES, (hp + 1) * LANES)
            qp = q_ref[pl.ds(q0, GRID_W), cs]
            lhs = jnp.concatenate([qp * m_lo, qp * m_hi], axis=0)
            kw = kbuf[pl.ds(w0, win), cs]
            s_out[hp] = _dot_nt(lhs, kw)

    def probs(r, s_in, p_out):
        _, par, m0 = window(r)
        for hp in range(NA_HEADS // 2):
            for rb in range(2 * GRID_W // SM_ROWS):
                rs = slice(rb * SM_ROWS, (rb + 1) * SM_ROWS)
                bias = jnp.concatenate(
                    [bias_ref[par, hp, m0 + j, rs, :] for j in range(WIN_ROWS // 2)], axis=1)
                s = s_in[hp, rs, :] + bias
                p = jnp.exp(s - jnp.max(s, axis=-1, keepdims=True))
                p = p * (1.0 / jnp.sum(p, axis=-1, keepdims=True))
                p_out[hp, rs, :] = p.astype(BF16)

    def values(r, p_in):
        w0, _, _ = window(r)
        q0 = r * GRID_W if isinstance(r, int) else pl.multiple_of(r * GRID_W, GRID_W)
        for hp in range(NA_HEADS // 2):
            cs = slice(hp * LANES, (hp + 1) * LANES)
            vw = vbuf[pl.ds(w0, win), cs]
            o2 = jnp.dot(p_in[hp], vw, preferred_element_type=F32)
            y = jnp.where(first_head, o2[0:GRID_W], o2[GRID_W:2 * GRID_W])
            mix[pl.ds(q0, GRID_W), cs] = y.astype(BF16)

    s_buf = (s_scr0, s_scr1)
    p_buf = (p_scr0, p_scr1)

    def steady(k, carry):
        for half in range(2):
            r = 1 + 2 * k + half
            values(r - 1, p_buf[half])
            probs(r, s_buf[1 - half], p_buf[1 - half])
            scores(r + 1, s_buf[half])
        return carry

    scores(0, s_buf[0])
    probs(0, s_buf[0], p_buf[0])
    scores(1, s_buf[1])
    lax.fori_loop(0, (ROW_BLOCK - 2) // 2, steady, 0)
    values(ROW_BLOCK - 2, p_buf[0])
    probs(ROW_BLOCK - 1, s_buf[1], p_buf[1])
    values(ROW_BLOCK - 1, p_buf[1])

    u = u_ref[...]
    prev_row = up_ref[SUBLANES - 1:SUBLANES, :] * jnp.where(i > 0, 1.0, 0.0)
    next_row = un_ref[0:1, :] * jnp.where(i < nb - 1, 1.0, 0.0)
    rid = lax.broadcasted_iota(I32, (tb, 1), 0)
    um1 = jnp.where(rid == 0, prev_row, pltpu.roll(u, 1, axis=0))
    up1 = jnp.where(rid == tb - 1, next_row, pltpu.roll(u, tb - 1, axis=0))
    cw = cw_ref[...]
    yc = b_ref[...] * (cw[0:1] * um1 + cw[1:2] * u + cw[2:3] * up1)
    mix[:, NA_WIDTH:] = yc.astype(BF16)

    y = jnp.dot(mix[...], wout_ref[...], preferred_element_type=F32)
    z = alpha * x_ref[...] + y
    o_ref[...] = _layer_norm(z, g_ref[...], be_ref[...])


def _mixer(x, q, k, v, bg, u, w_out_b, rpb, conv_w, g, b, alpha):
    S, D = x.shape
    rpb_pad = jnp.pad(rpb.astype(F32), ((0, 0), (1, 1), (0, LANES - rpb.shape[2])))
    nw = NA_WIDTH
    tb = TOK_BLOCK
    nb = S // tb
    rows = S // GRID_W
    cur = lambda i: (i, 0)
    prv = lambda i: (jnp.maximum(i - 1, 0), 0)
    nxt = lambda i: (jnp.minimum(i + 1, nb - 1), 0)
    halo = tb // SUBLANES
    hprev = lambda i: (jnp.maximum(i * halo - 1, 0), 0)
    hnext = lambda i: (jnp.minimum((i + 1) * halo, S // SUBLANES - 1), 0)
    const2 = lambda i: (0, 0)
    kv = lambda im: pl.BlockSpec((tb, nw), im)
    return pl.pallas_call(
        functools.partial(_mixer_kernel, rows, alpha),
        grid=(nb,),
        in_specs=[kv(cur), kv(prv), kv(cur), kv(nxt), kv(prv), kv(cur), kv(nxt),
                  kv(cur), kv(cur),
                  pl.BlockSpec((SUBLANES, nw), hprev),
                  pl.BlockSpec((SUBLANES, nw), hnext),
                  pl.BlockSpec((tb, D), cur),
                  pl.BlockSpec((D, D), const2),
                  pl.BlockSpec(rpb_pad.shape, lambda i: (0, 0, 0)),
                  pl.BlockSpec(conv_w.shape, const2),
                  pl.BlockSpec((1, D), const2),
                  pl.BlockSpec((1, D), const2)],
        out_specs=pl.BlockSpec((tb, D), cur),
        out_shape=jax.ShapeDtypeStruct((S, D), F32),
        scratch_shapes=[pltpu.VMEM((3 * tb, nw), BF16),
                        pltpu.VMEM((3 * tb, nw), BF16),
                        pltpu.VMEM((tb, D), BF16),
                        pltpu.VMEM((2, NA_HEADS // 2, WIN_ROWS, 2 * GRID_W, LANES), F32),
                        pltpu.VMEM((NA_HEADS // 2, 2 * GRID_W, WIN_ROWS * GRID_W), F32),
                        pltpu.VMEM((NA_HEADS // 2, 2 * GRID_W, WIN_ROWS * GRID_W), F32),
                        pltpu.VMEM((NA_HEADS // 2, 2 * GRID_W, WIN_ROWS * GRID_W), BF16),
                        pltpu.VMEM((NA_HEADS // 2, 2 * GRID_W, WIN_ROWS * GRID_W), BF16)],
        compiler_params=pltpu.CompilerParams(
            dimension_semantics=("arbitrary",), vmem_limit_bytes=VMEM_LIMIT),
        name="mixer",
    )(q, k, k, k, v, v, v, bg, u, u, u, x, w_out_b, rpb_pad, conv_w, g, b)


def _xattn_kernel(alpha, x_ref, mem_ref, wq_ref, wk_ref, wv_ref, wo_ref,
                  g_ref, be_ref, wrh_ref, wrl_ref,
                  x2_ref, x2b_ref, aff_ref, kmem, vmem):
    i = pl.program_id(0)

    @pl.when(i == 0)
    def _():
        mb = mem_ref[...].astype(BF16)
        kmem[...] = jnp.dot(mb, wk_ref[...], preferred_element_type=F32).astype(BF16)
        vmem[...] = jnp.dot(mb, wv_ref[...], preferred_element_type=F32).astype(BF16)

    x1 = x_ref[...]
    qb = jnp.dot(x1.astype(BF16), wq_ref[...], preferred_element_type=F32).astype(BF16)
    outs = []
    for h in range(MEM_HEADS):
        cs = slice(h * MEM_HEAD_DIM, (h + 1) * MEM_HEAD_DIM)
        s = _dot_nt(qb[:, cs], kmem[:, cs]) * (MEM_HEAD_DIM ** -0.5)
        mx = jnp.max(s, axis=-1, keepdims=True)
        p = jnp.exp(s - mx)
        l = jnp.sum(p, axis=-1, keepdims=True)
        oh = jnp.dot(p.astype(BF16), vmem[:, cs], preferred_element_type=F32)
        outs.append(oh / l)
    o = jnp.concatenate(outs, axis=1).astype(BF16)
    y = jnp.dot(o, wo_ref[...], preferred_element_type=F32)
    x2 = _layer_norm(alpha * x1 + y, g_ref[...], be_ref[...])
    x2_ref[...] = x2
    hi = x2.astype(BF16)
    x2b_ref[...] = hi
    lo = (x2 - hi.astype(F32)).astype(BF16)
    wrh = wrh_ref[...]
    lg = _dot_nt(wrh, hi) + _dot_nt(wrh, lo) + _dot_nt(wrl_ref[...], hi)
    mx = jnp.max(lg, axis=0, keepdims=True)
    p = jnp.exp(lg - mx)
    aff = p / jnp.sum(p, axis=0, keepdims=True)
    for j in range(aff_ref.shape[0]):
        aff_ref[j] = aff[:, j * LANES:(j + 1) * LANES]


def _xattn(x1, mem, wq_b, wk_b, wv_b, wo_b, g, b, wr_hi, wr_lo, alpha):
    S, D = x1.shape
    M = mem.shape[0]
    mw = MEM_HEADS * MEM_HEAD_DIM
    tm = TOK_BLOCK
    cur = lambda i: (i, 0)
    const2 = lambda i: (0, 0)
    full = lambda a: pl.BlockSpec(a.shape, const2)
    return pl.pallas_call(
        functools.partial(_xattn_kernel, alpha),
        grid=(S // tm,),
        in_specs=[pl.BlockSpec((tm, D), cur), full(mem), full(wq_b), full(wk_b),
                  full(wv_b), full(wo_b), full(g), full(b), full(wr_hi), full(wr_lo)],
        out_specs=[pl.BlockSpec((tm, D), cur), pl.BlockSpec((tm, D), cur),
                   pl.BlockSpec((tm // LANES, N_EXPERTS, LANES), lambda i: (i, 0, 0))],
        out_shape=[jax.ShapeDtypeStruct((S, D), F32),
                   jax.ShapeDtypeStruct((S, D), BF16),
                   jax.ShapeDtypeStruct((S // LANES, N_EXPERTS, LANES), F32)],
        scratch_shapes=[pltpu.VMEM((M, mw), BF16), pltpu.VMEM((M, mw), BF16)],
        compiler_params=pltpu.CompilerParams(
            dimension_semantics=("arbitrary",), vmem_limit_bytes=VMEM_LIMIT),
        name="xattn",
    )(x1, mem, wq_b, wk_b, wv_b, wo_b, g, b, wr_hi, wr_lo)


def _route_kernel(cap, aff_ref, loc_ref, b_ref, c_ref):
    nblk = aff_ref.shape[0]
    per_tile = ROUTE_TILE // LANES
    ntile = nblk // per_tile

    def count_ge(t):
        ge = jnp.where(aff_ref[...] >= t[None], 1.0, 0.0)
        return jnp.sum(jnp.sum(ge, axis=0), axis=1, keepdims=True)

    def coarse(it, bits):
        cand = bits | jnp.left_shift(jnp.int32(1), 30 - it)
        ok = count_ge(pltpu.bitcast(cand, F32)) >= cap
        return jnp.where(ok, cand, bits)

    def fine(it, lohi):
        lo, hi = lohi
        mid = lo + (hi - lo) * 0.5
        ok = count_ge(mid) >= cap
        return jnp.where(ok, mid, lo), jnp.where(ok, hi, mid)

    bits = lax.fori_loop(0, 31, coarse, jnp.zeros((N_EXPERTS, 1), I32))
    thr, _ = lax.fori_loop(
        0, 30, fine, (pltpu.bitcast(bits, F32), pltpu.bitcast(bits + 1, F32)))
    gt_all = jnp.where(aff_ref[...] > thr[None], 1.0, 0.0)
    n_gt = jnp.sum(jnp.sum(gt_all, axis=0), axis=1, keepdims=True)
    need = cap - n_gt

    ii = lax.broadcasted_iota(I32, (LANES, LANES), 0)
    jj = lax.broadcasted_iota(I32, (LANES, LANES), 1)
    tri = jnp.where(ii <= jj, 1.0, 0.0).astype(BF16)

    def tile_body(t, carry):
        run, run_eq = carry
        tile_run = jnp.zeros((N_EXPERTS, 1), F32)
        for j in range(per_tile):
            blk = aff_ref[t * per_tile + j]
            gt = jnp.where(blk > thr, 1.0, 0.0)
            eq = jnp.where(blk == thr, 1.0, 0.0)
            eq_incl = run_eq + jnp.dot(eq.astype(BF16), tri, preferred_element_type=F32)
            sel = gt + eq * jnp.where(eq_incl <= need, 1.0, 0.0)
            run_eq = run_eq + jnp.sum(eq, axis=1, keepdims=True)
            incl = jnp.dot(sel.astype(BF16), tri, preferred_element_type=F32)
            loc = tile_run + incl - sel
            loc_ref[t * per_tile + j] = jnp.where(sel > 0.5, loc, -1.0).astype(I32)
            tile_run = tile_run + jnp.sum(sel, axis=1, keepdims=True)
        b_ref[t] = run.astype(I32)
        c_ref[t] = tile_run.astype(I32)
        return run + tile_run, run_eq

    zero = jnp.zeros((N_EXPERTS, 1), F32)
    lax.fori_loop(0, ntile, tile_body, (zero, zero))


def _route(aff3, cap):
    nblk = aff3.shape[0]
    ntile = nblk * LANES // ROUTE_TILE
    full3 = lambda a: pl.BlockSpec(a, lambda: (0, 0, 0))
    return pl.pallas_call(
        functools.partial(_route_kernel, cap),
        in_specs=[full3(aff3.shape)],
        out_specs=[full3(aff3.shape), full3((ntile, N_EXPERTS, 1)),
                   full3((ntile, N_EXPERTS, 1))],
        out_shape=[jax.ShapeDtypeStruct(aff3.shape, I32),
                   jax.ShapeDtypeStruct((ntile, N_EXPERTS, 1), I32),
                   jax.ShapeDtypeStruct((ntile, N_EXPERTS, 1), I32)],
        compiler_params=pltpu.CompilerParams(vmem_limit_bytes=VMEM_LIMIT),
        name="route",
    )(aff3)


def _segment_copies(b_sm, c_sm, t, make_copy, action):
    off = 0
    nbits = ROUTE_TILE.bit_length()
    for e in range(N_EXPERTS):
        c = c_sm[t * N_EXPERTS + e]
        b = b_sm[t * N_EXPERTS + e]
        for kbit in range(nbits - 1, -1, -1):
            n = 1 << kbit
            r0 = (c >> (kbit + 1)) << (kbit + 1)

            @pl.when((c & n) != 0)
            def _(e=e, b=b, r0=r0, n=n, off=off):
                action(make_copy(e, b + r0, off + r0, n))
        off = off + c
    return off


def _tile_rows(c_sm, t):
    rows = 0
    for e in range(N_EXPERTS):
        rows = rows + c_sm[t * N_EXPERTS + e]
    return rows


def _dispatch_kernel(cap, b_sm, c_sm, x_ref, loc_ref, xe_hbm, stg, sem):
    t = pl.program_id(0)
    nt = pl.num_programs(0)
    slot = t % 2

    offs = []
    off = 0
    for e in range(N_EXPERTS):
        offs.append(off)
        off = off + c_sm[t * N_EXPERTS + e]
    offs.append(off)
    nchunk = (off + SLOT_CHUNK - 1) // SLOT_CHUNK
    sub = lax.broadcasted_iota(I32, (SLOT_CHUNK, 1), 0)
    lane = lax.broadcasted_iota(I32, (1, LANES), 1)
    seg_lo = jnp.zeros((1, LANES), I32)
    seg_hi = jnp.zeros((1, LANES), I32)
    for e in range(N_EXPERTS):
        seg_lo = jnp.where(lane == e, offs[e], seg_lo)
        seg_hi = jnp.where(lane == e, offs[e + 1], seg_hi)
    seg_lo_f = seg_lo.astype(F32)
    locf = jnp.concatenate(
        [loc_ref[j] for j in range(ROUTE_TILE // LANES)], axis=1).astype(F32)
    locb = jnp.concatenate(
        [locf, jnp.zeros((LANES - N_EXPERTS, ROUTE_TILE), F32)], axis=0).astype(BF16)

    def chunk_body(m, carry):
        base = m * SLOT_CHUNK
        srow = sub + base
        inseg = jnp.where(srow >= seg_lo, jnp.where(srow < seg_hi, 1.0, 0.0), 0.0)
        rank = srow.astype(F32) - jnp.sum(inseg * seg_lo_f, axis=1, keepdims=True)
        want = jnp.dot(inseg.astype(BF16), locb, preferred_element_type=F32)
        p = jnp.where(want == rank, 1.0, 0.0)
        res = jnp.dot(p.astype(BF16), x_ref[...], preferred_element_type=F32)
        row0 = pl.multiple_of(base * SUBLANES, SUBLANES)
        for s in range(SUBLANES):
            stg[slot, pl.ds(row0 + s, SLOT_CHUNK, stride=SUBLANES), :] = (
                res[:, s * LANES:(s + 1) * LANES])
        return carry

    lax.fori_loop(0, nchunk, chunk_body, 0)

    def copy_for(slot_):
        def make(e, dst_row, src_row, n):
            return pltpu.make_async_copy(
                stg.at[slot_, pl.ds(pl.multiple_of(src_row * SUBLANES, SUBLANES), n * SUBLANES)],
                xe_hbm.at[pl.ds(pl.multiple_of((e * cap + dst_row) * SUBLANES, SUBLANES),
                                n * SUBLANES)],
                sem.at[slot_])
        return make

    _segment_copies(b_sm, c_sm, t, copy_for(slot), lambda cp: cp.start())

    def wait_tile(t_, slot_):
        rows = _tile_rows(c_sm, t_)

        @pl.when(rows > 0)
        def _():
            n = rows * SUBLANES
            pltpu.make_async_copy(stg.at[slot_, pl.ds(0, n)], xe_hbm.at[pl.ds(0, n)],
                                  sem.at[slot_]).wait()

    @pl.when(t > 0)
    def _():
        wait_tile(t - 1, 1 - slot)

    @pl.when(t == nt - 1)
    def _():
        wait_tile(t, slot)


def _dispatch(b_flat, c_flat, x2b, loc3, cap):
    S, D = x2b.shape
    nt = S // ROUTE_TILE
    per_tile = ROUTE_TILE // LANES
    max_rows = N_EXPERTS * ROUTE_TILE
    return pl.pallas_call(
        functools.partial(_dispatch_kernel, cap),
        grid_spec=pltpu.PrefetchScalarGridSpec(
            num_scalar_prefetch=2,
            grid=(nt,),
            in_specs=[pl.BlockSpec((ROUTE_TILE, D), lambda t, b, c: (t, 0)),
                      pl.BlockSpec((per_tile, N_EXPERTS, LANES), lambda t, b, c: (t, 0, 0))],
            out_specs=pl.BlockSpec(memory_space=pl.ANY),
            scratch_shapes=[pltpu.VMEM((2, max_rows * SUBLANES, LANES), F32),
                            pltpu.SemaphoreType.DMA((2,))]),
        out_shape=jax.ShapeDtypeStruct((N_EXPERTS * cap * SUBLANES, LANES), F32),
        compiler_params=pltpu.CompilerParams(
            dimension_semantics=("arbitrary",), vmem_limit_bytes=VMEM_LIMIT),
        name="dispatch",
    )(b_flat, c_flat, x2b, loc3)


def _ffn_kernel(xe_ref, wg_ref, wu_ref, wd_ref, wr_ref, o_ref, xb, acc, gate):
    e = pl.program_id(0)
    f = pl.program_id(2)
    tm = xb.shape[0]

    @pl.when(f == 0)
    def _():
        for s in range(SUBLANES):
            xb[:, s * LANES:(s + 1) * LANES] = (
                xe_ref[pl.ds(s, tm, stride=SUBLANES), :].astype(BF16))
        lg = jnp.dot(xb[...], wr_ref[...], preferred_element_type=F32)
        lane = lax.broadcasted_iota(I32, (1, LANES), 1)
        lg = jnp.where(lane < N_EXPERTS, lg, NEG)
        p = jnp.exp(lg - jnp.max(lg, axis=-1, keepdims=True))
        mine = jnp.sum(jnp.where(lane == e, p, 0.0), axis=-1, keepdims=True)
        gate[...] = mine / jnp.sum(p, axis=-1, keepdims=True)
        acc[...] = jnp.zeros_like(acc)

    x = xb[...]
    g = jnp.dot(x, wg_ref[...].astype(BF16), preferred_element_type=F32)
    u = jnp.dot(x, wu_ref[...].astype(BF16), preferred_element_type=F32)
    h = (g * jax.nn.sigmoid(g) * u).astype(BF16)
    acc[...] += jnp.dot(h, wd_ref[...].astype(BF16), preferred_element_type=F32)

    @pl.when(f == pl.num_programs(2) - 1)
    def _():
        y = acc[...] * gate[...]
        for s in range(SUBLANES):
            o_ref[pl.ds(s, tm, stride=SUBLANES), :] = y[:, s * LANES:(s + 1) * LANES]


def _ffn(xe2d, w_gate, w_up, w_down, wr_pad, cap):
    E, D, FF = w_gate.shape
    tm, tf = FFN_TM, FFN_TF
    mt = cap // tm
    rows = lambda e, m, f: (e * mt + m, 0)
    return pl.pallas_call(
        _ffn_kernel,
        grid=(E, mt, FF // tf),
        in_specs=[pl.BlockSpec((tm * SUBLANES, LANES), rows),
                  pl.BlockSpec((None, D, tf), lambda e, m, f: (e, 0, f)),
                  pl.BlockSpec((None, D, tf), lambda e, m, f: (e, 0, f)),
                  pl.BlockSpec((None, tf, D), lambda e, m, f: (e, f, 0)),
                  pl.BlockSpec(wr_pad.shape, lambda e, m, f: (0, 0))],
        out_specs=pl.BlockSpec((tm * SUBLANES, LANES), rows),
        out_shape=jax.ShapeDtypeStruct(xe2d.shape, F32),
        scratch_shapes=[pltpu.VMEM((tm, D), BF16), pltpu.VMEM((tm, D), F32),
                        pltpu.VMEM((tm, 1), F32)],
        compiler_params=pltpu.CompilerParams(
            dimension_semantics=("arbitrary", "arbitrary", "arbitrary"),
            vmem_limit_bytes=VMEM_LIMIT),
        name="ffn",
    )(xe2d, w_gate, w_up, w_down, wr_pad)


def _combine_kernel(cap, alpha, b_sm, c_sm, x_ref, loc_ref, g_ref, be_ref, ye_hbm,
                    o_ref, ybuf, sem):
    t = pl.program_id(0)
    nt = pl.num_programs(0)
    slot = t % 2

    def copy_for(slot_):
        def make(e, src_row, dst_row, n):
            return pltpu.make_async_copy(
                ye_hbm.at[pl.ds(pl.multiple_of((e * cap + src_row) * SUBLANES, SUBLANES),
                                n * SUBLANES)],
                ybuf.at[slot_, pl.ds(pl.multiple_of(dst_row * SUBLANES, SUBLANES), n * SUBLANES)],
                sem.at[slot_])
        return make

    @pl.when(t == 0)
    def _():
        ybuf[...] = jnp.zeros_like(ybuf)
        _segment_copies(b_sm, c_sm, t, copy_for(slot), lambda cp: cp.start())

    @pl.when(t < nt - 1)
    def _():
        _segment_copies(b_sm, c_sm, t + 1, copy_for(1 - slot), lambda cp: cp.start())

    rows_in = _tile_rows(c_sm, t)

    @pl.when(rows_in > 0)
    def _():
        n = rows_in * SUBLANES
        pltpu.make_async_copy(ye_hbm.at[pl.ds(0, n)], ybuf.at[slot, pl.ds(0, n)],
                              sem.at[slot]).wait()

    offs = []
    off = 0
    for e in range(N_EXPERTS):
        offs.append(off)
        off = off + c_sm[t * N_EXPERTS + e]
    offs.append(off)
    total = off
    nchunk = (total + SLOT_CHUNK - 1) // SLOT_CHUNK
    lane = lax.broadcasted_iota(I32, (1, SLOT_CHUNK), 1)
    sub = lax.broadcasted_iota(I32, (SLOT_CHUNK, 1), 0)
    eid = lax.broadcasted_iota(I32, (N_EXPERTS, 1), 0)
    seg_lo = jnp.zeros((N_EXPERTS, 1), I32)
    seg_hi = jnp.zeros((N_EXPERTS, 1), I32)
    for e in range(N_EXPERTS):
        seg_lo = jnp.where(eid == e, offs[e], seg_lo)
        seg_hi = jnp.where(eid == e, offs[e + 1], seg_hi)
    seg_lo_f = seg_lo.astype(F32)
    locb = loc_ref[...].astype(F32).astype(BF16)

    def chunk_body(m, acc):
        base = m * SLOT_CHUNK
        srow = lane + base
        inseg = jnp.where(srow >= seg_lo, jnp.where(srow < seg_hi, 1.0, 0.0), 0.0)
        rank = srow.astype(F32) - jnp.sum(inseg * seg_lo_f, axis=0, keepdims=True)
        want = jnp.dot(locb, inseg.astype(BF16), preferred_element_type=F32)
        ptb = jnp.where(want == rank, 1.0, 0.0).astype(BF16)
        row0 = pl.multiple_of(base * SUBLANES, SUBLANES)
        yc = jnp.concatenate(
            [ybuf[slot, pl.ds(row0 + s, SLOT_CHUNK, stride=SUBLANES), :]
             for s in range(SUBLANES)], axis=1)
        yc = jnp.where(sub < total - base, yc, 0.0)
        hi = yc.astype(BF16)
        lo = (yc - hi.astype(F32)).astype(BF16)
        return (acc + jnp.dot(ptb, hi, preferred_element_type=F32)
                + jnp.dot(ptb, lo, preferred_element_type=F32))

    acc = lax.fori_loop(0, nchunk, chunk_body, alpha * x_ref[...])
    o_ref[...] = _layer_norm(acc, g_ref[...], be_ref[...])


def _combine(b_flat, c_flat, x2, loc_t, g, b, ye2d, cap, alpha):
    S, D = x2.shape
    nt = S // ROUTE_TILE
    max_rows = N_EXPERTS * ROUTE_TILE
    cur = lambda t, b_, c_: (t, 0)
    const2 = lambda t, b_, c_: (0, 0)
    return pl.pallas_call(
        functools.partial(_combine_kernel, cap, alpha),
        grid_spec=pltpu.PrefetchScalarGridSpec(
            num_scalar_prefetch=2,
            grid=(nt,),
            in_specs=[pl.BlockSpec((ROUTE_TILE, D), cur),
                      pl.BlockSpec((ROUTE_TILE, N_EXPERTS), cur),
                      pl.BlockSpec((1, D), const2),
                      pl.BlockSpec((1, D), const2),
                      pl.BlockSpec(memory_space=pl.ANY)],
            out_specs=pl.BlockSpec((ROUTE_TILE, D), cur),
            scratch_shapes=[pltpu.VMEM((2, max_rows * SUBLANES, LANES), F32),
                            pltpu.SemaphoreType.DMA((2,))]),
        out_shape=jax.ShapeDtypeStruct((S, D), F32),
        compiler_params=pltpu.CompilerParams(
            dimension_semantics=("arbitrary",), vmem_limit_bytes=VMEM_LIMIT),
        name="combine",
    )(b_flat, c_flat, x2, loc_t, g, b, ye2d)


def _layer(x, mem, w_in, rpb, conv_w, w_mix_out, ln1_g, ln1_b, wq, wk, wv, wo,
           ln2_g, ln2_b, w_router, w_gate, w_up, w_down, ln3_g, ln3_b, alpha):
    S, D = x.shape
    cap = CAPACITY_FACTOR * S // N_EXPERTS
    assert S % TOK_BLOCK == 0 and S // GRID_W >= 2 * ROW_BLOCK
    assert cap % FFN_TM == 0 and cap <= S
    row = lambda a: a.reshape(1, -1)

    q, k, v, bg, u = _proj(x, w_in.astype(BF16))
    x1 = _mixer(x, q, k, v, bg, u, w_mix_out.astype(BF16), rpb,
                conv_w, row(ln1_g), row(ln1_b), alpha)

    wr_t = w_router.T
    wr_hi = wr_t.astype(BF16)
    wr_lo = (wr_t - wr_hi.astype(F32)).astype(BF16)
    x2, x2b, aff3 = _xattn(x1, mem, wq.astype(BF16), wk.astype(BF16), wv.astype(BF16),
                           wo.astype(BF16), row(ln2_g), row(ln2_b), wr_hi, wr_lo, alpha)

    loc3, b3, c3 = _route(aff3, cap)
    b_flat = b3.reshape(-1)
    c_flat = c3.reshape(-1)
    loc_t = loc3.transpose(0, 2, 1).reshape(S, N_EXPERTS)

    xe2d = _dispatch(b_flat, c_flat, x2b, loc3, cap)
    wr_pad = jnp.pad(w_router.astype(BF16), ((0, 0), (0, LANES - N_EXPERTS)))
    ye2d = _ffn(xe2d, w_gate, w_up, w_down, wr_pad, cap)
    return _combine(b_flat, c_flat, x2, loc_t, row(ln3_g), row(ln3_b), ye2d, cap, alpha)


def kernel(x, mem, w_in, na_rpb, conv_w, w_mix_out, ln1_g, ln1_b, w_mem_q, w_mem_k,
           w_mem_v, w_mem_out, ln2_g, ln2_b, w_router, w_exp_gate, w_exp_up,
           w_exp_down, ln3_g, ln3_b):
    depth = w_in.shape[0]
    alpha = (2 * depth) ** 0.25
    outs = []
    for bi in range(x.shape[0]):
        xb = x[bi]
        for l in range(depth):
            xb = _layer(xb, mem[bi], w_in[l], na_rpb[l], conv_w[l], w_mix_out[l],
                        ln1_g[l], ln1_b[l], w_mem_q[l], w_mem_k[l], w_mem_v[l],
                        w_mem_out[l], ln2_g[l], ln2_b[l], w_router[l], w_exp_gate[l],
                        w_exp_up[l], w_exp_down[l], ln3_g[l], ln3_b[l], alpha)
        outs.append(xb)
    return jnp.stack(outs)
```

```python
import functools

import jax
import jax.numpy as jnp
from jax import lax
from jax.experimental import pallas as pl
from jax.experimental.pallas import tpu as pltpu

F32 = jnp.float32
BF16 = jnp.bfloat16
I32 = jnp.int32

GRID_W = 64
HEAD_DIM = 64
NA_HEADS = 8
NA_WIDTH = NA_HEADS * HEAD_DIM
WIN_ROWS = 8
WIN_COLS = 16
MEM_HEADS = 4
MEM_HEAD_DIM = 128
N_EXPERTS = 16
CAPACITY_FACTOR = 2
LN_EPS = 1e-5
NEG = -1e30

LANES = 128
SUBLANES = 8
VMEM_LIMIT = 56 * 1024 * 1024

ROW_BLOCK = 8
TOK_BLOCK = ROW_BLOCK * GRID_W
ROUTE_TILE = 256
SLOT_CHUNK = 256
SM_ROWS = 16
LN_ROWS = 16
LN_UNROLL = 8
FFN_TM = 1024
FFN_TF = 512


def _layer_norm(z, g, b):
    mu = jnp.mean(z, axis=-1, keepdims=True)
    zc = z - mu
    var = jnp.mean(zc * zc, axis=-1, keepdims=True)
    return zc * lax.rsqrt(var + LN_EPS) * g + b


def _for_row_blocks(nrows, body):
    def step(j, carry):
        body(pl.ds(pl.multiple_of(j * LN_ROWS, LN_ROWS), LN_ROWS))
        return carry

    lax.fori_loop(0, nrows // LN_ROWS, step, 0, unroll=LN_UNROLL)


def _dot_nt(a, b):
    return lax.dot_general(a, b, (((1,), (1,)), ((), ())),
                           preferred_element_type=F32)


def _proj_kernel(x_ref, w_ref, q_ref, k_ref, v_ref, b_ref, u_ref):
    xb = x_ref[...].astype(BF16)
    nw = NA_WIDTH

    def mm(c0):
        return jnp.dot(xb, w_ref[:, c0:c0 + nw], preferred_element_type=F32)

    q_ref[...] = (mm(0) * (HEAD_DIM ** -0.5)).astype(BF16)
    k_ref[...] = mm(nw).astype(BF16)
    v_ref[...] = mm(2 * nw).astype(BF16)
    b_ref[...] = mm(3 * nw)
    u_ref[...] = mm(4 * nw) * mm(5 * nw)


def _proj(x, w_in_b):
    S, D = x.shape
    nw = NA_WIDTH
    tm = TOK_BLOCK
    blk = lambda i: (i, 0)
    return pl.pallas_call(
        _proj_kernel,
        grid=(S // tm,),
        in_specs=[pl.BlockSpec((tm, D), blk),
                  pl.BlockSpec((D, 6 * nw), lambda i: (0, 0))],
        out_specs=[pl.BlockSpec((tm, nw), blk)] * 5,
        out_shape=[jax.ShapeDtypeStruct((S, nw), BF16)] * 3
        + [jax.ShapeDtypeStruct((S, nw), F32)] * 2,
        compiler_params=pltpu.CompilerParams(
            dimension_semantics=("arbitrary",), vmem_limit_bytes=VMEM_LIMIT),
        name="proj",
    )(x, w_in_b)


def _build_na_bias(rpb_ref, bias_ref):
    shape = (GRID_W, LANES)
    qc = lax.broadcasted_iota(I32, shape, 0)
    ln = lax.broadcasted_iota(I32, shape, 1)
    kc = ln % GRID_W
    c_start = jnp.clip(qc - WIN_COLS // 2, 0, GRID_W - WIN_COLS)
    left = ln < GRID_W
    base = LANES - (WIN_COLS - 1)

    def canvas(h, d):
        ra = jnp.broadcast_to(rpb_ref[h, d:d + 1, :], shape)
        rb = jnp.broadcast_to(rpb_ref[h, d + 1:d + 2, :], shape)
        ta = pltpu.roll(pltpu.roll(ra, base, 1), 0, 1, stride=1, stride_axis=0)
        tb = pltpu.roll(pltpu.roll(rb, (base + GRID_W) % LANES, 1), 0, 1,
                        stride=1, stride_axis=0)
        t = jnp.where(left, ta, tb)
        return jnp.where(kc >= c_start, jnp.where(kc < c_start + WIN_COLS, t, NEG), NEG)

    for par in range(2):
        for hp in range(NA_HEADS // 2):
            for m in range(WIN_ROWS):
                d = 2 * m + par
                bias_ref[par, hp, m, 0:GRID_W] = canvas(2 * hp, d)
                bias_ref[par, hp, m, GRID_W:2 * GRID_W] = canvas(2 * hp + 1, d)


def _mixer_kernel(rows, alpha,
                  q_ref, kp_ref, kc_ref, kn_ref, vp_ref, vc_ref, vn_ref,
                  b_ref, u_ref, up_ref, un_ref, x_ref, wout_ref, rpb_ref,
                  cw_ref, g_ref, be_ref, o_ref, kbuf, vbuf, mix, bias_ref,
                  s_scr0, s_scr1, p_scr0, p_scr1, z_scr):
    i = pl.program_id(0)
    nb = pl.num_programs(0)
    tb = TOK_BLOCK
    win = WIN_ROWS * GRID_W

    @pl.when(i == 0)
    def _():
        _build_na_bias(rpb_ref, bias_ref)

    kbuf[0:tb] = kp_ref[...]
    kbuf[tb:2 * tb] = kc_ref[...]
    kbuf[2 * tb:3 * tb] = kn_ref[...]
    vbuf[0:tb] = vp_ref[...]
    vbuf[tb:2 * tb] = vc_ref[...]
    vbuf[2 * tb:3 * tb] = vn_ref[...]

    lane = lax.broadcasted_iota(I32, (1, LANES), 1)
    first_head = lane < HEAD_DIM
    m_lo = jnp.where(first_head, 1.0, 0.0).astype(BF16)
    m_hi = jnp.where(first_head, 0.0, 1.0).astype(BF16)

    def window(r):
        grow = i * ROW_BLOCK + r
        start = jnp.clip(grow - WIN_ROWS // 2, 0, rows - WIN_ROWS)
        w0 = pl.multiple_of((start - (i - 1) * ROW_BLOCK) * GRID_W, GRID_W)
        d0 = start - grow + WIN_ROWS
        return w0, d0 % 2, d0 // 2

    def scores(r, s_out):
        w0, _, _ = window(r)
        q0 = r * GRID_W if isinstance(r, int) else pl.multiple_of(r * GRID_W, GRID_W)
        for hp in range(NA_HEADS // 2):
            cs = slice(hp * LANES, (hp + 1) * LANES)
            qp = q_ref[pl.ds(q0, GRID_W), cs]
            lhs = jnp.concatenate([qp * m_lo, qp * m_hi], axis=0)
            kw = kbuf[pl.ds(w0, win), cs]
            s_out[hp] = _dot_nt(lhs, kw)

    def probs(r, s_in, p_out):
        _, par, m0 = window(r)
        for hp in range(NA_HEADS // 2):
            for rb in range(2 * GRID_W // SM_ROWS):
                rs = slice(rb * SM_ROWS, (rb + 1) * SM_ROWS)
                bias = jnp.concatenate(
                    [bias_ref[par, hp, m0 + j, rs, :] for j in range(WIN_ROWS // 2)], axis=1)
                s = s_in[hp, rs, :] + bias
                p = jnp.exp(s - jnp.max(s, axis=-1, keepdims=True))
                p = p * (1.0 / jnp.sum(p, axis=-1, keepdims=True))
                p_out[hp, rs, :] = p.astype(BF16)

    def values(r, p_in):
        w0, _, _ = window(r)
        q0 = r * GRID_W if isinstance(r, int) else pl.multiple_of(r * GRID_W, GRID_W)
        for hp in range(NA_HEADS // 2):
            cs = slice(hp * LANES, (hp + 1) * LANES)
            vw = vbuf[pl.ds(w0, win), cs]
            o2 = jnp.dot(p_in[hp], vw, preferred_element_type=F32)
            y = jnp.where(first_head, o2[0:GRID_W], o2[GRID_W:2 * GRID_W])
            mix[pl.ds(q0, GRID_W), cs] = y.astype(BF16)

    s_buf = (s_scr0, s_scr1)
    p_buf = (p_scr0, p_scr1)

    def steady(k, carry):
        for half in range(2):
            r = 1 + 2 * k + half
            values(r - 1, p_buf[half])
            probs(r, s_buf[1 - half], p_buf[1 - half])
            scores(r + 1, s_buf[half])
        return carry

    scores(0, s_buf[0])
    probs(0, s_buf[0], p_buf[0])
    scores(1, s_buf[1])
    lax.fori_loop(0, (ROW_BLOCK - 2) // 2, steady, 0)
    values(ROW_BLOCK - 2, p_buf[0])
    probs(ROW_BLOCK - 1, s_buf[1], p_buf[1])
    values(ROW_BLOCK - 1, p_buf[1])

    u = u_ref[...]
    prev_row = up_ref[SUBLANES - 1:SUBLANES, :] * jnp.where(i > 0, 1.0, 0.0)
    next_row = un_ref[0:1, :] * jnp.where(i < nb - 1, 1.0, 0.0)
    rid = lax.broadcasted_iota(I32, (tb, 1), 0)
    um1 = jnp.where(rid == 0, prev_row, pltpu.roll(u, 1, axis=0))
    up1 = jnp.where(rid == tb - 1, next_row, pltpu.roll(u, tb - 1, axis=0))
    cw = cw_ref[...]
    yc = b_ref[...] * (cw[0:1] * um1 + cw[1:2] * u + cw[2:3] * up1)
    mix[:, NA_WIDTH:] = yc.astype(BF16)

    z_scr[...] = alpha * x_ref[...] + jnp.dot(mix[...], wout_ref[...],
                                              preferred_element_type=F32)

    def ln_rows(rs):
        o_ref[rs, :] = _layer_norm(z_scr[rs, :], g_ref[...], be_ref[...])

    _for_row_blocks(tb, ln_rows)


def _mixer(x, q, k, v, bg, u, w_out_b, rpb, conv_w, g, b, alpha):
    S, D = x.shape
    rpb_pad = jnp.pad(rpb.astype(F32), ((0, 0), (1, 1), (0, LANES - rpb.shape[2])))
    nw = NA_WIDTH
    tb = TOK_BLOCK
    nb = S // tb
    rows = S // GRID_W
    cur = lambda i: (i, 0)
    prv = lambda i: (jnp.maximum(i - 1, 0), 0)
    nxt = lambda i: (jnp.minimum(i + 1, nb - 1), 0)
    halo = tb // SUBLANES
    hprev = lambda i: (jnp.maximum(i * halo - 1, 0), 0)
    hnext = lambda i: (jnp.minimum((i + 1) * halo, S // SUBLANES - 1), 0)
    const2 = lambda i: (0, 0)
    kv = lambda im: pl.BlockSpec((tb, nw), im)
    return pl.pallas_call(
        functools.partial(_mixer_kernel, rows, alpha),
        grid=(nb,),
        in_specs=[kv(cur), kv(prv), kv(cur), kv(nxt), kv(prv), kv(cur), kv(nxt),
                  kv(cur), kv(cur),
                  pl.BlockSpec((SUBLANES, nw), hprev),
                  pl.BlockSpec((SUBLANES, nw), hnext),
                  pl.BlockSpec((tb, D), cur),
                  pl.BlockSpec((D, D), const2),
                  pl.BlockSpec(rpb_pad.shape, lambda i: (0, 0, 0)),
                  pl.BlockSpec(conv_w.shape, const2),
                  pl.BlockSpec((1, D), const2),
                  pl.BlockSpec((1, D), const2)],
        out_specs=pl.BlockSpec((tb, D), cur),
        out_shape=jax.ShapeDtypeStruct((S, D), F32),
        scratch_shapes=[pltpu.VMEM((3 * tb, nw), BF16),
                        pltpu.VMEM((3 * tb, nw), BF16),
                        pltpu.VMEM((tb, D), BF16),
                        pltpu.VMEM((2, NA_HEADS // 2, WIN_ROWS, 2 * GRID_W, LANES), F32),
                        pltpu.VMEM((NA_HEADS // 2, 2 * GRID_W, WIN_ROWS * GRID_W), F32),
                        pltpu.VMEM((NA_HEADS // 2, 2 * GRID_W, WIN_ROWS * GRID_W), F32),
                        pltpu.VMEM((NA_HEADS // 2, 2 * GRID_W, WIN_ROWS * GRID_W), BF16),
                        pltpu.VMEM((NA_HEADS // 2, 2 * GRID_W, WIN_ROWS * GRID_W), BF16),
                        pltpu.VMEM((tb, D), F32)],
        compiler_params=pltpu.CompilerParams(
            dimension_semantics=("arbitrary",), vmem_limit_bytes=VMEM_LIMIT),
        name="mixer",
    )(q, k, k, k, v, v, v, bg, u, u, u, x, w_out_b, rpb_pad, conv_w, g, b)


def _xattn_kernel(alpha, x_ref, mem_ref, wq_ref, wk_ref, wv_ref, wo_ref,
                  g_ref, be_ref, wrh_ref, wrl_ref,
                  x2_ref, x2b_ref, aff_ref, kmem, vmem, lo_scr, z_scr):
    i = pl.program_id(0)

    @pl.when(i == 0)
    def _():
        mb = mem_ref[...].astype(BF16)
        kmem[...] = jnp.dot(mb, wk_ref[...], preferred_element_type=F32).astype(BF16)
        vmem[...] = jnp.dot(mb, wv_ref[...], preferred_element_type=F32).astype(BF16)

    x1 = x_ref[...]
    qb = jnp.dot(x1.astype(BF16), wq_ref[...], preferred_element_type=F32).astype(BF16)
    outs = []
    for h in range(MEM_HEADS):
        cs = slice(h * MEM_HEAD_DIM, (h + 1) * MEM_HEAD_DIM)
        s = _dot_nt(qb[:, cs], kmem[:, cs]) * (MEM_HEAD_DIM ** -0.5)
        mx = jnp.max(s, axis=-1, keepdims=True)
        p = jnp.exp(s - mx)
        p = p * (1.0 / jnp.sum(p, axis=-1, keepdims=True))
        outs.append(jnp.dot(p.astype(BF16), vmem[:, cs], preferred_element_type=F32))
    o = jnp.concatenate(outs, axis=1).astype(BF16)
    z_scr[...] = alpha * x1 + jnp.dot(o, wo_ref[...], preferred_element_type=F32)

    def ln_rows(rs):
        x2 = _layer_norm(z_scr[rs, :], g_ref[...], be_ref[...])
        x2_ref[rs, :] = x2
        hi = x2.astype(BF16)
        x2b_ref[rs, :] = hi
        lo_scr[rs, :] = (x2 - hi.astype(F32)).astype(BF16)

    _for_row_blocks(x2_ref.shape[0], ln_rows)
    wrh = wrh_ref[...]
    r1 = _dot_nt(jnp.concatenate([wrh, wrl_ref[...]], axis=0), x2b_ref[...])
    lg = r1[0:N_EXPERTS] + r1[N_EXPERTS:2 * N_EXPERTS] + _dot_nt(wrh, lo_scr[...])
    mx = jnp.max(lg, axis=0, keepdims=True)
    p = jnp.exp(lg - mx)
    aff = p / jnp.sum(p, axis=0, keepdims=True)
    for j in range(aff_ref.shape[0]):
        aff_ref[j] = aff[:, j * LANES:(j + 1) * LANES]


def _xattn(x1, mem, wq_b, wk_b, wv_b, wo_b, g, b, wr_hi, wr_lo, alpha):
    S, D = x1.shape
    M = mem.shape[0]
    mw = MEM_HEADS * MEM_HEAD_DIM
    tm = TOK_BLOCK
    cur = lambda i: (i, 0)
    const2 = lambda i: (0, 0)
    full = lambda a: pl.BlockSpec(a.shape, const2)
    return pl.pallas_call(
        functools.partial(_xattn_kernel, alpha),
        grid=(S // tm,),
        in_specs=[pl.BlockSpec((tm, D), cur), full(mem), full(wq_b), full(wk_b),
                  full(wv_b), full(wo_b), full(g), full(b), full(wr_hi), full(wr_lo)],
        out_specs=[pl.BlockSpec((tm, D), cur), pl.BlockSpec((tm, D), cur),
                   pl.BlockSpec((tm // LANES, N_EXPERTS, LANES), lambda i: (i, 0, 0))],
        out_shape=[jax.ShapeDtypeStruct((S, D), F32),
                   jax.ShapeDtypeStruct((S, D), BF16),
                   jax.ShapeDtypeStruct((S // LANES, N_EXPERTS, LANES), F32)],
        scratch_shapes=[pltpu.VMEM((M, mw), BF16), pltpu.VMEM((M, mw), BF16),
                        pltpu.VMEM((tm, D), BF16), pltpu.VMEM((tm, D), F32)],
        compiler_params=pltpu.CompilerParams(
            dimension_semantics=("arbitrary",), vmem_limit_bytes=VMEM_LIMIT),
        name="xattn",
    )(x1, mem, wq_b, wk_b, wv_b, wo_b, g, b, wr_hi, wr_lo)


def _route_kernel(cap, aff_ref, loc_ref, b_ref, c_ref):
    nblk = aff_ref.shape[0]
    per_tile = ROUTE_TILE // LANES
    ntile = nblk // per_tile

    def count_ge(t):
        ge = jnp.where(aff_ref[...] >= t[None], 1.0, 0.0)
        return jnp.sum(jnp.sum(ge, axis=0), axis=1, keepdims=True)

    def coarse(it, bits):
        cand = bits | jnp.left_shift(jnp.int32(1), 30 - it)
        ok = count_ge(pltpu.bitcast(cand, F32)) >= cap
        return jnp.where(ok, cand, bits)

    def fine(it, lohi):
        lo, hi = lohi
        mid = lo + (hi - lo) * 0.5
        ok = count_ge(mid) >= cap
        return jnp.where(ok, mid, lo), jnp.where(ok, hi, mid)

    bits = lax.fori_loop(0, 31, coarse, jnp.zeros((N_EXPERTS, 1), I32))
    thr, _ = lax.fori_loop(
        0, 30, fine, (pltpu.bitcast(bits, F32), pltpu.bitcast(bits + 1, F32)))
    gt_all = jnp.where(aff_ref[...] > thr[None], 1.0, 0.0)
    n_gt = jnp.sum(jnp.sum(gt_all, axis=0), axis=1, keepdims=True)
    need = cap - n_gt

    ii = lax.broadcasted_iota(I32, (LANES, LANES), 0)
    jj = lax.broadcasted_iota(I32, (LANES, LANES), 1)
    tri = jnp.where(ii <= jj, 1.0, 0.0).astype(BF16)

    def tile_body(t, carry):
        run, run_eq = carry
        tile_run = jnp.zeros((N_EXPERTS, 1), F32)
        for j in range(per_tile):
            blk = aff_ref[t * per_tile + j]
            gt = jnp.where(blk > thr, 1.0, 0.0)
            eq = jnp.where(blk == thr, 1.0, 0.0)
            eq_incl = run_eq + jnp.dot(eq.astype(BF16), tri, preferred_element_type=F32)
            sel = gt + eq * jnp.where(eq_incl <= need, 1.0, 0.0)
            run_eq = run_eq + jnp.sum(eq, axis=1, keepdims=True)
            incl = jnp.dot(sel.astype(BF16), tri, preferred_element_type=F32)
            loc = tile_run + incl - sel
            loc_ref[t * per_tile + j] = jnp.where(sel > 0.5, loc, -1.0).astype(I32)
            tile_run = tile_run + jnp.sum(sel, axis=1, keepdims=True)
        b_ref[t] = run.astype(I32)
        c_ref[t] = tile_run.astype(I32)
        return run + tile_run, run_eq

    zero = jnp.zeros((N_EXPERTS, 1), F32)
    lax.fori_loop(0, ntile, tile_body, (zero, zero))


def _route(aff3, cap):
    nblk = aff3.shape[0]
    ntile = nblk * LANES // ROUTE_TILE
    full3 = lambda a: pl.BlockSpec(a, lambda: (0, 0, 0))
    return pl.pallas_call(
        functools.partial(_route_kernel, cap),
        in_specs=[full3(aff3.shape)],
        out_specs=[full3(aff3.shape), full3((ntile, N_EXPERTS, 1)),
                   full3((ntile, N_EXPERTS, 1))],
        out_shape=[jax.ShapeDtypeStruct(aff3.shape, I32),
                   jax.ShapeDtypeStruct((ntile, N_EXPERTS, 1), I32),
                   jax.ShapeDtypeStruct((ntile, N_EXPERTS, 1), I32)],
        compiler_params=pltpu.CompilerParams(vmem_limit_bytes=VMEM_LIMIT),
        name="route",
    )(aff3)


def _segment_copies(b_sm, c_sm, t, make_copy, action):
    off = 0
    nbits = ROUTE_TILE.bit_length()
    for e in range(N_EXPERTS):
        c = c_sm[t * N_EXPERTS + e]
        b = b_sm[t * N_EXPERTS + e]
        for kbit in range(nbits - 1, -1, -1):
            n = 1 << kbit
            r0 = (c >> (kbit + 1)) << (kbit + 1)

            @pl.when((c & n) != 0)
            def _(e=e, b=b, r0=r0, n=n, off=off):
                action(make_copy(e, b + r0, off + r0, n))
        off = off + c
    return off


def _tile_rows(c_sm, t):
    rows = 0
    for e in range(N_EXPERTS):
        rows = rows + c_sm[t * N_EXPERTS + e]
    return rows


def _dispatch_kernel(cap, b_sm, c_sm, x_ref, loc_ref, xe_hbm, stg, sem):
    t = pl.program_id(0)
    nt = pl.num_programs(0)
    slot = t % 2

    offs = []
    off = 0
    for e in range(N_EXPERTS):
        offs.append(off)
        off = off + c_sm[t * N_EXPERTS + e]
    offs.append(off)
    nchunk = (off + SLOT_CHUNK - 1) // SLOT_CHUNK
    sub = lax.broadcasted_iota(I32, (SLOT_CHUNK, 1), 0)
    lane = lax.broadcasted_iota(I32, (1, LANES), 1)
    seg_lo = jnp.zeros((1, LANES), I32)
    seg_hi = jnp.zeros((1, LANES), I32)
    for e in range(N_EXPERTS):
        seg_lo = jnp.where(lane == e, offs[e], seg_lo)
        seg_hi = jnp.where(lane == e, offs[e + 1], seg_hi)
    seg_lo_f = seg_lo.astype(F32)
    locf = jnp.concatenate(
        [loc_ref[j] for j in range(ROUTE_TILE // LANES)], axis=1).astype(F32)
    locb = jnp.concatenate(
        [locf, jnp.zeros((LANES - N_EXPERTS, ROUTE_TILE), F32)], axis=0).astype(BF16)

    def chunk_body(m, carry):
        base = m * SLOT_CHUNK
        srow = sub + base
        inseg = jnp.where(srow >= seg_lo, jnp.where(srow < seg_hi, 1.0, 0.0), 0.0)
        rank = srow.astype(F32) - jnp.sum(inseg * seg_lo_f, axis=1, keepdims=True)
        want = jnp.dot(inseg.astype(BF16), locb, preferred_element_type=F32)
        p = jnp.where(want == rank, 1.0, 0.0)
        res = jnp.dot(p.astype(BF16), x_ref[...], preferred_element_type=F32)
        row0 = pl.multiple_of(base * SUBLANES, SUBLANES)
        for s in range(SUBLANES):
            stg[slot, pl.ds(row0 + s, SLOT_CHUNK, stride=SUBLANES), :] = (
                res[:, s * LANES:(s + 1) * LANES])
        return carry

    lax.fori_loop(0, nchunk, chunk_body, 0)

    def copy_for(slot_):
        def make(e, dst_row, src_row, n):
            return pltpu.make_async_copy(
                stg.at[slot_, pl.ds(pl.multiple_of(src_row * SUBLANES, SUBLANES), n * SUBLANES)],
                xe_hbm.at[pl.ds(pl.multiple_of((e * cap + dst_row) * SUBLANES, SUBLANES),
                                n * SUBLANES)],
                sem.at[slot_])
        return make

    _segment_copies(b_sm, c_sm, t, copy_for(slot), lambda cp: cp.start())

    def wait_tile(t_, slot_):
        rows = _tile_rows(c_sm, t_)

        @pl.when(rows > 0)
        def _():
            n = rows * SUBLANES
            pltpu.make_async_copy(stg.at[slot_, pl.ds(0, n)], xe_hbm.at[pl.ds(0, n)],
                                  sem.at[slot_]).wait()

    @pl.when(t > 0)
    def _():
        wait_tile(t - 1, 1 - slot)

    @pl.when(t == nt - 1)
    def _():
        wait_tile(t, slot)


def _dispatch(b_flat, c_flat, x2b, loc3, cap):
    S, D = x2b.shape
    nt = S // ROUTE_TILE
    per_tile = ROUTE_TILE // LANES
    max_rows = N_EXPERTS * ROUTE_TILE
    return pl.pallas_call(
        functools.partial(_dispatch_kernel, cap),
        grid_spec=pltpu.PrefetchScalarGridSpec(
            num_scalar_prefetch=2,
            grid=(nt,),
            in_specs=[pl.BlockSpec((ROUTE_TILE, D), lambda t, b, c: (t, 0)),
                      pl.BlockSpec((per_tile, N_EXPERTS, LANES), lambda t, b, c: (t, 0, 0))],
            out_specs=pl.BlockSpec(memory_space=pl.ANY),
            scratch_shapes=[pltpu.VMEM((2, max_rows * SUBLANES, LANES), F32),
                            pltpu.SemaphoreType.DMA((2,))]),
        out_shape=jax.ShapeDtypeStruct((N_EXPERTS * cap * SUBLANES, LANES), F32),
        compiler_params=pltpu.CompilerParams(
            dimension_semantics=("arbitrary",), vmem_limit_bytes=VMEM_LIMIT),
        name="dispatch",
    )(b_flat, c_flat, x2b, loc3)


def _ffn_kernel(xe_ref, wg_ref, wu_ref, wd_ref, wr_ref, o_ref, xb, acc, gate):
    e = pl.program_id(0)
    f = pl.program_id(2)
    tm = xb.shape[0]

    @pl.when(f == 0)
    def _():
        for s in range(SUBLANES):
            xb[:, s * LANES:(s + 1) * LANES] = (
                xe_ref[pl.ds(s, tm, stride=SUBLANES), :].astype(BF16))
        lg = jnp.dot(xb[...], wr_ref[...], preferred_element_type=F32)
        lane = lax.broadcasted_iota(I32, (1, LANES), 1)
        lg = jnp.where(lane < N_EXPERTS, lg, NEG)
        p = jnp.exp(lg - jnp.max(lg, axis=-1, keepdims=True))
        mine = jnp.sum(jnp.where(lane == e, p, 0.0), axis=-1, keepdims=True)
        gate[...] = mine / jnp.sum(p, axis=-1, keepdims=True)
        acc[...] = jnp.zeros_like(acc)

    x = xb[...]
    g = jnp.dot(x, wg_ref[...].astype(BF16), preferred_element_type=F32)
    u = jnp.dot(x, wu_ref[...].astype(BF16), preferred_element_type=F32)
    h = (g * jax.nn.sigmoid(g) * u).astype(BF16)
    acc[...] += jnp.dot(h, wd_ref[...].astype(BF16), preferred_element_type=F32)

    @pl.when(f == pl.num_programs(2) - 1)
    def _():
        y = acc[...] * gate[...]
        for s in range(SUBLANES):
            o_ref[pl.ds(s, tm, stride=SUBLANES), :] = y[:, s * LANES:(s + 1) * LANES]


def _ffn(xe2d, w_gate, w_up, w_down, wr_pad, cap):
    E, D, FF = w_gate.shape
    tm, tf = FFN_TM, FFN_TF
    mt = cap // tm
    rows = lambda e, m, f: (e * mt + m, 0)
    return pl.pallas_call(
        _ffn_kernel,
        grid=(E, mt, FF // tf),
        in_specs=[pl.BlockSpec((tm * SUBLANES, LANES), rows),
                  pl.BlockSpec((None, D, tf), lambda e, m, f: (e, 0, f)),
                  pl.BlockSpec((None, D, tf), lambda e, m, f: (e, 0, f)),
                  pl.BlockSpec((None, tf, D), lambda e, m, f: (e, f, 0)),
                  pl.BlockSpec(wr_pad.shape, lambda e, m, f: (0, 0))],
        out_specs=pl.BlockSpec((tm * SUBLANES, LANES), rows),
        out_shape=jax.ShapeDtypeStruct(xe2d.shape, F32),
        scratch_shapes=[pltpu.VMEM((tm, D), BF16), pltpu.VMEM((tm, D), F32),
                        pltpu.VMEM((tm, 1), F32)],
        compiler_params=pltpu.CompilerParams(
            dimension_semantics=("arbitrary", "arbitrary", "arbitrary"),
            vmem_limit_bytes=VMEM_LIMIT),
        name="ffn",
    )(xe2d, w_gate, w_up, w_down, wr_pad)


def _combine_kernel(cap, alpha, b_sm, c_sm, x_ref, loc_ref, g_ref, be_ref, ye_hbm,
                    o_ref, ybuf, sem, hl, z_scr):
    t = pl.program_id(0)
    nt = pl.num_programs(0)
    slot = t % 2

    def copy_for(slot_):
        def make(e, src_row, dst_row, n):
            return pltpu.make_async_copy(
                ye_hbm.at[pl.ds(pl.multiple_of((e * cap + src_row) * SUBLANES, SUBLANES),
                                n * SUBLANES)],
                ybuf.at[slot_, pl.ds(pl.multiple_of(dst_row * SUBLANES, SUBLANES), n * SUBLANES)],
                sem.at[slot_])
        return make

    @pl.when(t == 0)
    def _():
        ybuf[...] = jnp.zeros_like(ybuf)
        _segment_copies(b_sm, c_sm, t, copy_for(slot), lambda cp: cp.start())

    @pl.when(t < nt - 1)
    def _():
        _segment_copies(b_sm, c_sm, t + 1, copy_for(1 - slot), lambda cp: cp.start())

    rows_in = _tile_rows(c_sm, t)

    @pl.when(rows_in > 0)
    def _():
        n = rows_in * SUBLANES
        pltpu.make_async_copy(ye_hbm.at[pl.ds(0, n)], ybuf.at[slot, pl.ds(0, n)],
                              sem.at[slot]).wait()

    offs = []
    off = 0
    for e in range(N_EXPERTS):
        offs.append(off)
        off = off + c_sm[t * N_EXPERTS + e]
    offs.append(off)
    total = off
    nchunk = (total + SLOT_CHUNK - 1) // SLOT_CHUNK
    lane = lax.broadcasted_iota(I32, (1, SLOT_CHUNK), 1)
    sub = lax.broadcasted_iota(I32, (SLOT_CHUNK, 1), 0)
    eid = lax.broadcasted_iota(I32, (N_EXPERTS, 1), 0)
    seg_lo = jnp.zeros((N_EXPERTS, 1), I32)
    seg_hi = jnp.zeros((N_EXPERTS, 1), I32)
    for e in range(N_EXPERTS):
        seg_lo = jnp.where(eid == e, offs[e], seg_lo)
        seg_hi = jnp.where(eid == e, offs[e + 1], seg_hi)
    seg_lo_f = seg_lo.astype(F32)
    locb = loc_ref[...].astype(F32).astype(BF16)

    z_scr[...] = alpha * x_ref[...]

    def chunk_body(m, carry):
        base = m * SLOT_CHUNK
        srow = lane + base
        inseg = jnp.where(srow >= seg_lo, jnp.where(srow < seg_hi, 1.0, 0.0), 0.0)
        rank = srow.astype(F32) - jnp.sum(inseg * seg_lo_f, axis=0, keepdims=True)
        want = jnp.dot(locb, inseg.astype(BF16), preferred_element_type=F32)
        ptb = jnp.where(want == rank, 1.0, 0.0).astype(BF16)
        row0 = pl.multiple_of(base * SUBLANES, SUBLANES)
        yc = jnp.concatenate(
            [ybuf[slot, pl.ds(row0 + s, SLOT_CHUNK, stride=SUBLANES), :]
             for s in range(SUBLANES)], axis=1)
        yc = jnp.where(sub < total - base, yc, 0.0)
        hi = yc.astype(BF16)
        hl[0:SLOT_CHUNK] = hi
        hl[SLOT_CHUNK:2 * SLOT_CHUNK] = (yc - hi.astype(F32)).astype(BF16)
        z_scr[...] += jnp.dot(jnp.concatenate([ptb, ptb], axis=1), hl[...],
                              preferred_element_type=F32)
        return carry

    lax.fori_loop(0, nchunk, chunk_body, 0)

    def ln_rows(rs):
        o_ref[rs, :] = _layer_norm(z_scr[rs, :], g_ref[...], be_ref[...])

    _for_row_blocks(ROUTE_TILE, ln_rows)


def _combine(b_flat, c_flat, x2, loc_t, g, b, ye2d, cap, alpha):
    S, D = x2.shape
    nt = S // ROUTE_TILE
    max_rows = N_EXPERTS * ROUTE_TILE
    cur = lambda t, b_, c_: (t, 0)
    const2 = lambda t, b_, c_: (0, 0)
    return pl.pallas_call(
        functools.partial(_combine_kernel, cap, alpha),
        grid_spec=pltpu.PrefetchScalarGridSpec(
            num_scalar_prefetch=2,
            grid=(nt,),
            in_specs=[pl.BlockSpec((ROUTE_TILE, D), cur),
                      pl.BlockSpec((ROUTE_TILE, N_EXPERTS), cur),
                      pl.BlockSpec((1, D), const2),
                      pl.BlockSpec((1, D), const2),
                      pl.BlockSpec(memory_space=pl.ANY)],
            out_specs=pl.BlockSpec((ROUTE_TILE, D), cur),
            scratch_shapes=[pltpu.VMEM((2, max_rows * SUBLANES, LANES), F32),
                            pltpu.SemaphoreType.DMA((2,)),
                            pltpu.VMEM((2 * SLOT_CHUNK, D), BF16),
                            pltpu.VMEM((ROUTE_TILE, D), F32)]),
        out_shape=jax.ShapeDtypeStruct((S, D), F32),
        compiler_params=pltpu.CompilerParams(
            dimension_semantics=("arbitrary",), vmem_limit_bytes=VMEM_LIMIT),
        name="combine",
    )(b_flat, c_flat, x2, loc_t, g, b, ye2d)


def _layer(x, mem, w_in, rpb, conv_w, w_mix_out, ln1_g, ln1_b, wq, wk, wv, wo,
           ln2_g, ln2_b, w_router, w_gate, w_up, w_down, ln3_g, ln3_b, alpha):
    S, D = x.shape
    cap = CAPACITY_FACTOR * S // N_EXPERTS
    assert S % TOK_BLOCK == 0 and S // GRID_W >= 2 * ROW_BLOCK
    assert cap % FFN_TM == 0 and cap <= S
    row = lambda a: a.reshape(1, -1)

    q, k, v, bg, u = _proj(x, w_in.astype(BF16))
    x1 = _mixer(x, q, k, v, bg, u, w_mix_out.astype(BF16), rpb,
                conv_w, row(ln1_g), row(ln1_b), alpha)

    wr_t = w_router.T
    wr_hi = wr_t.astype(BF16)
    wr_lo = (wr_t - wr_hi.astype(F32)).astype(BF16)
    x2, x2b, aff3 = _xattn(x1, mem, wq.astype(BF16), wk.astype(BF16), wv.astype(BF16),
                           wo.astype(BF16), row(ln2_g), row(ln2_b), wr_hi, wr_lo, alpha)

    loc3, b3, c3 = _route(aff3, cap)
    b_flat = b3.reshape(-1)
    c_flat = c3.reshape(-1)
    loc_t = loc3.transpose(0, 2, 1).reshape(S, N_EXPERTS)

    xe2d = _dispatch(b_flat, c_flat, x2b, loc3, cap)
    wr_pad = jnp.pad(w_router.astype(BF16), ((0, 0), (0, LANES - N_EXPERTS)))
    ye2d = _ffn(xe2d, w_gate, w_up, w_down, wr_pad, cap)
    return _combine(b_flat, c_flat, x2, loc_t, row(ln3_g), row(ln3_b), ye2d, cap, alpha)


def kernel(x, mem, w_in, na_rpb, conv_w, w_mix_out, ln1_g, ln1_b, w_mem_q, w_mem_k,
           w_mem_v, w_mem_out, ln2_g, ln2_b, w_router, w_exp_gate, w_exp_up,
           w_exp_down, ln3_g, ln3_b):
    depth = w_in.shape[0]
    alpha = (2 * depth) ** 0.25
    outs = []
    for bi in range(x.shape[0]):
        xb = x[bi]
        for l in range(depth):
            xb = _layer(xb, mem[bi], w_in[l], na_rpb[l], conv_w[l], w_mix_out[l],
                        ln1_g[l], ln1_b[l], w_mem_q[l], w_mem_k[l], w_mem_v[l],
                        w_mem_out[l], ln2_g[l], ln2_b[l], w_router[l], w_exp_gate[l],
                        w_exp_up[l], w_exp_down[l], ln3_g[l], ln3_b[l], alpha)
        outs.append(xb)
    return jnp.stack(outs)
```

```python
import functools

import jax
import jax.numpy as jnp
from jax import lax
from jax.experimental import pallas as pl
from jax.experimental.pallas import tpu as pltpu

F32 = jnp.float32
BF16 = jnp.bfloat16
I32 = jnp.int32

GRID_W = 64
HEAD_DIM = 64
NA_HEADS = 8
NA_WIDTH = NA_HEADS * HEAD_DIM
WIN_ROWS = 8
WIN_COLS = 16
MEM_HEADS = 4
MEM_HEAD_DIM = 128
N_EXPERTS = 16
CAPACITY_FACTOR = 2
LN_EPS = 1e-5
NEG = -1e30

LANES = 128
SUBLANES = 8
VMEM_LIMIT = 56 * 1024 * 1024

ROW_BLOCK = 8
TOK_BLOCK = ROW_BLOCK * GRID_W
ROUTE_TILE = 256
SLOT_CHUNK = 256
SM_ROWS = 16
RARE_BIT = 6
PEELED_CHUNKS = 2
FFN_TM = 1024
FFN_TF = 512


def _layer_norm(z, g, b):
    mu = jnp.mean(z, axis=-1, keepdims=True)
    zc = z - mu
    var = jnp.mean(zc * zc, axis=-1, keepdims=True)
    return zc * lax.rsqrt(var + LN_EPS) * g + b


def _dot_nt(a, b):
    return lax.dot_general(a, b, (((1,), (1,)), ((), ())),
                           preferred_element_type=F32)


def _proj_kernel(x_ref, w_ref, q_ref, k_ref, v_ref, b_ref, u_ref):
    xb = x_ref[...].astype(BF16)
    nw = NA_WIDTH

    def mm(c0):
        return jnp.dot(xb, w_ref[:, c0:c0 + nw], preferred_element_type=F32)

    q_ref[...] = (mm(0) * (HEAD_DIM ** -0.5)).astype(BF16)
    k_ref[...] = mm(nw).astype(BF16)
    v_ref[...] = mm(2 * nw).astype(BF16)
    b_ref[...] = mm(3 * nw)
    u_ref[...] = mm(4 * nw) * mm(5 * nw)


def _proj(x, w_in_b):
    S, D = x.shape
    nw = NA_WIDTH
    tm = TOK_BLOCK
    blk = lambda i: (i, 0)
    return pl.pallas_call(
        _proj_kernel,
        grid=(S // tm,),
        in_specs=[pl.BlockSpec((tm, D), blk),
                  pl.BlockSpec((D, 6 * nw), lambda i: (0, 0))],
        out_specs=[pl.BlockSpec((tm, nw), blk)] * 5,
        out_shape=[jax.ShapeDtypeStruct((S, nw), BF16)] * 3
        + [jax.ShapeDtypeStruct((S, nw), F32)] * 2,
        compiler_params=pltpu.CompilerParams(
            dimension_semantics=("arbitrary",), vmem_limit_bytes=VMEM_LIMIT),
        name="proj",
    )(x, w_in_b)


def _build_na_bias(rpb_ref, bias_ref):
    shape = (GRID_W, LANES)
    qc = lax.broadcasted_iota(I32, shape, 0)
    ln = lax.broadcasted_iota(I32, shape, 1)
    kc = ln % GRID_W
    c_start = jnp.clip(qc - WIN_COLS // 2, 0, GRID_W - WIN_COLS)
    left = ln < GRID_W
    base = LANES - (WIN_COLS - 1)

    def canvas(h, d):
        ra = jnp.broadcast_to(rpb_ref[h, d:d + 1, :], shape)
        rb = jnp.broadcast_to(rpb_ref[h, d + 1:d + 2, :], shape)
        ta = pltpu.roll(pltpu.roll(ra, base, 1), 0, 1, stride=1, stride_axis=0)
        tb = pltpu.roll(pltpu.roll(rb, (base + GRID_W) % LANES, 1), 0, 1,
                        stride=1, stride_axis=0)
        t = jnp.where(left, ta, tb)
        return jnp.where(kc >= c_start, jnp.where(kc < c_start + WIN_COLS, t, NEG), NEG)

    for par in range(2):
        for hp in range(NA_HEADS // 2):
            for m in range(WIN_ROWS):
                d = 2 * m + par
                bias_ref[par, hp, m, 0:GRID_W] = canvas(2 * hp, d)
                bias_ref[par, hp, m, GRID_W:2 * GRID_W] = canvas(2 * hp + 1, d)


def _mixer_kernel(rows, alpha,
                  q_ref, kp_ref, kc_ref, kn_ref, vp_ref, vc_ref, vn_ref,
                  b_ref, u_ref, up_ref, un_ref, x_ref, wout_ref, rpb_ref,
                  cw_ref, g_ref, be_ref, o_ref, kbuf, vbuf, mix, bias_ref,
                  s_scr0, s_scr1, p_scr0, p_scr1):
    i = pl.program_id(0)
    nb = pl.num_programs(0)
    tb = TOK_BLOCK
    win = WIN_ROWS * GRID_W

    @pl.when(i == 0)
    def _():
        _build_na_bias(rpb_ref, bias_ref)

    kbuf[0:tb] = kp_ref[...]
    kbuf[tb:2 * tb] = kc_ref[...]
    kbuf[2 * tb:3 * tb] = kn_ref[...]
    vbuf[0:tb] = vp_ref[...]
    vbuf[tb:2 * tb] = vc_ref[...]
    vbuf[2 * tb:3 * tb] = vn_ref[...]

    lane = lax.broadcasted_iota(I32, (1, LANES), 1)
    first_head = lane < HEAD_DIM
    m_lo = jnp.where(first_head, 1.0, 0.0).astype(BF16)
    m_hi = jnp.where(first_head, 0.0, 1.0).astype(BF16)

    def window(r):
        grow = i * ROW_BLOCK + r
        start = jnp.clip(grow - WIN_ROWS // 2, 0, rows - WIN_ROWS)
        w0 = pl.multiple_of((start - (i - 1) * ROW_BLOCK) * GRID_W, GRID_W)
        d0 = start - grow + WIN_ROWS
        return w0, d0 % 2, d0 // 2

    def scores(r, s_out):
        w0, _, _ = window(r)
        q0 = r * GRID_W if isinstance(r, int) else pl.multiple_of(r * GRID_W, GRID_W)
        for hp in range(NA_HEADS // 2):
            cs = slice(hp * LANES, (hp + 1) * LANES)
            qp = q_ref[pl.ds(q0, GRID_W), cs]
            lhs = jnp.concatenate([qp * m_lo, qp * m_hi], axis=0)
            kw = kbuf[pl.ds(w0, win), cs]
            s_out[hp] = _dot_nt(lhs, kw)

    def probs(r, s_in, p_out):
        _, par, m0 = window(r)
        for hp in range(NA_HEADS // 2):
            for rb in range(2 * GRID_W // SM_ROWS):
                rs = slice(rb * SM_ROWS, (rb + 1) * SM_ROWS)
                bias = jnp.concatenate(
                    [bias_ref[par, hp, m0 + j, rs, :] for j in range(WIN_ROWS // 2)], axis=1)
                s = s_in[hp, rs, :] + bias
                p = jnp.exp(s - jnp.max(s, axis=-1, keepdims=True))
                p = p * (1.0 / jnp.sum(p, axis=-1, keepdims=True))
                p_out[hp, rs, :] = p.astype(BF16)

    def values(r, p_in):
        w0, _, _ = window(r)
        q0 = r * GRID_W if isinstance(r, int) else pl.multiple_of(r * GRID_W, GRID_W)
        for hp in range(NA_HEADS // 2):
            cs = slice(hp * LANES, (hp + 1) * LANES)
            vw = vbuf[pl.ds(w0, win), cs]
            o2 = jnp.dot(p_in[hp], vw, preferred_element_type=F32)
            y = jnp.where(first_head, o2[0:GRID_W], o2[GRID_W:2 * GRID_W])
            mix[pl.ds(q0, GRID_W), cs] = y.astype(BF16)

    s_buf = (s_scr0, s_scr1)
    p_buf = (p_scr0, p_scr1)

    def steady(k, carry):
        for half in range(2):
            r = 1 + 2 * k + half
            values(r - 1, p_buf[half])
            probs(r, s_buf[1 - half], p_buf[1 - half])
            scores(r + 1, s_buf[half])
        return carry

    scores(0, s_buf[0])
    probs(0, s_buf[0], p_buf[0])
    scores(1, s_buf[1])
    lax.fori_loop(0, (ROW_BLOCK - 2) // 2, steady, 0)
    values(ROW_BLOCK - 2, p_buf[0])
    probs(ROW_BLOCK - 1, s_buf[1], p_buf[1])
    values(ROW_BLOCK - 1, p_buf[1])

    u = u_ref[...]
    prev_row = up_ref[SUBLANES - 1:SUBLANES, :] * jnp.where(i > 0, 1.0, 0.0)
    next_row = un_ref[0:1, :] * jnp.where(i < nb - 1, 1.0, 0.0)
    rid = lax.broadcasted_iota(I32, (tb, 1), 0)
    um1 = jnp.where(rid == 0, prev_row, pltpu.roll(u, 1, axis=0))
    up1 = jnp.where(rid == tb - 1, next_row, pltpu.roll(u, tb - 1, axis=0))
    cw = cw_ref[...]
    yc = b_ref[...] * (cw[0:1] * um1 + cw[1:2] * u + cw[2:3] * up1)
    mix[:, NA_WIDTH:] = yc.astype(BF16)

    y = jnp.dot(mix[...], wout_ref[...], preferred_element_type=F32)
    z = alpha * x_ref[...] + y
    o_ref[...] = _layer_norm(z, g_ref[...], be_ref[...])


def _mixer(x, q, k, v, bg, u, w_out_b, rpb, conv_w, g, b, alpha):
    S, D = x.shape
    rpb_pad = jnp.pad(rpb.astype(F32), ((0, 0), (1, 1), (0, LANES - rpb.shape[2])))
    nw = NA_WIDTH
    tb = TOK_BLOCK
    nb = S // tb
    rows = S // GRID_W
    cur = lambda i: (i, 0)
    prv = lambda i: (jnp.maximum(i - 1, 0), 0)
    nxt = lambda i: (jnp.minimum(i + 1, nb - 1), 0)
    halo = tb // SUBLANES
    hprev = lambda i: (jnp.maximum(i * halo - 1, 0), 0)
    hnext = lambda i: (jnp.minimum((i + 1) * halo, S // SUBLANES - 1), 0)
    const2 = lambda i: (0, 0)
    kv = lambda im: pl.BlockSpec((tb, nw), im)
    return pl.pallas_call(
        functools.partial(_mixer_kernel, rows, alpha),
        grid=(nb,),
        in_specs=[kv(cur), kv(prv), kv(cur), kv(nxt), kv(prv), kv(cur), kv(nxt),
                  kv(cur), kv(cur),
                  pl.BlockSpec((SUBLANES, nw), hprev),
                  pl.BlockSpec((SUBLANES, nw), hnext),
                  pl.BlockSpec((tb, D), cur),
                  pl.BlockSpec((D, D), const2),
                  pl.BlockSpec(rpb_pad.shape, lambda i: (0, 0, 0)),
                  pl.BlockSpec(conv_w.shape, const2),
                  pl.BlockSpec((1, D), const2),
                  pl.BlockSpec((1, D), const2)],
        out_specs=pl.BlockSpec((tb, D), cur),
        out_shape=jax.ShapeDtypeStruct((S, D), F32),
        scratch_shapes=[pltpu.VMEM((3 * tb, nw), BF16),
                        pltpu.VMEM((3 * tb, nw), BF16),
                        pltpu.VMEM((tb, D), BF16),
                        pltpu.VMEM((2, NA_HEADS // 2, WIN_ROWS, 2 * GRID_W, LANES), F32),
                        pltpu.VMEM((NA_HEADS // 2, 2 * GRID_W, WIN_ROWS * GRID_W), F32),
                        pltpu.VMEM((NA_HEADS // 2, 2 * GRID_W, WIN_ROWS * GRID_W), F32),
                        pltpu.VMEM((NA_HEADS // 2, 2 * GRID_W, WIN_ROWS * GRID_W), BF16),
                        pltpu.VMEM((NA_HEADS // 2, 2 * GRID_W, WIN_ROWS * GRID_W), BF16)],
        compiler_params=pltpu.CompilerParams(
            dimension_semantics=("arbitrary",), vmem_limit_bytes=VMEM_LIMIT),
        name="mixer",
    )(q, k, k, k, v, v, v, bg, u, u, u, x, w_out_b, rpb_pad, conv_w, g, b)


def _xattn_kernel(alpha, x_ref, mem_ref, wq_ref, wk_ref, wv_ref, wo_ref,
                  g_ref, be_ref, wrh_ref, wrl_ref,
                  x2_ref, x2b_ref, aff_ref, kmem, vmem):
    i = pl.program_id(0)

    @pl.when(i == 0)
    def _():
        mb = mem_ref[...].astype(BF16)
        kmem[...] = jnp.dot(mb, wk_ref[...], preferred_element_type=F32).astype(BF16)
        vmem[...] = jnp.dot(mb, wv_ref[...], preferred_element_type=F32).astype(BF16)

    x1 = x_ref[...]
    qb = jnp.dot(x1.astype(BF16), wq_ref[...], preferred_element_type=F32).astype(BF16)
    outs = []
    for h in range(MEM_HEADS):
        cs = slice(h * MEM_HEAD_DIM, (h + 1) * MEM_HEAD_DIM)
        s = _dot_nt(qb[:, cs], kmem[:, cs]) * (MEM_HEAD_DIM ** -0.5)
        mx = jnp.max(s, axis=-1, keepdims=True)
        p = jnp.exp(s - mx)
        l = jnp.sum(p, axis=-1, keepdims=True)
        oh = jnp.dot(p.astype(BF16), vmem[:, cs], preferred_element_type=F32)
        outs.append(oh / l)
    o = jnp.concatenate(outs, axis=1).astype(BF16)
    y = jnp.dot(o, wo_ref[...], preferred_element_type=F32)
    x2 = _layer_norm(alpha * x1 + y, g_ref[...], be_ref[...])
    x2_ref[...] = x2
    hi = x2.astype(BF16)
    x2b_ref[...] = hi
    lo = (x2 - hi.astype(F32)).astype(BF16)
    wrh = wrh_ref[...]
    lg = _dot_nt(wrh, hi) + _dot_nt(wrh, lo) + _dot_nt(wrl_ref[...], hi)
    mx = jnp.max(lg, axis=0, keepdims=True)
    p = jnp.exp(lg - mx)
    aff = p / jnp.sum(p, axis=0, keepdims=True)
    for j in range(aff_ref.shape[0]):
        aff_ref[j] = aff[:, j * LANES:(j + 1) * LANES]


def _xattn(x1, mem, wq_b, wk_b, wv_b, wo_b, g, b, wr_hi, wr_lo, alpha):
    S, D = x1.shape
    M = mem.shape[0]
    mw = MEM_HEADS * MEM_HEAD_DIM
    tm = TOK_BLOCK
    cur = lambda i: (i, 0)
    const2 = lambda i: (0, 0)
    full = lambda a: pl.BlockSpec(a.shape, const2)
    return pl.pallas_call(
        functools.partial(_xattn_kernel, alpha),
        grid=(S // tm,),
        in_specs=[pl.BlockSpec((tm, D), cur), full(mem), full(wq_b), full(wk_b),
                  full(wv_b), full(wo_b), full(g), full(b), full(wr_hi), full(wr_lo)],
        out_specs=[pl.BlockSpec((tm, D), cur), pl.BlockSpec((tm, D), cur),
                   pl.BlockSpec((tm // LANES, N_EXPERTS, LANES), lambda i: (i, 0, 0))],
        out_shape=[jax.ShapeDtypeStruct((S, D), F32),
                   jax.ShapeDtypeStruct((S, D), BF16),
                   jax.ShapeDtypeStruct((S // LANES, N_EXPERTS, LANES), F32)],
        scratch_shapes=[pltpu.VMEM((M, mw), BF16), pltpu.VMEM((M, mw), BF16)],
        compiler_params=pltpu.CompilerParams(
            dimension_semantics=("arbitrary",), vmem_limit_bytes=VMEM_LIMIT),
        name="xattn",
    )(x1, mem, wq_b, wk_b, wv_b, wo_b, g, b, wr_hi, wr_lo)


def _route_kernel(cap, aff_ref, loc_ref, b_ref, c_ref):
    nblk = aff_ref.shape[0]
    per_tile = ROUTE_TILE // LANES
    ntile = nblk // per_tile

    def count_ge(t):
        ge = jnp.where(aff_ref[...] >= t[None], 1.0, 0.0)
        return jnp.sum(jnp.sum(ge, axis=0), axis=1, keepdims=True)

    def coarse(it, bits):
        cand = bits | jnp.left_shift(jnp.int32(1), 30 - it)
        ok = count_ge(pltpu.bitcast(cand, F32)) >= cap
        return jnp.where(ok, cand, bits)

    def fine(it, lohi):
        lo, hi = lohi
        mid = lo + (hi - lo) * 0.5
        ok = count_ge(mid) >= cap
        return jnp.where(ok, mid, lo), jnp.where(ok, hi, mid)

    bits = lax.fori_loop(0, 31, coarse, jnp.zeros((N_EXPERTS, 1), I32))
    thr, _ = lax.fori_loop(
        0, 30, fine, (pltpu.bitcast(bits, F32), pltpu.bitcast(bits + 1, F32)))
    gt_all = jnp.where(aff_ref[...] > thr[None], 1.0, 0.0)
    n_gt = jnp.sum(jnp.sum(gt_all, axis=0), axis=1, keepdims=True)
    need = cap - n_gt

    ii = lax.broadcasted_iota(I32, (LANES, LANES), 0)
    jj = lax.broadcasted_iota(I32, (LANES, LANES), 1)
    tri = jnp.where(ii <= jj, 1.0, 0.0).astype(BF16)

    def tile_body(t, carry):
        run, run_eq = carry
        tile_run = jnp.zeros((N_EXPERTS, 1), F32)
        for j in range(per_tile):
            blk = aff_ref[t * per_tile + j]
            gt = jnp.where(blk > thr, 1.0, 0.0)
            eq = jnp.where(blk == thr, 1.0, 0.0)
            eq_incl = run_eq + jnp.dot(eq.astype(BF16), tri, preferred_element_type=F32)
            sel = gt + eq * jnp.where(eq_incl <= need, 1.0, 0.0)
            run_eq = run_eq + jnp.sum(eq, axis=1, keepdims=True)
            incl = jnp.dot(sel.astype(BF16), tri, preferred_element_type=F32)
            loc = tile_run + incl - sel
            loc_ref[t * per_tile + j] = jnp.where(sel > 0.5, loc, -1.0).astype(I32)
            tile_run = tile_run + jnp.sum(sel, axis=1, keepdims=True)
        b_ref[t] = run.astype(I32)
        c_ref[t] = tile_run.astype(I32)
        return run + tile_run, run_eq

    zero = jnp.zeros((N_EXPERTS, 1), F32)
    lax.fori_loop(0, ntile, tile_body, (zero, zero))


def _route(aff3, cap):
    nblk = aff3.shape[0]
    ntile = nblk * LANES // ROUTE_TILE
    full3 = lambda a: pl.BlockSpec(a, lambda: (0, 0, 0))
    return pl.pallas_call(
        functools.partial(_route_kernel, cap),
        in_specs=[full3(aff3.shape)],
        out_specs=[full3(aff3.shape), full3((ntile, N_EXPERTS, 1)),
                   full3((ntile, N_EXPERTS, 1))],
        out_shape=[jax.ShapeDtypeStruct(aff3.shape, I32),
                   jax.ShapeDtypeStruct((ntile, N_EXPERTS, 1), I32),
                   jax.ShapeDtypeStruct((ntile, N_EXPERTS, 1), I32)],
        compiler_params=pltpu.CompilerParams(vmem_limit_bytes=VMEM_LIMIT),
        name="route",
    )(aff3)


def _segment_copies(b_sm, c_sm, t, make_copy, action):
    off = 0
    nbits = ROUTE_TILE.bit_length()

    def bit_copies(e, b, c, off, bits):
        for kbit in bits:
            n = 1 << kbit
            r0 = (c >> (kbit + 1)) << (kbit + 1)

            @pl.when((c & n) != 0)
            def _(r0=r0, n=n):
                action(make_copy(e, b + r0, off + r0, n))

    for e in range(N_EXPERTS):
        c = c_sm[t * N_EXPERTS + e]
        b = b_sm[t * N_EXPERTS + e]

        @pl.when(c >= (1 << RARE_BIT))
        def _(e=e, b=b, c=c, off=off):
            bit_copies(e, b, c, off, range(nbits - 1, RARE_BIT - 1, -1))

        bit_copies(e, b, c, off, range(RARE_BIT - 1, -1, -1))
        off = off + c
    return off


def _tile_rows(c_sm, t):
    rows = 0
    for e in range(N_EXPERTS):
        rows = rows + c_sm[t * N_EXPERTS + e]
    return rows


def _dispatch_kernel(cap, b_sm, c_sm, x_ref, loc_ref, xe_hbm, stg, sem):
    t = pl.program_id(0)
    nt = pl.num_programs(0)
    slot = t % 2

    offs = []
    off = 0
    for e in range(N_EXPERTS):
        offs.append(off)
        off = off + c_sm[t * N_EXPERTS + e]
    offs.append(off)
    nchunk = (off + SLOT_CHUNK - 1) // SLOT_CHUNK
    sub = lax.broadcasted_iota(I32, (SLOT_CHUNK, 1), 0)
    lane = lax.broadcasted_iota(I32, (1, LANES), 1)
    seg_lo = jnp.zeros((1, LANES), I32)
    seg_hi = jnp.zeros((1, LANES), I32)
    for e in range(N_EXPERTS):
        seg_lo = jnp.where(lane == e, offs[e], seg_lo)
        seg_hi = jnp.where(lane == e, offs[e + 1], seg_hi)
    seg_lo_f = seg_lo.astype(F32)
    locf = jnp.concatenate(
        [loc_ref[j] for j in range(ROUTE_TILE // LANES)], axis=1).astype(F32)
    locb = jnp.concatenate(
        [locf, jnp.zeros((LANES - N_EXPERTS, ROUTE_TILE), F32)], axis=0).astype(BF16)

    def chunk_body(m, carry):
        base = m * SLOT_CHUNK
        srow = sub + base
        inseg = jnp.where(srow >= seg_lo, jnp.where(srow < seg_hi, 1.0, 0.0), 0.0)
        rank = srow.astype(F32) - jnp.sum(inseg * seg_lo_f, axis=1, keepdims=True)
        want = jnp.dot(inseg.astype(BF16), locb, preferred_element_type=F32)
        p = jnp.where(want == rank, 1.0, 0.0)
        res = jnp.dot(p.astype(BF16), x_ref[...], preferred_element_type=F32)
        row0 = base * SUBLANES if isinstance(m, int) else pl.multiple_of(base * SUBLANES, SUBLANES)
        for s in range(SUBLANES):
            stg[slot, pl.ds(row0 + s, SLOT_CHUNK, stride=SUBLANES), :] = (
                res[:, s * LANES:(s + 1) * LANES])
        return carry

    for m in range(PEELED_CHUNKS):
        chunk_body(m, 0)
    lax.fori_loop(PEELED_CHUNKS, jnp.maximum(nchunk, PEELED_CHUNKS), chunk_body, 0)

    def copy_for(slot_):
        def make(e, dst_row, src_row, n):
            return pltpu.make_async_copy(
                stg.at[slot_, pl.ds(pl.multiple_of(src_row * SUBLANES, SUBLANES), n * SUBLANES)],
                xe_hbm.at[pl.ds(pl.multiple_of((e * cap + dst_row) * SUBLANES, SUBLANES),
                                n * SUBLANES)],
                sem.at[slot_])
        return make

    _segment_copies(b_sm, c_sm, t, copy_for(slot), lambda cp: cp.start())

    def wait_tile(t_, slot_):
        rows = _tile_rows(c_sm, t_)

        @pl.when(rows > 0)
        def _():
            n = rows * SUBLANES
            pltpu.make_async_copy(stg.at[slot_, pl.ds(0, n)], xe_hbm.at[pl.ds(0, n)],
                                  sem.at[slot_]).wait()

    @pl.when(t > 0)
    def _():
        wait_tile(t - 1, 1 - slot)

    @pl.when(t == nt - 1)
    def _():
        wait_tile(t, slot)


def _dispatch(b_flat, c_flat, x2b, loc3, cap):
    S, D = x2b.shape
    nt = S // ROUTE_TILE
    per_tile = ROUTE_TILE // LANES
    max_rows = N_EXPERTS * ROUTE_TILE
    return pl.pallas_call(
        functools.partial(_dispatch_kernel, cap),
        grid_spec=pltpu.PrefetchScalarGridSpec(
            num_scalar_prefetch=2,
            grid=(nt,),
            in_specs=[pl.BlockSpec((ROUTE_TILE, D), lambda t, b, c: (t, 0)),
                      pl.BlockSpec((per_tile, N_EXPERTS, LANES), lambda t, b, c: (t, 0, 0))],
            out_specs=pl.BlockSpec(memory_space=pl.ANY),
            scratch_shapes=[pltpu.VMEM((2, max_rows * SUBLANES, LANES), F32),
                            pltpu.SemaphoreType.DMA((2,))]),
        out_shape=jax.ShapeDtypeStruct((N_EXPERTS * cap * SUBLANES, LANES), F32),
        compiler_params=pltpu.CompilerParams(
            dimension_semantics=("arbitrary",), vmem_limit_bytes=VMEM_LIMIT),
        name="dispatch",
    )(b_flat, c_flat, x2b, loc3)


def _ffn_kernel(xe_ref, wg_ref, wu_ref, wd_ref, wr_ref, o_ref, xb, acc, gate):
    e = pl.program_id(0)
    f = pl.program_id(2)
    tm = xb.shape[0]

    @pl.when(f == 0)
    def _():
        for s in range(SUBLANES):
            xb[:, s * LANES:(s + 1) * LANES] = (
                xe_ref[pl.ds(s, tm, stride=SUBLANES), :].astype(BF16))
        lg = jnp.dot(xb[...], wr_ref[...], preferred_element_type=F32)
        lane = lax.broadcasted_iota(I32, (1, LANES), 1)
        lg = jnp.where(lane < N_EXPERTS, lg, NEG)
        p = jnp.exp(lg - jnp.max(lg, axis=-1, keepdims=True))
        mine = jnp.sum(jnp.where(lane == e, p, 0.0), axis=-1, keepdims=True)
        gate[...] = mine / jnp.sum(p, axis=-1, keepdims=True)
        acc[...] = jnp.zeros_like(acc)

    x = xb[...]
    g = jnp.dot(x, wg_ref[...].astype(BF16), preferred_element_type=F32)
    u = jnp.dot(x, wu_ref[...].astype(BF16), preferred_element_type=F32)
    h = (g * jax.nn.sigmoid(g) * u).astype(BF16)
    acc[...] += jnp.dot(h, wd_ref[...].astype(BF16), preferred_element_type=F32)

    @pl.when(f == pl.num_programs(2) - 1)
    def _():
        y = acc[...] * gate[...]
        for s in range(SUBLANES):
            o_ref[pl.ds(s, tm, stride=SUBLANES), :] = y[:, s * LANES:(s + 1) * LANES]


def _ffn(xe2d, w_gate, w_up, w_down, wr_pad, cap):
    E, D, FF = w_gate.shape
    tm, tf = FFN_TM, FFN_TF
    mt = cap // tm
    rows = lambda e, m, f: (e * mt + m, 0)
    return pl.pallas_call(
        _ffn_kernel,
        grid=(E, mt, FF // tf),
        in_specs=[pl.BlockSpec((tm * SUBLANES, LANES), rows),
                  pl.BlockSpec((None, D, tf), lambda e, m, f: (e, 0, f)),
                  pl.BlockSpec((None, D, tf), lambda e, m, f: (e, 0, f)),
                  pl.BlockSpec((None, tf, D), lambda e, m, f: (e, f, 0)),
                  pl.BlockSpec(wr_pad.shape, lambda e, m, f: (0, 0))],
        out_specs=pl.BlockSpec((tm * SUBLANES, LANES), rows),
        out_shape=jax.ShapeDtypeStruct(xe2d.shape, F32),
        scratch_shapes=[pltpu.VMEM((tm, D), BF16), pltpu.VMEM((tm, D), F32),
                        pltpu.VMEM((tm, 1), F32)],
        compiler_params=pltpu.CompilerParams(
            dimension_semantics=("arbitrary", "arbitrary", "arbitrary"),
            vmem_limit_bytes=VMEM_LIMIT),
        name="ffn",
    )(xe2d, w_gate, w_up, w_down, wr_pad)


def _combine_kernel(cap, alpha, b_sm, c_sm, x_ref, loc_ref, g_ref, be_ref, ye_hbm,
                    o_ref, ybuf, sem, hl_scr, z_scr):
    t = pl.program_id(0)
    nt = pl.num_programs(0)
    slot = t % 2

    def copy_for(slot_):
        def make(e, src_row, dst_row, n):
            return pltpu.make_async_copy(
                ye_hbm.at[pl.ds(pl.multiple_of((e * cap + src_row) * SUBLANES, SUBLANES),
                                n * SUBLANES)],
                ybuf.at[slot_, pl.ds(pl.multiple_of(dst_row * SUBLANES, SUBLANES), n * SUBLANES)],
                sem.at[slot_])
        return make

    @pl.when(t == 0)
    def _():
        ybuf[...] = jnp.zeros_like(ybuf)
        _segment_copies(b_sm, c_sm, t, copy_for(slot), lambda cp: cp.start())

    @pl.when(t < nt - 1)
    def _():
        _segment_copies(b_sm, c_sm, t + 1, copy_for(1 - slot), lambda cp: cp.start())

    rows_in = _tile_rows(c_sm, t)

    @pl.when(rows_in > 0)
    def _():
        n = rows_in * SUBLANES
        pltpu.make_async_copy(ye_hbm.at[pl.ds(0, n)], ybuf.at[slot, pl.ds(0, n)],
                              sem.at[slot]).wait()

    offs = []
    off = 0
    for e in range(N_EXPERTS):
        offs.append(off)
        off = off + c_sm[t * N_EXPERTS + e]
    offs.append(off)
    total = off
    nchunk = (total + SLOT_CHUNK - 1) // SLOT_CHUNK
    lane = lax.broadcasted_iota(I32, (1, SLOT_CHUNK), 1)
    sub = lax.broadcasted_iota(I32, (SLOT_CHUNK, 1), 0)
    eid = lax.broadcasted_iota(I32, (N_EXPERTS, 1), 0)
    seg_lo = jnp.zeros((N_EXPERTS, 1), I32)
    seg_hi = jnp.zeros((N_EXPERTS, 1), I32)
    for e in range(N_EXPERTS):
        seg_lo = jnp.where(eid == e, offs[e], seg_lo)
        seg_hi = jnp.where(eid == e, offs[e + 1], seg_hi)
    seg_lo_f = seg_lo.astype(F32)
    locb = loc_ref[...].astype(F32).astype(BF16)

    z_scr[...] = alpha * x_ref[...]

    def chunk_body(m, carry):
        base = m * SLOT_CHUNK
        srow = lane + base
        inseg = jnp.where(srow >= seg_lo, jnp.where(srow < seg_hi, 1.0, 0.0), 0.0)
        rank = srow.astype(F32) - jnp.sum(inseg * seg_lo_f, axis=0, keepdims=True)
        want = jnp.dot(locb, inseg.astype(BF16), preferred_element_type=F32)
        ptb = jnp.where(want == rank, 1.0, 0.0).astype(BF16)
        row0 = base * SUBLANES if isinstance(m, int) else pl.multiple_of(base * SUBLANES, SUBLANES)
        yc = jnp.concatenate(
            [ybuf[slot, pl.ds(row0 + s, SLOT_CHUNK, stride=SUBLANES), :]
             for s in range(SUBLANES)], axis=1)
        yc = jnp.where(sub < total - base, yc, 0.0)
        hi = yc.astype(BF16)
        hl = hl_scr.at[m % PEELED_CHUNKS]
        hl[0:SLOT_CHUNK] = hi
        hl[SLOT_CHUNK:2 * SLOT_CHUNK] = (yc - hi.astype(F32)).astype(BF16)
        z_scr[...] += jnp.dot(jnp.concatenate([ptb, ptb], axis=1), hl[...],
                              preferred_element_type=F32)
        return carry

    for m in range(PEELED_CHUNKS):
        chunk_body(m, 0)
    lax.fori_loop(PEELED_CHUNKS, jnp.maximum(nchunk, PEELED_CHUNKS), chunk_body, 0)

    o_ref[...] = _layer_norm(z_scr[...], g_ref[...], be_ref[...])


def _combine(b_flat, c_flat, x2, loc_t, g, b, ye2d, cap, alpha):
    S, D = x2.shape
    nt = S // ROUTE_TILE
    max_rows = N_EXPERTS * ROUTE_TILE
    cur = lambda t, b_, c_: (t, 0)
    const2 = lambda t, b_, c_: (0, 0)
    return pl.pallas_call(
        functools.partial(_combine_kernel, cap, alpha),
        grid_spec=pltpu.PrefetchScalarGridSpec(
            num_scalar_prefetch=2,
            grid=(nt,),
            in_specs=[pl.BlockSpec((ROUTE_TILE, D), cur),
                      pl.BlockSpec((ROUTE_TILE, N_EXPERTS), cur),
                      pl.BlockSpec((1, D), const2),
                      pl.BlockSpec((1, D), const2),
                      pl.BlockSpec(memory_space=pl.ANY)],
            out_specs=pl.BlockSpec((ROUTE_TILE, D), cur),
            scratch_shapes=[pltpu.VMEM((2, max_rows * SUBLANES, LANES), F32),
                            pltpu.SemaphoreType.DMA((2,)),
                            pltpu.VMEM((PEELED_CHUNKS, 2 * SLOT_CHUNK, D), BF16),
                            pltpu.VMEM((ROUTE_TILE, D), F32)]),
        out_shape=jax.ShapeDtypeStruct((S, D), F32),
        compiler_params=pltpu.CompilerParams(
            dimension_semantics=("arbitrary",), vmem_limit_bytes=VMEM_LIMIT),
        name="combine",
    )(b_flat, c_flat, x2, loc_t, g, b, ye2d)


def _layer(x, mem, w_in, rpb, conv_w, w_mix_out, ln1_g, ln1_b, wq, wk, wv, wo,
           ln2_g, ln2_b, w_router, w_gate, w_up, w_down, ln3_g, ln3_b, alpha):
    S, D = x.shape
    cap = CAPACITY_FACTOR * S // N_EXPERTS
    assert S % TOK_BLOCK == 0 and S // GRID_W >= 2 * ROW_BLOCK
    assert cap % FFN_TM == 0 and cap <= S
    row = lambda a: a.reshape(1, -1)

    q, k, v, bg, u = _proj(x, w_in.astype(BF16))
    x1 = _mixer(x, q, k, v, bg, u, w_mix_out.astype(BF16), rpb,
                conv_w, row(ln1_g), row(ln1_b), alpha)

    wr_t = w_router.T
    wr_hi = wr_t.astype(BF16)
    wr_lo = (wr_t - wr_hi.astype(F32)).astype(BF16)
    x2, x2b, aff3 = _xattn(x1, mem, wq.astype(BF16), wk.astype(BF16), wv.astype(BF16),
                           wo.astype(BF16), row(ln2_g), row(ln2_b), wr_hi, wr_lo, alpha)

    loc3, b3, c3 = _route(aff3, cap)
    b_flat = b3.reshape(-1)
    c_flat = c3.reshape(-1)
    loc_t = loc3.transpose(0, 2, 1).reshape(S, N_EXPERTS)

    xe2d = _dispatch(b_flat, c_flat, x2b, loc3, cap)
    wr_pad = jnp.pad(w_router.astype(BF16), ((0, 0), (0, LANES - N_EXPERTS)))
    ye2d = _ffn(xe2d, w_gate, w_up, w_down, wr_pad, cap)
    return _combine(b_flat, c_flat, x2, loc_t, row(ln3_g), row(ln3_b), ye2d, cap, alpha)


def kernel(x, mem, w_in, na_rpb, conv_w, w_mix_out, ln1_g, ln1_b, w_mem_q, w_mem_k,
           w_mem_v, w_mem_out, ln2_g, ln2_b, w_router, w_exp_gate, w_exp_up,
           w_exp_down, ln3_g, ln3_b):
    depth = w_in.shape[0]
    alpha = (2 * depth) ** 0.25
    outs = []
    for bi in range(x.shape[0]):
        xb = x[bi]
        for l in range(depth):
            xb = _layer(xb, mem[bi], w_in[l], na_rpb[l], conv_w[l], w_mix_out[l],
                        ln1_g[l], ln1_b[l], w_mem_q[l], w_mem_k[l], w_mem_v[l],
                        w_mem_out[l], ln2_g[l], ln2_b[l], w_router[l], w_exp_gate[l],
                        w_exp_up[l], w_exp_down[l], ln3_g[l], ln3_b[l], alpha)
        outs.append(xb)
    return jnp.stack(outs)
```

```python
import functools

import jax
import jax.numpy as jnp
from jax import lax
from jax.experimental import pallas as pl
from jax.experimental.pallas import tpu as pltpu

F32 = jnp.float32
BF16 = jnp.bfloat16
I32 = jnp.int32

GRID_W = 64
HEAD_DIM = 64
NA_HEADS = 8
NA_WIDTH = NA_HEADS * HEAD_DIM
WIN_ROWS = 8
WIN_COLS = 16
MEM_HEADS = 4
MEM_HEAD_DIM = 128
N_EXPERTS = 16
CAPACITY_FACTOR = 2
LN_EPS = 1e-5
NEG = -1e30

LANES = 128
SUBLANES = 8
VMEM_LIMIT = 56 * 1024 * 1024

ROW_BLOCK = 8
TOK_BLOCK = ROW_BLOCK * GRID_W
PROJ_TM = 1024
ROUTE_TILE = 256
SLOT_CHUNK = 256
SM_ROWS = 16
RARE_BIT = 6
PEELED_CHUNKS = 2
FFN_TM = 1024
FFN_TF = 512


def _layer_norm(z, g, b):
    mu = jnp.mean(z, axis=-1, keepdims=True)
    zc = z - mu
    var = jnp.mean(zc * zc, axis=-1, keepdims=True)
    return zc * lax.rsqrt(var + LN_EPS) * g + b


def _dot_nt(a, b):
    return lax.dot_general(a, b, (((1,), (1,)), ((), ())),
                           preferred_element_type=F32)


def _proj_kernel(x_ref, w_ref, q_ref, k_ref, v_ref, b_ref, u_ref):
    xb = x_ref[...].astype(BF16)
    nw = NA_WIDTH

    def mm(c0):
        return jnp.dot(xb, w_ref[:, c0:c0 + nw], preferred_element_type=F32)

    q_ref[...] = (mm(0) * (HEAD_DIM ** -0.5)).astype(BF16)
    k_ref[...] = mm(nw).astype(BF16)
    v_ref[...] = mm(2 * nw).astype(BF16)
    b_ref[...] = mm(3 * nw)
    u_ref[...] = mm(4 * nw) * mm(5 * nw)


def _proj(x, w_in_b):
    S, D = x.shape
    nw = NA_WIDTH
    tm = PROJ_TM
    blk = lambda i: (i, 0)
    return pl.pallas_call(
        _proj_kernel,
        grid=(S // tm,),
        in_specs=[pl.BlockSpec((tm, D), blk),
                  pl.BlockSpec((D, 6 * nw), lambda i: (0, 0))],
        out_specs=[pl.BlockSpec((tm, nw), blk)] * 5,
        out_shape=[jax.ShapeDtypeStruct((S, nw), BF16)] * 3
        + [jax.ShapeDtypeStruct((S, nw), F32)] * 2,
        compiler_params=pltpu.CompilerParams(
            dimension_semantics=("arbitrary",), vmem_limit_bytes=VMEM_LIMIT),
        name="proj",
    )(x, w_in_b)


def _build_na_bias(rpb_ref, bias_ref):
    shape = (GRID_W, LANES)
    qc = lax.broadcasted_iota(I32, shape, 0)
    ln = lax.broadcasted_iota(I32, shape, 1)
    kc = ln % GRID_W
    c_start = jnp.clip(qc - WIN_COLS // 2, 0, GRID_W - WIN_COLS)
    base = LANES - (WIN_COLS - 1)

    def canvas(h, d):
        both = rpb_ref[h, d:d + 1, :] + pltpu.roll(rpb_ref[h, d + 1:d + 2, :], GRID_W, 1)
        t = pltpu.roll(pltpu.roll(jnp.broadcast_to(both, shape), base, 1), 0, 1,
                       stride=1, stride_axis=0)
        return jnp.where(kc >= c_start, jnp.where(kc < c_start + WIN_COLS, t, NEG), NEG)

    for par in range(2):
        for hp in range(NA_HEADS // 2):
            for m in range(WIN_ROWS):
                d = 2 * m + par
                bias_ref[par, hp, m, 0:GRID_W] = canvas(2 * hp, d)
                bias_ref[par, hp, m, GRID_W:2 * GRID_W] = canvas(2 * hp + 1, d)


def _mixer_kernel(rows, alpha,
                  q_ref, kp_ref, kc_ref, kn_ref, vp_ref, vc_ref, vn_ref,
                  b_ref, u_ref, up_ref, un_ref, x_ref, wout_ref, rpb_ref,
                  cw_ref, g_ref, be_ref, o_ref, kbuf, vbuf, mix, bias_ref,
                  s_scr0, s_scr1, p_scr0, p_scr1):
    i = pl.program_id(0)
    nb = pl.num_programs(0)
    tb = TOK_BLOCK
    win = WIN_ROWS * GRID_W

    @pl.when(i == 0)
    def _():
        _build_na_bias(rpb_ref, bias_ref)

    kbuf[0:tb] = kp_ref[...]
    kbuf[tb:2 * tb] = kc_ref[...]
    kbuf[2 * tb:3 * tb] = kn_ref[...]
    vbuf[0:tb] = vp_ref[...]
    vbuf[tb:2 * tb] = vc_ref[...]
    vbuf[2 * tb:3 * tb] = vn_ref[...]

    lane = lax.broadcasted_iota(I32, (1, LANES), 1)
    first_head = lane < HEAD_DIM
    m_lo = jnp.where(first_head, 1.0, 0.0).astype(BF16)
    m_hi = jnp.where(first_head, 0.0, 1.0).astype(BF16)

    def window(r):
        grow = i * ROW_BLOCK + r
        start = jnp.clip(grow - WIN_ROWS // 2, 0, rows - WIN_ROWS)
        w0 = pl.multiple_of((start - (i - 1) * ROW_BLOCK) * GRID_W, GRID_W)
        d0 = start - grow + WIN_ROWS
        return w0, d0 % 2, d0 // 2

    def scores(r, s_out):
        w0, _, _ = window(r)
        q0 = r * GRID_W if isinstance(r, int) else pl.multiple_of(r * GRID_W, GRID_W)
        for hp in range(NA_HEADS // 2):
            cs = slice(hp * LANES, (hp + 1) * LANES)
            qp = q_ref[pl.ds(q0, GRID_W), cs]
            lhs = jnp.concatenate([qp * m_lo, qp * m_hi], axis=0)
            kw = kbuf[pl.ds(w0, win), cs]
            s_out[hp] = _dot_nt(lhs, kw)

    def probs(r, s_in, p_out):
        _, par, m0 = window(r)
        for hp in range(NA_HEADS // 2):
            for rb in range(2 * GRID_W // SM_ROWS):
                rs = slice(rb * SM_ROWS, (rb + 1) * SM_ROWS)
                bias = jnp.concatenate(
                    [bias_ref[par, hp, m0 + j, rs, :] for j in range(WIN_ROWS // 2)], axis=1)
                s = s_in[hp, rs, :] + bias
                p = jnp.exp(s - jnp.max(s, axis=-1, keepdims=True))
                p = p * (1.0 / jnp.sum(p, axis=-1, keepdims=True))
                p_out[hp, rs, :] = p.astype(BF16)

    def values(r, p_in):
        w0, _, _ = window(r)
        q0 = r * GRID_W if isinstance(r, int) else pl.multiple_of(r * GRID_W, GRID_W)
        for hp in range(NA_HEADS // 2):
            cs = slice(hp * LANES, (hp + 1) * LANES)
            vw = vbuf[pl.ds(w0, win), cs]
            o2 = jnp.dot(p_in[hp], vw, preferred_element_type=F32)
            y = jnp.where(first_head, o2[0:GRID_W], o2[GRID_W:2 * GRID_W])
            mix[pl.ds(q0, GRID_W), cs] = y.astype(BF16)

    s_buf = (s_scr0, s_scr1)
    p_buf = (p_scr0, p_scr1)

    def steady(k, carry):
        for half in range(2):
            r = 1 + 2 * k + half
            values(r - 1, p_buf[half])
            probs(r, s_buf[1 - half], p_buf[1 - half])
            scores(r + 1, s_buf[half])
        return carry

    scores(0, s_buf[0])
    probs(0, s_buf[0], p_buf[0])
    scores(1, s_buf[1])
    lax.fori_loop(0, (ROW_BLOCK - 2) // 2, steady, 0)
    values(ROW_BLOCK - 2, p_buf[0])
    probs(ROW_BLOCK - 1, s_buf[1], p_buf[1])
    values(ROW_BLOCK - 1, p_buf[1])

    u = u_ref[...]
    prev_row = up_ref[SUBLANES - 1:SUBLANES, :] * jnp.where(i > 0, 1.0, 0.0)
    next_row = un_ref[0:1, :] * jnp.where(i < nb - 1, 1.0, 0.0)
    rid = lax.broadcasted_iota(I32, (tb, 1), 0)
    um1 = jnp.where(rid == 0, prev_row, pltpu.roll(u, 1, axis=0))
    up1 = jnp.where(rid == tb - 1, next_row, pltpu.roll(u, tb - 1, axis=0))
    cw = cw_ref[...]
    yc = b_ref[...] * (cw[0:1] * um1 + cw[1:2] * u + cw[2:3] * up1)
    mix[:, NA_WIDTH:] = yc.astype(BF16)

    y = jnp.dot(mix[...], wout_ref[...], preferred_element_type=F32)
    z = alpha * x_ref[...] + y
    o_ref[...] = _layer_norm(z, g_ref[...], be_ref[...])


def _mixer(x, q, k, v, bg, u, w_out_b, rpb, conv_w, g, b, alpha):
    S, D = x.shape
    rpb_pad = jnp.pad(rpb.astype(F32), ((0, 0), (1, 1), (0, LANES - rpb.shape[2])))
    nw = NA_WIDTH
    tb = TOK_BLOCK
    nb = S // tb
    rows = S // GRID_W
    cur = lambda i: (i, 0)
    prv = lambda i: (jnp.maximum(i - 1, 0), 0)
    nxt = lambda i: (jnp.minimum(i + 1, nb - 1), 0)
    halo = tb // SUBLANES
    hprev = lambda i: (jnp.maximum(i * halo - 1, 0), 0)
    hnext = lambda i: (jnp.minimum((i + 1) * halo, S // SUBLANES - 1), 0)
    const2 = lambda i: (0, 0)
    kv = lambda im: pl.BlockSpec((tb, nw), im)
    return pl.pallas_call(
        functools.partial(_mixer_kernel, rows, alpha),
        grid=(nb,),
        in_specs=[kv(cur), kv(prv), kv(cur), kv(nxt), kv(prv), kv(cur), kv(nxt),
                  kv(cur), kv(cur),
                  pl.BlockSpec((SUBLANES, nw), hprev),
                  pl.BlockSpec((SUBLANES, nw), hnext),
                  pl.BlockSpec((tb, D), cur),
                  pl.BlockSpec((D, D), const2),
                  pl.BlockSpec(rpb_pad.shape, lambda i: (0, 0, 0)),
                  pl.BlockSpec(conv_w.shape, const2),
                  pl.BlockSpec((1, D), const2),
                  pl.BlockSpec((1, D), const2)],
        out_specs=pl.BlockSpec((tb, D), cur),
        out_shape=jax.ShapeDtypeStruct((S, D), F32),
        scratch_shapes=[pltpu.VMEM((3 * tb, nw), BF16),
                        pltpu.VMEM((3 * tb, nw), BF16),
                        pltpu.VMEM((tb, D), BF16),
                        pltpu.VMEM((2, NA_HEADS // 2, WIN_ROWS, 2 * GRID_W, LANES), F32),
                        pltpu.VMEM((NA_HEADS // 2, 2 * GRID_W, WIN_ROWS * GRID_W), F32),
                        pltpu.VMEM((NA_HEADS // 2, 2 * GRID_W, WIN_ROWS * GRID_W), F32),
                        pltpu.VMEM((NA_HEADS // 2, 2 * GRID_W, WIN_ROWS * GRID_W), BF16),
                        pltpu.VMEM((NA_HEADS // 2, 2 * GRID_W, WIN_ROWS * GRID_W), BF16)],
        compiler_params=pltpu.CompilerParams(
            dimension_semantics=("arbitrary",), vmem_limit_bytes=VMEM_LIMIT),
        name="mixer",
    )(q, k, k, k, v, v, v, bg, u, u, u, x, w_out_b, rpb_pad, conv_w, g, b)


def _xattn_kernel(alpha, x_ref, mem_ref, wq_ref, wk_ref, wv_ref, wo_ref,
                  g_ref, be_ref, wrh_ref, wrl_ref,
                  x2_ref, x2b_ref, aff_ref, kmem, vmem):
    i = pl.program_id(0)

    @pl.when(i == 0)
    def _():
        mb = mem_ref[...].astype(BF16)
        kmem[...] = jnp.dot(mb, wk_ref[...], preferred_element_type=F32).astype(BF16)
        vmem[...] = jnp.dot(mb, wv_ref[...], preferred_element_type=F32).astype(BF16)

    x1 = x_ref[...]
    qb = jnp.dot(x1.astype(BF16), wq_ref[...], preferred_element_type=F32).astype(BF16)
    outs = []
    for h in range(MEM_HEADS):
        cs = slice(h * MEM_HEAD_DIM, (h + 1) * MEM_HEAD_DIM)
        s = _dot_nt(qb[:, cs], kmem[:, cs]) * (MEM_HEAD_DIM ** -0.5)
        mx = jnp.max(s, axis=-1, keepdims=True)
        p = jnp.exp(s - mx)
        l = jnp.sum(p, axis=-1, keepdims=True)
        oh = jnp.dot(p.astype(BF16), vmem[:, cs], preferred_element_type=F32)
        outs.append(oh / l)
    o = jnp.concatenate(outs, axis=1).astype(BF16)
    y = jnp.dot(o, wo_ref[...], preferred_element_type=F32)
    x2 = _layer_norm(alpha * x1 + y, g_ref[...], be_ref[...])
    x2_ref[...] = x2
    hi = x2.astype(BF16)
    x2b_ref[...] = hi
    lo = (x2 - hi.astype(F32)).astype(BF16)
    wrh = wrh_ref[...]
    r1 = _dot_nt(jnp.concatenate([wrh, wrl_ref[...]], axis=0), hi)
    lg = r1[0:N_EXPERTS] + r1[N_EXPERTS:2 * N_EXPERTS] + _dot_nt(wrh, lo)
    mx = jnp.max(lg, axis=0, keepdims=True)
    p = jnp.exp(lg - mx)
    aff = p / jnp.sum(p, axis=0, keepdims=True)
    for j in range(aff_ref.shape[0]):
        aff_ref[j] = aff[:, j * LANES:(j + 1) * LANES]


def _xattn(x1, mem, wq_b, wk_b, wv_b, wo_b, g, b, wr_hi, wr_lo, alpha):
    S, D = x1.shape
    M = mem.shape[0]
    mw = MEM_HEADS * MEM_HEAD_DIM
    tm = TOK_BLOCK
    cur = lambda i: (i, 0)
    const2 = lambda i: (0, 0)
    full = lambda a: pl.BlockSpec(a.shape, const2)
    return pl.pallas_call(
        functools.partial(_xattn_kernel, alpha),
        grid=(S // tm,),
        in_specs=[pl.BlockSpec((tm, D), cur), full(mem), full(wq_b), full(wk_b),
                  full(wv_b), full(wo_b), full(g), full(b), full(wr_hi), full(wr_lo)],
        out_specs=[pl.BlockSpec((tm, D), cur), pl.BlockSpec((tm, D), cur),
                   pl.BlockSpec((tm // LANES, N_EXPERTS, LANES), lambda i: (i, 0, 0))],
        out_shape=[jax.ShapeDtypeStruct((S, D), F32),
                   jax.ShapeDtypeStruct((S, D), BF16),
                   jax.ShapeDtypeStruct((S // LANES, N_EXPERTS, LANES), F32)],
        scratch_shapes=[pltpu.VMEM((M, mw), BF16), pltpu.VMEM((M, mw), BF16)],
        compiler_params=pltpu.CompilerParams(
            dimension_semantics=("arbitrary",), vmem_limit_bytes=VMEM_LIMIT),
        name="xattn",
    )(x1, mem, wq_b, wk_b, wv_b, wo_b, g, b, wr_hi, wr_lo)


def _route_kernel(cap, aff_ref, loc_ref, b_ref, c_ref, incl_scr, tot_scr, sel_scr):
    nblk = aff_ref.shape[0]
    per_tile = ROUTE_TILE // LANES
    ntile = nblk // per_tile

    def count_ge(t):
        ge = jnp.where(aff_ref[...] >= t[None], 1.0, 0.0)
        return jnp.sum(jnp.sum(ge, axis=0), axis=1, keepdims=True)

    def coarse(it, bits):
        cand = bits | jnp.left_shift(jnp.int32(1), 30 - it)
        ok = count_ge(pltpu.bitcast(cand, F32)) >= cap
        return jnp.where(ok, cand, bits)

    def fine(it, lohi):
        lo, hi = lohi
        mid = lo + (hi - lo) * 0.5
        ok = count_ge(mid) >= cap
        return jnp.where(ok, mid, lo), jnp.where(ok, hi, mid)

    bits = lax.fori_loop(0, 31, coarse, jnp.zeros((N_EXPERTS, 1), I32))
    thr, _ = lax.fori_loop(
        0, 30, fine, (pltpu.bitcast(bits, F32), pltpu.bitcast(bits + 1, F32)))
    gt_all = jnp.where(aff_ref[...] > thr[None], 1.0, 0.0)
    n_gt = jnp.sum(jnp.sum(gt_all, axis=0), axis=1, keepdims=True)
    need = cap - n_gt

    ii = lax.broadcasted_iota(I32, (LANES, LANES), 0)
    jj = lax.broadcasted_iota(I32, (LANES, LANES), 1)
    tri = jnp.where(ii <= jj, 1.0, 0.0).astype(BF16)
    ones = jnp.ones((LANES, LANES), BF16)
    flat = (nblk * N_EXPERTS, LANES)
    blocks = (nblk, N_EXPERTS, LANES)

    def block_scans(flags):
        f2 = flags.reshape(flat).astype(BF16)
        incl_scr[...] = jnp.dot(f2, tri, preferred_element_type=F32).reshape(blocks)
        tot_scr[...] = jnp.dot(f2, ones, preferred_element_type=F32).reshape(blocks)

    block_scans(jnp.where(aff_ref[...] == thr[None], 1.0, 0.0))

    def select_block(j, run_eq):
        blk = aff_ref[j]
        tie_rank = run_eq + incl_scr[j]
        sel_scr[j] = (jnp.where(blk > thr, 1.0, 0.0)
                      + jnp.where(blk == thr, 1.0, 0.0) * jnp.where(tie_rank <= need, 1.0, 0.0))
        return run_eq + tot_scr[j]

    zero = jnp.zeros((N_EXPERTS, LANES), F32)
    lax.fori_loop(0, nblk, select_block, zero)
    block_scans(sel_scr[...])

    def tile_body(t, run):
        tile_run = zero
        for j in range(per_tile):
            idx = t * per_tile + j
            sel = sel_scr[idx]
            loc = tile_run + incl_scr[idx] - sel
            loc_ref[idx] = jnp.where(sel > 0.5, loc, -1.0).astype(I32)
            tile_run = tile_run + tot_scr[idx]
        b_ref[t] = run[:, 0:1].astype(I32)
        c_ref[t] = tile_run[:, 0:1].astype(I32)
        return run + tile_run

    lax.fori_loop(0, ntile, tile_body, zero)


def _route(aff3, cap):
    nblk = aff3.shape[0]
    ntile = nblk * LANES // ROUTE_TILE
    full3 = lambda a: pl.BlockSpec(a, lambda: (0, 0, 0))
    return pl.pallas_call(
        functools.partial(_route_kernel, cap),
        in_specs=[full3(aff3.shape)],
        out_specs=[full3(aff3.shape), full3((ntile, N_EXPERTS, 1)),
                   full3((ntile, N_EXPERTS, 1))],
        out_shape=[jax.ShapeDtypeStruct(aff3.shape, I32),
                   jax.ShapeDtypeStruct((ntile, N_EXPERTS, 1), I32),
                   jax.ShapeDtypeStruct((ntile, N_EXPERTS, 1), I32)],
        scratch_shapes=[pltpu.VMEM(aff3.shape, F32)] * 3,
        compiler_params=pltpu.CompilerParams(vmem_limit_bytes=VMEM_LIMIT),
        name="route",
    )(aff3)


def _segment_copies(b_sm, c_sm, t, make_copy, action):
    off = 0
    nbits = ROUTE_TILE.bit_length()

    def bit_copies(e, b, c, off, bits):
        for kbit in bits:
            n = 1 << kbit
            r0 = (c >> (kbit + 1)) << (kbit + 1)

            @pl.when((c & n) != 0)
            def _(r0=r0, n=n):
                action(make_copy(e, b + r0, off + r0, n))

    for e in range(N_EXPERTS):
        c = c_sm[t * N_EXPERTS + e]
        b = b_sm[t * N_EXPERTS + e]

        @pl.when(c >= (1 << RARE_BIT))
        def _(e=e, b=b, c=c, off=off):
            bit_copies(e, b, c, off, range(nbits - 1, RARE_BIT - 1, -1))

        bit_copies(e, b, c, off, range(RARE_BIT - 1, -1, -1))
        off = off + c
    return off


def _tile_rows(c_sm, t):
    rows = 0
    for e in range(N_EXPERTS):
        rows = rows + c_sm[t * N_EXPERTS + e]
    return rows


def _dispatch_kernel(cap, b_sm, c_sm, x_ref, loc_ref, xe_hbm, stg, sem):
    t = pl.program_id(0)
    nt = pl.num_programs(0)
    slot = t % 2

    offs = []
    off = 0
    for e in range(N_EXPERTS):
        offs.append(off)
        off = off + c_sm[t * N_EXPERTS + e]
    offs.append(off)
    nchunk = (off + SLOT_CHUNK - 1) // SLOT_CHUNK
    sub = lax.broadcasted_iota(I32, (SLOT_CHUNK, 1), 0)
    lane = lax.broadcasted_iota(I32, (1, LANES), 1)
    seg_lo = jnp.zeros((1, LANES), I32)
    seg_hi = jnp.zeros((1, LANES), I32)
    for e in range(N_EXPERTS):
        seg_lo = jnp.where(lane == e, offs[e], seg_lo)
        seg_hi = jnp.where(lane == e, offs[e + 1], seg_hi)
    seg_lo_f = seg_lo.astype(F32)
    locf = jnp.concatenate(
        [loc_ref[j] for j in range(ROUTE_TILE // LANES)], axis=1).astype(F32)
    locb = jnp.concatenate(
        [locf, jnp.zeros((LANES - N_EXPERTS, ROUTE_TILE), F32)], axis=0).astype(BF16)

    def chunk_body(m, carry):
        base = m * SLOT_CHUNK
        srow = sub + base
        inseg = jnp.where(srow >= seg_lo, jnp.where(srow < seg_hi, 1.0, 0.0), 0.0)
        rank = srow.astype(F32) - jnp.sum(inseg * seg_lo_f, axis=1, keepdims=True)
        want = jnp.dot(inseg.astype(BF16), locb, preferred_element_type=F32)
        p = jnp.where(want == rank, 1.0, 0.0)
        res = jnp.dot(p.astype(BF16), x_ref[...], preferred_element_type=F32)
        row0 = base * SUBLANES if isinstance(m, int) else pl.multiple_of(base * SUBLANES, SUBLANES)
        for s in range(SUBLANES):
            stg[slot, pl.ds(row0 + s, SLOT_CHUNK, stride=SUBLANES), :] = (
                res[:, s * LANES:(s + 1) * LANES])
        return carry

    for m in range(PEELED_CHUNKS):
        chunk_body(m, 0)
    lax.fori_loop(PEELED_CHUNKS, jnp.maximum(nchunk, PEELED_CHUNKS), chunk_body, 0)

    def copy_for(slot_):
        def make(e, dst_row, src_row, n):
            return pltpu.make_async_copy(
                stg.at[slot_, pl.ds(pl.multiple_of(src_row * SUBLANES, SUBLANES), n * SUBLANES)],
                xe_hbm.at[pl.ds(pl.multiple_of((e * cap + dst_row) * SUBLANES, SUBLANES),
                                n * SUBLANES)],
                sem.at[slot_])
        return make

    _segment_copies(b_sm, c_sm, t, copy_for(slot), lambda cp: cp.start())

    def wait_tile(t_, slot_):
        rows = _tile_rows(c_sm, t_)

        @pl.when(rows > 0)
        def _():
            n = rows * SUBLANES
            pltpu.make_async_copy(stg.at[slot_, pl.ds(0, n)], xe_hbm.at[pl.ds(0, n)],
                                  sem.at[slot_]).wait()

    @pl.when(t > 0)
    def _():
        wait_tile(t - 1, 1 - slot)

    @pl.when(t == nt - 1)
    def _():
        wait_tile(t, slot)


def _dispatch(b_flat, c_flat, x2b, loc3, cap):
    S, D = x2b.shape
    nt = S // ROUTE_TILE
    per_tile = ROUTE_TILE // LANES
    max_rows = N_EXPERTS * ROUTE_TILE
    return pl.pallas_call(
        functools.partial(_dispatch_kernel, cap),
        grid_spec=pltpu.PrefetchScalarGridSpec(
            num_scalar_prefetch=2,
            grid=(nt,),
            in_specs=[pl.BlockSpec((ROUTE_TILE, D), lambda t, b, c: (t, 0)),
                      pl.BlockSpec((per_tile, N_EXPERTS, LANES), lambda t, b, c: (t, 0, 0))],
            out_specs=pl.BlockSpec(memory_space=pl.ANY),
            scratch_shapes=[pltpu.VMEM((2, max_rows * SUBLANES, LANES), F32),
                            pltpu.SemaphoreType.DMA((2,))]),
        out_shape=jax.ShapeDtypeStruct((N_EXPERTS * cap * SUBLANES, LANES), F32),
        compiler_params=pltpu.CompilerParams(
            dimension_semantics=("arbitrary",), vmem_limit_bytes=VMEM_LIMIT),
        name="dispatch",
    )(b_flat, c_flat, x2b, loc3)


def _ffn_kernel(xe_ref, wg_ref, wu_ref, wd_ref, wr_ref, o_ref, xb, acc, gate):
    e = pl.program_id(0)
    f = pl.program_id(2)
    tm = xb.shape[0]

    @pl.when(f == 0)
    def _():
        for s in range(SUBLANES):
            xb[:, s * LANES:(s + 1) * LANES] = (
                xe_ref[pl.ds(s, tm, stride=SUBLANES), :].astype(BF16))
        lg = jnp.dot(xb[...], wr_ref[...], preferred_element_type=F32)
        lane = lax.broadcasted_iota(I32, (1, LANES), 1)
        lg = jnp.where(lane < N_EXPERTS, lg, NEG)
        p = jnp.exp(lg - jnp.max(lg, axis=-1, keepdims=True))
        mine = jnp.sum(jnp.where(lane == e, p, 0.0), axis=-1, keepdims=True)
        gate[...] = mine / jnp.sum(p, axis=-1, keepdims=True)
        acc[...] = jnp.zeros_like(acc)

    x = xb[...]
    g = jnp.dot(x, wg_ref[...].astype(BF16), preferred_element_type=F32)
    u = jnp.dot(x, wu_ref[...].astype(BF16), preferred_element_type=F32)
    h = (g * jax.nn.sigmoid(g) * u).astype(BF16)
    acc[...] += jnp.dot(h, wd_ref[...].astype(BF16), preferred_element_type=F32)

    @pl.when(f == pl.num_programs(2) - 1)
    def _():
        y = acc[...] * gate[...]
        for s in range(SUBLANES):
            o_ref[pl.ds(s, tm, stride=SUBLANES), :] = y[:, s * LANES:(s + 1) * LANES]


def _ffn(xe2d, w_gate, w_up, w_down, wr_pad, cap):
    E, D, FF = w_gate.shape
    tm, tf = FFN_TM, FFN_TF
    mt = cap // tm
    rows = lambda e, m, f: (e * mt + m, 0)
    return pl.pallas_call(
        _ffn_kernel,
        grid=(E, mt, FF // tf),
        in_specs=[pl.BlockSpec((tm * SUBLANES, LANES), rows),
                  pl.BlockSpec((None, D, tf), lambda e, m, f: (e, 0, f)),
                  pl.BlockSpec((None, D, tf), lambda e, m, f: (e, 0, f)),
                  pl.BlockSpec((None, tf, D), lambda e, m, f: (e, f, 0)),
                  pl.BlockSpec(wr_pad.shape, lambda e, m, f: (0, 0))],
        out_specs=pl.BlockSpec((tm * SUBLANES, LANES), rows),
        out_shape=jax.ShapeDtypeStruct(xe2d.shape, F32),
        scratch_shapes=[pltpu.VMEM((tm, D), BF16), pltpu.VMEM((tm, D), F32),
                        pltpu.VMEM((tm, 1), F32)],
        compiler_params=pltpu.CompilerParams(
            dimension_semantics=("arbitrary", "arbitrary", "arbitrary"),
            vmem_limit_bytes=VMEM_LIMIT),
        name="ffn",
    )(xe2d, w_gate, w_up, w_down, wr_pad)


def _combine_kernel(cap, alpha, b_sm, c_sm, x_ref, loc_ref, g_ref, be_ref, ye_hbm,
                    o_ref, ybuf, sem, hl_scr, z_scr):
    t = pl.program_id(0)
    nt = pl.num_programs(0)
    slot = t % 2

    def copy_for(slot_):
        def make(e, src_row, dst_row, n):
            return pltpu.make_async_copy(
                ye_hbm.at[pl.ds(pl.multiple_of((e * cap + src_row) * SUBLANES, SUBLANES),
                                n * SUBLANES)],
                ybuf.at[slot_, pl.ds(pl.multiple_of(dst_row * SUBLANES, SUBLANES), n * SUBLANES)],
                sem.at[slot_])
        return make

    @pl.when(t == 0)
    def _():
        ybuf[...] = jnp.zeros_like(ybuf)
        _segment_copies(b_sm, c_sm, t, copy_for(slot), lambda cp: cp.start())

    @pl.when(t < nt - 1)
    def _():
        _segment_copies(b_sm, c_sm, t + 1, copy_for(1 - slot), lambda cp: cp.start())

    rows_in = _tile_rows(c_sm, t)

    @pl.when(rows_in > 0)
    def _():
        n = rows_in * SUBLANES
        pltpu.make_async_copy(ye_hbm.at[pl.ds(0, n)], ybuf.at[slot, pl.ds(0, n)],
                              sem.at[slot]).wait()

    offs = []
    off = 0
    for e in range(N_EXPERTS):
        offs.append(off)
        off = off + c_sm[t * N_EXPERTS + e]
    offs.append(off)
    total = off
    nchunk = (total + SLOT_CHUNK - 1) // SLOT_CHUNK
    lane = lax.broadcasted_iota(I32, (1, SLOT_CHUNK), 1)
    sub = lax.broadcasted_iota(I32, (SLOT_CHUNK, 1), 0)
    eid = lax.broadcasted_iota(I32, (N_EXPERTS, 1), 0)
    seg_lo = jnp.zeros((N_EXPERTS, 1), I32)
    seg_hi = jnp.zeros((N_EXPERTS, 1), I32)
    for e in range(N_EXPERTS):
        seg_lo = jnp.where(eid == e, offs[e], seg_lo)
        seg_hi = jnp.where(eid == e, offs[e + 1], seg_hi)
    seg_lo_f = seg_lo.astype(F32)
    locb = loc_ref[...].astype(F32).astype(BF16)

    z_scr[...] = alpha * x_ref[...]

    def chunk_body(m, carry):
        base = m * SLOT_CHUNK
        srow = lane + base
        inseg = jnp.where(srow >= seg_lo, jnp.where(srow < seg_hi, 1.0, 0.0), 0.0)
        rank = srow.astype(F32) - jnp.sum(inseg * seg_lo_f, axis=0, keepdims=True)
        want = jnp.dot(locb, inseg.astype(BF16), preferred_element_type=F32)
        ptb = jnp.where(want == rank, 1.0, 0.0).astype(BF16)
        row0 = base * SUBLANES if isinstance(m, int) else pl.multiple_of(base * SUBLANES, SUBLANES)
        yc = jnp.concatenate(
            [ybuf[slot, pl.ds(row0 + s, SLOT_CHUNK, stride=SUBLANES), :]
             for s in range(SUBLANES)], axis=1)
        yc = jnp.where(sub < total - base, yc, 0.0)
        hi = yc.astype(BF16)
        hl = hl_scr.at[m % PEELED_CHUNKS]
        hl[0:SLOT_CHUNK] = hi
        hl[SLOT_CHUNK:2 * SLOT_CHUNK] = (yc - hi.astype(F32)).astype(BF16)
        z_scr[...] += jnp.dot(jnp.concatenate([ptb, ptb], axis=1), hl[...],
                              preferred_element_type=F32)
        return carry

    for m in range(PEELED_CHUNKS):
        chunk_body(m, 0)
    lax.fori_loop(PEELED_CHUNKS, jnp.maximum(nchunk, PEELED_CHUNKS), chunk_body, 0)

    o_ref[...] = _layer_norm(z_scr[...], g_ref[...], be_ref[...])


def _combine(b_flat, c_flat, x2, loc_t, g, b, ye2d, cap, alpha):
    S, D = x2.shape
    nt = S // ROUTE_TILE
    max_rows = N_EXPERTS * ROUTE_TILE
    cur = lambda t, b_, c_: (t, 0)
    const2 = lambda t, b_, c_: (0, 0)
    return pl.pallas_call(
        functools.partial(_combine_kernel, cap, alpha),
        grid_spec=pltpu.PrefetchScalarGridSpec(
            num_scalar_prefetch=2,
            grid=(nt,),
            in_specs=[pl.BlockSpec((ROUTE_TILE, D), cur),
                      pl.BlockSpec((ROUTE_TILE, N_EXPERTS), cur),
                      pl.BlockSpec((1, D), const2),
                      pl.BlockSpec((1, D), const2),
                      pl.BlockSpec(memory_space=pl.ANY)],
            out_specs=pl.BlockSpec((ROUTE_TILE, D), cur),
            scratch_shapes=[pltpu.VMEM((2, max_rows * SUBLANES, LANES), F32),
                            pltpu.SemaphoreType.DMA((2,)),
                            pltpu.VMEM((PEELED_CHUNKS, 2 * SLOT_CHUNK, D), BF16),
                            pltpu.VMEM((ROUTE_TILE, D), F32)]),
        out_shape=jax.ShapeDtypeStruct((S, D), F32),
        compiler_params=pltpu.CompilerParams(
            dimension_semantics=("arbitrary",), vmem_limit_bytes=VMEM_LIMIT),
        name="combine",
    )(b_flat, c_flat, x2, loc_t, g, b, ye2d)


def _layer(x, mem, w_in, rpb, conv_w, w_mix_out, ln1_g, ln1_b, wq, wk, wv, wo,
           ln2_g, ln2_b, w_router, w_gate, w_up, w_down, ln3_g, ln3_b, alpha):
    S, D = x.shape
    cap = CAPACITY_FACTOR * S // N_EXPERTS
    assert S % TOK_BLOCK == 0 and S % PROJ_TM == 0 and S // GRID_W >= 2 * ROW_BLOCK
    assert cap % FFN_TM == 0 and cap <= S
    row = lambda a: a.reshape(1, -1)

    q, k, v, bg, u = _proj(x, w_in.astype(BF16))
    x1 = _mixer(x, q, k, v, bg, u, w_mix_out.astype(BF16), rpb,
                conv_w, row(ln1_g), row(ln1_b), alpha)

    wr_t = w_router.T
    wr_hi = wr_t.astype(BF16)
    wr_lo = (wr_t - wr_hi.astype(F32)).astype(BF16)
    x2, x2b, aff3 = _xattn(x1, mem, wq.astype(BF16), wk.astype(BF16), wv.astype(BF16),
                           wo.astype(BF16), row(ln2_g), row(ln2_b), wr_hi, wr_lo, alpha)

    loc3, b3, c3 = _route(aff3, cap)
    b_flat = b3.reshape(-1)
    c_flat = c3.reshape(-1)
    loc_t = loc3.transpose(0, 2, 1).reshape(S, N_EXPERTS)

    xe2d = _dispatch(b_flat, c_flat, x2b, loc3, cap)
    wr_pad = jnp.pad(w_router.astype(BF16), ((0, 0), (0, LANES - N_EXPERTS)))
    ye2d = _ffn(xe2d, w_gate, w_up, w_down, wr_pad, cap)
    return _combine(b_flat, c_flat, x2, loc_t, row(ln3_g), row(ln3_b), ye2d, cap, alpha)


def kernel(x, mem, w_in, na_rpb, conv_w, w_mix_out, ln1_g, ln1_b, w_mem_q, w_mem_k,
           w_mem_v, w_mem_out, ln2_g, ln2_b, w_router, w_exp_gate, w_exp_up,
           w_exp_down, ln3_g, ln3_b):
    depth = w_in.shape[0]
    alpha = (2 * depth) ** 0.25
    outs = []
    for bi in range(x.shape[0]):
        xb = x[bi]
        for l in range(depth):
            xb = _layer(xb, mem[bi], w_in[l], na_rpb[l], conv_w[l], w_mix_out[l],
                        ln1_g[l], ln1_b[l], w_mem_q[l], w_mem_k[l], w_mem_v[l],
                        w_mem_out[l], ln2_g[l], ln2_b[l], w_router[l], w_exp_gate[l],
                        w_exp_up[l], w_exp_down[l], ln3_g[l], ln3_b[l], alpha)
        outs.append(xb)
    return jnp.stack(outs)
```

```python
import functools

import jax
import jax.numpy as jnp
from jax import lax
from jax.experimental import pallas as pl
from jax.experimental.pallas import tpu as pltpu

F32 = jnp.float32
BF16 = jnp.bfloat16
I32 = jnp.int32

GRID_W = 64
HEAD_DIM = 64
NA_HEADS = 8
NA_WIDTH = NA_HEADS * HEAD_DIM
WIN_ROWS = 8
WIN_COLS = 16
MEM_HEADS = 4
MEM_HEAD_DIM = 128
N_EXPERTS = 16
CAPACITY_FACTOR = 2
LN_EPS = 1e-5
NEG = -1e30

LANES = 128
SUBLANES = 8
VMEM_LIMIT = 56 * 1024 * 1024

ROW_BLOCK = 8
TOK_BLOCK = ROW_BLOCK * GRID_W
PROJ_TM = 1024
ROUTE_TILE = 256
SLOT_CHUNK = 256
SM_ROWS = 16
RARE_BIT = 6
PEELED_CHUNKS = 2
FFN_TM = 1024
FFN_TF = 512


def _layer_norm(z, g, b):
    mu = jnp.mean(z, axis=-1, keepdims=True)
    zc = z - mu
    var = jnp.mean(zc * zc, axis=-1, keepdims=True)
    return zc * lax.rsqrt(var + LN_EPS) * g + b


def _dot_nt(a, b):
    return lax.dot_general(a, b, (((1,), (1,)), ((), ())),
                           preferred_element_type=F32)


def _proj_kernel(x_ref, w_ref, q_ref, k_ref, v_ref, b_ref, u_ref):
    xb = x_ref[...].astype(BF16)
    nw = NA_WIDTH

    def mm(c0):
        return jnp.dot(xb, w_ref[:, c0:c0 + nw], preferred_element_type=F32)

    q_ref[...] = (mm(0) * (HEAD_DIM ** -0.5)).astype(BF16)
    k_ref[...] = mm(nw).astype(BF16)
    v_ref[...] = mm(2 * nw).astype(BF16)
    b_ref[...] = mm(3 * nw)
    u_ref[...] = mm(4 * nw) * mm(5 * nw)


def _proj(x, w_in_b):
    S, D = x.shape
    nw = NA_WIDTH
    tm = PROJ_TM
    blk = lambda i: (i, 0)
    return pl.pallas_call(
        _proj_kernel,
        grid=(S // tm,),
        in_specs=[pl.BlockSpec((tm, D), blk),
                  pl.BlockSpec((D, 6 * nw), lambda i: (0, 0))],
        out_specs=[pl.BlockSpec((tm, nw), blk)] * 5,
        out_shape=[jax.ShapeDtypeStruct((S, nw), BF16)] * 3
        + [jax.ShapeDtypeStruct((S, nw), F32)] * 2,
        compiler_params=pltpu.CompilerParams(
            dimension_semantics=("arbitrary",), vmem_limit_bytes=VMEM_LIMIT),
        name="proj",
    )(x, w_in_b)


def _build_na_bias(rpb_ref, bias_ref):
    shape = (GRID_W, LANES)
    qc = lax.broadcasted_iota(I32, shape, 0)
    ln = lax.broadcasted_iota(I32, shape, 1)
    kc = ln % GRID_W
    c_start = jnp.clip(qc - WIN_COLS // 2, 0, GRID_W - WIN_COLS)
    base = LANES - (WIN_COLS - 1)

    def canvas(h, d):
        both = rpb_ref[h, d:d + 1, :] + pltpu.roll(rpb_ref[h, d + 1:d + 2, :], GRID_W, 1)
        t = pltpu.roll(pltpu.roll(jnp.broadcast_to(both, shape), base, 1), 0, 1,
                       stride=1, stride_axis=0)
        return jnp.where(kc >= c_start, jnp.where(kc < c_start + WIN_COLS, t, NEG), NEG)

    for par in range(2):
        for hp in range(NA_HEADS // 2):
            for m in range(WIN_ROWS):
                d = 2 * m + par
                bias_ref[par, hp, m, 0:GRID_W] = canvas(2 * hp, d)
                bias_ref[par, hp, m, GRID_W:2 * GRID_W] = canvas(2 * hp + 1, d)


def _mixer_kernel(rows, alpha,
                  q_ref, kp_ref, kc_ref, kn_ref, vp_ref, vc_ref, vn_ref,
                  b_ref, u_ref, up_ref, un_ref, x_ref, wout_ref, rpb_ref,
                  cw_ref, g_ref, be_ref, o_ref, kbuf, vbuf, mix, bias_ref,
                  s_scr0, s_scr1, p_scr0, p_scr1):
    i = pl.program_id(0)
    nb = pl.num_programs(0)
    tb = TOK_BLOCK
    win = WIN_ROWS * GRID_W

    @pl.when(i == 0)
    def _():
        _build_na_bias(rpb_ref, bias_ref)

    kbuf[0:tb] = kp_ref[...]
    kbuf[tb:2 * tb] = kc_ref[...]
    kbuf[2 * tb:3 * tb] = kn_ref[...]
    vbuf[0:tb] = vp_ref[...]
    vbuf[tb:2 * tb] = vc_ref[...]
    vbuf[2 * tb:3 * tb] = vn_ref[...]

    lane = lax.broadcasted_iota(I32, (1, LANES), 1)
    first_head = lane < HEAD_DIM
    m_lo = jnp.where(first_head, 1.0, 0.0).astype(BF16)
    m_hi = jnp.where(first_head, 0.0, 1.0).astype(BF16)

    def window(r):
        grow = i * ROW_BLOCK + r
        start = jnp.clip(grow - WIN_ROWS // 2, 0, rows - WIN_ROWS)
        w0 = pl.multiple_of((start - (i - 1) * ROW_BLOCK) * GRID_W, GRID_W)
        d0 = start - grow + WIN_ROWS
        return w0, d0 % 2, d0 // 2

    def scores(r, s_out):
        w0, _, _ = window(r)
        q0 = r * GRID_W if isinstance(r, int) else pl.multiple_of(r * GRID_W, GRID_W)
        for hp in range(NA_HEADS // 2):
            cs = slice(hp * LANES, (hp + 1) * LANES)
            qp = q_ref[pl.ds(q0, GRID_W), cs]
            lhs = jnp.concatenate([qp * m_lo, qp * m_hi], axis=0)
            kw = kbuf[pl.ds(w0, win), cs]
            s_out[hp] = _dot_nt(lhs, kw)

    def probs(r, s_in, p_out):
        _, par, m0 = window(r)
        for hp in range(NA_HEADS // 2):
            for rb in range(2 * GRID_W // SM_ROWS):
                rs = slice(rb * SM_ROWS, (rb + 1) * SM_ROWS)
                bias = jnp.concatenate(
                    [bias_ref[par, hp, m0 + j, rs, :] for j in range(WIN_ROWS // 2)], axis=1)
                s = s_in[hp, rs, :] + bias
                p = jnp.exp(s - jnp.max(s, axis=-1, keepdims=True))
                p = p * (1.0 / jnp.sum(p, axis=-1, keepdims=True))
                p_out[hp, rs, :] = p.astype(BF16)

    def values(r, p_in):
        w0, _, _ = window(r)
        q0 = r * GRID_W if isinstance(r, int) else pl.multiple_of(r * GRID_W, GRID_W)
        for hp in range(NA_HEADS // 2):
            cs = slice(hp * LANES, (hp + 1) * LANES)
            vw = vbuf[pl.ds(w0, win), cs]
            o2 = jnp.dot(p_in[hp], vw, preferred_element_type=F32)
            y = jnp.where(first_head, o2[0:GRID_W], o2[GRID_W:2 * GRID_W])
            mix[pl.ds(q0, GRID_W), cs] = y.astype(BF16)

    s_buf = (s_scr0, s_scr1)
    p_buf = (p_scr0, p_scr1)

    def steady(k, carry):
        for half in range(2):
            r = 1 + 2 * k + half
            values(r - 1, p_buf[half])
            probs(r, s_buf[1 - half], p_buf[1 - half])
            scores(r + 1, s_buf[half])
        return carry

    scores(0, s_buf[0])
    probs(0, s_buf[0], p_buf[0])
    scores(1, s_buf[1])
    lax.fori_loop(0, (ROW_BLOCK - 2) // 2, steady, 0)
    values(ROW_BLOCK - 2, p_buf[0])
    probs(ROW_BLOCK - 1, s_buf[1], p_buf[1])
    values(ROW_BLOCK - 1, p_buf[1])

    u = u_ref[...]
    prev_row = up_ref[SUBLANES - 1:SUBLANES, :] * jnp.where(i > 0, 1.0, 0.0)
    next_row = un_ref[0:1, :] * jnp.where(i < nb - 1, 1.0, 0.0)
    rid = lax.broadcasted_iota(I32, (tb, 1), 0)
    um1 = jnp.where(rid == 0, prev_row, pltpu.roll(u, 1, axis=0))
    up1 = jnp.where(rid == tb - 1, next_row, pltpu.roll(u, tb - 1, axis=0))
    cw = cw_ref[...]
    yc = b_ref[...] * (cw[0:1] * um1 + cw[1:2] * u + cw[2:3] * up1)
    mix[:, NA_WIDTH:] = yc.astype(BF16)

    y = jnp.dot(mix[...], wout_ref[...], preferred_element_type=F32)
    z = alpha * x_ref[...] + y
    o_ref[...] = _layer_norm(z, g_ref[...], be_ref[...])


def _mixer(x, q, k, v, bg, u, w_out_b, rpb, conv_w, g, b, alpha):
    S, D = x.shape
    rpb_pad = jnp.pad(rpb.astype(F32), ((0, 0), (1, 1), (0, LANES - rpb.shape[2])))
    nw = NA_WIDTH
    tb = TOK_BLOCK
    nb = S // tb
    rows = S // GRID_W
    cur = lambda i: (i, 0)
    prv = lambda i: (jnp.maximum(i - 1, 0), 0)
    nxt = lambda i: (jnp.minimum(i + 1, nb - 1), 0)
    halo = tb // SUBLANES
    hprev = lambda i: (jnp.maximum(i * halo - 1, 0), 0)
    hnext = lambda i: (jnp.minimum((i + 1) * halo, S // SUBLANES - 1), 0)
    const2 = lambda i: (0, 0)
    kv = lambda im: pl.BlockSpec((tb, nw), im)
    return pl.pallas_call(
        functools.partial(_mixer_kernel, rows, alpha),
        grid=(nb,),
        in_specs=[kv(cur), kv(prv), kv(cur), kv(nxt), kv(prv), kv(cur), kv(nxt),
                  kv(cur), kv(cur),
                  pl.BlockSpec((SUBLANES, nw), hprev),
                  pl.BlockSpec((SUBLANES, nw), hnext),
                  pl.BlockSpec((tb, D), cur),
                  pl.BlockSpec((D, D), const2),
                  pl.BlockSpec(rpb_pad.shape, lambda i: (0, 0, 0)),
                  pl.BlockSpec(conv_w.shape, const2),
                  pl.BlockSpec((1, D), const2),
                  pl.BlockSpec((1, D), const2)],
        out_specs=pl.BlockSpec((tb, D), cur),
        out_shape=jax.ShapeDtypeStruct((S, D), F32),
        scratch_shapes=[pltpu.VMEM((3 * tb, nw), BF16),
                        pltpu.VMEM((3 * tb, nw), BF16),
                        pltpu.VMEM((tb, D), BF16),
                        pltpu.VMEM((2, NA_HEADS // 2, WIN_ROWS, 2 * GRID_W, LANES), F32),
                        pltpu.VMEM((NA_HEADS // 2, 2 * GRID_W, WIN_ROWS * GRID_W), F32),
                        pltpu.VMEM((NA_HEADS // 2, 2 * GRID_W, WIN_ROWS * GRID_W), F32),
                        pltpu.VMEM((NA_HEADS // 2, 2 * GRID_W, WIN_ROWS * GRID_W), BF16),
                        pltpu.VMEM((NA_HEADS // 2, 2 * GRID_W, WIN_ROWS * GRID_W), BF16)],
        compiler_params=pltpu.CompilerParams(
            dimension_semantics=("arbitrary",), vmem_limit_bytes=VMEM_LIMIT),
        name="mixer",
    )(q, k, k, k, v, v, v, bg, u, u, u, x, w_out_b, rpb_pad, conv_w, g, b)


def _xattn_kernel(alpha, x_ref, mem_ref, wq_ref, wk_ref, wv_ref, wo_ref,
                  g_ref, be_ref, wrh_ref, wrl_ref,
                  x2_ref, x2b_ref, aff_ref, kmem, vmem):
    i = pl.program_id(0)

    @pl.when(i == 0)
    def _():
        mb = mem_ref[...].astype(BF16)
        kmem[...] = jnp.dot(mb, wk_ref[...], preferred_element_type=F32).astype(BF16)
        vmem[...] = jnp.dot(mb, wv_ref[...], preferred_element_type=F32).astype(BF16)

    x1 = x_ref[...]
    qb = jnp.dot(x1.astype(BF16), wq_ref[...], preferred_element_type=F32).astype(BF16)
    outs = []
    for h in range(MEM_HEADS):
        cs = slice(h * MEM_HEAD_DIM, (h + 1) * MEM_HEAD_DIM)
        s = _dot_nt(qb[:, cs], kmem[:, cs]) * (MEM_HEAD_DIM ** -0.5)
        mx = jnp.max(s, axis=-1, keepdims=True)
        p = jnp.exp(s - mx)
        l = jnp.sum(p, axis=-1, keepdims=True)
        oh = jnp.dot(p.astype(BF16), vmem[:, cs], preferred_element_type=F32)
        outs.append(oh / l)
    o = jnp.concatenate(outs, axis=1).astype(BF16)
    y = jnp.dot(o, wo_ref[...], preferred_element_type=F32)
    x2 = _layer_norm(alpha * x1 + y, g_ref[...], be_ref[...])
    x2_ref[...] = x2
    hi = x2.astype(BF16)
    x2b_ref[...] = hi
    lo = (x2 - hi.astype(F32)).astype(BF16)
    wrh = wrh_ref[...]
    r1 = _dot_nt(jnp.concatenate([wrh, wrl_ref[...]], axis=0), hi)
    lg = r1[0:N_EXPERTS] + r1[N_EXPERTS:2 * N_EXPERTS] + _dot_nt(wrh, lo)
    mx = jnp.max(lg, axis=0, keepdims=True)
    p = jnp.exp(lg - mx)
    aff = p / jnp.sum(p, axis=0, keepdims=True)
    for j in range(aff_ref.shape[0]):
        aff_ref[j] = aff[:, j * LANES:(j + 1) * LANES]


def _xattn(x1, mem, wq_b, wk_b, wv_b, wo_b, g, b, wr_hi, wr_lo, alpha):
    S, D = x1.shape
    M = mem.shape[0]
    mw = MEM_HEADS * MEM_HEAD_DIM
    tm = TOK_BLOCK
    cur = lambda i: (i, 0)
    const2 = lambda i: (0, 0)
    full = lambda a: pl.BlockSpec(a.shape, const2)
    return pl.pallas_call(
        functools.partial(_xattn_kernel, alpha),
        grid=(S // tm,),
        in_specs=[pl.BlockSpec((tm, D), cur), full(mem), full(wq_b), full(wk_b),
                  full(wv_b), full(wo_b), full(g), full(b), full(wr_hi), full(wr_lo)],
        out_specs=[pl.BlockSpec((tm, D), cur), pl.BlockSpec((tm, D), cur),
                   pl.BlockSpec((tm // LANES, N_EXPERTS, LANES), lambda i: (i, 0, 0))],
        out_shape=[jax.ShapeDtypeStruct((S, D), F32),
                   jax.ShapeDtypeStruct((S, D), BF16),
                   jax.ShapeDtypeStruct((S // LANES, N_EXPERTS, LANES), F32)],
        scratch_shapes=[pltpu.VMEM((M, mw), BF16), pltpu.VMEM((M, mw), BF16)],
        compiler_params=pltpu.CompilerParams(
            dimension_semantics=("arbitrary",), vmem_limit_bytes=VMEM_LIMIT),
        name="xattn",
    )(x1, mem, wq_b, wk_b, wv_b, wo_b, g, b, wr_hi, wr_lo)


def _route_kernel(cap, aff_ref, loc_ref, b_ref, c_ref, incl_scr, tot_scr, sel_scr):
    nblk = aff_ref.shape[0]
    per_tile = ROUTE_TILE // LANES
    ntile = nblk // per_tile

    def count_ge(t):
        ge = jnp.where(aff_ref[...] >= t[None], 1.0, 0.0)
        return jnp.sum(jnp.sum(ge, axis=0), axis=1, keepdims=True)

    def coarse(it, bits):
        cand = bits | jnp.left_shift(jnp.int32(1), 30 - it)
        ok = count_ge(pltpu.bitcast(cand, F32)) >= cap
        return jnp.where(ok, cand, bits)

    def fine(it, lohi):
        lo, hi = lohi
        mid = lo + (hi - lo) * 0.5
        ok = count_ge(mid) >= cap
        return jnp.where(ok, mid, lo), jnp.where(ok, hi, mid)

    bits = lax.fori_loop(0, 31, coarse, jnp.zeros((N_EXPERTS, 1), I32))
    thr, _ = lax.fori_loop(
        0, 30, fine, (pltpu.bitcast(bits, F32), pltpu.bitcast(bits + 1, F32)))
    gt_all = jnp.where(aff_ref[...] > thr[None], 1.0, 0.0)
    n_gt = jnp.sum(jnp.sum(gt_all, axis=0), axis=1, keepdims=True)
    need = cap - n_gt

    ii = lax.broadcasted_iota(I32, (LANES, LANES), 0)
    jj = lax.broadcasted_iota(I32, (LANES, LANES), 1)
    tri = jnp.where(ii <= jj, 1.0, 0.0).astype(BF16)
    ones = jnp.ones((LANES, LANES), BF16)
    flat = (nblk * N_EXPERTS, LANES)
    blocks = (nblk, N_EXPERTS, LANES)

    def block_scans(flags):
        f2 = flags.reshape(flat).astype(BF16)
        incl_scr[...] = jnp.dot(f2, tri, preferred_element_type=F32).reshape(blocks)
        tot_scr[...] = jnp.dot(f2, ones, preferred_element_type=F32).reshape(blocks)

    block_scans(jnp.where(aff_ref[...] == thr[None], 1.0, 0.0))

    def select_block(j, run_eq):
        blk = aff_ref[j]
        tie_rank = run_eq + incl_scr[j]
        sel_scr[j] = (jnp.where(blk > thr, 1.0, 0.0)
                      + jnp.where(blk == thr, 1.0, 0.0) * jnp.where(tie_rank <= need, 1.0, 0.0))
        return run_eq + tot_scr[j]

    zero = jnp.zeros((N_EXPERTS, LANES), F32)
    lax.fori_loop(0, nblk, select_block, zero)
    block_scans(sel_scr[...])

    def tile_body(t, run):
        tile_run = zero
        for j in range(per_tile):
            idx = t * per_tile + j
            sel = sel_scr[idx]
            loc = tile_run + incl_scr[idx] - sel
            loc_ref[idx] = jnp.where(sel > 0.5, loc, -1.0).astype(I32)
            tile_run = tile_run + tot_scr[idx]
        b_ref[t] = run[:, 0:1].astype(I32)
        c_ref[t] = tile_run[:, 0:1].astype(I32)
        return run + tile_run

    lax.fori_loop(0, ntile, tile_body, zero)


def _route(aff3, cap):
    nblk = aff3.shape[0]
    ntile = nblk * LANES // ROUTE_TILE
    full3 = lambda a: pl.BlockSpec(a, lambda: (0, 0, 0))
    return pl.pallas_call(
        functools.partial(_route_kernel, cap),
        in_specs=[full3(aff3.shape)],
        out_specs=[full3(aff3.shape), full3((ntile, N_EXPERTS, 1)),
                   full3((ntile, N_EXPERTS, 1))],
        out_shape=[jax.ShapeDtypeStruct(aff3.shape, I32),
                   jax.ShapeDtypeStruct((ntile, N_EXPERTS, 1), I32),
                   jax.ShapeDtypeStruct((ntile, N_EXPERTS, 1), I32)],
        scratch_shapes=[pltpu.VMEM(aff3.shape, F32)] * 3,
        compiler_params=pltpu.CompilerParams(vmem_limit_bytes=VMEM_LIMIT),
        name="route",
    )(aff3)


def _segment_copies(b_sm, c_sm, t, make_copy, action):
    off = 0
    nbits = ROUTE_TILE.bit_length()

    def bit_copies(e, b, c, off, bits):
        for kbit in bits:
            n = 1 << kbit
            r0 = (c >> (kbit + 1)) << (kbit + 1)

            @pl.when((c & n) != 0)
            def _(r0=r0, n=n):
                action(make_copy(e, b + r0, off + r0, n))

    for e in range(N_EXPERTS):
        c = c_sm[t * N_EXPERTS + e]
        b = b_sm[t * N_EXPERTS + e]

        @pl.when(c >= (1 << RARE_BIT))
        def _(e=e, b=b, c=c, off=off):
            bit_copies(e, b, c, off, range(nbits - 1, RARE_BIT - 1, -1))

        bit_copies(e, b, c, off, range(RARE_BIT - 1, -1, -1))
        off = off + c
    return off


def _tile_rows(c_sm, t):
    rows = 0
    for e in range(N_EXPERTS):
        rows = rows + c_sm[t * N_EXPERTS + e]
    return rows


def _dispatch_kernel(cap, b_sm, c_sm, x_ref, loc_ref, xe_hbm, stg, sem):
    t = pl.program_id(0)
    nt = pl.num_programs(0)
    slot = t % 2

    offs = []
    off = 0
    for e in range(N_EXPERTS):
        offs.append(off)
        off = off + c_sm[t * N_EXPERTS + e]
    offs.append(off)
    nchunk = (off + SLOT_CHUNK - 1) // SLOT_CHUNK
    sub = lax.broadcasted_iota(I32, (SLOT_CHUNK, 1), 0)
    lane = lax.broadcasted_iota(I32, (1, LANES), 1)
    seg_lo = jnp.zeros((1, LANES), I32)
    seg_hi = jnp.zeros((1, LANES), I32)
    for e in range(N_EXPERTS):
        seg_lo = jnp.where(lane == e, offs[e], seg_lo)
        seg_hi = jnp.where(lane == e, offs[e + 1], seg_hi)
    seg_lo_f = seg_lo.astype(F32)
    locf = jnp.concatenate(
        [loc_ref[j] for j in range(ROUTE_TILE // LANES)], axis=1).astype(F32)
    locb = jnp.concatenate(
        [locf, jnp.zeros((LANES - N_EXPERTS, ROUTE_TILE), F32)], axis=0).astype(BF16)

    def chunk_body(m, carry):
        base = m * SLOT_CHUNK
        srow = sub + base
        inseg = jnp.where(srow >= seg_lo, jnp.where(srow < seg_hi, 1.0, 0.0), 0.0)
        rank = srow.astype(F32) - jnp.sum(inseg * seg_lo_f, axis=1, keepdims=True)
        want = jnp.dot(inseg.astype(BF16), locb, preferred_element_type=F32)
        p = jnp.where(want == rank, 1.0, 0.0)
        res = jnp.dot(p.astype(BF16), x_ref[...], preferred_element_type=F32)
        row0 = base * SUBLANES if isinstance(m, int) else pl.multiple_of(base * SUBLANES, SUBLANES)
        for s in range(SUBLANES):
            stg[slot, pl.ds(row0 + s, SLOT_CHUNK, stride=SUBLANES), :] = (
                res[:, s * LANES:(s + 1) * LANES])
        return carry

    for m in range(PEELED_CHUNKS):
        chunk_body(m, 0)
    lax.fori_loop(PEELED_CHUNKS, jnp.maximum(nchunk, PEELED_CHUNKS), chunk_body, 0)

    def copy_for(slot_):
        def make(e, dst_row, src_row, n):
            return pltpu.make_async_copy(
                stg.at[slot_, pl.ds(pl.multiple_of(src_row * SUBLANES, SUBLANES), n * SUBLANES)],
                xe_hbm.at[pl.ds(pl.multiple_of((e * cap + dst_row) * SUBLANES, SUBLANES),
                                n * SUBLANES)],
                sem.at[slot_])
        return make

    _segment_copies(b_sm, c_sm, t, copy_for(slot), lambda cp: cp.start())

    def wait_tile(t_, slot_):
        rows = _tile_rows(c_sm, t_)

        @pl.when(rows > 0)
        def _():
            n = rows * SUBLANES
            pltpu.make_async_copy(stg.at[slot_, pl.ds(0, n)], xe_hbm.at[pl.ds(0, n)],
                                  sem.at[slot_]).wait()

    @pl.when(t > 0)
    def _():
        wait_tile(t - 1, 1 - slot)

    @pl.when(t == nt - 1)
    def _():
        wait_tile(t, slot)


def _dispatch(b_flat, c_flat, x2b, loc3, cap):
    S, D = x2b.shape
    nt = S // ROUTE_TILE
    per_tile = ROUTE_TILE // LANES
    max_rows = N_EXPERTS * ROUTE_TILE
    return pl.pallas_call(
        functools.partial(_dispatch_kernel, cap),
        grid_spec=pltpu.PrefetchScalarGridSpec(
            num_scalar_prefetch=2,
            grid=(nt,),
            in_specs=[pl.BlockSpec((ROUTE_TILE, D), lambda t, b, c: (t, 0)),
                      pl.BlockSpec((per_tile, N_EXPERTS, LANES), lambda t, b, c: (t, 0, 0))],
            out_specs=pl.BlockSpec(memory_space=pl.ANY),
            scratch_shapes=[pltpu.VMEM((2, max_rows * SUBLANES, LANES), F32),
                            pltpu.SemaphoreType.DMA((2,))]),
        out_shape=jax.ShapeDtypeStruct((N_EXPERTS * cap * SUBLANES, LANES), F32),
        compiler_params=pltpu.CompilerParams(
            dimension_semantics=("arbitrary",), vmem_limit_bytes=VMEM_LIMIT),
        name="dispatch",
    )(b_flat, c_flat, x2b, loc3)


def _ffn_kernel(xe_ref, wg_ref, wu_ref, wd_ref, wr_ref, o_ref, xb, acc, gate):
    e = pl.program_id(0)
    f = pl.program_id(2)
    tm = xb.shape[0]

    @pl.when(f == 0)
    def _():
        for s in range(SUBLANES):
            xb[:, s * LANES:(s + 1) * LANES] = (
                xe_ref[pl.ds(s, tm, stride=SUBLANES), :].astype(BF16))
        lg = jnp.dot(xb[...], wr_ref[...], preferred_element_type=F32)
        lane = lax.broadcasted_iota(I32, (1, LANES), 1)
        lg = jnp.where(lane < N_EXPERTS, lg, NEG)
        p = jnp.exp(lg - jnp.max(lg, axis=-1, keepdims=True))
        mine = jnp.sum(jnp.where(lane == e, p, 0.0), axis=-1, keepdims=True)
        gate[...] = mine / jnp.sum(p, axis=-1, keepdims=True)

    def down_proj():
        x = xb[...]
        g = jnp.dot(x, wg_ref[...].astype(BF16), preferred_element_type=F32)
        u = jnp.dot(x, wu_ref[...].astype(BF16), preferred_element_type=F32)
        h = (g * jax.nn.sigmoid(g) * u).astype(BF16)
        return jnp.dot(h, wd_ref[...].astype(BF16), preferred_element_type=F32)

    last = pl.num_programs(2) - 1

    @pl.when(f == 0)
    def _():
        acc[...] = down_proj()

    @pl.when((f > 0) & (f < last))
    def _():
        acc[...] += down_proj()

    @pl.when(f == last)
    def _():
        y = (acc[...] + down_proj()) * gate[...]
        for s in range(SUBLANES):
            o_ref[pl.ds(s, tm, stride=SUBLANES), :] = y[:, s * LANES:(s + 1) * LANES]


def _ffn(xe2d, w_gate, w_up, w_down, wr_pad, cap):
    E, D, FF = w_gate.shape
    tm, tf = FFN_TM, FFN_TF
    mt = cap // tm
    rows = lambda e, m, f: (e * mt + m, 0)
    return pl.pallas_call(
        _ffn_kernel,
        grid=(E, mt, FF // tf),
        in_specs=[pl.BlockSpec((tm * SUBLANES, LANES), rows),
                  pl.BlockSpec((None, D, tf), lambda e, m, f: (e, 0, f)),
                  pl.BlockSpec((None, D, tf), lambda e, m, f: (e, 0, f)),
                  pl.BlockSpec((None, tf, D), lambda e, m, f: (e, f, 0)),
                  pl.BlockSpec(wr_pad.shape, lambda e, m, f: (0, 0))],
        out_specs=pl.BlockSpec((tm * SUBLANES, LANES), rows),
        out_shape=jax.ShapeDtypeStruct(xe2d.shape, F32),
        scratch_shapes=[pltpu.VMEM((tm, D), BF16), pltpu.VMEM((tm, D), F32),
                        pltpu.VMEM((tm, 1), F32)],
        compiler_params=pltpu.CompilerParams(
            dimension_semantics=("arbitrary", "arbitrary", "arbitrary"),
            vmem_limit_bytes=VMEM_LIMIT),
        name="ffn",
    )(xe2d, w_gate, w_up, w_down, wr_pad)


def _combine_kernel(cap, alpha, b_sm, c_sm, x_ref, loc_ref, g_ref, be_ref, ye_hbm,
                    o_ref, ybuf, sem, hl_scr, z_scr):
    t = pl.program_id(0)
    nt = pl.num_programs(0)
    slot = t % 2

    def copy_for(slot_):
        def make(e, src_row, dst_row, n):
            return pltpu.make_async_copy(
                ye_hbm.at[pl.ds(pl.multiple_of((e * cap + src_row) * SUBLANES, SUBLANES),
                                n * SUBLANES)],
                ybuf.at[slot_, pl.ds(pl.multiple_of(dst_row * SUBLANES, SUBLANES), n * SUBLANES)],
                sem.at[slot_])
        return make

    @pl.when(t == 0)
    def _():
        ybuf[...] = jnp.zeros_like(ybuf)
        _segment_copies(b_sm, c_sm, t, copy_for(slot), lambda cp: cp.start())

    @pl.when(t < nt - 1)
    def _():
        _segment_copies(b_sm, c_sm, t + 1, copy_for(1 - slot), lambda cp: cp.start())

    rows_in = _tile_rows(c_sm, t)

    @pl.when(rows_in > 0)
    def _():
        n = rows_in * SUBLANES
        pltpu.make_async_copy(ye_hbm.at[pl.ds(0, n)], ybuf.at[slot, pl.ds(0, n)],
                              sem.at[slot]).wait()

    offs = []
    off = 0
    for e in range(N_EXPERTS):
        offs.append(off)
        off = off + c_sm[t * N_EXPERTS + e]
    offs.append(off)
    total = off
    nchunk = (total + SLOT_CHUNK - 1) // SLOT_CHUNK
    lane = lax.broadcasted_iota(I32, (1, SLOT_CHUNK), 1)
    sub = lax.broadcasted_iota(I32, (SLOT_CHUNK, 1), 0)
    eid = lax.broadcasted_iota(I32, (N_EXPERTS, 1), 0)
    seg_lo = jnp.zeros((N_EXPERTS, 1), I32)
    seg_hi = jnp.zeros((N_EXPERTS, 1), I32)
    for e in range(N_EXPERTS):
        seg_lo = jnp.where(eid == e, offs[e], seg_lo)
        seg_hi = jnp.where(eid == e, offs[e + 1], seg_hi)
    seg_lo_f = seg_lo.astype(F32)
    locb = loc_ref[...].astype(F32).astype(BF16)

    z_scr[...] = alpha * x_ref[...]

    def chunk_body(m, carry):
        base = m * SLOT_CHUNK
        srow = lane + base
        inseg = jnp.where(srow >= seg_lo, jnp.where(srow < seg_hi, 1.0, 0.0), 0.0)
        rank = srow.astype(F32) - jnp.sum(inseg * seg_lo_f, axis=0, keepdims=True)
        want = jnp.dot(locb, inseg.astype(BF16), preferred_element_type=F32)
        ptb = jnp.where(want == rank, 1.0, 0.0).astype(BF16)
        row0 = base * SUBLANES if isinstance(m, int) else pl.multiple_of(base * SUBLANES, SUBLANES)
        yc = jnp.concatenate(
            [ybuf[slot, pl.ds(row0 + s, SLOT_CHUNK, stride=SUBLANES), :]
             for s in range(SUBLANES)], axis=1)
        yc = jnp.where(sub < total - base, yc, 0.0)
        hi = yc.astype(BF16)
        hl = hl_scr.at[m % PEELED_CHUNKS]
        hl[0:SLOT_CHUNK] = hi
        hl[SLOT_CHUNK:2 * SLOT_CHUNK] = (yc - hi.astype(F32)).astype(BF16)
        z_scr[...] += jnp.dot(jnp.concatenate([ptb, ptb], axis=1), hl[...],
                              preferred_element_type=F32)
        return carry

    for m in range(PEELED_CHUNKS):
        chunk_body(m, 0)
    lax.fori_loop(PEELED_CHUNKS, jnp.maximum(nchunk, PEELED_CHUNKS), chunk_body, 0)

    o_ref[...] = _layer_norm(z_scr[...], g_ref[...], be_ref[...])


def _combine(b_flat, c_flat, x2, loc_t, g, b, ye2d, cap, alpha):
    S, D = x2.shape
    nt = S // ROUTE_TILE
    max_rows = N_EXPERTS * ROUTE_TILE
    cur = lambda t, b_, c_: (t, 0)
    const2 = lambda t, b_, c_: (0, 0)
    return pl.pallas_call(
        functools.partial(_combine_kernel, cap, alpha),
        grid_spec=pltpu.PrefetchScalarGridSpec(
            num_scalar_prefetch=2,
            grid=(nt,),
            in_specs=[pl.BlockSpec((ROUTE_TILE, D), cur),
                      pl.BlockSpec((ROUTE_TILE, N_EXPERTS), cur),
                      pl.BlockSpec((1, D), const2),
                      pl.BlockSpec((1, D), const2),
                      pl.BlockSpec(memory_space=pl.ANY)],
            out_specs=pl.BlockSpec((ROUTE_TILE, D), cur),
            scratch_shapes=[pltpu.VMEM((2, max_rows * SUBLANES, LANES), F32),
                            pltpu.SemaphoreType.DMA((2,)),
                            pltpu.VMEM((PEELED_CHUNKS, 2 * SLOT_CHUNK, D), BF16),
                            pltpu.VMEM((ROUTE_TILE, D), F32)]),
        out_shape=jax.ShapeDtypeStruct((S, D), F32),
        compiler_params=pltpu.CompilerParams(
            dimension_semantics=("arbitrary",), vmem_limit_bytes=VMEM_LIMIT),
        name="combine",
    )(b_flat, c_flat, x2, loc_t, g, b, ye2d)


def _layer(x, mem, w_in, rpb, conv_w, w_mix_out, ln1_g, ln1_b, wq, wk, wv, wo,
           ln2_g, ln2_b, w_router, w_gate, w_up, w_down, ln3_g, ln3_b, alpha):
    S, D = x.shape
    cap = CAPACITY_FACTOR * S // N_EXPERTS
    assert S % TOK_BLOCK == 0 and S % PROJ_TM == 0 and S // GRID_W >= 2 * ROW_BLOCK
    assert cap % FFN_TM == 0 and cap <= S
    row = lambda a: a.reshape(1, -1)

    q, k, v, bg, u = _proj(x, w_in.astype(BF16))
    x1 = _mixer(x, q, k, v, bg, u, w_mix_out.astype(BF16), rpb,
                conv_w, row(ln1_g), row(ln1_b), alpha)

    wr_t = w_router.T
    wr_hi = wr_t.astype(BF16)
    wr_lo = (wr_t - wr_hi.astype(F32)).astype(BF16)
    x2, x2b, aff3 = _xattn(x1, mem, wq.astype(BF16), wk.astype(BF16), wv.astype(BF16),
                           wo.astype(BF16), row(ln2_g), row(ln2_b), wr_hi, wr_lo, alpha)

    loc3, b3, c3 = _route(aff3, cap)
    b_flat = b3.reshape(-1)
    c_flat = c3.reshape(-1)
    loc_t = loc3.transpose(0, 2, 1).reshape(S, N_EXPERTS)

    xe2d = _dispatch(b_flat, c_flat, x2b, loc3, cap)
    wr_pad = jnp.pad(w_router.astype(BF16), ((0, 0), (0, LANES - N_EXPERTS)))
    ye2d = _ffn(xe2d, w_gate, w_up, w_down, wr_pad, cap)
    return _combine(b_flat, c_flat, x2, loc_t, row(ln3_g), row(ln3_b), ye2d, cap, alpha)


def kernel(x, mem, w_in, na_rpb, conv_w, w_mix_out, ln1_g, ln1_b, w_mem_q, w_mem_k,
           w_mem_v, w_mem_out, ln2_g, ln2_b, w_router, w_exp_gate, w_exp_up,
           w_exp_down, ln3_g, ln3_b):
    depth = w_in.shape[0]
    alpha = (2 * depth) ** 0.25
    outs = []
    for bi in range(x.shape[0]):
        xb = x[bi]
        for l in range(depth):
            xb = _layer(xb, mem[bi], w_in[l], na_rpb[l], conv_w[l], w_mix_out[l],
                        ln1_g[l], ln1_b[l], w_mem_q[l], w_mem_k[l], w_mem_v[l],
                        w_mem_out[l], ln2_g[l], ln2_b[l], w_router[l], w_exp_gate[l],
                        w_exp_up[l], w_exp_down[l], ln3_g[l], ln3_b[l], alpha)
        outs.append(xb)
    return jnp.stack(outs)
```

```python
import functools

import jax
import jax.numpy as jnp
from jax import lax
from jax.experimental import pallas as pl
from jax.experimental.pallas import tpu as pltpu

F32 = jnp.float32
BF16 = jnp.bfloat16
I32 = jnp.int32

GRID_W = 64
HEAD_DIM = 64
NA_HEADS = 8
NA_WIDTH = NA_HEADS * HEAD_DIM
WIN_ROWS = 8
WIN_COLS = 16
MEM_HEADS = 4
MEM_HEAD_DIM = 128
N_EXPERTS = 16
CAPACITY_FACTOR = 2
LN_EPS = 1e-5
NEG = -1e30

LANES = 128
SUBLANES = 8
VMEM_LIMIT = 56 * 1024 * 1024

ROW_BLOCK = 8
TOK_BLOCK = ROW_BLOCK * GRID_W
PROJ_TM = 1024
ROUTE_TILE = 256
SLOT_CHUNK = 256
SM_ROWS = 16
PEELED_CHUNKS = 2
FFN_TM = 1024
FFN_TF = 512


def _layer_norm(z, g, b):
    mu = jnp.mean(z, axis=-1, keepdims=True)
    zc = z - mu
    var = jnp.mean(zc * zc, axis=-1, keepdims=True)
    return zc * lax.rsqrt(var + LN_EPS) * g + b


def _dot_nt(a, b):
    return lax.dot_general(a, b, (((1,), (1,)), ((), ())),
                           preferred_element_type=F32)


def _proj_kernel(x_ref, w_ref, q_ref, k_ref, v_ref, b_ref, u_ref):
    xb = x_ref[...].astype(BF16)
    nw = NA_WIDTH

    def mm(c0):
        return jnp.dot(xb, w_ref[:, c0:c0 + nw], preferred_element_type=F32)

    q_ref[...] = (mm(0) * (HEAD_DIM ** -0.5)).astype(BF16)
    k_ref[...] = mm(nw).astype(BF16)
    v_ref[...] = mm(2 * nw).astype(BF16)
    b_ref[...] = mm(3 * nw)
    u_ref[...] = mm(4 * nw) * mm(5 * nw)


def _proj(x, w_in_b):
    S, D = x.shape
    nw = NA_WIDTH
    tm = PROJ_TM
    blk = lambda i: (i, 0)
    return pl.pallas_call(
        _proj_kernel,
        grid=(S // tm,),
        in_specs=[pl.BlockSpec((tm, D), blk),
                  pl.BlockSpec((D, 6 * nw), lambda i: (0, 0))],
        out_specs=[pl.BlockSpec((tm, nw), blk)] * 5,
        out_shape=[jax.ShapeDtypeStruct((S, nw), BF16)] * 3
        + [jax.ShapeDtypeStruct((S, nw), F32)] * 2,
        compiler_params=pltpu.CompilerParams(
            dimension_semantics=("arbitrary",), vmem_limit_bytes=VMEM_LIMIT),
        name="proj",
    )(x, w_in_b)


def _build_na_bias(rpb_ref, bias_ref):
    shape = (GRID_W, LANES)
    qc = lax.broadcasted_iota(I32, shape, 0)
    ln = lax.broadcasted_iota(I32, shape, 1)
    kc = ln % GRID_W
    c_start = jnp.clip(qc - WIN_COLS // 2, 0, GRID_W - WIN_COLS)
    base = LANES - (WIN_COLS - 1)

    def canvas(h, d):
        both = rpb_ref[h, d:d + 1, :] + pltpu.roll(rpb_ref[h, d + 1:d + 2, :], GRID_W, 1)
        t = pltpu.roll(pltpu.roll(jnp.broadcast_to(both, shape), base, 1), 0, 1,
                       stride=1, stride_axis=0)
        return jnp.where(kc >= c_start, jnp.where(kc < c_start + WIN_COLS, t, NEG), NEG)

    for par in range(2):
        for hp in range(NA_HEADS // 2):
            for m in range(WIN_ROWS):
                d = 2 * m + par
                bias_ref[par, hp, m, 0:GRID_W] = canvas(2 * hp, d)
                bias_ref[par, hp, m, GRID_W:2 * GRID_W] = canvas(2 * hp + 1, d)


def _mixer_kernel(rows, alpha,
                  q_ref, kp_ref, kc_ref, kn_ref, vp_ref, vc_ref, vn_ref,
                  b_ref, u_ref, up_ref, un_ref, x_ref, wout_ref, rpb_ref,
                  cw_ref, g_ref, be_ref, o_ref, kbuf, vbuf, mix, bias_ref,
                  s_scr0, s_scr1, p_scr0, p_scr1):
    i = pl.program_id(0)
    nb = pl.num_programs(0)
    tb = TOK_BLOCK
    win = WIN_ROWS * GRID_W

    @pl.when(i == 0)
    def _():
        _build_na_bias(rpb_ref, bias_ref)

    kbuf[0:tb] = kp_ref[...]
    kbuf[tb:2 * tb] = kc_ref[...]
    kbuf[2 * tb:3 * tb] = kn_ref[...]
    vbuf[0:tb] = vp_ref[...]
    vbuf[tb:2 * tb] = vc_ref[...]
    vbuf[2 * tb:3 * tb] = vn_ref[...]

    lane = lax.broadcasted_iota(I32, (1, LANES), 1)
    first_head = lane < HEAD_DIM
    m_lo = jnp.where(first_head, 1.0, 0.0).astype(BF16)
    m_hi = jnp.where(first_head, 0.0, 1.0).astype(BF16)

    def window(r):
        grow = i * ROW_BLOCK + r
        start = jnp.clip(grow - WIN_ROWS // 2, 0, rows - WIN_ROWS)
        w0 = pl.multiple_of((start - (i - 1) * ROW_BLOCK) * GRID_W, GRID_W)
        d0 = start - grow + WIN_ROWS
        return w0, d0 % 2, d0 // 2

    def scores(r, s_out):
        w0, _, _ = window(r)
        q0 = r * GRID_W if isinstance(r, int) else pl.multiple_of(r * GRID_W, GRID_W)
        for hp in range(NA_HEADS // 2):
            cs = slice(hp * LANES, (hp + 1) * LANES)
            qp = q_ref[pl.ds(q0, GRID_W), cs]
            lhs = jnp.concatenate([qp * m_lo, qp * m_hi], axis=0)
            kw = kbuf[pl.ds(w0, win), cs]
            s_out[hp] = _dot_nt(lhs, kw)

    def probs(r, s_in, p_out):
        _, par, m0 = window(r)
        for hp in range(NA_HEADS // 2):
            for rb in range(2 * GRID_W // SM_ROWS):
                rs = slice(rb * SM_ROWS, (rb + 1) * SM_ROWS)
                bias = jnp.concatenate(
                    [bias_ref[par, hp, m0 + j, rs, :] for j in range(WIN_ROWS // 2)], axis=1)
                s = s_in[hp, rs, :] + bias
                p = jnp.exp(s - jnp.max(s, axis=-1, keepdims=True))
                p = p * (1.0 / jnp.sum(p, axis=-1, keepdims=True))
                p_out[hp, rs, :] = p.astype(BF16)

    def values(r, p_in):
        w0, _, _ = window(r)
        q0 = r * GRID_W if isinstance(r, int) else pl.multiple_of(r * GRID_W, GRID_W)
        for hp in range(NA_HEADS // 2):
            cs = slice(hp * LANES, (hp + 1) * LANES)
            vw = vbuf[pl.ds(w0, win), cs]
            o2 = jnp.dot(p_in[hp], vw, preferred_element_type=F32)
            y = jnp.where(first_head, o2[0:GRID_W], o2[GRID_W:2 * GRID_W])
            mix[pl.ds(q0, GRID_W), cs] = y.astype(BF16)

    s_buf = (s_scr0, s_scr1)
    p_buf = (p_scr0, p_scr1)

    def steady(k, carry):
        for half in range(2):
            r = 1 + 2 * k + half
            values(r - 1, p_buf[half])
            probs(r, s_buf[1 - half], p_buf[1 - half])
            scores(r + 1, s_buf[half])
        return carry

    scores(0, s_buf[0])
    probs(0, s_buf[0], p_buf[0])
    scores(1, s_buf[1])
    lax.fori_loop(0, (ROW_BLOCK - 2) // 2, steady, 0)
    values(ROW_BLOCK - 2, p_buf[0])
    probs(ROW_BLOCK - 1, s_buf[1], p_buf[1])
    values(ROW_BLOCK - 1, p_buf[1])

    u = u_ref[...]
    prev_row = up_ref[SUBLANES - 1:SUBLANES, :] * jnp.where(i > 0, 1.0, 0.0)
    next_row = un_ref[0:1, :] * jnp.where(i < nb - 1, 1.0, 0.0)
    rid = lax.broadcasted_iota(I32, (tb, 1), 0)
    um1 = jnp.where(rid == 0, prev_row, pltpu.roll(u, 1, axis=0))
    up1 = jnp.where(rid == tb - 1, next_row, pltpu.roll(u, tb - 1, axis=0))
    cw = cw_ref[...]
    yc = b_ref[...] * (cw[0:1] * um1 + cw[1:2] * u + cw[2:3] * up1)
    mix[:, NA_WIDTH:] = yc.astype(BF16)

    y = jnp.dot(mix[...], wout_ref[...], preferred_element_type=F32)
    z = alpha * x_ref[...] + y
    o_ref[...] = _layer_norm(z, g_ref[...], be_ref[...])


def _mixer(x, q, k, v, bg, u, w_out_b, rpb, conv_w, g, b, alpha):
    S, D = x.shape
    rpb_pad = jnp.pad(rpb.astype(F32), ((0, 0), (1, 1), (0, LANES - rpb.shape[2])))
    nw = NA_WIDTH
    tb = TOK_BLOCK
    nb = S // tb
    rows = S // GRID_W
    cur = lambda i: (i, 0)
    prv = lambda i: (jnp.maximum(i - 1, 0), 0)
    nxt = lambda i: (jnp.minimum(i + 1, nb - 1), 0)
    halo = tb // SUBLANES
    hprev = lambda i: (jnp.maximum(i * halo - 1, 0), 0)
    hnext = lambda i: (jnp.minimum((i + 1) * halo, S // SUBLANES - 1), 0)
    const2 = lambda i: (0, 0)
    kv = lambda im: pl.BlockSpec((tb, nw), im)
    return pl.pallas_call(
        functools.partial(_mixer_kernel, rows, alpha),
        grid=(nb,),
        in_specs=[kv(cur), kv(prv), kv(cur), kv(nxt), kv(prv), kv(cur), kv(nxt),
                  kv(cur), kv(cur),
                  pl.BlockSpec((SUBLANES, nw), hprev),
                  pl.BlockSpec((SUBLANES, nw), hnext),
                  pl.BlockSpec((tb, D), cur),
                  pl.BlockSpec((D, D), const2),
                  pl.BlockSpec(rpb_pad.shape, lambda i: (0, 0, 0)),
                  pl.BlockSpec(conv_w.shape, const2),
                  pl.BlockSpec((1, D), const2),
                  pl.BlockSpec((1, D), const2)],
        out_specs=pl.BlockSpec((tb, D), cur),
        out_shape=jax.ShapeDtypeStruct((S, D), F32),
        scratch_shapes=[pltpu.VMEM((3 * tb, nw), BF16),
                        pltpu.VMEM((3 * tb, nw), BF16),
                        pltpu.VMEM((tb, D), BF16),
                        pltpu.VMEM((2, NA_HEADS // 2, WIN_ROWS, 2 * GRID_W, LANES), F32),
                        pltpu.VMEM((NA_HEADS // 2, 2 * GRID_W, WIN_ROWS * GRID_W), F32),
                        pltpu.VMEM((NA_HEADS // 2, 2 * GRID_W, WIN_ROWS * GRID_W), F32),
                        pltpu.VMEM((NA_HEADS // 2, 2 * GRID_W, WIN_ROWS * GRID_W), BF16),
                        pltpu.VMEM((NA_HEADS // 2, 2 * GRID_W, WIN_ROWS * GRID_W), BF16)],
        compiler_params=pltpu.CompilerParams(
            dimension_semantics=("arbitrary",), vmem_limit_bytes=VMEM_LIMIT),
        name="mixer",
    )(q, k, k, k, v, v, v, bg, u, u, u, x, w_out_b, rpb_pad, conv_w, g, b)


def _xattn_kernel(alpha, x_ref, mem_ref, wq_ref, wk_ref, wv_ref, wo_ref,
                  g_ref, be_ref, wrh_ref, wrl_ref,
                  x2_ref, x2b_ref, aff_ref, kmem, vmem):
    i = pl.program_id(0)

    @pl.when(i == 0)
    def _():
        mb = mem_ref[...].astype(BF16)
        kmem[...] = jnp.dot(mb, wk_ref[...], preferred_element_type=F32).astype(BF16)
        vmem[...] = jnp.dot(mb, wv_ref[...], preferred_element_type=F32).astype(BF16)

    x1 = x_ref[...]
    qb = jnp.dot(x1.astype(BF16), wq_ref[...], preferred_element_type=F32).astype(BF16)
    outs = []
    for h in range(MEM_HEADS):
        cs = slice(h * MEM_HEAD_DIM, (h + 1) * MEM_HEAD_DIM)
        s = _dot_nt(qb[:, cs], kmem[:, cs]) * (MEM_HEAD_DIM ** -0.5)
        mx = jnp.max(s, axis=-1, keepdims=True)
        p = jnp.exp(s - mx)
        l = jnp.sum(p, axis=-1, keepdims=True)
        oh = jnp.dot(p.astype(BF16), vmem[:, cs], preferred_element_type=F32)
        outs.append(oh / l)
    o = jnp.concatenate(outs, axis=1).astype(BF16)
    y = jnp.dot(o, wo_ref[...], preferred_element_type=F32)
    x2 = _layer_norm(alpha * x1 + y, g_ref[...], be_ref[...])
    x2_ref[...] = x2
    hi = x2.astype(BF16)
    x2b_ref[...] = hi
    lo = (x2 - hi.astype(F32)).astype(BF16)
    wrh = wrh_ref[...]
    r1 = _dot_nt(jnp.concatenate([wrh, wrl_ref[...]], axis=0), hi)
    lg = r1[0:N_EXPERTS] + r1[N_EXPERTS:2 * N_EXPERTS] + _dot_nt(wrh, lo)
    mx = jnp.max(lg, axis=0, keepdims=True)
    p = jnp.exp(lg - mx)
    aff = p / jnp.sum(p, axis=0, keepdims=True)
    for j in range(aff_ref.shape[0]):
        aff_ref[j] = aff[:, j * LANES:(j + 1) * LANES]


def _xattn(x1, mem, wq_b, wk_b, wv_b, wo_b, g, b, wr_hi, wr_lo, alpha):
    S, D = x1.shape
    M = mem.shape[0]
    mw = MEM_HEADS * MEM_HEAD_DIM
    tm = TOK_BLOCK
    cur = lambda i: (i, 0)
    const2 = lambda i: (0, 0)
    full = lambda a: pl.BlockSpec(a.shape, const2)
    return pl.pallas_call(
        functools.partial(_xattn_kernel, alpha),
        grid=(S // tm,),
        in_specs=[pl.BlockSpec((tm, D), cur), full(mem), full(wq_b), full(wk_b),
                  full(wv_b), full(wo_b), full(g), full(b), full(wr_hi), full(wr_lo)],
        out_specs=[pl.BlockSpec((tm, D), cur), pl.BlockSpec((tm, D), cur),
                   pl.BlockSpec((tm // LANES, N_EXPERTS, LANES), lambda i: (i, 0, 0))],
        out_shape=[jax.ShapeDtypeStruct((S, D), F32),
                   jax.ShapeDtypeStruct((S, D), BF16),
                   jax.ShapeDtypeStruct((S // LANES, N_EXPERTS, LANES), F32)],
        scratch_shapes=[pltpu.VMEM((M, mw), BF16), pltpu.VMEM((M, mw), BF16)],
        compiler_params=pltpu.CompilerParams(
            dimension_semantics=("arbitrary",), vmem_limit_bytes=VMEM_LIMIT),
        name="xattn",
    )(x1, mem, wq_b, wk_b, wv_b, wo_b, g, b, wr_hi, wr_lo)


def _route_kernel(cap, aff_ref, loc_ref, b_ref, c_ref, incl_scr, tot_scr, sel_scr):
    nblk = aff_ref.shape[0]
    per_tile = ROUTE_TILE // LANES
    ntile = nblk // per_tile

    def count_ge(t):
        ge = jnp.where(aff_ref[...] >= t[None], 1.0, 0.0)
        return jnp.sum(jnp.sum(ge, axis=0), axis=1, keepdims=True)

    def coarse(it, bits):
        cand = bits | jnp.left_shift(jnp.int32(1), 30 - it)
        ok = count_ge(pltpu.bitcast(cand, F32)) >= cap
        return jnp.where(ok, cand, bits)

    def fine(it, lohi):
        lo, hi = lohi
        mid = lo + (hi - lo) * 0.5
        ok = count_ge(mid) >= cap
        return jnp.where(ok, mid, lo), jnp.where(ok, hi, mid)

    bits = lax.fori_loop(0, 31, coarse, jnp.zeros((N_EXPERTS, 1), I32))
    thr, _ = lax.fori_loop(
        0, 30, fine, (pltpu.bitcast(bits, F32), pltpu.bitcast(bits + 1, F32)))
    gt_all = jnp.where(aff_ref[...] > thr[None], 1.0, 0.0)
    n_gt = jnp.sum(jnp.sum(gt_all, axis=0), axis=1, keepdims=True)
    need = cap - n_gt

    ii = lax.broadcasted_iota(I32, (LANES, LANES), 0)
    jj = lax.broadcasted_iota(I32, (LANES, LANES), 1)
    tri = jnp.where(ii <= jj, 1.0, 0.0).astype(BF16)
    ones = jnp.ones((LANES, LANES), BF16)
    flat = (nblk * N_EXPERTS, LANES)
    blocks = (nblk, N_EXPERTS, LANES)

    def block_scans(flags):
        f2 = flags.reshape(flat).astype(BF16)
        incl_scr[...] = jnp.dot(f2, tri, preferred_element_type=F32).reshape(blocks)
        tot_scr[...] = jnp.dot(f2, ones, preferred_element_type=F32).reshape(blocks)

    block_scans(jnp.where(aff_ref[...] == thr[None], 1.0, 0.0))

    def select_block(j, run_eq):
        blk = aff_ref[j]
        tie_rank = run_eq + incl_scr[j]
        sel_scr[j] = (jnp.where(blk > thr, 1.0, 0.0)
                      + jnp.where(blk == thr, 1.0, 0.0) * jnp.where(tie_rank <= need, 1.0, 0.0))
        return run_eq + tot_scr[j]

    zero = jnp.zeros((N_EXPERTS, LANES), F32)
    lax.fori_loop(0, nblk, select_block, zero)
    block_scans(sel_scr[...])

    def tile_body(t, run):
        tile_run = zero
        for j in range(per_tile):
            idx = t * per_tile + j
            sel = sel_scr[idx]
            loc = tile_run + incl_scr[idx] - sel
            loc_ref[idx] = jnp.where(sel > 0.5, loc, -1.0).astype(I32)
            tile_run = tile_run + tot_scr[idx]
        b_ref[t] = run[:, 0:1].astype(I32)
        c_ref[t] = tile_run[:, 0:1].astype(I32)
        return run + tile_run

    lax.fori_loop(0, ntile, tile_body, zero)


def _route(aff3, cap):
    nblk = aff3.shape[0]
    ntile = nblk * LANES // ROUTE_TILE
    full3 = lambda a: pl.BlockSpec(a, lambda: (0, 0, 0))
    return pl.pallas_call(
        functools.partial(_route_kernel, cap),
        in_specs=[full3(aff3.shape)],
        out_specs=[full3(aff3.shape), full3((ntile, N_EXPERTS, 1)),
                   full3((ntile, N_EXPERTS, 1))],
        out_shape=[jax.ShapeDtypeStruct(aff3.shape, I32),
                   jax.ShapeDtypeStruct((ntile, N_EXPERTS, 1), I32),
                   jax.ShapeDtypeStruct((ntile, N_EXPERTS, 1), I32)],
        scratch_shapes=[pltpu.VMEM(aff3.shape, F32)] * 3,
        compiler_params=pltpu.CompilerParams(vmem_limit_bytes=VMEM_LIMIT),
        name="route",
    )(aff3)


def _segment_copies(b_sm, c_sm, t, make_copy, action):
    off = 0
    for e in range(N_EXPERTS):
        c = c_sm[t * N_EXPERTS + e]
        b = b_sm[t * N_EXPERTS + e]

        @pl.when(c > 0)
        def _(e=e, b=b, c=c, off=off):
            action(make_copy(e, b, off, c))

        off = off + c
    return off


def _tile_rows(c_sm, t):
    rows = 0
    for e in range(N_EXPERTS):
        rows = rows + c_sm[t * N_EXPERTS + e]
    return rows


def _dispatch_kernel(cap, b_sm, c_sm, x_ref, loc_ref, xe_hbm, stg, sem):
    t = pl.program_id(0)
    nt = pl.num_programs(0)
    slot = t % 2

    offs = []
    off = 0
    for e in range(N_EXPERTS):
        offs.append(off)
        off = off + c_sm[t * N_EXPERTS + e]
    offs.append(off)
    nchunk = (off + SLOT_CHUNK - 1) // SLOT_CHUNK
    sub = lax.broadcasted_iota(I32, (SLOT_CHUNK, 1), 0)
    lane = lax.broadcasted_iota(I32, (1, LANES), 1)
    seg_lo = jnp.zeros((1, LANES), I32)
    seg_hi = jnp.zeros((1, LANES), I32)
    for e in range(N_EXPERTS):
        seg_lo = jnp.where(lane == e, offs[e], seg_lo)
        seg_hi = jnp.where(lane == e, offs[e + 1], seg_hi)
    seg_lo_f = seg_lo.astype(F32)
    locf = jnp.concatenate(
        [loc_ref[j] for j in range(ROUTE_TILE // LANES)], axis=1).astype(F32)
    locb = jnp.concatenate(
        [locf, jnp.zeros((LANES - N_EXPERTS, ROUTE_TILE), F32)], axis=0).astype(BF16)

    def chunk_body(m, carry):
        base = m * SLOT_CHUNK
        srow = sub + base
        inseg = jnp.where(srow >= seg_lo, jnp.where(srow < seg_hi, 1.0, 0.0), 0.0)
        rank = srow.astype(F32) - jnp.sum(inseg * seg_lo_f, axis=1, keepdims=True)
        want = jnp.dot(inseg.astype(BF16), locb, preferred_element_type=F32)
        p = jnp.where(want == rank, 1.0, 0.0)
        res = jnp.dot(p.astype(BF16), x_ref[...], preferred_element_type=F32)
        row0 = base * SUBLANES if isinstance(m, int) else pl.multiple_of(base * SUBLANES, SUBLANES)
        for s in range(SUBLANES):
            stg[slot, pl.ds(row0 + s, SLOT_CHUNK, stride=SUBLANES), :] = (
                res[:, s * LANES:(s + 1) * LANES])
        return carry

    for m in range(PEELED_CHUNKS):
        chunk_body(m, 0)
    lax.fori_loop(PEELED_CHUNKS, jnp.maximum(nchunk, PEELED_CHUNKS), chunk_body, 0)

    def copy_for(slot_):
        def make(e, dst_row, src_row, n):
            return pltpu.make_async_copy(
                stg.at[slot_, pl.ds(pl.multiple_of(src_row * SUBLANES, SUBLANES), n * SUBLANES)],
                xe_hbm.at[pl.ds(pl.multiple_of((e * cap + dst_row) * SUBLANES, SUBLANES),
                                n * SUBLANES)],
                sem.at[slot_])
        return make

    _segment_copies(b_sm, c_sm, t, copy_for(slot), lambda cp: cp.start())

    def wait_tile(t_, slot_):
        rows = _tile_rows(c_sm, t_)

        @pl.when(rows > 0)
        def _():
            n = rows * SUBLANES
            pltpu.make_async_copy(stg.at[slot_, pl.ds(0, n)], xe_hbm.at[pl.ds(0, n)],
                                  sem.at[slot_]).wait()

    @pl.when(t > 0)
    def _():
        wait_tile(t - 1, 1 - slot)

    @pl.when(t == nt - 1)
    def _():
        wait_tile(t, slot)


def _dispatch(b_flat, c_flat, x2b, loc3, cap):
    S, D = x2b.shape
    nt = S // ROUTE_TILE
    per_tile = ROUTE_TILE // LANES
    max_rows = N_EXPERTS * ROUTE_TILE
    return pl.pallas_call(
        functools.partial(_dispatch_kernel, cap),
        grid_spec=pltpu.PrefetchScalarGridSpec(
            num_scalar_prefetch=2,
            grid=(nt,),
            in_specs=[pl.BlockSpec((ROUTE_TILE, D), lambda t, b, c: (t, 0)),
                      pl.BlockSpec((per_tile, N_EXPERTS, LANES), lambda t, b, c: (t, 0, 0))],
            out_specs=pl.BlockSpec(memory_space=pl.ANY),
            scratch_shapes=[pltpu.VMEM((2, max_rows * SUBLANES, LANES), F32),
                            pltpu.SemaphoreType.DMA((2,))]),
        out_shape=jax.ShapeDtypeStruct((N_EXPERTS * cap * SUBLANES, LANES), F32),
        compiler_params=pltpu.CompilerParams(
            dimension_semantics=("arbitrary",), vmem_limit_bytes=VMEM_LIMIT),
        name="dispatch",
    )(b_flat, c_flat, x2b, loc3)


def _ffn_kernel(xe_ref, wg_ref, wu_ref, wd_ref, wr_ref, o_ref, xb, acc, gate):
    e = pl.program_id(0)
    f = pl.program_id(2)
    tm = xb.shape[0]

    @pl.when(f == 0)
    def _():
        for s in range(SUBLANES):
            xb[:, s * LANES:(s + 1) * LANES] = (
                xe_ref[pl.ds(s, tm, stride=SUBLANES), :].astype(BF16))
        lg = jnp.dot(xb[...], wr_ref[...], preferred_element_type=F32)
        lane = lax.broadcasted_iota(I32, (1, LANES), 1)
        lg = jnp.where(lane < N_EXPERTS, lg, NEG)
        p = jnp.exp(lg - jnp.max(lg, axis=-1, keepdims=True))
        mine = jnp.sum(jnp.where(lane == e, p, 0.0), axis=-1, keepdims=True)
        gate[...] = mine / jnp.sum(p, axis=-1, keepdims=True)

    def down_proj():
        x = xb[...]
        g = jnp.dot(x, wg_ref[...].astype(BF16), preferred_element_type=F32)
        u = jnp.dot(x, wu_ref[...].astype(BF16), preferred_element_type=F32)
        h = (g * jax.nn.sigmoid(g) * u).astype(BF16)
        return jnp.dot(h, wd_ref[...].astype(BF16), preferred_element_type=F32)

    last = pl.num_programs(2) - 1

    @pl.when(f == 0)
    def _():
        acc[...] = down_proj()

    @pl.when((f > 0) & (f < last))
    def _():
        acc[...] += down_proj()

    @pl.when(f == last)
    def _():
        y = (acc[...] + down_proj()) * gate[...]
        for s in range(SUBLANES):
            o_ref[pl.ds(s, tm, stride=SUBLANES), :] = y[:, s * LANES:(s + 1) * LANES]


def _ffn(xe2d, w_gate, w_up, w_down, wr_pad, cap):
    E, D, FF = w_gate.shape
    tm, tf = FFN_TM, FFN_TF
    mt = cap // tm
    rows = lambda e, m, f: (e * mt + m, 0)
    return pl.pallas_call(
        _ffn_kernel,
        grid=(E, mt, FF // tf),
        in_specs=[pl.BlockSpec((tm * SUBLANES, LANES), rows),
                  pl.BlockSpec((None, D, tf), lambda e, m, f: (e, 0, f)),
                  pl.BlockSpec((None, D, tf), lambda e, m, f: (e, 0, f)),
                  pl.BlockSpec((None, tf, D), lambda e, m, f: (e, f, 0)),
                  pl.BlockSpec(wr_pad.shape, lambda e, m, f: (0, 0))],
        out_specs=pl.BlockSpec((tm * SUBLANES, LANES), rows),
        out_shape=jax.ShapeDtypeStruct(xe2d.shape, F32),
        scratch_shapes=[pltpu.VMEM((tm, D), BF16), pltpu.VMEM((tm, D), F32),
                        pltpu.VMEM((tm, 1), F32)],
        compiler_params=pltpu.CompilerParams(
            dimension_semantics=("arbitrary", "arbitrary", "arbitrary"),
            vmem_limit_bytes=VMEM_LIMIT),
        name="ffn",
    )(xe2d, w_gate, w_up, w_down, wr_pad)


def _combine_kernel(cap, alpha, b_sm, c_sm, x_ref, loc_ref, g_ref, be_ref, ye_hbm,
                    o_ref, ybuf, sem, hl_scr, z_scr):
    t = pl.program_id(0)
    nt = pl.num_programs(0)
    slot = t % 2

    def copy_for(slot_):
        def make(e, src_row, dst_row, n):
            return pltpu.make_async_copy(
                ye_hbm.at[pl.ds(pl.multiple_of((e * cap + src_row) * SUBLANES, SUBLANES),
                                n * SUBLANES)],
                ybuf.at[slot_, pl.ds(pl.multiple_of(dst_row * SUBLANES, SUBLANES), n * SUBLANES)],
                sem.at[slot_])
        return make

    @pl.when(t == 0)
    def _():
        ybuf[...] = jnp.zeros_like(ybuf)
        _segment_copies(b_sm, c_sm, t, copy_for(slot), lambda cp: cp.start())

    @pl.when(t < nt - 1)
    def _():
        _segment_copies(b_sm, c_sm, t + 1, copy_for(1 - slot), lambda cp: cp.start())

    rows_in = _tile_rows(c_sm, t)

    @pl.when(rows_in > 0)
    def _():
        n = rows_in * SUBLANES
        pltpu.make_async_copy(ye_hbm.at[pl.ds(0, n)], ybuf.at[slot, pl.ds(0, n)],
                              sem.at[slot]).wait()

    offs = []
    off = 0
    for e in range(N_EXPERTS):
        offs.append(off)
        off = off + c_sm[t * N_EXPERTS + e]
    offs.append(off)
    total = off
    nchunk = (total + SLOT_CHUNK - 1) // SLOT_CHUNK
    lane = lax.broadcasted_iota(I32, (1, SLOT_CHUNK), 1)
    sub = lax.broadcasted_iota(I32, (SLOT_CHUNK, 1), 0)
    eid = lax.broadcasted_iota(I32, (N_EXPERTS, 1), 0)
    seg_lo = jnp.zeros((N_EXPERTS, 1), I32)
    seg_hi = jnp.zeros((N_EXPERTS, 1), I32)
    for e in range(N_EXPERTS):
        seg_lo = jnp.where(eid == e, offs[e], seg_lo)
        seg_hi = jnp.where(eid == e, offs[e + 1], seg_hi)
    seg_lo_f = seg_lo.astype(F32)
    locb = loc_ref[...].astype(F32).astype(BF16)

    z_scr[...] = alpha * x_ref[...]

    def chunk_body(m, carry):
        base = m * SLOT_CHUNK
        srow = lane + base
        inseg = jnp.where(srow >= seg_lo, jnp.where(srow < seg_hi, 1.0, 0.0), 0.0)
        rank = srow.astype(F32) - jnp.sum(inseg * seg_lo_f, axis=0, keepdims=True)
        want = jnp.dot(locb, inseg.astype(BF16), preferred_element_type=F32)
        ptb = jnp.where(want == rank, 1.0, 0.0).astype(BF16)
        row0 = base * SUBLANES if isinstance(m, int) else pl.multiple_of(base * SUBLANES, SUBLANES)
        yc = jnp.concatenate(
            [ybuf[slot, pl.ds(row0 + s, SLOT_CHUNK, stride=SUBLANES), :]
             for s in range(SUBLANES)], axis=1)
        yc = jnp.where(sub < total - base, yc, 0.0)
        hi = yc.astype(BF16)
        hl = hl_scr.at[m % PEELED_CHUNKS]
        hl[0:SLOT_CHUNK] = hi
        hl[SLOT_CHUNK:2 * SLOT_CHUNK] = (yc - hi.astype(F32)).astype(BF16)
        z_scr[...] += jnp.dot(jnp.concatenate([ptb, ptb], axis=1), hl[...],
                              preferred_element_type=F32)
        return carry

    for m in range(PEELED_CHUNKS):
        chunk_body(m, 0)
    lax.fori_loop(PEELED_CHUNKS, jnp.maximum(nchunk, PEELED_CHUNKS), chunk_body, 0)

    o_ref[...] = _layer_norm(z_scr[...], g_ref[...], be_ref[...])


def _combine(b_flat, c_flat, x2, loc_t, g, b, ye2d, cap, alpha):
    S, D = x2.shape
    nt = S // ROUTE_TILE
    max_rows = N_EXPERTS * ROUTE_TILE
    cur = lambda t, b_, c_: (t, 0)
    const2 = lambda t, b_, c_: (0, 0)
    return pl.pallas_call(
        functools.partial(_combine_kernel, cap, alpha),
        grid_spec=pltpu.PrefetchScalarGridSpec(
            num_scalar_prefetch=2,
            grid=(nt,),
            in_specs=[pl.BlockSpec((ROUTE_TILE, D), cur),
                      pl.BlockSpec((ROUTE_TILE, N_EXPERTS), cur),
                      pl.BlockSpec((1, D), const2),
                      pl.BlockSpec((1, D), const2),
                      pl.BlockSpec(memory_space=pl.ANY)],
            out_specs=pl.BlockSpec((ROUTE_TILE, D), cur),
            scratch_shapes=[pltpu.VMEM((2, max_rows * SUBLANES, LANES), F32),
                            pltpu.SemaphoreType.DMA((2,)),
                            pltpu.VMEM((PEELED_CHUNKS, 2 * SLOT_CHUNK, D), BF16),
                            pltpu.VMEM((ROUTE_TILE, D), F32)]),
        out_shape=jax.ShapeDtypeStruct((S, D), F32),
        compiler_params=pltpu.CompilerParams(
            dimension_semantics=("arbitrary",), vmem_limit_bytes=VMEM_LIMIT),
        name="combine",
    )(b_flat, c_flat, x2, loc_t, g, b, ye2d)


def _layer(x, mem, w_in, rpb, conv_w, w_mix_out, ln1_g, ln1_b, wq, wk, wv, wo,
           ln2_g, ln2_b, w_router, w_gate, w_up, w_down, ln3_g, ln3_b, alpha):
    S, D = x.shape
    cap = CAPACITY_FACTOR * S // N_EXPERTS
    assert S % TOK_BLOCK == 0 and S % PROJ_TM == 0 and S // GRID_W >= 2 * ROW_BLOCK
    assert cap % FFN_TM == 0 and cap <= S
    row = lambda a: a.reshape(1, -1)

    q, k, v, bg, u = _proj(x, w_in.astype(BF16))
    x1 = _mixer(x, q, k, v, bg, u, w_mix_out.astype(BF16), rpb,
                conv_w, row(ln1_g), row(ln1_b), alpha)

    wr_t = w_router.T
    wr_hi = wr_t.astype(BF16)
    wr_lo = (wr_t - wr_hi.astype(F32)).astype(BF16)
    x2, x2b, aff3 = _xattn(x1, mem, wq.astype(BF16), wk.astype(BF16), wv.astype(BF16),
                           wo.astype(BF16), row(ln2_g), row(ln2_b), wr_hi, wr_lo, alpha)

    loc3, b3, c3 = _route(aff3, cap)
    b_flat = b3.reshape(-1)
    c_flat = c3.reshape(-1)
    loc_t = loc3.transpose(0, 2, 1).reshape(S, N_EXPERTS)

    xe2d = _dispatch(b_flat, c_flat, x2b, loc3, cap)
    wr_pad = jnp.pad(w_router.astype(BF16), ((0, 0), (0, LANES - N_EXPERTS)))
    ye2d = _ffn(xe2d, w_gate, w_up, w_down, wr_pad, cap)
    return _combine(b_flat, c_flat, x2, loc_t, row(ln3_g), row(ln3_b), ye2d, cap, alpha)


def kernel(x, mem, w_in, na_rpb, conv_w, w_mix_out, ln1_g, ln1_b, w_mem_q, w_mem_k,
           w_mem_v, w_mem_out, ln2_g, ln2_b, w_router, w_exp_gate, w_exp_up,
           w_exp_down, ln3_g, ln3_b):
    depth = w_in.shape[0]
    alpha = (2 * depth) ** 0.25
    outs = []
    for bi in range(x.shape[0]):
        xb = x[bi]
        for l in range(depth):
            xb = _layer(xb, mem[bi], w_in[l], na_rpb[l], conv_w[l], w_mix_out[l],
                        ln1_g[l], ln1_b[l], w_mem_q[l], w_mem_k[l], w_mem_v[l],
                        w_mem_out[l], ln2_g[l], ln2_b[l], w_router[l], w_exp_gate[l],
                        w_exp_up[l], w_exp_down[l], ln3_g[l], ln3_b[l], alpha)
        outs.append(xb)
    return jnp.stack(outs)
```

```python
import functools

import jax
import jax.numpy as jnp
from jax import lax
from jax.experimental import pallas as pl
from jax.experimental.pallas import tpu as pltpu

F32 = jnp.float32
BF16 = jnp.bfloat16
I32 = jnp.int32

GRID_W = 64
HEAD_DIM = 64
NA_HEADS = 8
NA_WIDTH = NA_HEADS * HEAD_DIM
WIN_ROWS = 8
WIN_COLS = 16
MEM_HEADS = 4
MEM_HEAD_DIM = 128
N_EXPERTS = 16
CAPACITY_FACTOR = 2
LN_EPS = 1e-5
NEG = -1e30

LANES = 128
SUBLANES = 8
VMEM_LIMIT = 56 * 1024 * 1024

ROW_BLOCK = 8
TOK_BLOCK = ROW_BLOCK * GRID_W
PROJ_TM = 1024
XATTN_TM = 1024
ROUTE_TILE = 256
SLOT_CHUNK = 256
SM_ROWS = 16
PEELED_CHUNKS = 2
FFN_TM = 1024
FFN_TF = 512


def _layer_norm(z, g, b):
    mu = jnp.mean(z, axis=-1, keepdims=True)
    zc = z - mu
    var = jnp.mean(zc * zc, axis=-1, keepdims=True)
    return zc * lax.rsqrt(var + LN_EPS) * g + b


def _dot_nt(a, b):
    return lax.dot_general(a, b, (((1,), (1,)), ((), ())),
                           preferred_element_type=F32)


def _proj_kernel(x_ref, w_ref, q_ref, k_ref, v_ref, b_ref, u_ref):
    xb = x_ref[...].astype(BF16)
    nw = NA_WIDTH

    def mm(c0):
        return jnp.dot(xb, w_ref[:, c0:c0 + nw], preferred_element_type=F32)

    q_ref[...] = (mm(0) * (HEAD_DIM ** -0.5)).astype(BF16)
    k_ref[...] = mm(nw).astype(BF16)
    v_ref[...] = mm(2 * nw).astype(BF16)
    b_ref[...] = mm(3 * nw)
    u_ref[...] = mm(4 * nw) * mm(5 * nw)


def _proj(x, w_in_b):
    S, D = x.shape
    nw = NA_WIDTH
    tm = PROJ_TM
    blk = lambda i: (i, 0)
    return pl.pallas_call(
        _proj_kernel,
        grid=(S // tm,),
        in_specs=[pl.BlockSpec((tm, D), blk),
                  pl.BlockSpec((D, 6 * nw), lambda i: (0, 0))],
        out_specs=[pl.BlockSpec((tm, nw), blk)] * 5,
        out_shape=[jax.ShapeDtypeStruct((S, nw), BF16)] * 3
        + [jax.ShapeDtypeStruct((S, nw), F32)] * 2,
        compiler_params=pltpu.CompilerParams(
            dimension_semantics=("arbitrary",), vmem_limit_bytes=VMEM_LIMIT),
        name="proj",
    )(x, w_in_b)


def _build_na_bias(rpb_ref, bias_ref):
    shape = (GRID_W, LANES)
    qc = lax.broadcasted_iota(I32, shape, 0)
    ln = lax.broadcasted_iota(I32, shape, 1)
    kc = ln % GRID_W
    c_start = jnp.clip(qc - WIN_COLS // 2, 0, GRID_W - WIN_COLS)
    base = LANES - (WIN_COLS - 1)

    def canvas(h, d):
        both = rpb_ref[h, d:d + 1, :] + pltpu.roll(rpb_ref[h, d + 1:d + 2, :], GRID_W, 1)
        t = pltpu.roll(pltpu.roll(jnp.broadcast_to(both, shape), base, 1), 0, 1,
                       stride=1, stride_axis=0)
        return jnp.where(kc >= c_start, jnp.where(kc < c_start + WIN_COLS, t, NEG), NEG)

    for par in range(2):
        for hp in range(NA_HEADS // 2):
            for m in range(WIN_ROWS):
                d = 2 * m + par
                bias_ref[par, hp, m, 0:GRID_W] = canvas(2 * hp, d)
                bias_ref[par, hp, m, GRID_W:2 * GRID_W] = canvas(2 * hp + 1, d)


def _mixer_kernel(rows, alpha,
                  q_ref, kp_ref, kc_ref, kn_ref, vp_ref, vc_ref, vn_ref,
                  b_ref, u_ref, up_ref, un_ref, x_ref, wout_ref, rpb_ref,
                  cw_ref, g_ref, be_ref, o_ref, kbuf, vbuf, mix, bias_ref,
                  s_scr0, s_scr1, p_scr0, p_scr1):
    i = pl.program_id(0)
    nb = pl.num_programs(0)
    tb = TOK_BLOCK
    win = WIN_ROWS * GRID_W

    @pl.when(i == 0)
    def _():
        _build_na_bias(rpb_ref, bias_ref)

    kbuf[0:tb] = kp_ref[...]
    kbuf[tb:2 * tb] = kc_ref[...]
    kbuf[2 * tb:3 * tb] = kn_ref[...]
    vbuf[0:tb] = vp_ref[...]
    vbuf[tb:2 * tb] = vc_ref[...]
    vbuf[2 * tb:3 * tb] = vn_ref[...]

    lane = lax.broadcasted_iota(I32, (1, LANES), 1)
    first_head = lane < HEAD_DIM
    m_lo = jnp.where(first_head, 1.0, 0.0).astype(BF16)
    m_hi = jnp.where(first_head, 0.0, 1.0).astype(BF16)

    def window(r):
        grow = i * ROW_BLOCK + r
        start = jnp.clip(grow - WIN_ROWS // 2, 0, rows - WIN_ROWS)
        w0 = pl.multiple_of((start - (i - 1) * ROW_BLOCK) * GRID_W, GRID_W)
        d0 = start - grow + WIN_ROWS
        return w0, d0 % 2, d0 // 2

    def scores(r, s_out):
        w0, _, _ = window(r)
        q0 = r * GRID_W if isinstance(r, int) else pl.multiple_of(r * GRID_W, GRID_W)
        for hp in range(NA_HEADS // 2):
            cs = slice(hp * LANES, (hp + 1) * LANES)
            qp = q_ref[pl.ds(q0, GRID_W), cs]
            lhs = jnp.concatenate([qp * m_lo, qp * m_hi], axis=0)
            kw = kbuf[pl.ds(w0, win), cs]
            s_out[hp] = _dot_nt(lhs, kw)

    def probs(r, s_in, p_out):
        _, par, m0 = window(r)
        for hp in range(NA_HEADS // 2):
            for rb in range(2 * GRID_W // SM_ROWS):
                rs = slice(rb * SM_ROWS, (rb + 1) * SM_ROWS)
                bias = jnp.concatenate(
                    [bias_ref[par, hp, m0 + j, rs, :] for j in range(WIN_ROWS // 2)], axis=1)
                s = s_in[hp, rs, :] + bias
                p = jnp.exp(s - jnp.max(s, axis=-1, keepdims=True))
                p = p * (1.0 / jnp.sum(p, axis=-1, keepdims=True))
                p_out[hp, rs, :] = p.astype(BF16)

    def values(r, p_in):
        w0, _, _ = window(r)
        q0 = r * GRID_W if isinstance(r, int) else pl.multiple_of(r * GRID_W, GRID_W)
        for hp in range(NA_HEADS // 2):
            cs = slice(hp * LANES, (hp + 1) * LANES)
            vw = vbuf[pl.ds(w0, win), cs]
            o2 = jnp.dot(p_in[hp], vw, preferred_element_type=F32)
            y = jnp.where(first_head, o2[0:GRID_W], o2[GRID_W:2 * GRID_W])
            mix[pl.ds(q0, GRID_W), cs] = y.astype(BF16)

    s_buf = (s_scr0, s_scr1)
    p_buf = (p_scr0, p_scr1)

    def steady(k, carry):
        for half in range(2):
            r = 1 + 2 * k + half
            scores(r + 1, s_buf[half])
            probs(r, s_buf[1 - half], p_buf[1 - half])
            values(r - 1, p_buf[half])
        return carry

    scores(0, s_buf[0])
    probs(0, s_buf[0], p_buf[0])
    scores(1, s_buf[1])
    lax.fori_loop(0, (ROW_BLOCK - 2) // 2, steady, 0)
    values(ROW_BLOCK - 2, p_buf[0])
    probs(ROW_BLOCK - 1, s_buf[1], p_buf[1])
    values(ROW_BLOCK - 1, p_buf[1])

    u = u_ref[...]
    prev_row = up_ref[SUBLANES - 1:SUBLANES, :] * jnp.where(i > 0, 1.0, 0.0)
    next_row = un_ref[0:1, :] * jnp.where(i < nb - 1, 1.0, 0.0)
    rid = lax.broadcasted_iota(I32, (tb, 1), 0)
    um1 = jnp.where(rid == 0, prev_row, pltpu.roll(u, 1, axis=0))
    up1 = jnp.where(rid == tb - 1, next_row, pltpu.roll(u, tb - 1, axis=0))
    cw = cw_ref[...]
    yc = b_ref[...] * (cw[0:1] * um1 + cw[1:2] * u + cw[2:3] * up1)
    mix[:, NA_WIDTH:] = yc.astype(BF16)

    y = jnp.dot(mix[...], wout_ref[...], preferred_element_type=F32)
    z = alpha * x_ref[...] + y
    o_ref[...] = _layer_norm(z, g_ref[...], be_ref[...])


def _mixer(x, q, k, v, bg, u, w_out_b, rpb, conv_w, g, b, alpha):
    S, D = x.shape
    rpb_pad = jnp.pad(rpb.astype(F32), ((0, 0), (1, 1), (0, LANES - rpb.shape[2])))
    nw = NA_WIDTH
    tb = TOK_BLOCK
    nb = S // tb
    rows = S // GRID_W
    cur = lambda i: (i, 0)
    prv = lambda i: (jnp.maximum(i - 1, 0), 0)
    nxt = lambda i: (jnp.minimum(i + 1, nb - 1), 0)
    halo = tb // SUBLANES
    hprev = lambda i: (jnp.maximum(i * halo - 1, 0), 0)
    hnext = lambda i: (jnp.minimum((i + 1) * halo, S // SUBLANES - 1), 0)
    const2 = lambda i: (0, 0)
    kv = lambda im: pl.BlockSpec((tb, nw), im)
    return pl.pallas_call(
        functools.partial(_mixer_kernel, rows, alpha),
        grid=(nb,),
        in_specs=[kv(cur), kv(prv), kv(cur), kv(nxt), kv(prv), kv(cur), kv(nxt),
                  kv(cur), kv(cur),
                  pl.BlockSpec((SUBLANES, nw), hprev),
                  pl.BlockSpec((SUBLANES, nw), hnext),
                  pl.BlockSpec((tb, D), cur),
                  pl.BlockSpec((D, D), const2),
                  pl.BlockSpec(rpb_pad.shape, lambda i: (0, 0, 0)),
                  pl.BlockSpec(conv_w.shape, const2),
                  pl.BlockSpec((1, D), const2),
                  pl.BlockSpec((1, D), const2)],
        out_specs=pl.BlockSpec((tb, D), cur),
        out_shape=jax.ShapeDtypeStruct((S, D), F32),
        scratch_shapes=[pltpu.VMEM((3 * tb, nw), BF16),
                        pltpu.VMEM((3 * tb, nw), BF16),
                        pltpu.VMEM((tb, D), BF16),
                        pltpu.VMEM((2, NA_HEADS // 2, WIN_ROWS, 2 * GRID_W, LANES), F32),
                        pltpu.VMEM((NA_HEADS // 2, 2 * GRID_W, WIN_ROWS * GRID_W), F32),
                        pltpu.VMEM((NA_HEADS // 2, 2 * GRID_W, WIN_ROWS * GRID_W), F32),
                        pltpu.VMEM((NA_HEADS // 2, 2 * GRID_W, WIN_ROWS * GRID_W), BF16),
                        pltpu.VMEM((NA_HEADS // 2, 2 * GRID_W, WIN_ROWS * GRID_W), BF16)],
        compiler_params=pltpu.CompilerParams(
            dimension_semantics=("arbitrary",), vmem_limit_bytes=VMEM_LIMIT),
        name="mixer",
    )(q, k, k, k, v, v, v, bg, u, u, u, x, w_out_b, rpb_pad, conv_w, g, b)


def _xattn_kernel(alpha, x_ref, mem_ref, wq_ref, wk_ref, wv_ref, wo_ref,
                  g_ref, be_ref, wrh_ref, wrl_ref,
                  x2_ref, x2b_ref, aff_ref, kmem, vmem):
    i = pl.program_id(0)

    @pl.when(i == 0)
    def _():
        mb = mem_ref[...].astype(BF16)
        kmem[...] = jnp.dot(mb, wk_ref[...], preferred_element_type=F32).astype(BF16)
        vmem[...] = jnp.dot(mb, wv_ref[...], preferred_element_type=F32).astype(BF16)

    x1 = x_ref[...]
    qb = jnp.dot(x1.astype(BF16), wq_ref[...], preferred_element_type=F32).astype(BF16)
    outs = []
    for h in range(MEM_HEADS):
        cs = slice(h * MEM_HEAD_DIM, (h + 1) * MEM_HEAD_DIM)
        s = _dot_nt(qb[:, cs], kmem[:, cs]) * (MEM_HEAD_DIM ** -0.5)
        mx = jnp.max(s, axis=-1, keepdims=True)
        p = jnp.exp(s - mx)
        l = jnp.sum(p, axis=-1, keepdims=True)
        oh = jnp.dot(p.astype(BF16), vmem[:, cs], preferred_element_type=F32)
        outs.append(oh / l)
    o = jnp.concatenate(outs, axis=1).astype(BF16)
    y = jnp.dot(o, wo_ref[...], preferred_element_type=F32)
    x2 = _layer_norm(alpha * x1 + y, g_ref[...], be_ref[...])
    x2_ref[...] = x2
    hi = x2.astype(BF16)
    x2b_ref[...] = hi
    lo = (x2 - hi.astype(F32)).astype(BF16)
    wrh = wrh_ref[...]
    r1 = _dot_nt(jnp.concatenate([wrh, wrl_ref[...]], axis=0), hi)
    lg = r1[0:N_EXPERTS] + r1[N_EXPERTS:2 * N_EXPERTS] + _dot_nt(wrh, lo)
    mx = jnp.max(lg, axis=0, keepdims=True)
    p = jnp.exp(lg - mx)
    aff = p / jnp.sum(p, axis=0, keepdims=True)
    for j in range(aff_ref.shape[0]):
        aff_ref[j] = aff[:, j * LANES:(j + 1) * LANES]


def _xattn(x1, mem, wq_b, wk_b, wv_b, wo_b, g, b, wr_hi, wr_lo, alpha):
    S, D = x1.shape
    M = mem.shape[0]
    mw = MEM_HEADS * MEM_HEAD_DIM
    tm = XATTN_TM
    cur = lambda i: (i, 0)
    const2 = lambda i: (0, 0)
    full = lambda a: pl.BlockSpec(a.shape, const2)
    return pl.pallas_call(
        functools.partial(_xattn_kernel, alpha),
        grid=(S // tm,),
        in_specs=[pl.BlockSpec((tm, D), cur), full(mem), full(wq_b), full(wk_b),
                  full(wv_b), full(wo_b), full(g), full(b), full(wr_hi), full(wr_lo)],
        out_specs=[pl.BlockSpec((tm, D), cur), pl.BlockSpec((tm, D), cur),
                   pl.BlockSpec((tm // LANES, N_EXPERTS, LANES), lambda i: (i, 0, 0))],
        out_shape=[jax.ShapeDtypeStruct((S, D), F32),
                   jax.ShapeDtypeStruct((S, D), BF16),
                   jax.ShapeDtypeStruct((S // LANES, N_EXPERTS, LANES), F32)],
        scratch_shapes=[pltpu.VMEM((M, mw), BF16), pltpu.VMEM((M, mw), BF16)],
        compiler_params=pltpu.CompilerParams(
            dimension_semantics=("arbitrary",), vmem_limit_bytes=VMEM_LIMIT),
        name="xattn",
    )(x1, mem, wq_b, wk_b, wv_b, wo_b, g, b, wr_hi, wr_lo)


def _route_kernel(cap, aff_ref, loc_ref, b_ref, c_ref, incl_scr, tot_scr, sel_scr):
    nblk = aff_ref.shape[0]
    per_tile = ROUTE_TILE // LANES
    ntile = nblk // per_tile

    def count_ge(t):
        ge = jnp.where(aff_ref[...] >= t[None], 1.0, 0.0)
        return jnp.sum(jnp.sum(ge, axis=0), axis=1, keepdims=True)

    def coarse(it, bits):
        cand = bits | jnp.left_shift(jnp.int32(1), 30 - it)
        ok = count_ge(pltpu.bitcast(cand, F32)) >= cap
        return jnp.where(ok, cand, bits)

    def fine(it, lohi):
        lo, hi = lohi
        mid = lo + (hi - lo) * 0.5
        ok = count_ge(mid) >= cap
        return jnp.where(ok, mid, lo), jnp.where(ok, hi, mid)

    bits = lax.fori_loop(0, 31, coarse, jnp.zeros((N_EXPERTS, 1), I32))
    thr, _ = lax.fori_loop(
        0, 30, fine, (pltpu.bitcast(bits, F32), pltpu.bitcast(bits + 1, F32)))
    gt_all = jnp.where(aff_ref[...] > thr[None], 1.0, 0.0)
    n_gt = jnp.sum(jnp.sum(gt_all, axis=0), axis=1, keepdims=True)
    need = cap - n_gt

    ii = lax.broadcasted_iota(I32, (LANES, LANES), 0)
    jj = lax.broadcasted_iota(I32, (LANES, LANES), 1)
    tri = jnp.where(ii <= jj, 1.0, 0.0).astype(BF16)
    ones = jnp.ones((LANES, LANES), BF16)
    flat = (nblk * N_EXPERTS, LANES)
    blocks = (nblk, N_EXPERTS, LANES)

    def block_scans(flags):
        f2 = flags.reshape(flat).astype(BF16)
        incl_scr[...] = jnp.dot(f2, tri, preferred_element_type=F32).reshape(blocks)
        tot_scr[...] = jnp.dot(f2, ones, preferred_element_type=F32).reshape(blocks)

    block_scans(jnp.where(aff_ref[...] == thr[None], 1.0, 0.0))

    def select_block(j, run_eq):
        blk = aff_ref[j]
        tie_rank = run_eq + incl_scr[j]
        sel_scr[j] = (jnp.where(blk > thr, 1.0, 0.0)
                      + jnp.where(blk == thr, 1.0, 0.0) * jnp.where(tie_rank <= need, 1.0, 0.0))
        return run_eq + tot_scr[j]

    zero = jnp.zeros((N_EXPERTS, LANES), F32)
    lax.fori_loop(0, nblk, select_block, zero)
    block_scans(sel_scr[...])

    def tile_body(t, run):
        tile_run = zero
        for j in range(per_tile):
            idx = t * per_tile + j
            sel = sel_scr[idx]
            loc = tile_run + incl_scr[idx] - sel
            loc_ref[idx] = jnp.where(sel > 0.5, loc, -1.0).astype(I32)
            tile_run = tile_run + tot_scr[idx]
        b_ref[t] = run[:, 0:1].astype(I32)
        c_ref[t] = tile_run[:, 0:1].astype(I32)
        return run + tile_run

    lax.fori_loop(0, ntile, tile_body, zero)


def _route(aff3, cap):
    nblk = aff3.shape[0]
    ntile = nblk * LANES // ROUTE_TILE
    full3 = lambda a: pl.BlockSpec(a, lambda: (0, 0, 0))
    return pl.pallas_call(
        functools.partial(_route_kernel, cap),
        in_specs=[full3(aff3.shape)],
        out_specs=[full3(aff3.shape), full3((ntile, N_EXPERTS, 1)),
                   full3((ntile, N_EXPERTS, 1))],
        out_shape=[jax.ShapeDtypeStruct(aff3.shape, I32),
                   jax.ShapeDtypeStruct((ntile, N_EXPERTS, 1), I32),
                   jax.ShapeDtypeStruct((ntile, N_EXPERTS, 1), I32)],
        scratch_shapes=[pltpu.VMEM(aff3.shape, F32)] * 3,
        compiler_params=pltpu.CompilerParams(vmem_limit_bytes=VMEM_LIMIT),
        name="route",
    )(aff3)


def _segment_copies(b_sm, c_sm, t, make_copy, action):
    off = 0
    for e in range(N_EXPERTS):
        c = c_sm[t * N_EXPERTS + e]
        b = b_sm[t * N_EXPERTS + e]

        @pl.when(c > 0)
        def _(e=e, b=b, c=c, off=off):
            action(make_copy(e, b, off, c))

        off = off + c
    return off


def _tile_rows(c_sm, t):
    rows = 0
    for e in range(N_EXPERTS):
        rows = rows + c_sm[t * N_EXPERTS + e]
    return rows


def _dispatch_kernel(cap, b_sm, c_sm, x_ref, loc_ref, xe_hbm, stg, sem):
    t = pl.program_id(0)
    nt = pl.num_programs(0)
    slot = t % 2

    offs = []
    off = 0
    for e in range(N_EXPERTS):
        offs.append(off)
        off = off + c_sm[t * N_EXPERTS + e]
    offs.append(off)
    nchunk = (off + SLOT_CHUNK - 1) // SLOT_CHUNK
    sub = lax.broadcasted_iota(I32, (SLOT_CHUNK, 1), 0)
    lane = lax.broadcasted_iota(I32, (1, LANES), 1)
    seg_lo = jnp.zeros((1, LANES), I32)
    seg_hi = jnp.zeros((1, LANES), I32)
    for e in range(N_EXPERTS):
        seg_lo = jnp.where(lane == e, offs[e], seg_lo)
        seg_hi = jnp.where(lane == e, offs[e + 1], seg_hi)
    seg_lo_f = seg_lo.astype(F32)
    locf = jnp.concatenate(
        [loc_ref[j] for j in range(ROUTE_TILE // LANES)], axis=1).astype(F32)
    locb = jnp.concatenate(
        [locf, jnp.zeros((LANES - N_EXPERTS, ROUTE_TILE), F32)], axis=0).astype(BF16)

    def chunk_body(m, carry):
        base = m * SLOT_CHUNK
        srow = sub + base
        inseg = jnp.where(srow >= seg_lo, jnp.where(srow < seg_hi, 1.0, 0.0), 0.0)
        rank = srow.astype(F32) - jnp.sum(inseg * seg_lo_f, axis=1, keepdims=True)
        want = jnp.dot(inseg.astype(BF16), locb, preferred_element_type=F32)
        p = jnp.where(want == rank, 1.0, 0.0)
        res = jnp.dot(p.astype(BF16), x_ref[...], preferred_element_type=F32)
        row0 = base * SUBLANES if isinstance(m, int) else pl.multiple_of(base * SUBLANES, SUBLANES)
        for s in range(SUBLANES):
            stg[slot, pl.ds(row0 + s, SLOT_CHUNK, stride=SUBLANES), :] = (
                res[:, s * LANES:(s + 1) * LANES])
        return carry

    for m in range(PEELED_CHUNKS):
        chunk_body(m, 0)
    lax.fori_loop(PEELED_CHUNKS, jnp.maximum(nchunk, PEELED_CHUNKS), chunk_body, 0)

    def copy_for(slot_):
        def make(e, dst_row, src_row, n):
            return pltpu.make_async_copy(
                stg.at[slot_, pl.ds(pl.multiple_of(src_row * SUBLANES, SUBLANES), n * SUBLANES)],
                xe_hbm.at[pl.ds(pl.multiple_of((e * cap + dst_row) * SUBLANES, SUBLANES),
                                n * SUBLANES)],
                sem.at[slot_])
        return make

    _segment_copies(b_sm, c_sm, t, copy_for(slot), lambda cp: cp.start())

    def wait_tile(t_, slot_):
        rows = _tile_rows(c_sm, t_)

        @pl.when(rows > 0)
        def _():
            n = rows * SUBLANES
            pltpu.make_async_copy(stg.at[slot_, pl.ds(0, n)], xe_hbm.at[pl.ds(0, n)],
                                  sem.at[slot_]).wait()

    @pl.when(t > 0)
    def _():
        wait_tile(t - 1, 1 - slot)

    @pl.when(t == nt - 1)
    def _():
        wait_tile(t, slot)


def _dispatch(b_flat, c_flat, x2b, loc3, cap):
    S, D = x2b.shape
    nt = S // ROUTE_TILE
    per_tile = ROUTE_TILE // LANES
    max_rows = N_EXPERTS * ROUTE_TILE
    return pl.pallas_call(
        functools.partial(_dispatch_kernel, cap),
        grid_spec=pltpu.PrefetchScalarGridSpec(
            num_scalar_prefetch=2,
            grid=(nt,),
            in_specs=[pl.BlockSpec((ROUTE_TILE, D), lambda t, b, c: (t, 0)),
                      pl.BlockSpec((per_tile, N_EXPERTS, LANES), lambda t, b, c: (t, 0, 0))],
            out_specs=pl.BlockSpec(memory_space=pl.ANY),
            scratch_shapes=[pltpu.VMEM((2, max_rows * SUBLANES, LANES), F32),
                            pltpu.SemaphoreType.DMA((2,))]),
        out_shape=jax.ShapeDtypeStruct((N_EXPERTS * cap * SUBLANES, LANES), F32),
        compiler_params=pltpu.CompilerParams(
            dimension_semantics=("arbitrary",), vmem_limit_bytes=VMEM_LIMIT),
        name="dispatch",
    )(b_flat, c_flat, x2b, loc3)


def _ffn_kernel(xe_ref, wg_ref, wu_ref, wd_ref, wr_ref, o_ref, xb, acc, gate):
    e = pl.program_id(0)
    f = pl.program_id(2)
    tm = xb.shape[0]

    @pl.when(f == 0)
    def _():
        for s in range(SUBLANES):
            xb[:, s * LANES:(s + 1) * LANES] = (
                xe_ref[pl.ds(s, tm, stride=SUBLANES), :].astype(BF16))
        lg = jnp.dot(xb[...], wr_ref[...], preferred_element_type=F32)
        lane = lax.broadcasted_iota(I32, (1, LANES), 1)
        lg = jnp.where(lane < N_EXPERTS, lg, NEG)
        p = jnp.exp(lg - jnp.max(lg, axis=-1, keepdims=True))
        mine = jnp.sum(jnp.where(lane == e, p, 0.0), axis=-1, keepdims=True)
        gate[...] = mine / jnp.sum(p, axis=-1, keepdims=True)

    def down_proj():
        x = xb[...]
        g = jnp.dot(x, wg_ref[...].astype(BF16), preferred_element_type=F32)
        u = jnp.dot(x, wu_ref[...].astype(BF16), preferred_element_type=F32)
        h = (g * jax.nn.sigmoid(g) * u).astype(BF16)
        return jnp.dot(h, wd_ref[...].astype(BF16), preferred_element_type=F32)

    last = pl.num_programs(2) - 1

    @pl.when(f == 0)
    def _():
        acc[...] = down_proj()

    @pl.when((f > 0) & (f < last))
    def _():
        acc[...] += down_proj()

    @pl.when(f == last)
    def _():
        y = (acc[...] + down_proj()) * gate[...]
        for s in range(SUBLANES):
            o_ref[pl.ds(s, tm, stride=SUBLANES), :] = y[:, s * LANES:(s + 1) * LANES]


def _ffn(xe2d, w_gate, w_up, w_down, wr_pad, cap):
    E, D, FF = w_gate.shape
    tm, tf = FFN_TM, FFN_TF
    mt = cap // tm
    rows = lambda e, m, f: (e * mt + m, 0)
    return pl.pallas_call(
        _ffn_kernel,
        grid=(E, mt, FF // tf),
        in_specs=[pl.BlockSpec((tm * SUBLANES, LANES), rows),
                  pl.BlockSpec((None, D, tf), lambda e, m, f: (e, 0, f)),
                  pl.BlockSpec((None, D, tf), lambda e, m, f: (e, 0, f)),
                  pl.BlockSpec((None, tf, D), lambda e, m, f: (e, f, 0)),
                  pl.BlockSpec(wr_pad.shape, lambda e, m, f: (0, 0))],
        out_specs=pl.BlockSpec((tm * SUBLANES, LANES), rows),
        out_shape=jax.ShapeDtypeStruct(xe2d.shape, F32),
        scratch_shapes=[pltpu.VMEM((tm, D), BF16), pltpu.VMEM((tm, D), F32),
                        pltpu.VMEM((tm, 1), F32)],
        compiler_params=pltpu.CompilerParams(
            dimension_semantics=("arbitrary", "arbitrary", "arbitrary"),
            vmem_limit_bytes=VMEM_LIMIT),
        name="ffn",
    )(xe2d, w_gate, w_up, w_down, wr_pad)


def _combine_kernel(cap, alpha, b_sm, c_sm, x_ref, loc_ref, g_ref, be_ref, ye_hbm,
                    o_ref, ybuf, sem, hl_scr, z_scr):
    t = pl.program_id(0)
    nt = pl.num_programs(0)
    slot = t % 2

    def copy_for(slot_):
        def make(e, src_row, dst_row, n):
            return pltpu.make_async_copy(
                ye_hbm.at[pl.ds(pl.multiple_of((e * cap + src_row) * SUBLANES, SUBLANES),
                                n * SUBLANES)],
                ybuf.at[slot_, pl.ds(pl.multiple_of(dst_row * SUBLANES, SUBLANES), n * SUBLANES)],
                sem.at[slot_])
        return make

    @pl.when(t == 0)
    def _():
        ybuf[...] = jnp.zeros_like(ybuf)
        _segment_copies(b_sm, c_sm, t, copy_for(slot), lambda cp: cp.start())

    @pl.when(t < nt - 1)
    def _():
        _segment_copies(b_sm, c_sm, t + 1, copy_for(1 - slot), lambda cp: cp.start())

    rows_in = _tile_rows(c_sm, t)

    @pl.when(rows_in > 0)
    def _():
        n = rows_in * SUBLANES
        pltpu.make_async_copy(ye_hbm.at[pl.ds(0, n)], ybuf.at[slot, pl.ds(0, n)],
                              sem.at[slot]).wait()

    offs = []
    off = 0
    for e in range(N_EXPERTS):
        offs.append(off)
        off = off + c_sm[t * N_EXPERTS + e]
    offs.append(off)
    total = off
    nchunk = (total + SLOT_CHUNK - 1) // SLOT_CHUNK
    lane = lax.broadcasted_iota(I32, (1, SLOT_CHUNK), 1)
    sub = lax.broadcasted_iota(I32, (SLOT_CHUNK, 1), 0)
    eid = lax.broadcasted_iota(I32, (N_EXPERTS, 1), 0)
    seg_lo = jnp.zeros((N_EXPERTS, 1), I32)
    seg_hi = jnp.zeros((N_EXPERTS, 1), I32)
    for e in range(N_EXPERTS):
        seg_lo = jnp.where(eid == e, offs[e], seg_lo)
        seg_hi = jnp.where(eid == e, offs[e + 1], seg_hi)
    seg_lo_f = seg_lo.astype(F32)
    locb = loc_ref[...].astype(F32).astype(BF16)

    z_scr[...] = alpha * x_ref[...]

    def chunk_body(m, carry):
        base = m * SLOT_CHUNK
        srow = lane + base
        inseg = jnp.where(srow >= seg_lo, jnp.where(srow < seg_hi, 1.0, 0.0), 0.0)
        rank = srow.astype(F32) - jnp.sum(inseg * seg_lo_f, axis=0, keepdims=True)
        want = jnp.dot(locb, inseg.astype(BF16), preferred_element_type=F32)
        ptb = jnp.where(want == rank, 1.0, 0.0).astype(BF16)
        row0 = base * SUBLANES if isinstance(m, int) else pl.multiple_of(base * SUBLANES, SUBLANES)
        yc = jnp.concatenate(
            [ybuf[slot, pl.ds(row0 + s, SLOT_CHUNK, stride=SUBLANES), :]
             for s in range(SUBLANES)], axis=1)
        yc = jnp.where(sub < total - base, yc, 0.0)
        hi = yc.astype(BF16)
        hl = hl_scr.at[m % PEELED_CHUNKS]
        hl[0:SLOT_CHUNK] = hi
        hl[SLOT_CHUNK:2 * SLOT_CHUNK] = (yc - hi.astype(F32)).astype(BF16)
        z_scr[...] += jnp.dot(jnp.concatenate([ptb, ptb], axis=1), hl[...],
                              preferred_element_type=F32)
        return carry

    for m in range(PEELED_CHUNKS):
        chunk_body(m, 0)
    lax.fori_loop(PEELED_CHUNKS, jnp.maximum(nchunk, PEELED_CHUNKS), chunk_body, 0)

    o_ref[...] = _layer_norm(z_scr[...], g_ref[...], be_ref[...])


def _combine(b_flat, c_flat, x2, loc_t, g, b, ye2d, cap, alpha):
    S, D = x2.shape
    nt = S // ROUTE_TILE
    max_rows = N_EXPERTS * ROUTE_TILE
    cur = lambda t, b_, c_: (t, 0)
    const2 = lambda t, b_, c_: (0, 0)
    return pl.pallas_call(
        functools.partial(_combine_kernel, cap, alpha),
        grid_spec=pltpu.PrefetchScalarGridSpec(
            num_scalar_prefetch=2,
            grid=(nt,),
            in_specs=[pl.BlockSpec((ROUTE_TILE, D), cur),
                      pl.BlockSpec((ROUTE_TILE, N_EXPERTS), cur),
                      pl.BlockSpec((1, D), const2),
                      pl.BlockSpec((1, D), const2),
                      pl.BlockSpec(memory_space=pl.ANY)],
            out_specs=pl.BlockSpec((ROUTE_TILE, D), cur),
            scratch_shapes=[pltpu.VMEM((2, max_rows * SUBLANES, LANES), F32),
                            pltpu.SemaphoreType.DMA((2,)),
                            pltpu.VMEM((PEELED_CHUNKS, 2 * SLOT_CHUNK, D), BF16),
                            pltpu.VMEM((ROUTE_TILE, D), F32)]),
        out_shape=jax.ShapeDtypeStruct((S, D), F32),
        compiler_params=pltpu.CompilerParams(
            dimension_semantics=("arbitrary",), vmem_limit_bytes=VMEM_LIMIT),
        name="combine",
    )(b_flat, c_flat, x2, loc_t, g, b, ye2d)


def _layer(x, mem, w_in, rpb, conv_w, w_mix_out, ln1_g, ln1_b, wq, wk, wv, wo,
           ln2_g, ln2_b, w_router, w_gate, w_up, w_down, ln3_g, ln3_b, alpha):
    S, D = x.shape
    cap = CAPACITY_FACTOR * S // N_EXPERTS
    assert S % TOK_BLOCK == 0 and S % PROJ_TM == 0 and S % XATTN_TM == 0
    assert S // GRID_W >= 2 * ROW_BLOCK
    assert cap % FFN_TM == 0 and cap <= S
    row = lambda a: a.reshape(1, -1)

    q, k, v, bg, u = _proj(x, w_in.astype(BF16))
    x1 = _mixer(x, q, k, v, bg, u, w_mix_out.astype(BF16), rpb,
                conv_w, row(ln1_g), row(ln1_b), alpha)

    wr_t = w_router.T
    wr_hi = wr_t.astype(BF16)
    wr_lo = (wr_t - wr_hi.astype(F32)).astype(BF16)
    x2, x2b, aff3 = _xattn(x1, mem, wq.astype(BF16), wk.astype(BF16), wv.astype(BF16),
                           wo.astype(BF16), row(ln2_g), row(ln2_b), wr_hi, wr_lo, alpha)

    loc3, b3, c3 = _route(aff3, cap)
    b_flat = b3.reshape(-1)
    c_flat = c3.reshape(-1)
    loc_t = loc3.transpose(0, 2, 1).reshape(S, N_EXPERTS)

    xe2d = _dispatch(b_flat, c_flat, x2b, loc3, cap)
    wr_pad = jnp.pad(w_router.astype(BF16), ((0, 0), (0, LANES - N_EXPERTS)))
    ye2d = _ffn(xe2d, w_gate, w_up, w_down, wr_pad, cap)
    return _combine(b_flat, c_flat, x2, loc_t, row(ln3_g), row(ln3_b), ye2d, cap, alpha)


def kernel(x, mem, w_in, na_rpb, conv_w, w_mix_out, ln1_g, ln1_b, w_mem_q, w_mem_k,
           w_mem_v, w_mem_out, ln2_g, ln2_b, w_router, w_exp_gate, w_exp_up,
           w_exp_down, ln3_g, ln3_b):
    depth = w_in.shape[0]
    alpha = (2 * depth) ** 0.25
    outs = []
    for bi in range(x.shape[0]):
        xb = x[bi]
        for l in range(depth):
            xb = _layer(xb, mem[bi], w_in[l], na_rpb[l], conv_w[l], w_mix_out[l],
                        ln1_g[l], ln1_b[l], w_mem_q[l], w_mem_k[l], w_mem_v[l],
                        w_mem_out[l], ln2_g[l], ln2_b[l], w_router[l], w_exp_gate[l],
                        w_exp_up[l], w_exp_down[l], ln3_g[l], ln3_b[l], alpha)
        outs.append(xb)
    return jnp.stack(outs)
```

```python
import functools

import jax
import jax.numpy as jnp
from jax import lax
from jax.experimental import pallas as pl
from jax.experimental.pallas import tpu as pltpu

F32 = jnp.float32
BF16 = jnp.bfloat16
I32 = jnp.int32

GRID_W = 64
HEAD_DIM = 64
NA_HEADS = 8
NA_WIDTH = NA_HEADS * HEAD_DIM
WIN_ROWS = 8
WIN_COLS = 16
MEM_HEADS = 4
MEM_HEAD_DIM = 128
N_EXPERTS = 16
CAPACITY_FACTOR = 2
LN_EPS = 1e-5
NEG = -1e30

LANES = 128
SUBLANES = 8
VMEM_LIMIT = 56 * 1024 * 1024

ROW_BLOCK = 8
TOK_BLOCK = ROW_BLOCK * GRID_W
PROJ_TM = 1024
XATTN_TM = 1024
ROUTE_TILE = 256
SLOT_CHUNK = 256
SM_ROWS = 16
PEELED_CHUNKS = 2
DISPATCH_TILES = 2
DISPATCH_SLOTS = 3
FFN_TM = 1024
FFN_TF = 512


def _layer_norm(z, g, b):
    mu = jnp.mean(z, axis=-1, keepdims=True)
    zc = z - mu
    var = jnp.mean(zc * zc, axis=-1, keepdims=True)
    return zc * lax.rsqrt(var + LN_EPS) * g + b


def _dot_nt(a, b):
    return lax.dot_general(a, b, (((1,), (1,)), ((), ())),
                           preferred_element_type=F32)


def _proj_kernel(x_ref, w_ref, q_ref, k_ref, v_ref, b_ref, u_ref):
    xb = x_ref[...].astype(BF16)
    nw = NA_WIDTH

    def mm(c0):
        return jnp.dot(xb, w_ref[:, c0:c0 + nw], preferred_element_type=F32)

    q_ref[...] = (mm(0) * (HEAD_DIM ** -0.5)).astype(BF16)
    k_ref[...] = mm(nw).astype(BF16)
    v_ref[...] = mm(2 * nw).astype(BF16)
    b_ref[...] = mm(3 * nw)
    u_ref[...] = mm(4 * nw) * mm(5 * nw)


def _proj(x, w_in_b):
    S, D = x.shape
    nw = NA_WIDTH
    tm = PROJ_TM
    blk = lambda i: (i, 0)
    return pl.pallas_call(
        _proj_kernel,
        grid=(S // tm,),
        in_specs=[pl.BlockSpec((tm, D), blk),
                  pl.BlockSpec((D, 6 * nw), lambda i: (0, 0))],
        out_specs=[pl.BlockSpec((tm, nw), blk)] * 5,
        out_shape=[jax.ShapeDtypeStruct((S, nw), BF16)] * 3
        + [jax.ShapeDtypeStruct((S, nw), F32)] * 2,
        compiler_params=pltpu.CompilerParams(
            dimension_semantics=("arbitrary",), vmem_limit_bytes=VMEM_LIMIT),
        name="proj",
    )(x, w_in_b)


def _build_na_bias(rpb_ref, bias_ref):
    shape = (GRID_W, LANES)
    qc = lax.broadcasted_iota(I32, shape, 0)
    ln = lax.broadcasted_iota(I32, shape, 1)
    kc = ln % GRID_W
    c_start = jnp.clip(qc - WIN_COLS // 2, 0, GRID_W - WIN_COLS)
    base = LANES - (WIN_COLS - 1)

    def canvas(h, d):
        both = rpb_ref[h, d:d + 1, :] + pltpu.roll(rpb_ref[h, d + 1:d + 2, :], GRID_W, 1)
        t = pltpu.roll(pltpu.roll(jnp.broadcast_to(both, shape), base, 1), 0, 1,
                       stride=1, stride_axis=0)
        return jnp.where(kc >= c_start, jnp.where(kc < c_start + WIN_COLS, t, NEG), NEG)

    for par in range(2):
        for hp in range(NA_HEADS // 2):
            for m in range(WIN_ROWS):
                d = 2 * m + par
                bias_ref[par, hp, m, 0:GRID_W] = canvas(2 * hp, d)
                bias_ref[par, hp, m, GRID_W:2 * GRID_W] = canvas(2 * hp + 1, d)


def _mixer_kernel(rows, alpha,
                  q_ref, kp_ref, kc_ref, kn_ref, vp_ref, vc_ref, vn_ref,
                  b_ref, u_ref, up_ref, un_ref, x_ref, wout_ref, rpb_ref,
                  cw_ref, g_ref, be_ref, o_ref, kbuf, vbuf, mix, bias_ref,
                  s_scr0, s_scr1, p_scr0, p_scr1):
    i = pl.program_id(0)
    nb = pl.num_programs(0)
    tb = TOK_BLOCK
    win = WIN_ROWS * GRID_W

    @pl.when(i == 0)
    def _():
        _build_na_bias(rpb_ref, bias_ref)

    kbuf[0:tb] = kp_ref[...]
    kbuf[tb:2 * tb] = kc_ref[...]
    kbuf[2 * tb:3 * tb] = kn_ref[...]
    vbuf[0:tb] = vp_ref[...]
    vbuf[tb:2 * tb] = vc_ref[...]
    vbuf[2 * tb:3 * tb] = vn_ref[...]

    lane = lax.broadcasted_iota(I32, (1, LANES), 1)
    first_head = lane < HEAD_DIM
    m_lo = jnp.where(first_head, 1.0, 0.0).astype(BF16)
    m_hi = jnp.where(first_head, 0.0, 1.0).astype(BF16)

    def window(r):
        grow = i * ROW_BLOCK + r
        start = jnp.clip(grow - WIN_ROWS // 2, 0, rows - WIN_ROWS)
        w0 = pl.multiple_of((start - (i - 1) * ROW_BLOCK) * GRID_W, GRID_W)
        d0 = start - grow + WIN_ROWS
        return w0, d0 % 2, d0 // 2

    def scores(r, s_out):
        w0, _, _ = window(r)
        q0 = r * GRID_W if isinstance(r, int) else pl.multiple_of(r * GRID_W, GRID_W)
        for hp in range(NA_HEADS // 2):
            cs = slice(hp * LANES, (hp + 1) * LANES)
            qp = q_ref[pl.ds(q0, GRID_W), cs]
            lhs = jnp.concatenate([qp * m_lo, qp * m_hi], axis=0)
            kw = kbuf[pl.ds(w0, win), cs]
            s_out[hp] = _dot_nt(lhs, kw)

    def probs(r, s_in, p_out):
        _, par, m0 = window(r)
        for hp in range(NA_HEADS // 2):
            for rb in range(2 * GRID_W // SM_ROWS):
                rs = slice(rb * SM_ROWS, (rb + 1) * SM_ROWS)
                bias = jnp.concatenate(
                    [bias_ref[par, hp, m0 + j, rs, :] for j in range(WIN_ROWS // 2)], axis=1)
                s = s_in[hp, rs, :] + bias
                p = jnp.exp(s - jnp.max(s, axis=-1, keepdims=True))
                p = p * (1.0 / jnp.sum(p, axis=-1, keepdims=True))
                p_out[hp, rs, :] = p.astype(BF16)

    def values(r, p_in):
        w0, _, _ = window(r)
        q0 = r * GRID_W if isinstance(r, int) else pl.multiple_of(r * GRID_W, GRID_W)
        for hp in range(NA_HEADS // 2):
            cs = slice(hp * LANES, (hp + 1) * LANES)
            vw = vbuf[pl.ds(w0, win), cs]
            o2 = jnp.dot(p_in[hp], vw, preferred_element_type=F32)
            y = jnp.where(first_head, o2[0:GRID_W], o2[GRID_W:2 * GRID_W])
            mix[pl.ds(q0, GRID_W), cs] = y.astype(BF16)

    s_buf = (s_scr0, s_scr1)
    p_buf = (p_scr0, p_scr1)

    def steady(k, carry):
        for half in range(2):
            r = 1 + 2 * k + half
            values(r - 1, p_buf[half])
            probs(r, s_buf[1 - half], p_buf[1 - half])
            scores(r + 1, s_buf[half])
        return carry

    scores(0, s_buf[0])
    probs(0, s_buf[0], p_buf[0])
    scores(1, s_buf[1])
    lax.fori_loop(0, (ROW_BLOCK - 2) // 2, steady, 0)
    values(ROW_BLOCK - 2, p_buf[0])
    probs(ROW_BLOCK - 1, s_buf[1], p_buf[1])
    values(ROW_BLOCK - 1, p_buf[1])

    u = u_ref[...]
    prev_row = up_ref[SUBLANES - 1:SUBLANES, :] * jnp.where(i > 0, 1.0, 0.0)
    next_row = un_ref[0:1, :] * jnp.where(i < nb - 1, 1.0, 0.0)
    rid = lax.broadcasted_iota(I32, (tb, 1), 0)
    um1 = jnp.where(rid == 0, prev_row, pltpu.roll(u, 1, axis=0))
    up1 = jnp.where(rid == tb - 1, next_row, pltpu.roll(u, tb - 1, axis=0))
    cw = cw_ref[...]
    yc = b_ref[...] * (cw[0:1] * um1 + cw[1:2] * u + cw[2:3] * up1)
    mix[:, NA_WIDTH:] = yc.astype(BF16)

    y = jnp.dot(mix[...], wout_ref[...], preferred_element_type=F32)
    z = alpha * x_ref[...] + y
    o_ref[...] = _layer_norm(z, g_ref[...], be_ref[...])


def _mixer(x, q, k, v, bg, u, w_out_b, rpb, conv_w, g, b, alpha):
    S, D = x.shape
    rpb_pad = jnp.pad(rpb.astype(F32), ((0, 0), (1, 1), (0, LANES - rpb.shape[2])))
    nw = NA_WIDTH
    tb = TOK_BLOCK
    nb = S // tb
    rows = S // GRID_W
    cur = lambda i: (i, 0)
    prv = lambda i: (jnp.maximum(i - 1, 0), 0)
    nxt = lambda i: (jnp.minimum(i + 1, nb - 1), 0)
    halo = tb // SUBLANES
    hprev = lambda i: (jnp.maximum(i * halo - 1, 0), 0)
    hnext = lambda i: (jnp.minimum((i + 1) * halo, S // SUBLANES - 1), 0)
    const2 = lambda i: (0, 0)
    kv = lambda im: pl.BlockSpec((tb, nw), im)
    return pl.pallas_call(
        functools.partial(_mixer_kernel, rows, alpha),
        grid=(nb,),
        in_specs=[kv(cur), kv(prv), kv(cur), kv(nxt), kv(prv), kv(cur), kv(nxt),
                  kv(cur), kv(cur),
                  pl.BlockSpec((SUBLANES, nw), hprev),
                  pl.BlockSpec((SUBLANES, nw), hnext),
                  pl.BlockSpec((tb, D), cur),
                  pl.BlockSpec((D, D), const2),
                  pl.BlockSpec(rpb_pad.shape, lambda i: (0, 0, 0)),
                  pl.BlockSpec(conv_w.shape, const2),
                  pl.BlockSpec((1, D), const2),
                  pl.BlockSpec((1, D), const2)],
        out_specs=pl.BlockSpec((tb, D), cur),
        out_shape=jax.ShapeDtypeStruct((S, D), F32),
        scratch_shapes=[pltpu.VMEM((3 * tb, nw), BF16),
                        pltpu.VMEM((3 * tb, nw), BF16),
                        pltpu.VMEM((tb, D), BF16),
                        pltpu.VMEM((2, NA_HEADS // 2, WIN_ROWS, 2 * GRID_W, LANES), F32),
                        pltpu.VMEM((NA_HEADS // 2, 2 * GRID_W, WIN_ROWS * GRID_W), F32),
                        pltpu.VMEM((NA_HEADS // 2, 2 * GRID_W, WIN_ROWS * GRID_W), F32),
                        pltpu.VMEM((NA_HEADS // 2, 2 * GRID_W, WIN_ROWS * GRID_W), BF16),
                        pltpu.VMEM((NA_HEADS // 2, 2 * GRID_W, WIN_ROWS * GRID_W), BF16)],
        compiler_params=pltpu.CompilerParams(
            dimension_semantics=("arbitrary",), vmem_limit_bytes=VMEM_LIMIT),
        name="mixer",
    )(q, k, k, k, v, v, v, bg, u, u, u, x, w_out_b, rpb_pad, conv_w, g, b)


def _xattn_kernel(alpha, x_ref, mem_ref, wq_ref, wk_ref, wv_ref, wo_ref,
                  g_ref, be_ref, wrh_ref, wrl_ref,
                  x2_ref, x2b_ref, aff_ref, kmem, vmem):
    i = pl.program_id(0)

    @pl.when(i == 0)
    def _():
        mb = mem_ref[...].astype(BF16)
        kmem[...] = jnp.dot(mb, wk_ref[...], preferred_element_type=F32).astype(BF16)
        vmem[...] = jnp.dot(mb, wv_ref[...], preferred_element_type=F32).astype(BF16)

    x1 = x_ref[...]
    qb = jnp.dot(x1.astype(BF16), wq_ref[...], preferred_element_type=F32).astype(BF16)
    outs = []
    for h in range(MEM_HEADS):
        cs = slice(h * MEM_HEAD_DIM, (h + 1) * MEM_HEAD_DIM)
        s = _dot_nt(qb[:, cs], kmem[:, cs]) * (MEM_HEAD_DIM ** -0.5)
        mx = jnp.max(s, axis=-1, keepdims=True)
        p = jnp.exp(s - mx)
        l = jnp.sum(p, axis=-1, keepdims=True)
        oh = jnp.dot(p.astype(BF16), vmem[:, cs], preferred_element_type=F32)
        outs.append(oh / l)
    o = jnp.concatenate(outs, axis=1).astype(BF16)
    y = jnp.dot(o, wo_ref[...], preferred_element_type=F32)
    x2 = _layer_norm(alpha * x1 + y, g_ref[...], be_ref[...])
    x2_ref[...] = x2
    hi = x2.astype(BF16)
    x2b_ref[...] = hi
    lo = (x2 - hi.astype(F32)).astype(BF16)
    wrh = wrh_ref[...]
    r1 = _dot_nt(jnp.concatenate([wrh, wrl_ref[...]], axis=0), hi)
    lg = r1[0:N_EXPERTS] + r1[N_EXPERTS:2 * N_EXPERTS] + _dot_nt(wrh, lo)
    mx = jnp.max(lg, axis=0, keepdims=True)
    p = jnp.exp(lg - mx)
    aff = p / jnp.sum(p, axis=0, keepdims=True)
    for j in range(aff_ref.shape[0]):
        aff_ref[j] = aff[:, j * LANES:(j + 1) * LANES]


def _xattn(x1, mem, wq_b, wk_b, wv_b, wo_b, g, b, wr_hi, wr_lo, alpha):
    S, D = x1.shape
    M = mem.shape[0]
    mw = MEM_HEADS * MEM_HEAD_DIM
    tm = XATTN_TM
    cur = lambda i: (i, 0)
    const2 = lambda i: (0, 0)
    full = lambda a: pl.BlockSpec(a.shape, const2)
    return pl.pallas_call(
        functools.partial(_xattn_kernel, alpha),
        grid=(S // tm,),
        in_specs=[pl.BlockSpec((tm, D), cur), full(mem), full(wq_b), full(wk_b),
                  full(wv_b), full(wo_b), full(g), full(b), full(wr_hi), full(wr_lo)],
        out_specs=[pl.BlockSpec((tm, D), cur), pl.BlockSpec((tm, D), cur),
                   pl.BlockSpec((tm // LANES, N_EXPERTS, LANES), lambda i: (i, 0, 0))],
        out_shape=[jax.ShapeDtypeStruct((S, D), F32),
                   jax.ShapeDtypeStruct((S, D), BF16),
                   jax.ShapeDtypeStruct((S // LANES, N_EXPERTS, LANES), F32)],
        scratch_shapes=[pltpu.VMEM((M, mw), BF16), pltpu.VMEM((M, mw), BF16)],
        compiler_params=pltpu.CompilerParams(
            dimension_semantics=("arbitrary",), vmem_limit_bytes=VMEM_LIMIT),
        name="xattn",
    )(x1, mem, wq_b, wk_b, wv_b, wo_b, g, b, wr_hi, wr_lo)


def _route_kernel(cap, aff_ref, loc_ref, b_ref, c_ref, incl_scr, tot_scr, sel_scr):
    nblk = aff_ref.shape[0]
    per_tile = ROUTE_TILE // LANES
    ntile = nblk // per_tile

    def count_ge(t):
        ge = jnp.where(aff_ref[...] >= t[None], 1.0, 0.0)
        return jnp.sum(jnp.sum(ge, axis=0), axis=1, keepdims=True)

    def coarse(it, bits):
        cand = bits | jnp.left_shift(jnp.int32(1), 30 - it)
        ok = count_ge(pltpu.bitcast(cand, F32)) >= cap
        return jnp.where(ok, cand, bits)

    def fine(it, lohi):
        lo, hi = lohi
        mid = lo + (hi - lo) * 0.5
        ok = count_ge(mid) >= cap
        return jnp.where(ok, mid, lo), jnp.where(ok, hi, mid)

    bits = lax.fori_loop(0, 31, coarse, jnp.zeros((N_EXPERTS, 1), I32))
    thr, _ = lax.fori_loop(
        0, 30, fine, (pltpu.bitcast(bits, F32), pltpu.bitcast(bits + 1, F32)))
    gt_all = jnp.where(aff_ref[...] > thr[None], 1.0, 0.0)
    n_gt = jnp.sum(jnp.sum(gt_all, axis=0), axis=1, keepdims=True)
    need = cap - n_gt

    ii = lax.broadcasted_iota(I32, (LANES, LANES), 0)
    jj = lax.broadcasted_iota(I32, (LANES, LANES), 1)
    tri = jnp.where(ii <= jj, 1.0, 0.0).astype(BF16)
    ones = jnp.ones((LANES, LANES), BF16)
    flat = (nblk * N_EXPERTS, LANES)
    blocks = (nblk, N_EXPERTS, LANES)

    def block_scans(flags):
        f2 = flags.reshape(flat).astype(BF16)
        incl_scr[...] = jnp.dot(f2, tri, preferred_element_type=F32).reshape(blocks)
        tot_scr[...] = jnp.dot(f2, ones, preferred_element_type=F32).reshape(blocks)

    block_scans(jnp.where(aff_ref[...] == thr[None], 1.0, 0.0))

    def select_block(j, run_eq):
        blk = aff_ref[j]
        tie_rank = run_eq + incl_scr[j]
        sel_scr[j] = (jnp.where(blk > thr, 1.0, 0.0)
                      + jnp.where(blk == thr, 1.0, 0.0) * jnp.where(tie_rank <= need, 1.0, 0.0))
        return run_eq + tot_scr[j]

    zero = jnp.zeros((N_EXPERTS, LANES), F32)
    lax.fori_loop(0, nblk, select_block, zero)
    block_scans(sel_scr[...])

    def tile_body(t, run):
        tile_run = zero
        for j in range(per_tile):
            idx = t * per_tile + j
            sel = sel_scr[idx]
            loc = tile_run + incl_scr[idx] - sel
            loc_ref[idx] = jnp.where(sel > 0.5, loc, -1.0).astype(I32)
            tile_run = tile_run + tot_scr[idx]
        b_ref[t] = run[:, 0:1].astype(I32)
        c_ref[t] = tile_run[:, 0:1].astype(I32)
        return run + tile_run

    lax.fori_loop(0, ntile, tile_body, zero)


def _route(aff3, cap):
    nblk = aff3.shape[0]
    ntile = nblk * LANES // ROUTE_TILE
    full3 = lambda a: pl.BlockSpec(a, lambda: (0, 0, 0))
    return pl.pallas_call(
        functools.partial(_route_kernel, cap),
        in_specs=[full3(aff3.shape)],
        out_specs=[full3(aff3.shape), full3((ntile, N_EXPERTS, 1)),
                   full3((ntile, N_EXPERTS, 1))],
        out_shape=[jax.ShapeDtypeStruct(aff3.shape, I32),
                   jax.ShapeDtypeStruct((ntile, N_EXPERTS, 1), I32),
                   jax.ShapeDtypeStruct((ntile, N_EXPERTS, 1), I32)],
        scratch_shapes=[pltpu.VMEM(aff3.shape, F32)] * 3,
        compiler_params=pltpu.CompilerParams(vmem_limit_bytes=VMEM_LIMIT),
        name="route",
    )(aff3)


def _segment_copies(b_sm, c_sm, t, make_copy, action):
    off = 0
    for e in range(N_EXPERTS):
        c = c_sm[t * N_EXPERTS + e]
        b = b_sm[t * N_EXPERTS + e]

        @pl.when(c > 0)
        def _(e=e, b=b, c=c, off=off):
            action(make_copy(e, b, off, c))

        off = off + c
    return off


def _tile_rows(c_sm, t):
    rows = 0
    for e in range(N_EXPERTS):
        rows = rows + c_sm[t * N_EXPERTS + e]
    return rows


def _dispatch_kernel(cap, b_sm, c_sm, x_ref, loc_ref, xe_hbm, stg, sem):
    step = pl.program_id(0)
    nsteps = pl.num_programs(0)
    per_tile = ROUTE_TILE // LANES
    sub = lax.broadcasted_iota(I32, (SLOT_CHUNK, 1), 0)
    lane = lax.broadcasted_iota(I32, (1, LANES), 1)

    def copy_for(slot_):
        def make(e, dst_row, src_row, n):
            return pltpu.make_async_copy(
                stg.at[slot_, pl.ds(pl.multiple_of(src_row * SUBLANES, SUBLANES), n * SUBLANES)],
                xe_hbm.at[pl.ds(pl.multiple_of((e * cap + dst_row) * SUBLANES, SUBLANES),
                                n * SUBLANES)],
                sem.at[slot_])
        return make

    def wait_tile(t_):
        rows = _tile_rows(c_sm, t_)
        slot_ = t_ % DISPATCH_SLOTS

        @pl.when(rows > 0)
        def _():
            n = rows * SUBLANES
            pltpu.make_async_copy(stg.at[slot_, pl.ds(0, n)], xe_hbm.at[pl.ds(0, n)],
                                  sem.at[slot_]).wait()

    def one_tile(k):
        t = step * DISPATCH_TILES + k
        slot = t % DISPATCH_SLOTS
        tok = slice(k * ROUTE_TILE, (k + 1) * ROUTE_TILE)
        offs = []
        off = 0
        for e in range(N_EXPERTS):
            offs.append(off)
            off = off + c_sm[t * N_EXPERTS + e]
        offs.append(off)
        nchunk = (off + SLOT_CHUNK - 1) // SLOT_CHUNK
        seg_lo = jnp.zeros((1, LANES), I32)
        seg_hi = jnp.zeros((1, LANES), I32)
        for e in range(N_EXPERTS):
            seg_lo = jnp.where(lane == e, offs[e], seg_lo)
            seg_hi = jnp.where(lane == e, offs[e + 1], seg_hi)
        seg_lo_f = seg_lo.astype(F32)
        locf = jnp.concatenate(
            [loc_ref[k * per_tile + j] for j in range(per_tile)], axis=1).astype(F32)
        locb = jnp.concatenate(
            [locf, jnp.zeros((LANES - N_EXPERTS, ROUTE_TILE), F32)], axis=0).astype(BF16)

        def chunk_body(m, carry):
            base = m * SLOT_CHUNK
            srow = sub + base
            inseg = jnp.where(srow >= seg_lo, jnp.where(srow < seg_hi, 1.0, 0.0), 0.0)
            rank = srow.astype(F32) - jnp.sum(inseg * seg_lo_f, axis=1, keepdims=True)
            want = jnp.dot(inseg.astype(BF16), locb, preferred_element_type=F32)
            p = jnp.where(want == rank, 1.0, 0.0)
            res = jnp.dot(p.astype(BF16), x_ref[tok, :], preferred_element_type=F32)
            row0 = (base * SUBLANES if isinstance(m, int)
                    else pl.multiple_of(base * SUBLANES, SUBLANES))
            for s in range(SUBLANES):
                stg[slot, pl.ds(row0 + s, SLOT_CHUNK, stride=SUBLANES), :] = (
                    res[:, s * LANES:(s + 1) * LANES])
            return carry

        for m in range(PEELED_CHUNKS):
            chunk_body(m, 0)
        lax.fori_loop(PEELED_CHUNKS, jnp.maximum(nchunk, PEELED_CHUNKS), chunk_body, 0)

        _segment_copies(b_sm, c_sm, t, copy_for(slot), lambda cp: cp.start())

        @pl.when(t >= DISPATCH_SLOTS - 1)
        def _():
            wait_tile(t - (DISPATCH_SLOTS - 1))

    for k in range(DISPATCH_TILES):
        one_tile(k)

    @pl.when(step == nsteps - 1)
    def _():
        last = nsteps * DISPATCH_TILES - 1
        for d in range(DISPATCH_SLOTS - 2, -1, -1):
            wait_tile(last - d)


def _dispatch(b_flat, c_flat, x2b, loc3, cap):
    S, D = x2b.shape
    tok = DISPATCH_TILES * ROUTE_TILE
    assert S % tok == 0 and S // ROUTE_TILE >= DISPATCH_SLOTS
    max_rows = N_EXPERTS * ROUTE_TILE
    return pl.pallas_call(
        functools.partial(_dispatch_kernel, cap),
        grid_spec=pltpu.PrefetchScalarGridSpec(
            num_scalar_prefetch=2,
            grid=(S // tok,),
            in_specs=[pl.BlockSpec((tok, D), lambda t, b, c: (t, 0)),
                      pl.BlockSpec((tok // LANES, N_EXPERTS, LANES), lambda t, b, c: (t, 0, 0))],
            out_specs=pl.BlockSpec(memory_space=pl.ANY),
            scratch_shapes=[pltpu.VMEM((DISPATCH_SLOTS, max_rows * SUBLANES, LANES), F32),
                            pltpu.SemaphoreType.DMA((DISPATCH_SLOTS,))]),
        out_shape=jax.ShapeDtypeStruct((N_EXPERTS * cap * SUBLANES, LANES), F32),
        compiler_params=pltpu.CompilerParams(
            dimension_semantics=("arbitrary",), vmem_limit_bytes=VMEM_LIMIT),
        name="dispatch",
    )(b_flat, c_flat, x2b, loc3)


def _ffn_kernel(xe_ref, wg_ref, wu_ref, wd_ref, wr_ref, o_ref, xb, acc, gate):
    e = pl.program_id(0)
    f = pl.program_id(2)
    tm = xb.shape[0]

    @pl.when(f == 0)
    def _():
        for s in range(SUBLANES):
            xb[:, s * LANES:(s + 1) * LANES] = (
                xe_ref[pl.ds(s, tm, stride=SUBLANES), :].astype(BF16))
        lg = jnp.dot(xb[...], wr_ref[...], preferred_element_type=F32)
        lane = lax.broadcasted_iota(I32, (1, LANES), 1)
        lg = jnp.where(lane < N_EXPERTS, lg, NEG)
        p = jnp.exp(lg - jnp.max(lg, axis=-1, keepdims=True))
        mine = jnp.sum(jnp.where(lane == e, p, 0.0), axis=-1, keepdims=True)
        gate[...] = mine / jnp.sum(p, axis=-1, keepdims=True)

    def down_proj():
        x = xb[...]
        g = jnp.dot(x, wg_ref[...].astype(BF16), preferred_element_type=F32)
        u = jnp.dot(x, wu_ref[...].astype(BF16), preferred_element_type=F32)
        h = (g * jax.nn.sigmoid(g) * u).astype(BF16)
        return jnp.dot(h, wd_ref[...].astype(BF16), preferred_element_type=F32)

    last = pl.num_programs(2) - 1

    @pl.when(f == 0)
    def _():
        acc[...] = down_proj()

    @pl.when((f > 0) & (f < last))
    def _():
        acc[...] += down_proj()

    @pl.when(f == last)
    def _():
        y = (acc[...] + down_proj()) * gate[...]
        for s in range(SUBLANES):
            o_ref[pl.ds(s, tm, stride=SUBLANES), :] = y[:, s * LANES:(s + 1) * LANES]


def _ffn(xe2d, w_gate, w_up, w_down, wr_pad, cap):
    E, D, FF = w_gate.shape
    tm, tf = FFN_TM, FFN_TF
    mt = cap // tm
    rows = lambda e, m, f: (e * mt + m, 0)
    return pl.pallas_call(
        _ffn_kernel,
        grid=(E, mt, FF // tf),
        in_specs=[pl.BlockSpec((tm * SUBLANES, LANES), rows),
                  pl.BlockSpec((None, D, tf), lambda e, m, f: (e, 0, f)),
                  pl.BlockSpec((None, D, tf), lambda e, m, f: (e, 0, f)),
                  pl.BlockSpec((None, tf, D), lambda e, m, f: (e, f, 0)),
                  pl.BlockSpec(wr_pad.shape, lambda e, m, f: (0, 0))],
        out_specs=pl.BlockSpec((tm * SUBLANES, LANES), rows),
        out_shape=jax.ShapeDtypeStruct(xe2d.shape, F32),
        scratch_shapes=[pltpu.VMEM((tm, D), BF16), pltpu.VMEM((tm, D), F32),
                        pltpu.VMEM((tm, 1), F32)],
        compiler_params=pltpu.CompilerParams(
            dimension_semantics=("arbitrary", "arbitrary", "arbitrary"),
            vmem_limit_bytes=VMEM_LIMIT),
        name="ffn",
    )(xe2d, w_gate, w_up, w_down, wr_pad)


def _combine_kernel(cap, alpha, b_sm, c_sm, x_ref, loc_ref, g_ref, be_ref, ye_hbm,
                    o_ref, ybuf, sem, hl_scr, z_scr):
    t = pl.program_id(0)
    nt = pl.num_programs(0)
    slot = t % 2

    def copy_for(slot_):
        def make(e, src_row, dst_row, n):
            return pltpu.make_async_copy(
                ye_hbm.at[pl.ds(pl.multiple_of((e * cap + src_row) * SUBLANES, SUBLANES),
                                n * SUBLANES)],
                ybuf.at[slot_, pl.ds(pl.multiple_of(dst_row * SUBLANES, SUBLANES), n * SUBLANES)],
                sem.at[slot_])
        return make

    @pl.when(t == 0)
    def _():
        ybuf[...] = jnp.zeros_like(ybuf)
        _segment_copies(b_sm, c_sm, t, copy_for(slot), lambda cp: cp.start())

    @pl.when(t < nt - 1)
    def _():
        _segment_copies(b_sm, c_sm, t + 1, copy_for(1 - slot), lambda cp: cp.start())

    rows_in = _tile_rows(c_sm, t)

    @pl.when(rows_in > 0)
    def _():
        n = rows_in * SUBLANES
        pltpu.make_async_copy(ye_hbm.at[pl.ds(0, n)], ybuf.at[slot, pl.ds(0, n)],
                              sem.at[slot]).wait()

    offs = []
    off = 0
    for e in range(N_EXPERTS):
        offs.append(off)
        off = off + c_sm[t * N_EXPERTS + e]
    offs.append(off)
    total = off
    nchunk = (total + SLOT_CHUNK - 1) // SLOT_CHUNK
    lane = lax.broadcasted_iota(I32, (1, SLOT_CHUNK), 1)
    sub = lax.broadcasted_iota(I32, (SLOT_CHUNK, 1), 0)
    eid = lax.broadcasted_iota(I32, (N_EXPERTS, 1), 0)
    seg_lo = jnp.zeros((N_EXPERTS, 1), I32)
    seg_hi = jnp.zeros((N_EXPERTS, 1), I32)
    for e in range(N_EXPERTS):
        seg_lo = jnp.where(eid == e, offs[e], seg_lo)
        seg_hi = jnp.where(eid == e, offs[e + 1], seg_hi)
    seg_lo_f = seg_lo.astype(F32)
    locb = loc_ref[...].astype(F32).astype(BF16)

    z_scr[...] = alpha * x_ref[...]

    def chunk_body(m, carry):
        base = m * SLOT_CHUNK
        srow = lane + base
        inseg = jnp.where(srow >= seg_lo, jnp.where(srow < seg_hi, 1.0, 0.0), 0.0)
        rank = srow.astype(F32) - jnp.sum(inseg * seg_lo_f, axis=0, keepdims=True)
        want = jnp.dot(locb, inseg.astype(BF16), preferred_element_type=F32)
        ptb = jnp.where(want == rank, 1.0, 0.0).astype(BF16)
        row0 = base * SUBLANES if isinstance(m, int) else pl.multiple_of(base * SUBLANES, SUBLANES)
        yc = jnp.concatenate(
            [ybuf[slot, pl.ds(row0 + s, SLOT_CHUNK, stride=SUBLANES), :]
             for s in range(SUBLANES)], axis=1)
        yc = jnp.where(sub < total - base, yc, 0.0)
        hi = yc.astype(BF16)
        hl = hl_scr.at[m % PEELED_CHUNKS]
        hl[0:SLOT_CHUNK] = hi
        hl[SLOT_CHUNK:2 * SLOT_CHUNK] = (yc - hi.astype(F32)).astype(BF16)
        z_scr[...] += jnp.dot(jnp.concatenate([ptb, ptb], axis=1), hl[...],
                              preferred_element_type=F32)
        return carry

    for m in range(PEELED_CHUNKS):
        chunk_body(m, 0)
    lax.fori_loop(PEELED_CHUNKS, jnp.maximum(nchunk, PEELED_CHUNKS), chunk_body, 0)

    o_ref[...] = _layer_norm(z_scr[...], g_ref[...], be_ref[...])


def _combine(b_flat, c_flat, x2, loc_t, g, b, ye2d, cap, alpha):
    S, D = x2.shape
    nt = S // ROUTE_TILE
    max_rows = N_EXPERTS * ROUTE_TILE
    cur = lambda t, b_, c_: (t, 0)
    const2 = lambda t, b_, c_: (0, 0)
    return pl.pallas_call(
        functools.partial(_combine_kernel, cap, alpha),
        grid_spec=pltpu.PrefetchScalarGridSpec(
            num_scalar_prefetch=2,
            grid=(nt,),
            in_specs=[pl.BlockSpec((ROUTE_TILE, D), cur),
                      pl.BlockSpec((ROUTE_TILE, N_EXPERTS), cur),
                      pl.BlockSpec((1, D), const2),
                      pl.BlockSpec((1, D), const2),
                      pl.BlockSpec(memory_space=pl.ANY)],
            out_specs=pl.BlockSpec((ROUTE_TILE, D), cur),
            scratch_shapes=[pltpu.VMEM((2, max_rows * SUBLANES, LANES), F32),
                            pltpu.SemaphoreType.DMA((2,)),
                            pltpu.VMEM((PEELED_CHUNKS, 2 * SLOT_CHUNK, D), BF16),
                            pltpu.VMEM((ROUTE_TILE, D), F32)]),
        out_shape=jax.ShapeDtypeStruct((S, D), F32),
        compiler_params=pltpu.CompilerParams(
            dimension_semantics=("arbitrary",), vmem_limit_bytes=VMEM_LIMIT),
        name="combine",
    )(b_flat, c_flat, x2, loc_t, g, b, ye2d)


def _layer(x, mem, w_in, rpb, conv_w, w_mix_out, ln1_g, ln1_b, wq, wk, wv, wo,
           ln2_g, ln2_b, w_router, w_gate, w_up, w_down, ln3_g, ln3_b, alpha):
    S, D = x.shape
    cap = CAPACITY_FACTOR * S // N_EXPERTS
    assert S % TOK_BLOCK == 0 and S % PROJ_TM == 0 and S % XATTN_TM == 0
    assert S // GRID_W >= 2 * ROW_BLOCK
    assert cap % FFN_TM == 0 and cap <= S
    row = lambda a: a.reshape(1, -1)

    q, k, v, bg, u = _proj(x, w_in.astype(BF16))
    x1 = _mixer(x, q, k, v, bg, u, w_mix_out.astype(BF16), rpb,
                conv_w, row(ln1_g), row(ln1_b), alpha)

    wr_t = w_router.T
    wr_hi = wr_t.astype(BF16)
    wr_lo = (wr_t - wr_hi.astype(F32)).astype(BF16)
    x2, x2b, aff3 = _xattn(x1, mem, wq.astype(BF16), wk.astype(BF16), wv.astype(BF16),
                           wo.astype(BF16), row(ln2_g), row(ln2_b), wr_hi, wr_lo, alpha)

    loc3, b3, c3 = _route(aff3, cap)
    b_flat = b3.reshape(-1)
    c_flat = c3.reshape(-1)
    loc_t = loc3.transpose(0, 2, 1).reshape(S, N_EXPERTS)

    xe2d = _dispatch(b_flat, c_flat, x2b, loc3, cap)
    wr_pad = jnp.pad(w_router.astype(BF16), ((0, 0), (0, LANES - N_EXPERTS)))
    ye2d = _ffn(xe2d, w_gate, w_up, w_down, wr_pad, cap)
    return _combine(b_flat, c_flat, x2, loc_t, row(ln3_g), row(ln3_b), ye2d, cap, alpha)


def kernel(x, mem, w_in, na_rpb, conv_w, w_mix_out, ln1_g, ln1_b, w_mem_q, w_mem_k,
           w_mem_v, w_mem_out, ln2_g, ln2_b, w_router, w_exp_gate, w_exp_up,
           w_exp_down, ln3_g, ln3_b):
    depth = w_in.shape[0]
    alpha = (2 * depth) ** 0.25
    outs = []
    for bi in range(x.shape[0]):
        xb = x[bi]
        for l in range(depth):
            xb = _layer(xb, mem[bi], w_in[l], na_rpb[l], conv_w[l], w_mix_out[l],
                        ln1_g[l], ln1_b[l], w_mem_q[l], w_mem_k[l], w_mem_v[l],
                        w_mem_out[l], ln2_g[l], ln2_b[l], w_router[l], w_exp_gate[l],
                        w_exp_up[l], w_exp_down[l], ln3_g[l], ln3_b[l], alpha)
        outs.append(xb)
    return jnp.stack(outs)
```

```python
import functools

import jax
import jax.numpy as jnp
from jax import lax
from jax.experimental import pallas as pl
from jax.experimental.pallas import tpu as pltpu

F32 = jnp.float32
BF16 = jnp.bfloat16
I32 = jnp.int32

GRID_W = 64
HEAD_DIM = 64
NA_HEADS = 8
NA_WIDTH = NA_HEADS * HEAD_DIM
WIN_ROWS = 8
WIN_COLS = 16
MEM_HEADS = 4
MEM_HEAD_DIM = 128
N_EXPERTS = 16
CAPACITY_FACTOR = 2
LN_EPS = 1e-5
NEG = -1e30

LANES = 128
SUBLANES = 8
VMEM_LIMIT = 56 * 1024 * 1024

ROW_BLOCK = 8
TOK_BLOCK = ROW_BLOCK * GRID_W
PROJ_TM = 1024
XATTN_TM = 1024
ROUTE_TILE = 256
SLOT_CHUNK = 256
SM_ROWS = 16
PEELED_CHUNKS = 2
DISPATCH_TILES = 4
COMBINE_TILES = 2
DISPATCH_SLOTS = 3
FFN_TM = 1024
FFN_TF = 512


def _layer_norm(z, g, b):
    mu = jnp.mean(z, axis=-1, keepdims=True)
    zc = z - mu
    var = jnp.mean(zc * zc, axis=-1, keepdims=True)
    return zc * lax.rsqrt(var + LN_EPS) * g + b


def _dot_nt(a, b):
    return lax.dot_general(a, b, (((1,), (1,)), ((), ())),
                           preferred_element_type=F32)


def _proj_kernel(x_ref, w_ref, q_ref, k_ref, v_ref, b_ref, u_ref):
    xb = x_ref[...].astype(BF16)
    nw = NA_WIDTH

    def mm(c0):
        return jnp.dot(xb, w_ref[:, c0:c0 + nw], preferred_element_type=F32)

    q_ref[...] = (mm(0) * (HEAD_DIM ** -0.5)).astype(BF16)
    k_ref[...] = mm(nw).astype(BF16)
    v_ref[...] = mm(2 * nw).astype(BF16)
    b_ref[...] = mm(3 * nw)
    u_ref[...] = mm(4 * nw) * mm(5 * nw)


def _proj(x, w_in_b):
    S, D = x.shape
    nw = NA_WIDTH
    tm = PROJ_TM
    blk = lambda i: (i, 0)
    return pl.pallas_call(
        _proj_kernel,
        grid=(S // tm,),
        in_specs=[pl.BlockSpec((tm, D), blk),
                  pl.BlockSpec((D, 6 * nw), lambda i: (0, 0))],
        out_specs=[pl.BlockSpec((tm, nw), blk)] * 5,
        out_shape=[jax.ShapeDtypeStruct((S, nw), BF16)] * 3
        + [jax.ShapeDtypeStruct((S, nw), F32)] * 2,
        compiler_params=pltpu.CompilerParams(
            dimension_semantics=("arbitrary",), vmem_limit_bytes=VMEM_LIMIT),
        name="proj",
    )(x, w_in_b)


def _build_na_bias(rpb_ref, bias_ref):
    shape = (GRID_W, LANES)
    qc = lax.broadcasted_iota(I32, shape, 0)
    ln = lax.broadcasted_iota(I32, shape, 1)
    kc = ln % GRID_W
    c_start = jnp.clip(qc - WIN_COLS // 2, 0, GRID_W - WIN_COLS)
    base = LANES - (WIN_COLS - 1)

    def canvas(h, d):
        both = rpb_ref[h, d:d + 1, :] + pltpu.roll(rpb_ref[h, d + 1:d + 2, :], GRID_W, 1)
        t = pltpu.roll(pltpu.roll(jnp.broadcast_to(both, shape), base, 1), 0, 1,
                       stride=1, stride_axis=0)
        return jnp.where(kc >= c_start, jnp.where(kc < c_start + WIN_COLS, t, NEG), NEG)

    for par in range(2):
        for hp in range(NA_HEADS // 2):
            for m in range(WIN_ROWS):
                d = 2 * m + par
                bias_ref[par, hp, m, 0:GRID_W] = canvas(2 * hp, d)
                bias_ref[par, hp, m, GRID_W:2 * GRID_W] = canvas(2 * hp + 1, d)


def _mixer_kernel(rows, alpha,
                  q_ref, kp_ref, kc_ref, kn_ref, vp_ref, vc_ref, vn_ref,
                  b_ref, u_ref, up_ref, un_ref, x_ref, wout_ref, rpb_ref,
                  cw_ref, g_ref, be_ref, o_ref, kbuf, vbuf, mix, bias_ref,
                  s_scr0, s_scr1, p_scr0, p_scr1):
    i = pl.program_id(0)
    nb = pl.num_programs(0)
    tb = TOK_BLOCK
    win = WIN_ROWS * GRID_W

    @pl.when(i == 0)
    def _():
        _build_na_bias(rpb_ref, bias_ref)

    kbuf[0:tb] = kp_ref[...]
    kbuf[tb:2 * tb] = kc_ref[...]
    kbuf[2 * tb:3 * tb] = kn_ref[...]
    vbuf[0:tb] = vp_ref[...]
    vbuf[tb:2 * tb] = vc_ref[...]
    vbuf[2 * tb:3 * tb] = vn_ref[...]

    lane = lax.broadcasted_iota(I32, (1, LANES), 1)
    first_head = lane < HEAD_DIM
    m_lo = jnp.where(first_head, 1.0, 0.0).astype(BF16)
    m_hi = jnp.where(first_head, 0.0, 1.0).astype(BF16)

    def window(r):
        grow = i * ROW_BLOCK + r
        start = jnp.clip(grow - WIN_ROWS // 2, 0, rows - WIN_ROWS)
        w0 = pl.multiple_of((start - (i - 1) * ROW_BLOCK) * GRID_W, GRID_W)
        d0 = start - grow + WIN_ROWS
        return w0, d0 % 2, d0 // 2

    def scores(r, s_out):
        w0, _, _ = window(r)
        q0 = r * GRID_W if isinstance(r, int) else pl.multiple_of(r * GRID_W, GRID_W)
        for hp in range(NA_HEADS // 2):
            cs = slice(hp * LANES, (hp + 1) * LANES)
            qp = q_ref[pl.ds(q0, GRID_W), cs]
            lhs = jnp.concatenate([qp * m_lo, qp * m_hi], axis=0)
            kw = kbuf[pl.ds(w0, win), cs]
            s_out[hp] = _dot_nt(lhs, kw)

    def probs(r, s_in, p_out):
        _, par, m0 = window(r)
        for hp in range(NA_HEADS // 2):
            for rb in range(2 * GRID_W // SM_ROWS):
                rs = slice(rb * SM_ROWS, (rb + 1) * SM_ROWS)
                bias = jnp.concatenate(
                    [bias_ref[par, hp, m0 + j, rs, :] for j in range(WIN_ROWS // 2)], axis=1)
                s = s_in[hp, rs, :] + bias
                p = jnp.exp(s - jnp.max(s, axis=-1, keepdims=True))
                p = p * (1.0 / jnp.sum(p, axis=-1, keepdims=True))
                p_out[hp, rs, :] = p.astype(BF16)

    def values(r, p_in):
        w0, _, _ = window(r)
        q0 = r * GRID_W if isinstance(r, int) else pl.multiple_of(r * GRID_W, GRID_W)
        for hp in range(NA_HEADS // 2):
            cs = slice(hp * LANES, (hp + 1) * LANES)
            vw = vbuf[pl.ds(w0, win), cs]
            o2 = jnp.dot(p_in[hp], vw, preferred_element_type=F32)
            y = jnp.where(first_head, o2[0:GRID_W], o2[GRID_W:2 * GRID_W])
            mix[pl.ds(q0, GRID_W), cs] = y.astype(BF16)

    s_buf = (s_scr0, s_scr1)
    p_buf = (p_scr0, p_scr1)

    def steady(k, carry):
        for half in range(2):
            r = 1 + 2 * k + half
            values(r - 1, p_buf[half])
            probs(r, s_buf[1 - half], p_buf[1 - half])
            scores(r + 1, s_buf[half])
        return carry

    scores(0, s_buf[0])
    probs(0, s_buf[0], p_buf[0])
    scores(1, s_buf[1])
    lax.fori_loop(0, (ROW_BLOCK - 2) // 2, steady, 0)
    values(ROW_BLOCK - 2, p_buf[0])
    probs(ROW_BLOCK - 1, s_buf[1], p_buf[1])
    values(ROW_BLOCK - 1, p_buf[1])

    u = u_ref[...]
    prev_row = up_ref[SUBLANES - 1:SUBLANES, :] * jnp.where(i > 0, 1.0, 0.0)
    next_row = un_ref[0:1, :] * jnp.where(i < nb - 1, 1.0, 0.0)
    rid = lax.broadcasted_iota(I32, (tb, 1), 0)
    um1 = jnp.where(rid == 0, prev_row, pltpu.roll(u, 1, axis=0))
    up1 = jnp.where(rid == tb - 1, next_row, pltpu.roll(u, tb - 1, axis=0))
    cw = cw_ref[...]
    yc = b_ref[...] * (cw[0:1] * um1 + cw[1:2] * u + cw[2:3] * up1)
    mix[:, NA_WIDTH:] = yc.astype(BF16)

    y = jnp.dot(mix[...], wout_ref[...], preferred_element_type=F32)
    z = alpha * x_ref[...] + y
    o_ref[...] = _layer_norm(z, g_ref[...], be_ref[...])


def _mixer(x, q, k, v, bg, u, w_out_b, rpb, conv_w, g, b, alpha):
    S, D = x.shape
    rpb_pad = jnp.pad(rpb.astype(F32), ((0, 0), (1, 1), (0, LANES - rpb.shape[2])))
    nw = NA_WIDTH
    tb = TOK_BLOCK
    nb = S // tb
    rows = S // GRID_W
    cur = lambda i: (i, 0)
    prv = lambda i: (jnp.maximum(i - 1, 0), 0)
    nxt = lambda i: (jnp.minimum(i + 1, nb - 1), 0)
    halo = tb // SUBLANES
    hprev = lambda i: (jnp.maximum(i * halo - 1, 0), 0)
    hnext = lambda i: (jnp.minimum((i + 1) * halo, S // SUBLANES - 1), 0)
    const2 = lambda i: (0, 0)
    kv = lambda im: pl.BlockSpec((tb, nw), im)
    return pl.pallas_call(
        functools.partial(_mixer_kernel, rows, alpha),
        grid=(nb,),
        in_specs=[kv(cur), kv(prv), kv(cur), kv(nxt), kv(prv), kv(cur), kv(nxt),
                  kv(cur), kv(cur),
                  pl.BlockSpec((SUBLANES, nw), hprev),
                  pl.BlockSpec((SUBLANES, nw), hnext),
                  pl.BlockSpec((tb, D), cur),
                  pl.BlockSpec((D, D), const2),
                  pl.BlockSpec(rpb_pad.shape, lambda i: (0, 0, 0)),
                  pl.BlockSpec(conv_w.shape, const2),
                  pl.BlockSpec((1, D), const2),
                  pl.BlockSpec((1, D), const2)],
        out_specs=pl.BlockSpec((tb, D), cur),
        out_shape=jax.ShapeDtypeStruct((S, D), F32),
        scratch_shapes=[pltpu.VMEM((3 * tb, nw), BF16),
                        pltpu.VMEM((3 * tb, nw), BF16),
                        pltpu.VMEM((tb, D), BF16),
                        pltpu.VMEM((2, NA_HEADS // 2, WIN_ROWS, 2 * GRID_W, LANES), F32),
                        pltpu.VMEM((NA_HEADS // 2, 2 * GRID_W, WIN_ROWS * GRID_W), F32),
                        pltpu.VMEM((NA_HEADS // 2, 2 * GRID_W, WIN_ROWS * GRID_W), F32),
                        pltpu.VMEM((NA_HEADS // 2, 2 * GRID_W, WIN_ROWS * GRID_W), BF16),
                        pltpu.VMEM((NA_HEADS // 2, 2 * GRID_W, WIN_ROWS * GRID_W), BF16)],
        compiler_params=pltpu.CompilerParams(
            dimension_semantics=("arbitrary",), vmem_limit_bytes=VMEM_LIMIT),
        name="mixer",
    )(q, k, k, k, v, v, v, bg, u, u, u, x, w_out_b, rpb_pad, conv_w, g, b)


def _xattn_kernel(alpha, x_ref, mem_ref, wq_ref, wk_ref, wv_ref, wo_ref,
                  g_ref, be_ref, wrh_ref, wrl_ref,
                  x2_ref, x2b_ref, aff_ref, kmem, vmem):
    i = pl.program_id(0)

    @pl.when(i == 0)
    def _():
        mb = mem_ref[...].astype(BF16)
        kmem[...] = jnp.dot(mb, wk_ref[...], preferred_element_type=F32).astype(BF16)
        vmem[...] = jnp.dot(mb, wv_ref[...], preferred_element_type=F32).astype(BF16)

    x1 = x_ref[...]
    qb = jnp.dot(x1.astype(BF16), wq_ref[...], preferred_element_type=F32).astype(BF16)
    outs = []
    for h in range(MEM_HEADS):
        cs = slice(h * MEM_HEAD_DIM, (h + 1) * MEM_HEAD_DIM)
        s = _dot_nt(qb[:, cs], kmem[:, cs]) * (MEM_HEAD_DIM ** -0.5)
        mx = jnp.max(s, axis=-1, keepdims=True)
        p = jnp.exp(s - mx)
        l = jnp.sum(p, axis=-1, keepdims=True)
        oh = jnp.dot(p.astype(BF16), vmem[:, cs], preferred_element_type=F32)
        outs.append(oh / l)
    o = jnp.concatenate(outs, axis=1).astype(BF16)
    y = jnp.dot(o, wo_ref[...], preferred_element_type=F32)
    x2 = _layer_norm(alpha * x1 + y, g_ref[...], be_ref[...])
    x2_ref[...] = x2
    hi = x2.astype(BF16)
    x2b_ref[...] = hi
    lo = (x2 - hi.astype(F32)).astype(BF16)
    wrh = wrh_ref[...]
    r1 = _dot_nt(jnp.concatenate([wrh, wrl_ref[...]], axis=0), hi)
    lg = r1[0:N_EXPERTS] + r1[N_EXPERTS:2 * N_EXPERTS] + _dot_nt(wrh, lo)
    mx = jnp.max(lg, axis=0, keepdims=True)
    p = jnp.exp(lg - mx)
    aff = p / jnp.sum(p, axis=0, keepdims=True)
    for j in range(aff_ref.shape[0]):
        aff_ref[j] = aff[:, j * LANES:(j + 1) * LANES]


def _xattn(x1, mem, wq_b, wk_b, wv_b, wo_b, g, b, wr_hi, wr_lo, alpha):
    S, D = x1.shape
    M = mem.shape[0]
    mw = MEM_HEADS * MEM_HEAD_DIM
    tm = XATTN_TM
    cur = lambda i: (i, 0)
    const2 = lambda i: (0, 0)
    full = lambda a: pl.BlockSpec(a.shape, const2)
    return pl.pallas_call(
        functools.partial(_xattn_kernel, alpha),
        grid=(S // tm,),
        in_specs=[pl.BlockSpec((tm, D), cur), full(mem), full(wq_b), full(wk_b),
                  full(wv_b), full(wo_b), full(g), full(b), full(wr_hi), full(wr_lo)],
        out_specs=[pl.BlockSpec((tm, D), cur), pl.BlockSpec((tm, D), cur),
                   pl.BlockSpec((tm // LANES, N_EXPERTS, LANES), lambda i: (i, 0, 0))],
        out_shape=[jax.ShapeDtypeStruct((S, D), F32),
                   jax.ShapeDtypeStruct((S, D), BF16),
                   jax.ShapeDtypeStruct((S // LANES, N_EXPERTS, LANES), F32)],
        scratch_shapes=[pltpu.VMEM((M, mw), BF16), pltpu.VMEM((M, mw), BF16)],
        compiler_params=pltpu.CompilerParams(
            dimension_semantics=("arbitrary",), vmem_limit_bytes=VMEM_LIMIT),
        name="xattn",
    )(x1, mem, wq_b, wk_b, wv_b, wo_b, g, b, wr_hi, wr_lo)


def _route_kernel(cap, aff_ref, loc_ref, b_ref, c_ref, incl_scr, tot_scr, sel_scr):
    nblk = aff_ref.shape[0]
    per_tile = ROUTE_TILE // LANES
    ntile = nblk // per_tile

    def count_ge(t):
        ge = jnp.where(aff_ref[...] >= t[None], 1.0, 0.0)
        return jnp.sum(jnp.sum(ge, axis=0), axis=1, keepdims=True)

    def coarse(it, bits):
        cand = bits | jnp.left_shift(jnp.int32(1), 30 - it)
        ok = count_ge(pltpu.bitcast(cand, F32)) >= cap
        return jnp.where(ok, cand, bits)

    def fine(it, lohi):
        lo, hi = lohi
        mid = lo + (hi - lo) * 0.5
        ok = count_ge(mid) >= cap
        return jnp.where(ok, mid, lo), jnp.where(ok, hi, mid)

    bits = lax.fori_loop(0, 31, coarse, jnp.zeros((N_EXPERTS, 1), I32))
    thr, _ = lax.fori_loop(
        0, 30, fine, (pltpu.bitcast(bits, F32), pltpu.bitcast(bits + 1, F32)))
    gt_all = jnp.where(aff_ref[...] > thr[None], 1.0, 0.0)
    n_gt = jnp.sum(jnp.sum(gt_all, axis=0), axis=1, keepdims=True)
    need = cap - n_gt

    ii = lax.broadcasted_iota(I32, (LANES, LANES), 0)
    jj = lax.broadcasted_iota(I32, (LANES, LANES), 1)
    tri = jnp.where(ii <= jj, 1.0, 0.0).astype(BF16)
    ones = jnp.ones((LANES, LANES), BF16)
    flat = (nblk * N_EXPERTS, LANES)
    blocks = (nblk, N_EXPERTS, LANES)

    def block_scans(flags):
        f2 = flags.reshape(flat).astype(BF16)
        incl_scr[...] = jnp.dot(f2, tri, preferred_element_type=F32).reshape(blocks)
        tot_scr[...] = jnp.dot(f2, ones, preferred_element_type=F32).reshape(blocks)

    block_scans(jnp.where(aff_ref[...] == thr[None], 1.0, 0.0))

    def select_block(j, run_eq):
        blk = aff_ref[j]
        tie_rank = run_eq + incl_scr[j]
        sel_scr[j] = (jnp.where(blk > thr, 1.0, 0.0)
                      + jnp.where(blk == thr, 1.0, 0.0) * jnp.where(tie_rank <= need, 1.0, 0.0))
        return run_eq + tot_scr[j]

    zero = jnp.zeros((N_EXPERTS, LANES), F32)
    lax.fori_loop(0, nblk, select_block, zero)
    block_scans(sel_scr[...])

    def tile_body(t, run):
        tile_run = zero
        for j in range(per_tile):
            idx = t * per_tile + j
            sel = sel_scr[idx]
            loc = tile_run + incl_scr[idx] - sel
            loc_ref[idx] = jnp.where(sel > 0.5, loc, -1.0).astype(I32)
            tile_run = tile_run + tot_scr[idx]
        b_ref[t] = run[:, 0:1].astype(I32)
        c_ref[t] = tile_run[:, 0:1].astype(I32)
        return run + tile_run

    lax.fori_loop(0, ntile, tile_body, zero)


def _route(aff3, cap):
    nblk = aff3.shape[0]
    ntile = nblk * LANES // ROUTE_TILE
    full3 = lambda a: pl.BlockSpec(a, lambda: (0, 0, 0))
    return pl.pallas_call(
        functools.partial(_route_kernel, cap),
        in_specs=[full3(aff3.shape)],
        out_specs=[full3(aff3.shape), full3((ntile, N_EXPERTS, 1)),
                   full3((ntile, N_EXPERTS, 1))],
        out_shape=[jax.ShapeDtypeStruct(aff3.shape, I32),
                   jax.ShapeDtypeStruct((ntile, N_EXPERTS, 1), I32),
                   jax.ShapeDtypeStruct((ntile, N_EXPERTS, 1), I32)],
        scratch_shapes=[pltpu.VMEM(aff3.shape, F32)] * 3,
        compiler_params=pltpu.CompilerParams(vmem_limit_bytes=VMEM_LIMIT),
        name="route",
    )(aff3)


def _segment_copies(b_sm, c_sm, t, make_copy, action):
    off = 0
    for e in range(N_EXPERTS):
        c = c_sm[t * N_EXPERTS + e]
        b = b_sm[t * N_EXPERTS + e]

        @pl.when(c > 0)
        def _(e=e, b=b, c=c, off=off):
            action(make_copy(e, b, off, c))

        off = off + c
    return off


def _tile_rows(c_sm, t):
    rows = 0
    for e in range(N_EXPERTS):
        rows = rows + c_sm[t * N_EXPERTS + e]
    return rows


def _dispatch_kernel(cap, b_sm, c_sm, x_ref, loc_ref, xe_hbm, stg, sem):
    step = pl.program_id(0)
    nsteps = pl.num_programs(0)
    per_tile = ROUTE_TILE // LANES
    sub = lax.broadcasted_iota(I32, (SLOT_CHUNK, 1), 0)
    lane = lax.broadcasted_iota(I32, (1, LANES), 1)

    def copy_for(slot_):
        def make(e, dst_row, src_row, n):
            return pltpu.make_async_copy(
                stg.at[slot_, pl.ds(pl.multiple_of(src_row * SUBLANES, SUBLANES), n * SUBLANES)],
                xe_hbm.at[pl.ds(pl.multiple_of((e * cap + dst_row) * SUBLANES, SUBLANES),
                                n * SUBLANES)],
                sem.at[slot_])
        return make

    def wait_tile(t_):
        rows = _tile_rows(c_sm, t_)
        slot_ = t_ % DISPATCH_SLOTS

        @pl.when(rows > 0)
        def _():
            n = rows * SUBLANES
            pltpu.make_async_copy(stg.at[slot_, pl.ds(0, n)], xe_hbm.at[pl.ds(0, n)],
                                  sem.at[slot_]).wait()

    def one_tile(k):
        t = step * DISPATCH_TILES + k
        slot = t % DISPATCH_SLOTS
        tok = slice(k * ROUTE_TILE, (k + 1) * ROUTE_TILE)
        offs = []
        off = 0
        for e in range(N_EXPERTS):
            offs.append(off)
            off = off + c_sm[t * N_EXPERTS + e]
        offs.append(off)
        nchunk = (off + SLOT_CHUNK - 1) // SLOT_CHUNK
        seg_lo = jnp.zeros((1, LANES), I32)
        seg_hi = jnp.zeros((1, LANES), I32)
        for e in range(N_EXPERTS):
            seg_lo = jnp.where(lane == e, offs[e], seg_lo)
            seg_hi = jnp.where(lane == e, offs[e + 1], seg_hi)
        seg_lo_f = seg_lo.astype(F32)
        locf = jnp.concatenate(
            [loc_ref[k * per_tile + j] for j in range(per_tile)], axis=1).astype(F32)
        locb = jnp.concatenate(
            [locf, jnp.zeros((LANES - N_EXPERTS, ROUTE_TILE), F32)], axis=0).astype(BF16)

        def chunk_body(m, carry):
            base = m * SLOT_CHUNK
            srow = sub + base
            inseg = jnp.where(srow >= seg_lo, jnp.where(srow < seg_hi, 1.0, 0.0), 0.0)
            rank = srow.astype(F32) - jnp.sum(inseg * seg_lo_f, axis=1, keepdims=True)
            want = jnp.dot(inseg.astype(BF16), locb, preferred_element_type=F32)
            p = jnp.where(want == rank, 1.0, 0.0)
            res = jnp.dot(p.astype(BF16), x_ref[tok, :], preferred_element_type=F32)
            row0 = (base * SUBLANES if isinstance(m, int)
                    else pl.multiple_of(base * SUBLANES, SUBLANES))
            for s in range(SUBLANES):
                stg[slot, pl.ds(row0 + s, SLOT_CHUNK, stride=SUBLANES), :] = (
                    res[:, s * LANES:(s + 1) * LANES])
            return carry

        for m in range(PEELED_CHUNKS):
            chunk_body(m, 0)
        lax.fori_loop(PEELED_CHUNKS, jnp.maximum(nchunk, PEELED_CHUNKS), chunk_body, 0)

        _segment_copies(b_sm, c_sm, t, copy_for(slot), lambda cp: cp.start())

        @pl.when(t >= DISPATCH_SLOTS - 1)
        def _():
            wait_tile(t - (DISPATCH_SLOTS - 1))

    for k in range(DISPATCH_TILES):
        one_tile(k)

    @pl.when(step == nsteps - 1)
    def _():
        last = nsteps * DISPATCH_TILES - 1
        for d in range(DISPATCH_SLOTS - 2, -1, -1):
            wait_tile(last - d)


def _dispatch(b_flat, c_flat, x2b, loc3, cap):
    S, D = x2b.shape
    tok = DISPATCH_TILES * ROUTE_TILE
    assert S % tok == 0 and S // ROUTE_TILE >= DISPATCH_SLOTS
    max_rows = N_EXPERTS * ROUTE_TILE
    return pl.pallas_call(
        functools.partial(_dispatch_kernel, cap),
        grid_spec=pltpu.PrefetchScalarGridSpec(
            num_scalar_prefetch=2,
            grid=(S // tok,),
            in_specs=[pl.BlockSpec((tok, D), lambda t, b, c: (t, 0)),
                      pl.BlockSpec((tok // LANES, N_EXPERTS, LANES), lambda t, b, c: (t, 0, 0))],
            out_specs=pl.BlockSpec(memory_space=pl.ANY),
            scratch_shapes=[pltpu.VMEM((DISPATCH_SLOTS, max_rows * SUBLANES, LANES), F32),
                            pltpu.SemaphoreType.DMA((DISPATCH_SLOTS,))]),
        out_shape=jax.ShapeDtypeStruct((N_EXPERTS * cap * SUBLANES, LANES), F32),
        compiler_params=pltpu.CompilerParams(
            dimension_semantics=("arbitrary",), vmem_limit_bytes=VMEM_LIMIT),
        name="dispatch",
    )(b_flat, c_flat, x2b, loc3)


def _ffn_kernel(xe_ref, wg_ref, wu_ref, wd_ref, wr_ref, o_ref, xb, acc, gate):
    e = pl.program_id(0)
    f = pl.program_id(2)
    tm = xb.shape[0]

    @pl.when(f == 0)
    def _():
        for s in range(SUBLANES):
            xb[:, s * LANES:(s + 1) * LANES] = (
                xe_ref[pl.ds(s, tm, stride=SUBLANES), :].astype(BF16))
        lg = jnp.dot(xb[...], wr_ref[...], preferred_element_type=F32)
        lane = lax.broadcasted_iota(I32, (1, LANES), 1)
        lg = jnp.where(lane < N_EXPERTS, lg, NEG)
        p = jnp.exp(lg - jnp.max(lg, axis=-1, keepdims=True))
        mine = jnp.sum(jnp.where(lane == e, p, 0.0), axis=-1, keepdims=True)
        gate[...] = mine / jnp.sum(p, axis=-1, keepdims=True)

    def down_proj():
        x = xb[...]
        g = jnp.dot(x, wg_ref[...].astype(BF16), preferred_element_type=F32)
        u = jnp.dot(x, wu_ref[...].astype(BF16), preferred_element_type=F32)
        h = (g * jax.nn.sigmoid(g) * u).astype(BF16)
        return jnp.dot(h, wd_ref[...].astype(BF16), preferred_element_type=F32)

    last = pl.num_programs(2) - 1

    @pl.when(f == 0)
    def _():
        acc[...] = down_proj()

    @pl.when((f > 0) & (f < last))
    def _():
        acc[...] += down_proj()

    @pl.when(f == last)
    def _():
        y = (acc[...] + down_proj()) * gate[...]
        for s in range(SUBLANES):
            o_ref[pl.ds(s, tm, stride=SUBLANES), :] = y[:, s * LANES:(s + 1) * LANES]


def _ffn(xe2d, w_gate, w_up, w_down, wr_pad, cap):
    E, D, FF = w_gate.shape
    tm, tf = FFN_TM, FFN_TF
    mt = cap // tm
    rows = lambda e, m, f: (e * mt + m, 0)
    return pl.pallas_call(
        _ffn_kernel,
        grid=(E, mt, FF // tf),
        in_specs=[pl.BlockSpec((tm * SUBLANES, LANES), rows),
                  pl.BlockSpec((None, D, tf), lambda e, m, f: (e, 0, f)),
                  pl.BlockSpec((None, D, tf), lambda e, m, f: (e, 0, f)),
                  pl.BlockSpec((None, tf, D), lambda e, m, f: (e, f, 0)),
                  pl.BlockSpec(wr_pad.shape, lambda e, m, f: (0, 0))],
        out_specs=pl.BlockSpec((tm * SUBLANES, LANES), rows),
        out_shape=jax.ShapeDtypeStruct(xe2d.shape, F32),
        scratch_shapes=[pltpu.VMEM((tm, D), BF16), pltpu.VMEM((tm, D), F32),
                        pltpu.VMEM((tm, 1), F32)],
        compiler_params=pltpu.CompilerParams(
            dimension_semantics=("arbitrary", "arbitrary", "arbitrary"),
            vmem_limit_bytes=VMEM_LIMIT),
        name="ffn",
    )(xe2d, w_gate, w_up, w_down, wr_pad)


def _combine_kernel(cap, alpha, b_sm, c_sm, x_ref, loc_ref, g_ref, be_ref, ye_hbm,
                    o_ref, ybuf, sem, hl_scr, z_scr):
    step = pl.program_id(0)
    nt = pl.num_programs(0) * COMBINE_TILES
    lane = lax.broadcasted_iota(I32, (1, SLOT_CHUNK), 1)
    sub = lax.broadcasted_iota(I32, (SLOT_CHUNK, 1), 0)
    eid = lax.broadcasted_iota(I32, (N_EXPERTS, 1), 0)

    def copy_for(slot_):
        def make(e, src_row, dst_row, n):
            return pltpu.make_async_copy(
                ye_hbm.at[pl.ds(pl.multiple_of((e * cap + src_row) * SUBLANES, SUBLANES),
                                n * SUBLANES)],
                ybuf.at[slot_, pl.ds(pl.multiple_of(dst_row * SUBLANES, SUBLANES), n * SUBLANES)],
                sem.at[slot_])
        return make

    def fetch(t_):
        _segment_copies(b_sm, c_sm, t_, copy_for(t_ % 2), lambda cp: cp.start())

    @pl.when(step == 0)
    def _():
        ybuf[...] = jnp.zeros_like(ybuf)
        fetch(0)
        fetch(1)

    def one_tile(k):
        t = step * COMBINE_TILES + k
        slot = t % 2
        tok = slice(k * ROUTE_TILE, (k + 1) * ROUTE_TILE)

        rows_in = _tile_rows(c_sm, t)

        @pl.when(rows_in > 0)
        def _():
            n = rows_in * SUBLANES
            pltpu.make_async_copy(ye_hbm.at[pl.ds(0, n)], ybuf.at[slot, pl.ds(0, n)],
                                  sem.at[slot]).wait()

        offs = []
        off = 0
        for e in range(N_EXPERTS):
            offs.append(off)
            off = off + c_sm[t * N_EXPERTS + e]
        offs.append(off)
        total = off
        nchunk = (total + SLOT_CHUNK - 1) // SLOT_CHUNK
        seg_lo = jnp.zeros((N_EXPERTS, 1), I32)
        seg_hi = jnp.zeros((N_EXPERTS, 1), I32)
        for e in range(N_EXPERTS):
            seg_lo = jnp.where(eid == e, offs[e], seg_lo)
            seg_hi = jnp.where(eid == e, offs[e + 1], seg_hi)
        seg_lo_f = seg_lo.astype(F32)
        locb = loc_ref[tok, :].astype(F32).astype(BF16)

        z_scr[...] = alpha * x_ref[tok, :]

        def chunk_body(m, carry):
            base = m * SLOT_CHUNK
            srow = lane + base
            inseg = jnp.where(srow >= seg_lo, jnp.where(srow < seg_hi, 1.0, 0.0), 0.0)
            rank = srow.astype(F32) - jnp.sum(inseg * seg_lo_f, axis=0, keepdims=True)
            want = jnp.dot(locb, inseg.astype(BF16), preferred_element_type=F32)
            ptb = jnp.where(want == rank, 1.0, 0.0).astype(BF16)
            row0 = (base * SUBLANES if isinstance(m, int)
                    else pl.multiple_of(base * SUBLANES, SUBLANES))
            yc = jnp.concatenate(
                [ybuf[slot, pl.ds(row0 + s, SLOT_CHUNK, stride=SUBLANES), :]
                 for s in range(SUBLANES)], axis=1)
            yc = jnp.where(sub < total - base, yc, 0.0)
            hi = yc.astype(BF16)
            hl = hl_scr.at[m % PEELED_CHUNKS]
            hl[0:SLOT_CHUNK] = hi
            hl[SLOT_CHUNK:2 * SLOT_CHUNK] = (yc - hi.astype(F32)).astype(BF16)
            z_scr[...] += jnp.dot(jnp.concatenate([ptb, ptb], axis=1), hl[...],
                                  preferred_element_type=F32)
            return carry

        for m in range(PEELED_CHUNKS):
            chunk_body(m, 0)
        lax.fori_loop(PEELED_CHUNKS, jnp.maximum(nchunk, PEELED_CHUNKS), chunk_body, 0)

        @pl.when(t + 2 < nt)
        def _():
            fetch(t + 2)

        o_ref[tok, :] = _layer_norm(z_scr[...], g_ref[...], be_ref[...])

    for k in range(COMBINE_TILES):
        one_tile(k)


def _combine(b_flat, c_flat, x2, loc_t, g, b, ye2d, cap, alpha):
    S, D = x2.shape
    tok = COMBINE_TILES * ROUTE_TILE
    assert S % tok == 0 and S // ROUTE_TILE >= 2
    max_rows = N_EXPERTS * ROUTE_TILE
    cur = lambda t, b_, c_: (t, 0)
    const2 = lambda t, b_, c_: (0, 0)
    return pl.pallas_call(
        functools.partial(_combine_kernel, cap, alpha),
        grid_spec=pltpu.PrefetchScalarGridSpec(
            num_scalar_prefetch=2,
            grid=(S // tok,),
            in_specs=[pl.BlockSpec((tok, D), cur),
                      pl.BlockSpec((tok, N_EXPERTS), cur),
                      pl.BlockSpec((1, D), const2),
                      pl.BlockSpec((1, D), const2),
                      pl.BlockSpec(memory_space=pl.ANY)],
            out_specs=pl.BlockSpec((tok, D), cur),
            scratch_shapes=[pltpu.VMEM((2, max_rows * SUBLANES, LANES), F32),
                            pltpu.SemaphoreType.DMA((2,)),
                            pltpu.VMEM((PEELED_CHUNKS, 2 * SLOT_CHUNK, D), BF16),
                            pltpu.VMEM((ROUTE_TILE, D), F32)]),
        out_shape=jax.ShapeDtypeStruct((S, D), F32),
        compiler_params=pltpu.CompilerParams(
            dimension_semantics=("arbitrary",), vmem_limit_bytes=VMEM_LIMIT),
        name="combine",
    )(b_flat, c_flat, x2, loc_t, g, b, ye2d)


def _layer(x, mem, w_in, rpb, conv_w, w_mix_out, ln1_g, ln1_b, wq, wk, wv, wo,
           ln2_g, ln2_b, w_router, w_gate, w_up, w_down, ln3_g, ln3_b, alpha):
    S, D = x.shape
    cap = CAPACITY_FACTOR * S // N_EXPERTS
    assert S % TOK_BLOCK == 0 and S % PROJ_TM == 0 and S % XATTN_TM == 0
    assert S // GRID_W >= 2 * ROW_BLOCK
    assert cap % FFN_TM == 0 and cap <= S
    row = lambda a: a.reshape(1, -1)

    q, k, v, bg, u = _proj(x, w_in.astype(BF16))
    x1 = _mixer(x, q, k, v, bg, u, w_mix_out.astype(BF16), rpb,
                conv_w, row(ln1_g), row(ln1_b), alpha)

    wr_t = w_router.T
    wr_hi = wr_t.astype(BF16)
    wr_lo = (wr_t - wr_hi.astype(F32)).astype(BF16)
    x2, x2b, aff3 = _xattn(x1, mem, wq.astype(BF16), wk.astype(BF16), wv.astype(BF16),
                           wo.astype(BF16), row(ln2_g), row(ln2_b), wr_hi, wr_lo, alpha)

    loc3, b3, c3 = _route(aff3, cap)
    b_flat = b3.reshape(-1)
    c_flat = c3.reshape(-1)
    loc_t = loc3.transpose(0, 2, 1).reshape(S, N_EXPERTS)

    xe2d = _dispatch(b_flat, c_flat, x2b, loc3, cap)
    wr_pad = jnp.pad(w_router.astype(BF16), ((0, 0), (0, LANES - N_EXPERTS)))
    ye2d = _ffn(xe2d, w_gate, w_up, w_down, wr_pad, cap)
    return _combine(b_flat, c_flat, x2, loc_t, row(ln3_g), row(ln3_b), ye2d, cap, alpha)


def kernel(x, mem, w_in, na_rpb, conv_w, w_mix_out, ln1_g, ln1_b, w_mem_q, w_mem_k,
           w_mem_v, w_mem_out, ln2_g, ln2_b, w_router, w_exp_gate, w_exp_up,
           w_exp_down, ln3_g, ln3_b):
    depth = w_in.shape[0]
    alpha = (2 * depth) ** 0.25
    outs = []
    for bi in range(x.shape[0]):
        xb = x[bi]
        for l in range(depth):
            xb = _layer(xb, mem[bi], w_in[l], na_rpb[l], conv_w[l], w_mix_out[l],
                        ln1_g[l], ln1_b[l], w_mem_q[l], w_mem_k[l], w_mem_v[l],
                        w_mem_out[l], ln2_g[l], ln2_b[l], w_router[l], w_exp_gate[l],
                        w_exp_up[l], w_exp_down[l], ln3_g[l], ln3_b[l], alpha)
        outs.append(xb)
    return jnp.stack(outs)
```

```python
import functools

import jax
import jax.numpy as jnp
from jax import lax
from jax.experimental import pallas as pl
from jax.experimental.pallas import tpu as pltpu

F32 = jnp.float32
BF16 = jnp.bfloat16
I32 = jnp.int32

GRID_W = 64
HEAD_DIM = 64
NA_HEADS = 8
NA_WIDTH = NA_HEADS * HEAD_DIM
WIN_ROWS = 8
WIN_COLS = 16
MEM_HEADS = 4
MEM_HEAD_DIM = 128
N_EXPERTS = 16
CAPACITY_FACTOR = 2
LN_EPS = 1e-5
NEG = -1e30

LANES = 128
SUBLANES = 8
VMEM_LIMIT = 56 * 1024 * 1024

ROW_BLOCK = 8
TOK_BLOCK = ROW_BLOCK * GRID_W
PROJ_TM = 1024
XATTN_TM = 1024
ROUTE_TILE = 256
SLOT_CHUNK = 256
SM_ROWS = 16
PEELED_CHUNKS = 2
DISPATCH_TILES = 4
COMBINE_TILES = 2
DISPATCH_SLOTS = 3
FFN_TM = 1024
FFN_TF = 512


def _layer_norm(z, g, b):
    mu = jnp.mean(z, axis=-1, keepdims=True)
    zc = z - mu
    var = jnp.mean(zc * zc, axis=-1, keepdims=True)
    return zc * lax.rsqrt(var + LN_EPS) * g + b


def _dot_nt(a, b):
    return lax.dot_general(a, b, (((1,), (1,)), ((), ())),
                           preferred_element_type=F32)


def _proj_kernel(x_ref, w_ref, q_ref, k_ref, v_ref, b_ref, u_ref):
    xb = x_ref[...].astype(BF16)
    nw = NA_WIDTH

    def mm(c0):
        return jnp.dot(xb, w_ref[:, c0:c0 + nw], preferred_element_type=F32)

    q_ref[...] = (mm(0) * (HEAD_DIM ** -0.5)).astype(BF16)
    k_ref[...] = mm(nw).astype(BF16)
    v_ref[...] = mm(2 * nw).astype(BF16)
    b_ref[...] = mm(3 * nw)
    u_ref[...] = mm(4 * nw) * mm(5 * nw)


def _proj(x, w_in_b):
    S, D = x.shape
    nw = NA_WIDTH
    tm = PROJ_TM
    blk = lambda i: (i, 0)
    return pl.pallas_call(
        _proj_kernel,
        grid=(S // tm,),
        in_specs=[pl.BlockSpec((tm, D), blk),
                  pl.BlockSpec((D, 6 * nw), lambda i: (0, 0))],
        out_specs=[pl.BlockSpec((tm, nw), blk)] * 5,
        out_shape=[jax.ShapeDtypeStruct((S, nw), BF16)] * 3
        + [jax.ShapeDtypeStruct((S, nw), F32)] * 2,
        compiler_params=pltpu.CompilerParams(
            dimension_semantics=("arbitrary",), vmem_limit_bytes=VMEM_LIMIT),
        name="proj",
    )(x, w_in_b)


def _build_na_bias(rpb_ref, bias_ref):
    shape = (GRID_W, LANES)
    qc = lax.broadcasted_iota(I32, shape, 0)
    ln = lax.broadcasted_iota(I32, shape, 1)
    kc = ln % GRID_W
    c_start = jnp.clip(qc - WIN_COLS // 2, 0, GRID_W - WIN_COLS)
    base = LANES - (WIN_COLS - 1)

    def canvas(h, d):
        both = rpb_ref[h, d:d + 1, :] + pltpu.roll(rpb_ref[h, d + 1:d + 2, :], GRID_W, 1)
        t = pltpu.roll(pltpu.roll(jnp.broadcast_to(both, shape), base, 1), 0, 1,
                       stride=1, stride_axis=0)
        return jnp.where(kc >= c_start, jnp.where(kc < c_start + WIN_COLS, t, NEG), NEG)

    for par in range(2):
        for hp in range(NA_HEADS // 2):
            for m in range(WIN_ROWS):
                d = 2 * m + par
                bias_ref[par, hp, m, 0:GRID_W] = canvas(2 * hp, d)
                bias_ref[par, hp, m, GRID_W:2 * GRID_W] = canvas(2 * hp + 1, d)


def _mixer_kernel(rows, alpha,
                  q_ref, kp_ref, kc_ref, kn_ref, vp_ref, vc_ref, vn_ref,
                  b_ref, u_ref, up_ref, un_ref, x_ref, wout_ref, rpb_ref,
                  cw_ref, g_ref, be_ref, o_ref, kbuf, vbuf, mix, bias_ref,
                  s_scr0, s_scr1, p_scr0, p_scr1):
    i = pl.program_id(0)
    nb = pl.num_programs(0)
    tb = TOK_BLOCK
    win = WIN_ROWS * GRID_W

    @pl.when(i == 0)
    def _():
        _build_na_bias(rpb_ref, bias_ref)

    kbuf[0:tb] = kp_ref[...]
    kbuf[tb:2 * tb] = kc_ref[...]
    kbuf[2 * tb:3 * tb] = kn_ref[...]
    vbuf[0:tb] = vp_ref[...]
    vbuf[tb:2 * tb] = vc_ref[...]
    vbuf[2 * tb:3 * tb] = vn_ref[...]

    lane = lax.broadcasted_iota(I32, (1, LANES), 1)
    first_head = lane < HEAD_DIM
    m_lo = jnp.where(first_head, 1.0, 0.0).astype(BF16)
    m_hi = jnp.where(first_head, 0.0, 1.0).astype(BF16)

    def window(r):
        grow = i * ROW_BLOCK + r
        start = jnp.clip(grow - WIN_ROWS // 2, 0, rows - WIN_ROWS)
        w0 = pl.multiple_of((start - (i - 1) * ROW_BLOCK) * GRID_W, GRID_W)
        d0 = start - grow + WIN_ROWS
        return w0, d0 % 2, d0 // 2

    def scores(r, s_out):
        w0, _, _ = window(r)
        q0 = r * GRID_W if isinstance(r, int) else pl.multiple_of(r * GRID_W, GRID_W)
        for hp in range(NA_HEADS // 2):
            cs = slice(hp * LANES, (hp + 1) * LANES)
            qp = q_ref[pl.ds(q0, GRID_W), cs]
            lhs = jnp.concatenate([qp * m_lo, qp * m_hi], axis=0)
            kw = kbuf[pl.ds(w0, win), cs]
            s_out[hp] = _dot_nt(lhs, kw)

    def probs(r, s_in, p_out):
        _, par, m0 = window(r)
        for hp in range(NA_HEADS // 2):
            for rb in range(2 * GRID_W // SM_ROWS):
                rs = slice(rb * SM_ROWS, (rb + 1) * SM_ROWS)
                bias = jnp.concatenate(
                    [bias_ref[par, hp, m0 + j, rs, :] for j in range(WIN_ROWS // 2)], axis=1)
                s = s_in[hp, rs, :] + bias
                p = jnp.exp(s - jnp.max(s, axis=-1, keepdims=True))
                p = p * (1.0 / jnp.sum(p, axis=-1, keepdims=True))
                p_out[hp, rs, :] = p.astype(BF16)

    def values(r, p_in):
        w0, _, _ = window(r)
        q0 = r * GRID_W if isinstance(r, int) else pl.multiple_of(r * GRID_W, GRID_W)
        for hp in range(NA_HEADS // 2):
            cs = slice(hp * LANES, (hp + 1) * LANES)
            vw = vbuf[pl.ds(w0, win), cs]
            o2 = jnp.dot(p_in[hp], vw, preferred_element_type=F32)
            y = jnp.where(first_head, o2[0:GRID_W], o2[GRID_W:2 * GRID_W])
            mix[pl.ds(q0, GRID_W), cs] = y.astype(BF16)

    s_buf = (s_scr0, s_scr1)
    p_buf = (p_scr0, p_scr1)

    def steady(k, carry):
        for half in range(2):
            r = 1 + 2 * k + half
            values(r - 1, p_buf[half])
            probs(r, s_buf[1 - half], p_buf[1 - half])
            scores(r + 1, s_buf[half])
        return carry

    scores(0, s_buf[0])
    probs(0, s_buf[0], p_buf[0])
    scores(1, s_buf[1])
    lax.fori_loop(0, (ROW_BLOCK - 2) // 2, steady, 0)
    values(ROW_BLOCK - 2, p_buf[0])
    probs(ROW_BLOCK - 1, s_buf[1], p_buf[1])
    values(ROW_BLOCK - 1, p_buf[1])

    u = u_ref[...]
    prev_row = up_ref[SUBLANES - 1:SUBLANES, :] * jnp.where(i > 0, 1.0, 0.0)
    next_row = un_ref[0:1, :] * jnp.where(i < nb - 1, 1.0, 0.0)
    rid = lax.broadcasted_iota(I32, (tb, 1), 0)
    um1 = jnp.where(rid == 0, prev_row, pltpu.roll(u, 1, axis=0))
    up1 = jnp.where(rid == tb - 1, next_row, pltpu.roll(u, tb - 1, axis=0))
    cw = cw_ref[...]
    yc = b_ref[...] * (cw[0:1] * um1 + cw[1:2] * u + cw[2:3] * up1)
    mix[:, NA_WIDTH:] = yc.astype(BF16)

    y = jnp.dot(mix[...], wout_ref[...], preferred_element_type=F32)
    z = alpha * x_ref[...] + y
    o_ref[...] = _layer_norm(z, g_ref[...], be_ref[...])


def _mixer(x, q, k, v, bg, u, w_out_b, rpb, conv_w, g, b, alpha):
    S, D = x.shape
    rpb_pad = jnp.pad(rpb.astype(F32), ((0, 0), (1, 1), (0, LANES - rpb.shape[2])))
    nw = NA_WIDTH
    tb = TOK_BLOCK
    nb = S // tb
    rows = S // GRID_W
    cur = lambda i: (i, 0)
    prv = lambda i: (jnp.maximum(i - 1, 0), 0)
    nxt = lambda i: (jnp.minimum(i + 1, nb - 1), 0)
    halo = tb // SUBLANES
    hprev = lambda i: (jnp.maximum(i * halo - 1, 0), 0)
    hnext = lambda i: (jnp.minimum((i + 1) * halo, S // SUBLANES - 1), 0)
    const2 = lambda i: (0, 0)
    kv = lambda im: pl.BlockSpec((tb, nw), im)
    return pl.pallas_call(
        functools.partial(_mixer_kernel, rows, alpha),
        grid=(nb,),
        in_specs=[kv(cur), kv(prv), kv(cur), kv(nxt), kv(prv), kv(cur), kv(nxt),
                  kv(cur), kv(cur),
                  pl.BlockSpec((SUBLANES, nw), hprev),
                  pl.BlockSpec((SUBLANES, nw), hnext),
                  pl.BlockSpec((tb, D), cur),
                  pl.BlockSpec((D, D), const2),
                  pl.BlockSpec(rpb_pad.shape, lambda i: (0, 0, 0)),
                  pl.BlockSpec(conv_w.shape, const2),
                  pl.BlockSpec((1, D), const2),
                  pl.BlockSpec((1, D), const2)],
        out_specs=pl.BlockSpec((tb, D), cur),
        out_shape=jax.ShapeDtypeStruct((S, D), F32),
        scratch_shapes=[pltpu.VMEM((3 * tb, nw), BF16),
                        pltpu.VMEM((3 * tb, nw), BF16),
                        pltpu.VMEM((tb, D), BF16),
                        pltpu.VMEM((2, NA_HEADS // 2, WIN_ROWS, 2 * GRID_W, LANES), F32),
                        pltpu.VMEM((NA_HEADS // 2, 2 * GRID_W, WIN_ROWS * GRID_W), F32),
                        pltpu.VMEM((NA_HEADS // 2, 2 * GRID_W, WIN_ROWS * GRID_W), F32),
                        pltpu.VMEM((NA_HEADS // 2, 2 * GRID_W, WIN_ROWS * GRID_W), BF16),
                        pltpu.VMEM((NA_HEADS // 2, 2 * GRID_W, WIN_ROWS * GRID_W), BF16)],
        compiler_params=pltpu.CompilerParams(
            dimension_semantics=("arbitrary",), vmem_limit_bytes=VMEM_LIMIT),
        name="mixer",
    )(q, k, k, k, v, v, v, bg, u, u, u, x, w_out_b, rpb_pad, conv_w, g, b)


def _xattn_kernel(alpha, x_ref, mem_ref, wq_ref, wk_ref, wv_ref, wo_ref,
                  g_ref, be_ref, wrh_ref, wrl_ref,
                  x2_ref, x2b_ref, aff_ref, kmem, vmem):
    i = pl.program_id(0)

    @pl.when(i == 0)
    def _():
        mb = mem_ref[...].astype(BF16)
        kmem[...] = jnp.dot(mb, wk_ref[...], preferred_element_type=F32).astype(BF16)
        vmem[...] = jnp.dot(mb, wv_ref[...], preferred_element_type=F32).astype(BF16)

    x1 = x_ref[...]
    qb = jnp.dot(x1.astype(BF16), wq_ref[...], preferred_element_type=F32).astype(BF16)
    outs = []
    for h in range(MEM_HEADS):
        cs = slice(h * MEM_HEAD_DIM, (h + 1) * MEM_HEAD_DIM)
        s = _dot_nt(qb[:, cs], kmem[:, cs]) * (MEM_HEAD_DIM ** -0.5)
        mx = jnp.max(s, axis=-1, keepdims=True)
        p = jnp.exp(s - mx)
        l = jnp.sum(p, axis=-1, keepdims=True)
        oh = jnp.dot(p.astype(BF16), vmem[:, cs], preferred_element_type=F32)
        outs.append(oh / l)
    o = jnp.concatenate(outs, axis=1).astype(BF16)
    y = jnp.dot(o, wo_ref[...], preferred_element_type=F32)
    x2 = _layer_norm(alpha * x1 + y, g_ref[...], be_ref[...])
    x2_ref[...] = x2
    hi = x2.astype(BF16)
    x2b_ref[...] = hi
    lo = (x2 - hi.astype(F32)).astype(BF16)
    wrh = wrh_ref[...]
    r1 = _dot_nt(jnp.concatenate([wrh, wrl_ref[...]], axis=0), hi)
    lg = r1[0:N_EXPERTS] + r1[N_EXPERTS:2 * N_EXPERTS] + _dot_nt(wrh, lo)
    mx = jnp.max(lg, axis=0, keepdims=True)
    p = jnp.exp(lg - mx)
    aff = p / jnp.sum(p, axis=0, keepdims=True)
    for j in range(aff_ref.shape[0]):
        aff_ref[j] = aff[:, j * LANES:(j + 1) * LANES]


def _xattn(x1, mem, wq_b, wk_b, wv_b, wo_b, g, b, wr_hi, wr_lo, alpha):
    S, D = x1.shape
    M = mem.shape[0]
    mw = MEM_HEADS * MEM_HEAD_DIM
    tm = XATTN_TM
    cur = lambda i: (i, 0)
    const2 = lambda i: (0, 0)
    full = lambda a: pl.BlockSpec(a.shape, const2)
    return pl.pallas_call(
        functools.partial(_xattn_kernel, alpha),
        grid=(S // tm,),
        in_specs=[pl.BlockSpec((tm, D), cur), full(mem), full(wq_b), full(wk_b),
                  full(wv_b), full(wo_b), full(g), full(b), full(wr_hi), full(wr_lo)],
        out_specs=[pl.BlockSpec((tm, D), cur), pl.BlockSpec((tm, D), cur),
                   pl.BlockSpec((tm // LANES, N_EXPERTS, LANES), lambda i: (i, 0, 0))],
        out_shape=[jax.ShapeDtypeStruct((S, D), F32),
                   jax.ShapeDtypeStruct((S, D), BF16),
                   jax.ShapeDtypeStruct((S // LANES, N_EXPERTS, LANES), F32)],
        scratch_shapes=[pltpu.VMEM((M, mw), BF16), pltpu.VMEM((M, mw), BF16)],
        compiler_params=pltpu.CompilerParams(
            dimension_semantics=("arbitrary",), vmem_limit_bytes=VMEM_LIMIT),
        name="xattn",
    )(x1, mem, wq_b, wk_b, wv_b, wo_b, g, b, wr_hi, wr_lo)


def _route_kernel(cap, aff_ref, loc_ref, b_ref, c_ref, incl_scr, tot_scr, sel_scr):
    nblk = aff_ref.shape[0]
    per_tile = ROUTE_TILE // LANES
    ntile = nblk // per_tile

    def count_ge(t):
        ge = jnp.where(aff_ref[...] >= t[None], 1.0, 0.0)
        return jnp.sum(jnp.sum(ge, axis=0), axis=1, keepdims=True)

    def coarse(it, bits):
        cand = bits | jnp.left_shift(jnp.int32(1), 30 - it)
        ok = count_ge(pltpu.bitcast(cand, F32)) >= cap
        return jnp.where(ok, cand, bits)

    def fine(it, lohi):
        lo, hi = lohi
        mid = lo + (hi - lo) * 0.5
        ok = count_ge(mid) >= cap
        return jnp.where(ok, mid, lo), jnp.where(ok, hi, mid)

    bits = lax.fori_loop(0, 31, coarse, jnp.zeros((N_EXPERTS, 1), I32))
    thr, _ = lax.fori_loop(
        0, 30, fine, (pltpu.bitcast(bits, F32), pltpu.bitcast(bits + 1, F32)))
    gt_all = jnp.where(aff_ref[...] > thr[None], 1.0, 0.0)
    n_gt = jnp.sum(jnp.sum(gt_all, axis=0), axis=1, keepdims=True)
    need = cap - n_gt

    ii = lax.broadcasted_iota(I32, (LANES, LANES), 0)
    jj = lax.broadcasted_iota(I32, (LANES, LANES), 1)
    tri = jnp.where(ii <= jj, 1.0, 0.0).astype(BF16)
    ones = jnp.ones((LANES, LANES), BF16)
    flat = (nblk * N_EXPERTS, LANES)
    blocks = (nblk, N_EXPERTS, LANES)

    def block_scans(flags):
        f2 = flags.reshape(flat).astype(BF16)
        incl_scr[...] = jnp.dot(f2, tri, preferred_element_type=F32).reshape(blocks)
        tot_scr[...] = jnp.dot(f2, ones, preferred_element_type=F32).reshape(blocks)

    block_scans(jnp.where(aff_ref[...] == thr[None], 1.0, 0.0))

    def select_block(j, run_eq):
        blk = aff_ref[j]
        tie_rank = run_eq + incl_scr[j]
        sel_scr[j] = (jnp.where(blk > thr, 1.0, 0.0)
                      + jnp.where(blk == thr, 1.0, 0.0) * jnp.where(tie_rank <= need, 1.0, 0.0))
        return run_eq + tot_scr[j]

    zero = jnp.zeros((N_EXPERTS, LANES), F32)
    lax.fori_loop(0, nblk, select_block, zero)
    block_scans(sel_scr[...])

    def tile_body(t, run):
        tile_run = zero
        for j in range(per_tile):
            idx = t * per_tile + j
            sel = sel_scr[idx]
            loc = tile_run + incl_scr[idx] - sel
            loc_ref[idx] = jnp.where(sel > 0.5, loc, -1.0).astype(I32)
            tile_run = tile_run + tot_scr[idx]
        b_ref[t] = run[:, 0:1].astype(I32)
        c_ref[t] = tile_run[:, 0:1].astype(I32)
        return run + tile_run

    lax.fori_loop(0, ntile, tile_body, zero)


def _route(aff3, cap):
    nblk = aff3.shape[0]
    ntile = nblk * LANES // ROUTE_TILE
    full3 = lambda a: pl.BlockSpec(a, lambda: (0, 0, 0))
    return pl.pallas_call(
        functools.partial(_route_kernel, cap),
        in_specs=[full3(aff3.shape)],
        out_specs=[full3(aff3.shape), full3((ntile, N_EXPERTS, 1)),
                   full3((ntile, N_EXPERTS, 1))],
        out_shape=[jax.ShapeDtypeStruct(aff3.shape, I32),
                   jax.ShapeDtypeStruct((ntile, N_EXPERTS, 1), I32),
                   jax.ShapeDtypeStruct((ntile, N_EXPERTS, 1), I32)],
        scratch_shapes=[pltpu.VMEM(aff3.shape, F32)] * 3,
        compiler_params=pltpu.CompilerParams(vmem_limit_bytes=VMEM_LIMIT),
        name="route",
    )(aff3)


def _segment_copies(b_sm, c_sm, t, make_copy, action):
    off = 0
    for e in range(N_EXPERTS):
        c = c_sm[t * N_EXPERTS + e]
        b = b_sm[t * N_EXPERTS + e]

        @pl.when(c > 0)
        def _(e=e, b=b, c=c, off=off):
            action(make_copy(e, b, off, c))

        off = off + c
    return off


def _tile_rows(c_sm, t):
    rows = 0
    for e in range(N_EXPERTS):
        rows = rows + c_sm[t * N_EXPERTS + e]
    return rows


def _dispatch_kernel(cap, b_sm, c_sm, x_ref, loc_ref, xe_hbm, stg, sem):
    step = pl.program_id(0)
    nsteps = pl.num_programs(0)
    per_tile = ROUTE_TILE // LANES
    sub = lax.broadcasted_iota(I32, (SLOT_CHUNK, 1), 0)
    lane = lax.broadcasted_iota(I32, (1, LANES), 1)

    def copy_for(slot_):
        def make(e, dst_row, src_row, n):
            return pltpu.make_async_copy(
                stg.at[slot_, pl.ds(pl.multiple_of(src_row * SUBLANES, SUBLANES), n * SUBLANES)],
                xe_hbm.at[pl.ds(pl.multiple_of((e * cap + dst_row) * SUBLANES, SUBLANES),
                                n * SUBLANES)],
                sem.at[slot_])
        return make

    def wait_tile(t_):
        rows = _tile_rows(c_sm, t_)
        slot_ = t_ % DISPATCH_SLOTS

        @pl.when(rows > 0)
        def _():
            n = rows * SUBLANES
            pltpu.make_async_copy(stg.at[slot_, pl.ds(0, n)], xe_hbm.at[pl.ds(0, n)],
                                  sem.at[slot_]).wait()

    def one_tile(k):
        t = step * DISPATCH_TILES + k
        slot = t % DISPATCH_SLOTS
        tok = slice(k * ROUTE_TILE, (k + 1) * ROUTE_TILE)
        offs = []
        off = 0
        for e in range(N_EXPERTS):
            offs.append(off)
            off = off + c_sm[t * N_EXPERTS + e]
        offs.append(off)
        nchunk = (off + SLOT_CHUNK - 1) // SLOT_CHUNK
        seg_lo = jnp.zeros((1, LANES), I32)
        seg_hi = jnp.zeros((1, LANES), I32)
        for e in range(N_EXPERTS):
            seg_lo = jnp.where(lane == e, offs[e], seg_lo)
            seg_hi = jnp.where(lane == e, offs[e + 1], seg_hi)
        seg_lo_f = seg_lo.astype(F32)
        locf = jnp.concatenate(
            [loc_ref[k * per_tile + j] for j in range(per_tile)], axis=1).astype(F32)
        locb = jnp.concatenate(
            [locf, jnp.zeros((LANES - N_EXPERTS, ROUTE_TILE), F32)], axis=0).astype(BF16)

        def chunk_body(m, carry):
            base = m * SLOT_CHUNK
            srow = sub + base
            inseg = jnp.where(srow >= seg_lo, jnp.where(srow < seg_hi, 1.0, 0.0), 0.0)
            rank = srow.astype(F32) - jnp.sum(inseg * seg_lo_f, axis=1, keepdims=True)
            want = jnp.dot(inseg.astype(BF16), locb, preferred_element_type=F32)
            p = jnp.where(want == rank, 1.0, 0.0)
            res = jnp.dot(p.astype(BF16), x_ref[tok, :], preferred_element_type=F32)
            row0 = (base * SUBLANES if isinstance(m, int)
                    else pl.multiple_of(base * SUBLANES, SUBLANES))
            for s in range(SUBLANES):
                stg[slot, pl.ds(row0 + s, SLOT_CHUNK, stride=SUBLANES), :] = (
                    res[:, s * LANES:(s + 1) * LANES])
            return carry

        for m in range(PEELED_CHUNKS):
            chunk_body(m, 0)
        lax.fori_loop(PEELED_CHUNKS, jnp.maximum(nchunk, PEELED_CHUNKS), chunk_body, 0)

        _segment_copies(b_sm, c_sm, t, copy_for(slot), lambda cp: cp.start())

        @pl.when(t >= DISPATCH_SLOTS - 1)
        def _():
            wait_tile(t - (DISPATCH_SLOTS - 1))

    for k in range(DISPATCH_TILES):
        one_tile(k)

    @pl.when(step == nsteps - 1)
    def _():
        last = nsteps * DISPATCH_TILES - 1
        for d in range(DISPATCH_SLOTS - 2, -1, -1):
            wait_tile(last - d)


def _dispatch(b_flat, c_flat, x2b, loc3, cap):
    S, D = x2b.shape
    tok = DISPATCH_TILES * ROUTE_TILE
    assert S % tok == 0 and S // ROUTE_TILE >= DISPATCH_SLOTS
    max_rows = N_EXPERTS * ROUTE_TILE
    return pl.pallas_call(
        functools.partial(_dispatch_kernel, cap),
        grid_spec=pltpu.PrefetchScalarGridSpec(
            num_scalar_prefetch=2,
            grid=(S // tok,),
            in_specs=[pl.BlockSpec((tok, D), lambda t, b, c: (t, 0)),
                      pl.BlockSpec((tok // LANES, N_EXPERTS, LANES), lambda t, b, c: (t, 0, 0))],
            out_specs=pl.BlockSpec(memory_space=pl.ANY),
            scratch_shapes=[pltpu.VMEM((DISPATCH_SLOTS, max_rows * SUBLANES, LANES), F32),
                            pltpu.SemaphoreType.DMA((DISPATCH_SLOTS,))]),
        out_shape=jax.ShapeDtypeStruct((N_EXPERTS * cap * SUBLANES, LANES), F32),
        compiler_params=pltpu.CompilerParams(
            dimension_semantics=("arbitrary",), vmem_limit_bytes=VMEM_LIMIT),
        name="dispatch",
    )(b_flat, c_flat, x2b, loc3)


def _ffn_kernel(nf, xe_ref, wg_hbm, wu_hbm, wd_hbm, wr_ref, o_ref,
                xb, acc, gate, wgb, wub, wdb, sem):
    e = pl.program_id(0)
    m = pl.program_id(1)
    ne = pl.num_programs(0)
    nm = pl.num_programs(1)
    tm = xb.shape[0]
    tf = wgb.shape[2]

    def chunk_copies(e_, f_, slot):
        cols = pl.ds(pl.multiple_of(f_ * tf, tf), tf)
        return (pltpu.make_async_copy(wg_hbm.at[e_, :, cols], wgb.at[slot], sem.at[0, slot]),
                pltpu.make_async_copy(wu_hbm.at[e_, :, cols], wub.at[slot], sem.at[1, slot]),
                pltpu.make_async_copy(wd_hbm.at[e_, cols, :], wdb.at[slot], sem.at[2, slot]))

    def fetch(e_, f_, slot):
        for cp in chunk_copies(e_, f_, slot):
            cp.start()

    def arrive(e_, f_, slot):
        for cp in chunk_copies(e_, f_, slot):
            cp.wait()

    @pl.when((e == 0) & (m == 0))
    def _():
        fetch(0, 0, 0)

    for s in range(SUBLANES):
        xb[:, s * LANES:(s + 1) * LANES] = (
            xe_ref[pl.ds(s, tm, stride=SUBLANES), :].astype(BF16))
    lg = jnp.dot(xb[...], wr_ref[...], preferred_element_type=F32)
    lane = lax.broadcasted_iota(I32, (1, LANES), 1)
    lg = jnp.where(lane < N_EXPERTS, lg, NEG)
    p = jnp.exp(lg - jnp.max(lg, axis=-1, keepdims=True))
    mine = jnp.sum(jnp.where(lane == e, p, 0.0), axis=-1, keepdims=True)
    gate[...] = mine / jnp.sum(p, axis=-1, keepdims=True)

    def down_proj(slot):
        x = xb[...]
        g = jnp.dot(x, wgb[slot].astype(BF16), preferred_element_type=F32)
        u = jnp.dot(x, wub[slot].astype(BF16), preferred_element_type=F32)
        h = (g * jax.nn.sigmoid(g) * u).astype(BF16)
        return jnp.dot(h, wdb[slot].astype(BF16), preferred_element_type=F32)

    fetch(e, 1, 1)
    arrive(e, 0, 0)
    acc[...] = down_proj(0)

    def middle(f, carry):
        slot = f % 2
        fetch(e, f + 1, 1 - slot)
        arrive(e, f, slot)
        acc[...] += down_proj(slot)
        return carry

    lax.fori_loop(1, nf - 1, middle, 0)

    @pl.when(jnp.logical_not((e == ne - 1) & (m == nm - 1)))
    def _():
        fetch(jnp.where(m == nm - 1, e + 1, e), 0, 0)

    arrive(e, nf - 1, 1)
    y = (acc[...] + down_proj(1)) * gate[...]
    for s in range(SUBLANES):
        o_ref[pl.ds(s, tm, stride=SUBLANES), :] = y[:, s * LANES:(s + 1) * LANES]


def _ffn(xe2d, w_gate, w_up, w_down, wr_pad, cap):
    E, D, FF = w_gate.shape
    tm, tf = FFN_TM, FFN_TF
    mt = cap // tm
    nf = FF // tf
    assert nf % 2 == 0 and nf >= 4
    rows = lambda e, m: (e * mt + m, 0)
    hbm = pl.BlockSpec(memory_space=pl.ANY)
    return pl.pallas_call(
        functools.partial(_ffn_kernel, nf),
        grid=(E, mt),
        in_specs=[pl.BlockSpec((tm * SUBLANES, LANES), rows), hbm, hbm, hbm,
                  pl.BlockSpec(wr_pad.shape, lambda e, m: (0, 0))],
        out_specs=pl.BlockSpec((tm * SUBLANES, LANES), rows),
        out_shape=jax.ShapeDtypeStruct(xe2d.shape, F32),
        scratch_shapes=[pltpu.VMEM((tm, D), BF16), pltpu.VMEM((tm, D), F32),
                        pltpu.VMEM((tm, 1), F32),
                        pltpu.VMEM((2, D, tf), F32), pltpu.VMEM((2, D, tf), F32),
                        pltpu.VMEM((2, tf, D), F32),
                        pltpu.SemaphoreType.DMA((3, 2))],
        compiler_params=pltpu.CompilerParams(
            dimension_semantics=("arbitrary", "arbitrary"),
            vmem_limit_bytes=VMEM_LIMIT),
        name="ffn",
    )(xe2d, w_gate, w_up, w_down, wr_pad)


def _combine_kernel(cap, alpha, b_sm, c_sm, x_ref, loc_ref, g_ref, be_ref, ye_hbm,
                    o_ref, ybuf, sem, hl_scr, z_scr):
    step = pl.program_id(0)
    nt = pl.num_programs(0) * COMBINE_TILES
    lane = lax.broadcasted_iota(I32, (1, SLOT_CHUNK), 1)
    sub = lax.broadcasted_iota(I32, (SLOT_CHUNK, 1), 0)
    eid = lax.broadcasted_iota(I32, (N_EXPERTS, 1), 0)

    def copy_for(slot_):
        def make(e, src_row, dst_row, n):
            return pltpu.make_async_copy(
                ye_hbm.at[pl.ds(pl.multiple_of((e * cap + src_row) * SUBLANES, SUBLANES),
                                n * SUBLANES)],
                ybuf.at[slot_, pl.ds(pl.multiple_of(dst_row * SUBLANES, SUBLANES), n * SUBLANES)],
                sem.at[slot_])
        return make

    def fetch(t_):
        _segment_copies(b_sm, c_sm, t_, copy_for(t_ % 2), lambda cp: cp.start())

    @pl.when(step == 0)
    def _():
        ybuf[...] = jnp.zeros_like(ybuf)
        fetch(0)
        fetch(1)

    def one_tile(k):
        t = step * COMBINE_TILES + k
        slot = t % 2
        tok = slice(k * ROUTE_TILE, (k + 1) * ROUTE_TILE)

        rows_in = _tile_rows(c_sm, t)

        @pl.when(rows_in > 0)
        def _():
            n = rows_in * SUBLANES
            pltpu.make_async_copy(ye_hbm.at[pl.ds(0, n)], ybuf.at[slot, pl.ds(0, n)],
                                  sem.at[slot]).wait()

        offs = []
        off = 0
        for e in range(N_EXPERTS):
            offs.append(off)
            off = off + c_sm[t * N_EXPERTS + e]
        offs.append(off)
        total = off
        nchunk = (total + SLOT_CHUNK - 1) // SLOT_CHUNK
        seg_lo = jnp.zeros((N_EXPERTS, 1), I32)
        seg_hi = jnp.zeros((N_EXPERTS, 1), I32)
        for e in range(N_EXPERTS):
            seg_lo = jnp.where(eid == e, offs[e], seg_lo)
            seg_hi = jnp.where(eid == e, offs[e + 1], seg_hi)
        seg_lo_f = seg_lo.astype(F32)
        locb = loc_ref[tok, :].astype(F32).astype(BF16)

        z_scr[...] = alpha * x_ref[tok, :]

        def chunk_body(m, carry):
            base = m * SLOT_CHUNK
            srow = lane + base
            inseg = jnp.where(srow >= seg_lo, jnp.where(srow < seg_hi, 1.0, 0.0), 0.0)
            rank = srow.astype(F32) - jnp.sum(inseg * seg_lo_f, axis=0, keepdims=True)
            want = jnp.dot(locb, inseg.astype(BF16), preferred_element_type=F32)
            ptb = jnp.where(want == rank, 1.0, 0.0).astype(BF16)
            row0 = (base * SUBLANES if isinstance(m, int)
                    else pl.multiple_of(base * SUBLANES, SUBLANES))
            yc = jnp.concatenate(
                [ybuf[slot, pl.ds(row0 + s, SLOT_CHUNK, stride=SUBLANES), :]
                 for s in range(SUBLANES)], axis=1)
            yc = jnp.where(sub < total - base, yc, 0.0)
            hi = yc.astype(BF16)
            hl = hl_scr.at[m % PEELED_CHUNKS]
            hl[0:SLOT_CHUNK] = hi
            hl[SLOT_CHUNK:2 * SLOT_CHUNK] = (yc - hi.astype(F32)).astype(BF16)
            z_scr[...] += jnp.dot(jnp.concatenate([ptb, ptb], axis=1), hl[...],
                                  preferred_element_type=F32)
            return carry

        for m in range(PEELED_CHUNKS):
            chunk_body(m, 0)
        lax.fori_loop(PEELED_CHUNKS, jnp.maximum(nchunk, PEELED_CHUNKS), chunk_body, 0)

        @pl.when(t + 2 < nt)
        def _():
            fetch(t + 2)

        o_ref[tok, :] = _layer_norm(z_scr[...], g_ref[...], be_ref[...])

    for k in range(COMBINE_TILES):
        one_tile(k)


def _combine(b_flat, c_flat, x2, loc_t, g, b, ye2d, cap, alpha):
    S, D = x2.shape
    tok = COMBINE_TILES * ROUTE_TILE
    assert S % tok == 0 and S // ROUTE_TILE >= 2
    max_rows = N_EXPERTS * ROUTE_TILE
    cur = lambda t, b_, c_: (t, 0)
    const2 = lambda t, b_, c_: (0, 0)
    return pl.pallas_call(
        functools.partial(_combine_kernel, cap, alpha),
        grid_spec=pltpu.PrefetchScalarGridSpec(
            num_scalar_prefetch=2,
            grid=(S // tok,),
            in_specs=[pl.BlockSpec((tok, D), cur),
                      pl.BlockSpec((tok, N_EXPERTS), cur),
                      pl.BlockSpec((1, D), const2),
                      pl.BlockSpec((1, D), const2),
                      pl.BlockSpec(memory_space=pl.ANY)],
            out_specs=pl.BlockSpec((tok, D), cur),
            scratch_shapes=[pltpu.VMEM((2, max_rows * SUBLANES, LANES), F32),
                            pltpu.SemaphoreType.DMA((2,)),
                            pltpu.VMEM((PEELED_CHUNKS, 2 * SLOT_CHUNK, D), BF16),
                            pltpu.VMEM((ROUTE_TILE, D), F32)]),
        out_shape=jax.ShapeDtypeStruct((S, D), F32),
        compiler_params=pltpu.CompilerParams(
            dimension_semantics=("arbitrary",), vmem_limit_bytes=VMEM_LIMIT),
        name="combine",
    )(b_flat, c_flat, x2, loc_t, g, b, ye2d)


def _layer(x, mem, w_in, rpb, conv_w, w_mix_out, ln1_g, ln1_b, wq, wk, wv, wo,
           ln2_g, ln2_b, w_router, w_gate, w_up, w_down, ln3_g, ln3_b, alpha):
    S, D = x.shape
    cap = CAPACITY_FACTOR * S // N_EXPERTS
    assert S % TOK_BLOCK == 0 and S % PROJ_TM == 0 and S % XATTN_TM == 0
    assert S // GRID_W >= 2 * ROW_BLOCK
    assert cap % FFN_TM == 0 and cap <= S
    row = lambda a: a.reshape(1, -1)

    q, k, v, bg, u = _proj(x, w_in.astype(BF16))
    x1 = _mixer(x, q, k, v, bg, u, w_mix_out.astype(BF16), rpb,
                conv_w, row(ln1_g), row(ln1_b), alpha)

    wr_t = w_router.T
    wr_hi = wr_t.astype(BF16)
    wr_lo = (wr_t - wr_hi.astype(F32)).astype(BF16)
    x2, x2b, aff3 = _xattn(x1, mem, wq.astype(BF16), wk.astype(BF16), wv.astype(BF16),
                           wo.astype(BF16), row(ln2_g), row(ln2_b), wr_hi, wr_lo, alpha)

    loc3, b3, c3 = _route(aff3, cap)
    b_flat = b3.reshape(-1)
    c_flat = c3.reshape(-1)
    loc_t = loc3.transpose(0, 2, 1).reshape(S, N_EXPERTS)

    xe2d = _dispatch(b_flat, c_flat, x2b, loc3, cap)
    wr_pad = jnp.pad(w_router.astype(BF16), ((0, 0), (0, LANES - N_EXPERTS)))
    ye2d = _ffn(xe2d, w_gate, w_up, w_down, wr_pad, cap)
    return _combine(b_flat, c_flat, x2, loc_t, row(ln3_g), row(ln3_b), ye2d, cap, alpha)


def kernel(x, mem, w_in, na_rpb, conv_w, w_mix_out, ln1_g, ln1_b, w_mem_q, w_mem_k,
           w_mem_v, w_mem_out, ln2_g, ln2_b, w_router, w_exp_gate, w_exp_up,
           w_exp_down, ln3_g, ln3_b):
    depth = w_in.shape[0]
    alpha = (2 * depth) ** 0.25
    outs = []
    for bi in range(x.shape[0]):
        xb = x[bi]
        for l in range(depth):
            xb = _layer(xb, mem[bi], w_in[l], na_rpb[l], conv_w[l], w_mix_out[l],
                        ln1_g[l], ln1_b[l], w_mem_q[l], w_mem_k[l], w_mem_v[l],
                        w_mem_out[l], ln2_g[l], ln2_b[l], w_router[l], w_exp_gate[l],
                        w_exp_up[l], w_exp_down[l], ln3_g[l], ln3_b[l], alpha)
        outs.append(xb)
    return jnp.stack(outs)
```

```python
import functools

import jax
import jax.numpy as jnp
from jax import lax
from jax.experimental import pallas as pl
from jax.experimental.pallas import tpu as pltpu

F32 = jnp.float32
BF16 = jnp.bfloat16
I32 = jnp.int32

GRID_W = 64
HEAD_DIM = 64
NA_HEADS = 8
NA_WIDTH = NA_HEADS * HEAD_DIM
WIN_ROWS = 8
WIN_COLS = 16
MEM_HEADS = 4
MEM_HEAD_DIM = 128
N_EXPERTS = 16
CAPACITY_FACTOR = 2
LN_EPS = 1e-5
NEG = -1e30

LANES = 128
SUBLANES = 8
VMEM_LIMIT = 56 * 1024 * 1024

ROW_BLOCK = 8
TOK_BLOCK = ROW_BLOCK * GRID_W
PROJ_TM = 1024
XATTN_TM = 1024
ROUTE_TILE = 256
SLOT_CHUNK = 256
SM_ROWS = 16
PEELED_CHUNKS = 2
DISPATCH_TILES = 4
COMBINE_TILES = 4
DISPATCH_SLOTS = 3
FFN_TM = 1024
FFN_TF = 512


def _layer_norm(z, g, b):
    mu = jnp.mean(z, axis=-1, keepdims=True)
    zc = z - mu
    var = jnp.mean(zc * zc, axis=-1, keepdims=True)
    return zc * lax.rsqrt(var + LN_EPS) * g + b


def _dot_nt(a, b):
    return lax.dot_general(a, b, (((1,), (1,)), ((), ())),
                           preferred_element_type=F32)


def _proj_kernel(x_ref, w_ref, q_ref, k_ref, v_ref, b_ref, u_ref):
    xb = x_ref[...].astype(BF16)
    nw = NA_WIDTH

    def mm(c0):
        return jnp.dot(xb, w_ref[:, c0:c0 + nw], preferred_element_type=F32)

    q_ref[...] = (mm(0) * (HEAD_DIM ** -0.5)).astype(BF16)
    k_ref[...] = mm(nw).astype(BF16)
    v_ref[...] = mm(2 * nw).astype(BF16)
    b_ref[...] = mm(3 * nw)
    u_ref[...] = mm(4 * nw) * mm(5 * nw)


def _proj(x, w_in_b):
    S, D = x.shape
    nw = NA_WIDTH
    tm = PROJ_TM
    blk = lambda i: (i, 0)
    return pl.pallas_call(
        _proj_kernel,
        grid=(S // tm,),
        in_specs=[pl.BlockSpec((tm, D), blk),
                  pl.BlockSpec((D, 6 * nw), lambda i: (0, 0))],
        out_specs=[pl.BlockSpec((tm, nw), blk)] * 5,
        out_shape=[jax.ShapeDtypeStruct((S, nw), BF16)] * 3
        + [jax.ShapeDtypeStruct((S, nw), F32)] * 2,
        compiler_params=pltpu.CompilerParams(
            dimension_semantics=("arbitrary",), vmem_limit_bytes=VMEM_LIMIT),
        name="proj",
    )(x, w_in_b)


def _build_na_bias(rpb_ref, bias_ref):
    shape = (GRID_W, LANES)
    qc = lax.broadcasted_iota(I32, shape, 0)
    ln = lax.broadcasted_iota(I32, shape, 1)
    kc = ln % GRID_W
    c_start = jnp.clip(qc - WIN_COLS // 2, 0, GRID_W - WIN_COLS)
    base = LANES - (WIN_COLS - 1)

    def canvas(h, d):
        both = rpb_ref[h, d:d + 1, :] + pltpu.roll(rpb_ref[h, d + 1:d + 2, :], GRID_W, 1)
        t = pltpu.roll(pltpu.roll(jnp.broadcast_to(both, shape), base, 1), 0, 1,
                       stride=1, stride_axis=0)
        return jnp.where(kc >= c_start, jnp.where(kc < c_start + WIN_COLS, t, NEG), NEG)

    for par in range(2):
        for hp in range(NA_HEADS // 2):
            for m in range(WIN_ROWS):
                d = 2 * m + par
                bias_ref[par, hp, m, 0:GRID_W] = canvas(2 * hp, d)
                bias_ref[par, hp, m, GRID_W:2 * GRID_W] = canvas(2 * hp + 1, d)


def _kv_window_start(i, rows):
    return jnp.clip((i - 1) * ROW_BLOCK, 0, rows - 3 * ROW_BLOCK)


def _mixer_kernel(rows, alpha,
                  q_ref, kbuf, vbuf,
                  b_ref, u_ref, up_ref, un_ref, x_ref, wout_ref, rpb_ref,
                  cw_ref, g_ref, be_ref, o_ref, mix, bias_ref,
                  s_scr0, s_scr1, p_scr0, p_scr1):
    i = pl.program_id(0)
    nb = pl.num_programs(0)
    tb = TOK_BLOCK
    win = WIN_ROWS * GRID_W

    @pl.when(i == 0)
    def _():
        _build_na_bias(rpb_ref, bias_ref)

    lane = lax.broadcasted_iota(I32, (1, LANES), 1)
    first_head = lane < HEAD_DIM
    m_lo = jnp.where(first_head, 1.0, 0.0).astype(BF16)
    m_hi = jnp.where(first_head, 0.0, 1.0).astype(BF16)

    def window(r):
        grow = i * ROW_BLOCK + r
        start = jnp.clip(grow - WIN_ROWS // 2, 0, rows - WIN_ROWS)
        w0 = pl.multiple_of((start - _kv_window_start(i, rows)) * GRID_W, GRID_W)
        d0 = start - grow + WIN_ROWS
        return w0, d0 % 2, d0 // 2

    def scores(r, s_out):
        w0, _, _ = window(r)
        q0 = r * GRID_W if isinstance(r, int) else pl.multiple_of(r * GRID_W, GRID_W)
        for hp in range(NA_HEADS // 2):
            cs = slice(hp * LANES, (hp + 1) * LANES)
            qp = q_ref[pl.ds(q0, GRID_W), cs]
            lhs = jnp.concatenate([qp * m_lo, qp * m_hi], axis=0)
            kw = kbuf[pl.ds(w0, win), cs]
            s_out[hp] = _dot_nt(lhs, kw)

    def probs(r, s_in, p_out):
        _, par, m0 = window(r)
        for hp in range(NA_HEADS // 2):
            for rb in range(2 * GRID_W // SM_ROWS):
                rs = slice(rb * SM_ROWS, (rb + 1) * SM_ROWS)
                bias = jnp.concatenate(
                    [bias_ref[par, hp, m0 + j, rs, :] for j in range(WIN_ROWS // 2)], axis=1)
                s = s_in[hp, rs, :] + bias
                p = jnp.exp(s - jnp.max(s, axis=-1, keepdims=True))
                p = p * (1.0 / jnp.sum(p, axis=-1, keepdims=True))
                p_out[hp, rs, :] = p.astype(BF16)

    def values(r, p_in):
        w0, _, _ = window(r)
        q0 = r * GRID_W if isinstance(r, int) else pl.multiple_of(r * GRID_W, GRID_W)
        for hp in range(NA_HEADS // 2):
            cs = slice(hp * LANES, (hp + 1) * LANES)
            vw = vbuf[pl.ds(w0, win), cs]
            o2 = jnp.dot(p_in[hp], vw, preferred_element_type=F32)
            y = jnp.where(first_head, o2[0:GRID_W], o2[GRID_W:2 * GRID_W])
            mix[pl.ds(q0, GRID_W), cs] = y.astype(BF16)

    s_buf = (s_scr0, s_scr1)
    p_buf = (p_scr0, p_scr1)

    def steady(k, carry):
        for half in range(2):
            r = 1 + 2 * k + half
            values(r - 1, p_buf[half])
            probs(r, s_buf[1 - half], p_buf[1 - half])
            scores(r + 1, s_buf[half])
        return carry

    scores(0, s_buf[0])
    probs(0, s_buf[0], p_buf[0])
    scores(1, s_buf[1])
    lax.fori_loop(0, (ROW_BLOCK - 2) // 2, steady, 0)
    values(ROW_BLOCK - 2, p_buf[0])
    probs(ROW_BLOCK - 1, s_buf[1], p_buf[1])
    values(ROW_BLOCK - 1, p_buf[1])

    u = u_ref[...]
    prev_row = up_ref[SUBLANES - 1:SUBLANES, :] * jnp.where(i > 0, 1.0, 0.0)
    next_row = un_ref[0:1, :] * jnp.where(i < nb - 1, 1.0, 0.0)
    rid = lax.broadcasted_iota(I32, (tb, 1), 0)
    um1 = jnp.where(rid == 0, prev_row, pltpu.roll(u, 1, axis=0))
    up1 = jnp.where(rid == tb - 1, next_row, pltpu.roll(u, tb - 1, axis=0))
    cw = cw_ref[...]
    yc = b_ref[...] * (cw[0:1] * um1 + cw[1:2] * u + cw[2:3] * up1)
    mix[:, NA_WIDTH:] = yc.astype(BF16)

    y = jnp.dot(mix[...], wout_ref[...], preferred_element_type=F32)
    z = alpha * x_ref[...] + y
    o_ref[...] = _layer_norm(z, g_ref[...], be_ref[...])


def _mixer(x, q, k, v, bg, u, w_out_b, rpb, conv_w, g, b, alpha):
    S, D = x.shape
    rpb_pad = jnp.pad(rpb.astype(F32), ((0, 0), (1, 1), (0, LANES - rpb.shape[2])))
    nw = NA_WIDTH
    tb = TOK_BLOCK
    nb = S // tb
    rows = S // GRID_W
    cur = lambda i: (i, 0)
    kv_win = pl.BlockSpec((pl.Element(3 * tb), pl.Element(nw)),
                          lambda i: (_kv_window_start(i, rows) * GRID_W, 0))
    halo = tb // SUBLANES
    hprev = lambda i: (jnp.maximum(i * halo - 1, 0), 0)
    hnext = lambda i: (jnp.minimum((i + 1) * halo, S // SUBLANES - 1), 0)
    const2 = lambda i: (0, 0)
    kv = lambda im: pl.BlockSpec((tb, nw), im)
    return pl.pallas_call(
        functools.partial(_mixer_kernel, rows, alpha),
        grid=(nb,),
        in_specs=[kv(cur), kv_win, kv_win,
                  kv(cur), kv(cur),
                  pl.BlockSpec((SUBLANES, nw), hprev),
                  pl.BlockSpec((SUBLANES, nw), hnext),
                  pl.BlockSpec((tb, D), cur),
                  pl.BlockSpec((D, D), const2),
                  pl.BlockSpec(rpb_pad.shape, lambda i: (0, 0, 0)),
                  pl.BlockSpec(conv_w.shape, const2),
                  pl.BlockSpec((1, D), const2),
                  pl.BlockSpec((1, D), const2)],
        out_specs=pl.BlockSpec((tb, D), cur),
        out_shape=jax.ShapeDtypeStruct((S, D), F32),
        scratch_shapes=[pltpu.VMEM((tb, D), BF16),
                        pltpu.VMEM((2, NA_HEADS // 2, WIN_ROWS, 2 * GRID_W, LANES), F32),
                        pltpu.VMEM((NA_HEADS // 2, 2 * GRID_W, WIN_ROWS * GRID_W), F32),
                        pltpu.VMEM((NA_HEADS // 2, 2 * GRID_W, WIN_ROWS * GRID_W), F32),
                        pltpu.VMEM((NA_HEADS // 2, 2 * GRID_W, WIN_ROWS * GRID_W), BF16),
                        pltpu.VMEM((NA_HEADS // 2, 2 * GRID_W, WIN_ROWS * GRID_W), BF16)],
        compiler_params=pltpu.CompilerParams(
            dimension_semantics=("arbitrary",), vmem_limit_bytes=VMEM_LIMIT),
        name="mixer",
    )(q, k, v, bg, u, u, u, x, w_out_b, rpb_pad, conv_w, g, b)


def _xattn_kernel(alpha, x_ref, mem_ref, wq_ref, wk_ref, wv_ref, wo_ref,
                  g_ref, be_ref, wrh_ref, wrl_ref,
                  x2_ref, x2b_ref, aff_ref, kmem, vmem):
    i = pl.program_id(0)

    @pl.when(i == 0)
    def _():
        mb = mem_ref[...].astype(BF16)
        kmem[...] = jnp.dot(mb, wk_ref[...], preferred_element_type=F32).astype(BF16)
        vmem[...] = jnp.dot(mb, wv_ref[...], preferred_element_type=F32).astype(BF16)

    x1 = x_ref[...]
    qb = jnp.dot(x1.astype(BF16), wq_ref[...], preferred_element_type=F32).astype(BF16)
    outs = []
    for h in range(MEM_HEADS):
        cs = slice(h * MEM_HEAD_DIM, (h + 1) * MEM_HEAD_DIM)
        s = _dot_nt(qb[:, cs], kmem[:, cs]) * (MEM_HEAD_DIM ** -0.5)
        mx = jnp.max(s, axis=-1, keepdims=True)
        p = jnp.exp(s - mx)
        l = jnp.sum(p, axis=-1, keepdims=True)
        oh = jnp.dot(p.astype(BF16), vmem[:, cs], preferred_element_type=F32)
        outs.append(oh / l)
    o = jnp.concatenate(outs, axis=1).astype(BF16)
    y = jnp.dot(o, wo_ref[...], preferred_element_type=F32)
    x2 = _layer_norm(alpha * x1 + y, g_ref[...], be_ref[...])
    x2_ref[...] = x2
    hi = x2.astype(BF16)
    x2b_ref[...] = hi
    lo = (x2 - hi.astype(F32)).astype(BF16)
    wrh = wrh_ref[...]
    r1 = _dot_nt(jnp.concatenate([wrh, wrl_ref[...]], axis=0), hi)
    lg = r1[0:N_EXPERTS] + r1[N_EXPERTS:2 * N_EXPERTS] + _dot_nt(wrh, lo)
    mx = jnp.max(lg, axis=0, keepdims=True)
    p = jnp.exp(lg - mx)
    aff = p / jnp.sum(p, axis=0, keepdims=True)
    for j in range(aff_ref.shape[0]):
        aff_ref[j] = aff[:, j * LANES:(j + 1) * LANES]


def _xattn(x1, mem, wq_b, wk_b, wv_b, wo_b, g, b, wr_hi, wr_lo, alpha):
    S, D = x1.shape
    M = mem.shape[0]
    mw = MEM_HEADS * MEM_HEAD_DIM
    tm = XATTN_TM
    cur = lambda i: (i, 0)
    const2 = lambda i: (0, 0)
    full = lambda a: pl.BlockSpec(a.shape, const2)
    return pl.pallas_call(
        functools.partial(_xattn_kernel, alpha),
        grid=(S // tm,),
        in_specs=[pl.BlockSpec((tm, D), cur), full(mem), full(wq_b), full(wk_b),
                  full(wv_b), full(wo_b), full(g), full(b), full(wr_hi), full(wr_lo)],
        out_specs=[pl.BlockSpec((tm, D), cur), pl.BlockSpec((tm, D), cur),
                   pl.BlockSpec((tm // LANES, N_EXPERTS, LANES), lambda i: (i, 0, 0))],
        out_shape=[jax.ShapeDtypeStruct((S, D), F32),
                   jax.ShapeDtypeStruct((S, D), BF16),
                   jax.ShapeDtypeStruct((S // LANES, N_EXPERTS, LANES), F32)],
        scratch_shapes=[pltpu.VMEM((M, mw), BF16), pltpu.VMEM((M, mw), BF16)],
        compiler_params=pltpu.CompilerParams(
            dimension_semantics=("arbitrary",), vmem_limit_bytes=VMEM_LIMIT),
        name="xattn",
    )(x1, mem, wq_b, wk_b, wv_b, wo_b, g, b, wr_hi, wr_lo)


def _route_kernel(cap, aff_ref, loc_ref, b_ref, c_ref, incl_scr, tot_scr, sel_scr):
    nblk = aff_ref.shape[0]
    per_tile = ROUTE_TILE // LANES
    ntile = nblk // per_tile

    def count_ge(t):
        ge = jnp.where(aff_ref[...] >= t[None], 1.0, 0.0)
        return jnp.sum(jnp.sum(ge, axis=0), axis=1, keepdims=True)

    def coarse(it, bits):
        cand = bits | jnp.left_shift(jnp.int32(1), 30 - it)
        ok = count_ge(pltpu.bitcast(cand, F32)) >= cap
        return jnp.where(ok, cand, bits)

    def fine(it, lohi):
        lo, hi = lohi
        mid = lo + (hi - lo) * 0.5
        ok = count_ge(mid) >= cap
        return jnp.where(ok, mid, lo), jnp.where(ok, hi, mid)

    bits = lax.fori_loop(0, 31, coarse, jnp.zeros((N_EXPERTS, 1), I32))
    thr, _ = lax.fori_loop(
        0, 30, fine, (pltpu.bitcast(bits, F32), pltpu.bitcast(bits + 1, F32)))
    gt_all = jnp.where(aff_ref[...] > thr[None], 1.0, 0.0)
    n_gt = jnp.sum(jnp.sum(gt_all, axis=0), axis=1, keepdims=True)
    need = cap - n_gt

    ii = lax.broadcasted_iota(I32, (LANES, LANES), 0)
    jj = lax.broadcasted_iota(I32, (LANES, LANES), 1)
    tri = jnp.where(ii <= jj, 1.0, 0.0).astype(BF16)
    ones = jnp.ones((LANES, LANES), BF16)
    flat = (nblk * N_EXPERTS, LANES)
    blocks = (nblk, N_EXPERTS, LANES)

    def block_scans(flags):
        f2 = flags.reshape(flat).astype(BF16)
        incl_scr[...] = jnp.dot(f2, tri, preferred_element_type=F32).reshape(blocks)
        tot_scr[...] = jnp.dot(f2, ones, preferred_element_type=F32).reshape(blocks)

    block_scans(jnp.where(aff_ref[...] == thr[None], 1.0, 0.0))

    def select_block(j, run_eq):
        blk = aff_ref[j]
        tie_rank = run_eq + incl_scr[j]
        sel_scr[j] = (jnp.where(blk > thr, 1.0, 0.0)
                      + jnp.where(blk == thr, 1.0, 0.0) * jnp.where(tie_rank <= need, 1.0, 0.0))
        return run_eq + tot_scr[j]

    zero = jnp.zeros((N_EXPERTS, LANES), F32)
    lax.fori_loop(0, nblk, select_block, zero)
    block_scans(sel_scr[...])

    def tile_body(t, run):
        tile_run = zero
        for j in range(per_tile):
            idx = t * per_tile + j
            sel = sel_scr[idx]
            loc = tile_run + incl_scr[idx] - sel
            loc_ref[idx] = jnp.where(sel > 0.5, loc, -1.0).astype(I32)
            tile_run = tile_run + tot_scr[idx]
        b_ref[t] = run[:, 0:1].astype(I32)
        c_ref[t] = tile_run[:, 0:1].astype(I32)
        return run + tile_run

    lax.fori_loop(0, ntile, tile_body, zero)


def _route(aff3, cap):
    nblk = aff3.shape[0]
    ntile = nblk * LANES // ROUTE_TILE
    full3 = lambda a: pl.BlockSpec(a, lambda: (0, 0, 0))
    return pl.pallas_call(
        functools.partial(_route_kernel, cap),
        in_specs=[full3(aff3.shape)],
        out_specs=[full3(aff3.shape), full3((ntile, N_EXPERTS, 1)),
                   full3((ntile, N_EXPERTS, 1))],
        out_shape=[jax.ShapeDtypeStruct(aff3.shape, I32),
                   jax.ShapeDtypeStruct((ntile, N_EXPERTS, 1), I32),
                   jax.ShapeDtypeStruct((ntile, N_EXPERTS, 1), I32)],
        scratch_shapes=[pltpu.VMEM(aff3.shape, F32)] * 3,
        compiler_params=pltpu.CompilerParams(vmem_limit_bytes=VMEM_LIMIT),
        name="route",
    )(aff3)


def _segment_copies(b_sm, c_sm, t, make_copy, action):
    off = 0
    for e in range(N_EXPERTS):
        c = c_sm[t * N_EXPERTS + e]
        b = b_sm[t * N_EXPERTS + e]

        @pl.when(c > 0)
        def _(e=e, b=b, c=c, off=off):
            action(make_copy(e, b, off, c))

        off = off + c
    return off


def _tile_rows(c_sm, t):
    rows = 0
    for e in range(N_EXPERTS):
        rows = rows + c_sm[t * N_EXPERTS + e]
    return rows


def _dispatch_kernel(cap, b_sm, c_sm, x_ref, loc_ref, xe_hbm, stg, sem):
    step = pl.program_id(0)
    nsteps = pl.num_programs(0)
    per_tile = ROUTE_TILE // LANES
    sub = lax.broadcasted_iota(I32, (SLOT_CHUNK, 1), 0)
    lane = lax.broadcasted_iota(I32, (1, LANES), 1)

    def copy_for(slot_):
        def make(e, dst_row, src_row, n):
            return pltpu.make_async_copy(
                stg.at[slot_, pl.ds(pl.multiple_of(src_row * SUBLANES, SUBLANES), n * SUBLANES)],
                xe_hbm.at[pl.ds(pl.multiple_of((e * cap + dst_row) * SUBLANES, SUBLANES),
                                n * SUBLANES)],
                sem.at[slot_])
        return make

    def wait_tile(t_):
        rows = _tile_rows(c_sm, t_)
        slot_ = t_ % DISPATCH_SLOTS

        @pl.when(rows > 0)
        def _():
            n = rows * SUBLANES
            pltpu.make_async_copy(stg.at[slot_, pl.ds(0, n)], xe_hbm.at[pl.ds(0, n)],
                                  sem.at[slot_]).wait()

    def one_tile(k):
        t = step * DISPATCH_TILES + k
        slot = t % DISPATCH_SLOTS
        tok = slice(k * ROUTE_TILE, (k + 1) * ROUTE_TILE)
        offs = []
        off = 0
        for e in range(N_EXPERTS):
            offs.append(off)
            off = off + c_sm[t * N_EXPERTS + e]
        offs.append(off)
        nchunk = (off + SLOT_CHUNK - 1) // SLOT_CHUNK
        seg_lo = jnp.zeros((1, LANES), I32)
        seg_hi = jnp.zeros((1, LANES), I32)
        for e in range(N_EXPERTS):
            seg_lo = jnp.where(lane == e, offs[e], seg_lo)
            seg_hi = jnp.where(lane == e, offs[e + 1], seg_hi)
        seg_lo_f = seg_lo.astype(F32)
        locf = jnp.concatenate(
            [loc_ref[k * per_tile + j] for j in range(per_tile)], axis=1).astype(F32)
        locb = jnp.concatenate(
            [locf, jnp.zeros((LANES - N_EXPERTS, ROUTE_TILE), F32)], axis=0).astype(BF16)

        def chunk_body(m, carry):
            base = m * SLOT_CHUNK
            srow = sub + base
            inseg = jnp.where(srow >= seg_lo, jnp.where(srow < seg_hi, 1.0, 0.0), 0.0)
            rank = srow.astype(F32) - jnp.sum(inseg * seg_lo_f, axis=1, keepdims=True)
            want = jnp.dot(inseg.astype(BF16), locb, preferred_element_type=F32)
            p = jnp.where(want == rank, 1.0, 0.0)
            res = jnp.dot(p.astype(BF16), x_ref[tok, :], preferred_element_type=F32)
            row0 = (base * SUBLANES if isinstance(m, int)
                    else pl.multiple_of(base * SUBLANES, SUBLANES))
            for s in range(SUBLANES):
                stg[slot, pl.ds(row0 + s, SLOT_CHUNK, stride=SUBLANES), :] = (
                    res[:, s * LANES:(s + 1) * LANES])
            return carry

        for m in range(PEELED_CHUNKS):
            chunk_body(m, 0)
        lax.fori_loop(PEELED_CHUNKS, jnp.maximum(nchunk, PEELED_CHUNKS), chunk_body, 0)

        _segment_copies(b_sm, c_sm, t, copy_for(slot), lambda cp: cp.start())

        @pl.when(t >= DISPATCH_SLOTS - 1)
        def _():
            wait_tile(t - (DISPATCH_SLOTS - 1))

    for k in range(DISPATCH_TILES):
        one_tile(k)

    @pl.when(step == nsteps - 1)
    def _():
        last = nsteps * DISPATCH_TILES - 1
        for d in range(DISPATCH_SLOTS - 2, -1, -1):
            wait_tile(last - d)


def _dispatch(b_flat, c_flat, x2b, loc3, cap):
    S, D = x2b.shape
    tok = DISPATCH_TILES * ROUTE_TILE
    assert S % tok == 0 and S // ROUTE_TILE >= DISPATCH_SLOTS
    max_rows = N_EXPERTS * ROUTE_TILE
    return pl.pallas_call(
        functools.partial(_dispatch_kernel, cap),
        grid_spec=pltpu.PrefetchScalarGridSpec(
            num_scalar_prefetch=2,
            grid=(S // tok,),
            in_specs=[pl.BlockSpec((tok, D), lambda t, b, c: (t, 0)),
                      pl.BlockSpec((tok // LANES, N_EXPERTS, LANES), lambda t, b, c: (t, 0, 0))],
            out_specs=pl.BlockSpec(memory_space=pl.ANY),
            scratch_shapes=[pltpu.VMEM((DISPATCH_SLOTS, max_rows * SUBLANES, LANES), F32),
                            pltpu.SemaphoreType.DMA((DISPATCH_SLOTS,))]),
        out_shape=jax.ShapeDtypeStruct((N_EXPERTS * cap * SUBLANES, LANES), F32),
        compiler_params=pltpu.CompilerParams(
            dimension_semantics=("arbitrary",), vmem_limit_bytes=VMEM_LIMIT),
        name="dispatch",
    )(b_flat, c_flat, x2b, loc3)


def _ffn_kernel(xe_ref, wg_ref, wu_ref, wd_ref, wr_ref, o_ref, xb, acc, gate):
    e = pl.program_id(0)
    f = pl.program_id(2)
    tm = xb.shape[0]

    @pl.when(f == 0)
    def _():
        for s in range(SUBLANES):
            xb[:, s * LANES:(s + 1) * LANES] = (
                xe_ref[pl.ds(s, tm, stride=SUBLANES), :].astype(BF16))
        lg = jnp.dot(xb[...], wr_ref[...], preferred_element_type=F32)
        lane = lax.broadcasted_iota(I32, (1, LANES), 1)
        lg = jnp.where(lane < N_EXPERTS, lg, NEG)
        p = jnp.exp(lg - jnp.max(lg, axis=-1, keepdims=True))
        mine = jnp.sum(jnp.where(lane == e, p, 0.0), axis=-1, keepdims=True)
        gate[...] = mine / jnp.sum(p, axis=-1, keepdims=True)

    def down_proj():
        x = xb[...]
        g = jnp.dot(x, wg_ref[...].astype(BF16), preferred_element_type=F32)
        u = jnp.dot(x, wu_ref[...].astype(BF16), preferred_element_type=F32)
        h = (g * jax.nn.sigmoid(g) * u).astype(BF16)
        return jnp.dot(h, wd_ref[...].astype(BF16), preferred_element_type=F32)

    last = pl.num_programs(2) - 1

    @pl.when(f == 0)
    def _():
        acc[...] = down_proj()

    @pl.when((f > 0) & (f < last))
    def _():
        acc[...] += down_proj()

    @pl.when(f == last)
    def _():
        y = (acc[...] + down_proj()) * gate[...]
        for s in range(SUBLANES):
            o_ref[pl.ds(s, tm, stride=SUBLANES), :] = y[:, s * LANES:(s + 1) * LANES]


def _ffn(xe2d, w_gate, w_up, w_down, wr_pad, cap):
    E, D, FF = w_gate.shape
    tm, tf = FFN_TM, FFN_TF
    mt = cap // tm
    rows = lambda e, m, f: (e * mt + m, 0)
    return pl.pallas_call(
        _ffn_kernel,
        grid=(E, mt, FF // tf),
        in_specs=[pl.BlockSpec((tm * SUBLANES, LANES), rows),
                  pl.BlockSpec((None, D, tf), lambda e, m, f: (e, 0, f)),
                  pl.BlockSpec((None, D, tf), lambda e, m, f: (e, 0, f)),
                  pl.BlockSpec((None, tf, D), lambda e, m, f: (e, f, 0)),
                  pl.BlockSpec(wr_pad.shape, lambda e, m, f: (0, 0))],
        out_specs=pl.BlockSpec((tm * SUBLANES, LANES), rows),
        out_shape=jax.ShapeDtypeStruct(xe2d.shape, F32),
        scratch_shapes=[pltpu.VMEM((tm, D), BF16), pltpu.VMEM((tm, D), F32),
                        pltpu.VMEM((tm, 1), F32)],
        compiler_params=pltpu.CompilerParams(
            dimension_semantics=("arbitrary", "arbitrary", "arbitrary"),
            vmem_limit_bytes=VMEM_LIMIT),
        name="ffn",
    )(xe2d, w_gate, w_up, w_down, wr_pad)


def _combine_kernel(cap, alpha, b_sm, c_sm, x_ref, loc_ref, g_ref, be_ref, ye_hbm,
                    o_ref, ybuf, sem, hl_scr, z_scr):
    step = pl.program_id(0)
    nt = pl.num_programs(0) * COMBINE_TILES
    per_tile = ROUTE_TILE // LANES
    lane = lax.broadcasted_iota(I32, (1, SLOT_CHUNK), 1)
    sub = lax.broadcasted_iota(I32, (SLOT_CHUNK, 1), 0)
    eid = lax.broadcasted_iota(I32, (N_EXPERTS, 1), 0)

    def copy_for(slot_):
        def make(e, src_row, dst_row, n):
            return pltpu.make_async_copy(
                ye_hbm.at[pl.ds(pl.multiple_of((e * cap + src_row) * SUBLANES, SUBLANES),
                                n * SUBLANES)],
                ybuf.at[slot_, pl.ds(pl.multiple_of(dst_row * SUBLANES, SUBLANES), n * SUBLANES)],
                sem.at[slot_])
        return make

    def fetch(t_):
        _segment_copies(b_sm, c_sm, t_, copy_for(t_ % 2), lambda cp: cp.start())

    @pl.when(step == 0)
    def _():
        ybuf[...] = jnp.zeros_like(ybuf)
        fetch(0)
        fetch(1)

    def one_tile(k):
        t = step * COMBINE_TILES + k
        slot = t % 2
        tok = slice(k * ROUTE_TILE, (k + 1) * ROUTE_TILE)

        rows_in = _tile_rows(c_sm, t)

        @pl.when(rows_in > 0)
        def _():
            n = rows_in * SUBLANES
            pltpu.make_async_copy(ye_hbm.at[pl.ds(0, n)], ybuf.at[slot, pl.ds(0, n)],
                                  sem.at[slot]).wait()

        offs = []
        off = 0
        for e in range(N_EXPERTS):
            offs.append(off)
            off = off + c_sm[t * N_EXPERTS + e]
        offs.append(off)
        total = off
        nchunk = (total + SLOT_CHUNK - 1) // SLOT_CHUNK
        seg_lo = jnp.zeros((N_EXPERTS, 1), I32)
        seg_hi = jnp.zeros((N_EXPERTS, 1), I32)
        for e in range(N_EXPERTS):
            seg_lo = jnp.where(eid == e, offs[e], seg_lo)
            seg_hi = jnp.where(eid == e, offs[e + 1], seg_hi)
        seg_lo_f = seg_lo.astype(F32)
        locb = jnp.concatenate(
            [loc_ref[k * per_tile + j] for j in range(per_tile)], axis=1).astype(F32).astype(BF16)

        z_scr[...] = alpha * x_ref[tok, :]

        def chunk_body(m, carry):
            base = m * SLOT_CHUNK
            srow = lane + base
            inseg = jnp.where(srow >= seg_lo, jnp.where(srow < seg_hi, 1.0, 0.0), 0.0)
            rank = srow.astype(F32) - jnp.sum(inseg * seg_lo_f, axis=0, keepdims=True)
            want = lax.dot_general(locb, inseg.astype(BF16), (((0,), (0,)), ((), ())),
                                   preferred_element_type=F32)
            ptb = jnp.where(want == rank, 1.0, 0.0).astype(BF16)
            row0 = (base * SUBLANES if isinstance(m, int)
                    else pl.multiple_of(base * SUBLANES, SUBLANES))
            yc = jnp.concatenate(
                [ybuf[slot, pl.ds(row0 + s, SLOT_CHUNK, stride=SUBLANES), :]
                 for s in range(SUBLANES)], axis=1)
            yc = jnp.where(sub < total - base, yc, 0.0)
            hi = yc.astype(BF16)
            hl = hl_scr.at[m % PEELED_CHUNKS]
            hl[0:SLOT_CHUNK] = hi
            hl[SLOT_CHUNK:2 * SLOT_CHUNK] = (yc - hi.astype(F32)).astype(BF16)
            z_scr[...] += jnp.dot(jnp.concatenate([ptb, ptb], axis=1), hl[...],
                                  preferred_element_type=F32)
            return carry

        for m in range(PEELED_CHUNKS):
            chunk_body(m, 0)
        lax.fori_loop(PEELED_CHUNKS, jnp.maximum(nchunk, PEELED_CHUNKS), chunk_body, 0)

        @pl.when(t + 2 < nt)
        def _():
            fetch(t + 2)

        o_ref[tok, :] = _layer_norm(z_scr[...], g_ref[...], be_ref[...])

    for k in range(COMBINE_TILES):
        one_tile(k)


def _combine(b_flat, c_flat, x2, loc3, g, b, ye2d, cap, alpha):
    S, D = x2.shape
    tok = COMBINE_TILES * ROUTE_TILE
    assert S % tok == 0 and S // ROUTE_TILE >= 2
    max_rows = N_EXPERTS * ROUTE_TILE
    cur = lambda t, b_, c_: (t, 0)
    const2 = lambda t, b_, c_: (0, 0)
    return pl.pallas_call(
        functools.partial(_combine_kernel, cap, alpha),
        grid_spec=pltpu.PrefetchScalarGridSpec(
            num_scalar_prefetch=2,
            grid=(S // tok,),
            in_specs=[pl.BlockSpec((tok, D), cur),
                      pl.BlockSpec((tok // LANES, N_EXPERTS, LANES),
                                   lambda t, b_, c_: (t, 0, 0)),
                      pl.BlockSpec((1, D), const2),
                      pl.BlockSpec((1, D), const2),
                      pl.BlockSpec(memory_space=pl.ANY)],
            out_specs=pl.BlockSpec((tok, D), cur),
            scratch_shapes=[pltpu.VMEM((2, max_rows * SUBLANES, LANES), F32),
                            pltpu.SemaphoreType.DMA((2,)),
                            pltpu.VMEM((PEELED_CHUNKS, 2 * SLOT_CHUNK, D), BF16),
                            pltpu.VMEM((ROUTE_TILE, D), F32)]),
        out_shape=jax.ShapeDtypeStruct((S, D), F32),
        compiler_params=pltpu.CompilerParams(
            dimension_semantics=("arbitrary",), vmem_limit_bytes=VMEM_LIMIT),
        name="combine",
    )(b_flat, c_flat, x2, loc3, g, b, ye2d)


def _layer(x, mem, w_in, rpb, conv_w, w_mix_out, ln1_g, ln1_b, wq, wk, wv, wo,
           ln2_g, ln2_b, w_router, w_gate, w_up, w_down, ln3_g, ln3_b, alpha):
    S, D = x.shape
    cap = CAPACITY_FACTOR * S // N_EXPERTS
    assert S % TOK_BLOCK == 0 and S % PROJ_TM == 0 and S % XATTN_TM == 0
    assert S // GRID_W >= 3 * ROW_BLOCK
    assert cap % FFN_TM == 0 and cap <= S
    row = lambda a: a.reshape(1, -1)

    q, k, v, bg, u = _proj(x, w_in.astype(BF16))
    x1 = _mixer(x, q, k, v, bg, u, w_mix_out.astype(BF16), rpb,
                conv_w, row(ln1_g), row(ln1_b), alpha)

    wr_t = w_router.T
    wr_hi = wr_t.astype(BF16)
    wr_lo = (wr_t - wr_hi.astype(F32)).astype(BF16)
    x2, x2b, aff3 = _xattn(x1, mem, wq.astype(BF16), wk.astype(BF16), wv.astype(BF16),
                           wo.astype(BF16), row(ln2_g), row(ln2_b), wr_hi, wr_lo, alpha)

    loc3, b3, c3 = _route(aff3, cap)
    b_flat = b3.reshape(-1)
    c_flat = c3.reshape(-1)

    xe2d = _dispatch(b_flat, c_flat, x2b, loc3, cap)
    wr_pad = jnp.pad(w_router.astype(BF16), ((0, 0), (0, LANES - N_EXPERTS)))
    ye2d = _ffn(xe2d, w_gate, w_up, w_down, wr_pad, cap)
    return _combine(b_flat, c_flat, x2, loc3, row(ln3_g), row(ln3_b), ye2d, cap, alpha)


def kernel(x, mem, w_in, na_rpb, conv_w, w_mix_out, ln1_g, ln1_b, w_mem_q, w_mem_k,
           w_mem_v, w_mem_out, ln2_g, ln2_b, w_router, w_exp_gate, w_exp_up,
           w_exp_down, ln3_g, ln3_b):
    depth = w_in.shape[0]
    alpha = (2 * depth) ** 0.25
    outs = []
    for bi in range(x.shape[0]):
        xb = x[bi]
        for l in range(depth):
            xb = _layer(xb, mem[bi], w_in[l], na_rpb[l], conv_w[l], w_mix_out[l],
                        ln1_g[l], ln1_b[l], w_mem_q[l], w_mem_k[l], w_mem_v[l],
                        w_mem_out[l], ln2_g[l], ln2_b[l], w_router[l], w_exp_gate[l],
                        w_exp_up[l], w_exp_down[l], ln3_g[l], ln3_b[l], alpha)
        outs.append(xb)
    return jnp.stack(outs)
```

```python
import functools

import jax
import jax.numpy as jnp
from jax import lax
from jax.experimental import pallas as pl
from jax.experimental.pallas import tpu as pltpu

F32 = jnp.float32
BF16 = jnp.bfloat16
I32 = jnp.int32

GRID_W = 64
HEAD_DIM = 64
NA_HEADS = 8
NA_WIDTH = NA_HEADS * HEAD_DIM
WIN_ROWS = 8
WIN_COLS = 16
MEM_HEADS = 4
MEM_HEAD_DIM = 128
N_EXPERTS = 16
CAPACITY_FACTOR = 2
LN_EPS = 1e-5
NEG = -1e30

LANES = 128
SUBLANES = 8
VMEM_LIMIT = 56 * 1024 * 1024

ROW_BLOCK = 8
TOK_BLOCK = ROW_BLOCK * GRID_W
PROJ_TM = 1024
XATTN_TM = 1024
ROUTE_TILE = 256
SLOT_CHUNK = 256
SM_ROWS = 16
PEELED_CHUNKS = 2
DISPATCH_TILES = 4
COMBINE_TILES = 4
DISPATCH_SLOTS = 3
FFN_TM = 1024
FFN_TF = 512


def _layer_norm(z, g, b):
    mu = jnp.mean(z, axis=-1, keepdims=True)
    zc = z - mu
    var = jnp.mean(zc * zc, axis=-1, keepdims=True)
    return zc * lax.rsqrt(var + LN_EPS) * g + b


def _dot_nt(a, b):
    return lax.dot_general(a, b, (((1,), (1,)), ((), ())),
                           preferred_element_type=F32)


def _proj_kernel(x_ref, w_ref, q_ref, k_ref, v_ref, b_ref, u_ref):
    xb = x_ref[...].astype(BF16)
    nw = NA_WIDTH

    def mm(c0):
        return jnp.dot(xb, w_ref[:, c0:c0 + nw], preferred_element_type=F32)

    q_ref[...] = (mm(0) * (HEAD_DIM ** -0.5)).astype(BF16)
    k_ref[...] = mm(nw).astype(BF16)
    v_ref[...] = mm(2 * nw).astype(BF16)
    b_ref[...] = mm(3 * nw)
    u_ref[...] = mm(4 * nw) * mm(5 * nw)


def _proj(x, w_in_b):
    S, D = x.shape
    nw = NA_WIDTH
    tm = PROJ_TM
    blk = lambda i: (i, 0)
    return pl.pallas_call(
        _proj_kernel,
        grid=(S // tm,),
        in_specs=[pl.BlockSpec((tm, D), blk),
                  pl.BlockSpec((D, 6 * nw), lambda i: (0, 0))],
        out_specs=[pl.BlockSpec((tm, nw), blk)] * 5,
        out_shape=[jax.ShapeDtypeStruct((S, nw), BF16)] * 3
        + [jax.ShapeDtypeStruct((S, nw), F32)] * 2,
        compiler_params=pltpu.CompilerParams(
            dimension_semantics=("arbitrary",), vmem_limit_bytes=VMEM_LIMIT),
        name="proj",
    )(x, w_in_b)


def _build_na_bias(rpb_ref, bias_ref):
    shape = (GRID_W, LANES)
    qc = lax.broadcasted_iota(I32, shape, 0)
    ln = lax.broadcasted_iota(I32, shape, 1)
    kc = ln % GRID_W
    c_start = jnp.clip(qc - WIN_COLS // 2, 0, GRID_W - WIN_COLS)
    base = LANES - (WIN_COLS - 1)

    def canvas(h, d):
        both = rpb_ref[h, d:d + 1, :] + pltpu.roll(rpb_ref[h, d + 1:d + 2, :], GRID_W, 1)
        t = pltpu.roll(pltpu.roll(jnp.broadcast_to(both, shape), base, 1), 0, 1,
                       stride=1, stride_axis=0)
        return jnp.where(kc >= c_start, jnp.where(kc < c_start + WIN_COLS, t, NEG), NEG)

    for par in range(2):
        for hp in range(NA_HEADS // 2):
            for m in range(WIN_ROWS):
                d = 2 * m + par
                bias_ref[par, hp, m, 0:GRID_W] = canvas(2 * hp, d)
                bias_ref[par, hp, m, GRID_W:2 * GRID_W] = canvas(2 * hp + 1, d)


def _kv_window_start(i, rows):
    return jnp.clip((i - 1) * ROW_BLOCK, 0, rows - 3 * ROW_BLOCK)


def _mixer_kernel(rows, alpha,
                  q_ref, kbuf, vbuf,
                  b_ref, u_ref, up_ref, un_ref, x_ref, wout_ref, rpb_ref,
                  cw_ref, g_ref, be_ref, o_ref, mix, bias_ref,
                  s_scr0, s_scr1, p_scr0, p_scr1):
    i = pl.program_id(0)
    nb = pl.num_programs(0)
    tb = TOK_BLOCK
    win = WIN_ROWS * GRID_W

    @pl.when(i == 0)
    def _():
        _build_na_bias(rpb_ref, bias_ref)

    lane = lax.broadcasted_iota(I32, (1, LANES), 1)
    first_head = lane < HEAD_DIM
    m_lo = jnp.where(first_head, 1.0, 0.0).astype(BF16)
    m_hi = jnp.where(first_head, 0.0, 1.0).astype(BF16)

    def window(r):
        grow = i * ROW_BLOCK + r
        start = jnp.clip(grow - WIN_ROWS // 2, 0, rows - WIN_ROWS)
        w0 = pl.multiple_of((start - _kv_window_start(i, rows)) * GRID_W, GRID_W)
        d0 = start - grow + WIN_ROWS
        return w0, d0 % 2, d0 // 2

    def scores(r, s_out):
        w0, _, _ = window(r)
        q0 = r * GRID_W if isinstance(r, int) else pl.multiple_of(r * GRID_W, GRID_W)
        for hp in range(NA_HEADS // 2):
            cs = slice(hp * LANES, (hp + 1) * LANES)
            qp = q_ref[pl.ds(q0, GRID_W), cs]
            lhs = jnp.concatenate([qp * m_lo, qp * m_hi], axis=0)
            kw = kbuf[pl.ds(w0, win), cs]
            s_out[hp] = _dot_nt(lhs, kw)

    def probs(r, s_in, p_out):
        _, par, m0 = window(r)
        for hp in range(NA_HEADS // 2):
            for rb in range(2 * GRID_W // SM_ROWS):
                rs = slice(rb * SM_ROWS, (rb + 1) * SM_ROWS)
                bias = jnp.concatenate(
                    [bias_ref[par, hp, m0 + j, rs, :] for j in range(WIN_ROWS // 2)], axis=1)
                s = s_in[hp, rs, :] + bias
                p = jnp.exp(s - jnp.max(s, axis=-1, keepdims=True))
                p = p * (1.0 / jnp.sum(p, axis=-1, keepdims=True))
                p_out[hp, rs, :] = p.astype(BF16)

    def values(r, p_in):
        w0, _, _ = window(r)
        q0 = r * GRID_W if isinstance(r, int) else pl.multiple_of(r * GRID_W, GRID_W)
        for hp in range(NA_HEADS // 2):
            cs = slice(hp * LANES, (hp + 1) * LANES)
            vw = vbuf[pl.ds(w0, win), cs]
            o2 = jnp.dot(p_in[hp], vw, preferred_element_type=F32)
            y = jnp.where(first_head, o2[0:GRID_W], o2[GRID_W:2 * GRID_W])
            mix[pl.ds(q0, GRID_W), cs] = y.astype(BF16)

    s_buf = (s_scr0, s_scr1)
    p_buf = (p_scr0, p_scr1)

    def steady(k, carry):
        for half in range(2):
            r = 1 + 2 * k + half
            values(r - 1, p_buf[half])
            probs(r, s_buf[1 - half], p_buf[1 - half])
            scores(r + 1, s_buf[half])
        return carry

    scores(0, s_buf[0])
    probs(0, s_buf[0], p_buf[0])
    scores(1, s_buf[1])
    lax.fori_loop(0, (ROW_BLOCK - 2) // 2, steady, 0)
    values(ROW_BLOCK - 2, p_buf[0])
    probs(ROW_BLOCK - 1, s_buf[1], p_buf[1])
    values(ROW_BLOCK - 1, p_buf[1])

    u = u_ref[...]
    prev_row = up_ref[SUBLANES - 1:SUBLANES, :] * jnp.where(i > 0, 1.0, 0.0)
    next_row = un_ref[0:1, :] * jnp.where(i < nb - 1, 1.0, 0.0)
    rid = lax.broadcasted_iota(I32, (tb, 1), 0)
    um1 = jnp.where(rid == 0, prev_row, pltpu.roll(u, 1, axis=0))
    up1 = jnp.where(rid == tb - 1, next_row, pltpu.roll(u, tb - 1, axis=0))
    cw = cw_ref[...]
    yc = b_ref[...] * (cw[0:1] * um1 + cw[1:2] * u + cw[2:3] * up1)
    mix[:, NA_WIDTH:] = yc.astype(BF16)

    y = jnp.dot(mix[...], wout_ref[...], preferred_element_type=F32)
    z = alpha * x_ref[...] + y
    o_ref[...] = _layer_norm(z, g_ref[...], be_ref[...])


def _mixer(x, q, k, v, bg, u, w_out_b, rpb, conv_w, g, b, alpha):
    S, D = x.shape
    rpb_pad = jnp.pad(rpb.astype(F32), ((0, 0), (1, 1), (0, LANES - rpb.shape[2])))
    nw = NA_WIDTH
    tb = TOK_BLOCK
    nb = S // tb
    rows = S // GRID_W
    cur = lambda i: (i, 0)
    kv_win = pl.BlockSpec((pl.Element(3 * tb), pl.Element(nw)),
                          lambda i: (_kv_window_start(i, rows) * GRID_W, 0))
    halo = tb // SUBLANES
    hprev = lambda i: (jnp.maximum(i * halo - 1, 0), 0)
    hnext = lambda i: (jnp.minimum((i + 1) * halo, S // SUBLANES - 1), 0)
    const2 = lambda i: (0, 0)
    kv = lambda im: pl.BlockSpec((tb, nw), im)
    return pl.pallas_call(
        functools.partial(_mixer_kernel, rows, alpha),
        grid=(nb,),
        in_specs=[kv(cur), kv_win, kv_win,
                  kv(cur), kv(cur),
                  pl.BlockSpec((SUBLANES, nw), hprev),
                  pl.BlockSpec((SUBLANES, nw), hnext),
                  pl.BlockSpec((tb, D), cur),
                  pl.BlockSpec((D, D), const2),
                  pl.BlockSpec(rpb_pad.shape, lambda i: (0, 0, 0)),
                  pl.BlockSpec(conv_w.shape, const2),
                  pl.BlockSpec((1, D), const2),
                  pl.BlockSpec((1, D), const2)],
        out_specs=pl.BlockSpec((tb, D), cur),
        out_shape=jax.ShapeDtypeStruct((S, D), F32),
        scratch_shapes=[pltpu.VMEM((tb, D), BF16),
                        pltpu.VMEM((2, NA_HEADS // 2, WIN_ROWS, 2 * GRID_W, LANES), F32),
                        pltpu.VMEM((NA_HEADS // 2, 2 * GRID_W, WIN_ROWS * GRID_W), F32),
                        pltpu.VMEM((NA_HEADS // 2, 2 * GRID_W, WIN_ROWS * GRID_W), F32),
                        pltpu.VMEM((NA_HEADS // 2, 2 * GRID_W, WIN_ROWS * GRID_W), BF16),
                        pltpu.VMEM((NA_HEADS // 2, 2 * GRID_W, WIN_ROWS * GRID_W), BF16)],
        compiler_params=pltpu.CompilerParams(
            dimension_semantics=("arbitrary",), vmem_limit_bytes=VMEM_LIMIT),
        name="mixer",
    )(q, k, v, bg, u, u, u, x, w_out_b, rpb_pad, conv_w, g, b)


def _xattn_kernel(alpha, x_ref, mem_ref, wq_ref, wk_ref, wv_ref, wo_ref,
                  g_ref, be_ref, wrh_ref, wrl_ref,
                  x2_ref, x2b_ref, aff_ref, kmem, vmem):
    i = pl.program_id(0)

    @pl.when(i == 0)
    def _():
        mb = mem_ref[...].astype(BF16)
        kmem[...] = jnp.dot(mb, wk_ref[...], preferred_element_type=F32).astype(BF16)
        vmem[...] = jnp.dot(mb, wv_ref[...], preferred_element_type=F32).astype(BF16)

    x1 = x_ref[...]
    qb = jnp.dot(x1.astype(BF16), wq_ref[...], preferred_element_type=F32).astype(BF16)
    outs = []
    for h in range(MEM_HEADS):
        cs = slice(h * MEM_HEAD_DIM, (h + 1) * MEM_HEAD_DIM)
        s = _dot_nt(qb[:, cs], kmem[:, cs]) * (MEM_HEAD_DIM ** -0.5)
        mx = jnp.max(s, axis=-1, keepdims=True)
        p = jnp.exp(s - mx)
        l = jnp.sum(p, axis=-1, keepdims=True)
        oh = jnp.dot(p.astype(BF16), vmem[:, cs], preferred_element_type=F32)
        outs.append(oh / l)
    o = jnp.concatenate(outs, axis=1).astype(BF16)
    y = jnp.dot(o, wo_ref[...], preferred_element_type=F32)
    x2 = _layer_norm(alpha * x1 + y, g_ref[...], be_ref[...])
    x2_ref[...] = x2
    hi = x2.astype(BF16)
    x2b_ref[...] = hi
    lo = (x2 - hi.astype(F32)).astype(BF16)
    wrh = wrh_ref[...]
    r1 = _dot_nt(jnp.concatenate([wrh, wrl_ref[...]], axis=0), hi)
    lg = r1[0:N_EXPERTS] + r1[N_EXPERTS:2 * N_EXPERTS] + _dot_nt(wrh, lo)
    mx = jnp.max(lg, axis=0, keepdims=True)
    p = jnp.exp(lg - mx)
    aff = p / jnp.sum(p, axis=0, keepdims=True)
    for j in range(aff_ref.shape[0]):
        aff_ref[j] = aff[:, j * LANES:(j + 1) * LANES]


def _xattn(x1, mem, wq_b, wk_b, wv_b, wo_b, g, b, wr_hi, wr_lo, alpha):
    S, D = x1.shape
    M = mem.shape[0]
    mw = MEM_HEADS * MEM_HEAD_DIM
    tm = XATTN_TM
    cur = lambda i: (i, 0)
    const2 = lambda i: (0, 0)
    full = lambda a: pl.BlockSpec(a.shape, const2)
    return pl.pallas_call(
        functools.partial(_xattn_kernel, alpha),
        grid=(S // tm,),
        in_specs=[pl.BlockSpec((tm, D), cur), full(mem), full(wq_b), full(wk_b),
                  full(wv_b), full(wo_b), full(g), full(b), full(wr_hi), full(wr_lo)],
        out_specs=[pl.BlockSpec((tm, D), cur), pl.BlockSpec((tm, D), cur),
                   pl.BlockSpec((tm // LANES, N_EXPERTS, LANES), lambda i: (i, 0, 0))],
        out_shape=[jax.ShapeDtypeStruct((S, D), F32),
                   jax.ShapeDtypeStruct((S, D), BF16),
                   jax.ShapeDtypeStruct((S // LANES, N_EXPERTS, LANES), F32)],
        scratch_shapes=[pltpu.VMEM((M, mw), BF16), pltpu.VMEM((M, mw), BF16)],
        compiler_params=pltpu.CompilerParams(
            dimension_semantics=("arbitrary",), vmem_limit_bytes=VMEM_LIMIT),
        name="xattn",
    )(x1, mem, wq_b, wk_b, wv_b, wo_b, g, b, wr_hi, wr_lo)


def _route_kernel(cap, aff_ref, loc_ref, b_ref, c_ref, incl_scr, tot_scr, sel_scr):
    nblk = aff_ref.shape[0]
    per_tile = ROUTE_TILE // LANES
    ntile = nblk // per_tile

    def count_ge(t):
        ge = jnp.where(aff_ref[...] >= t[None], 1.0, 0.0)
        return jnp.sum(jnp.sum(ge, axis=0), axis=1, keepdims=True)

    def coarse(it, bits):
        cand = bits | jnp.left_shift(jnp.int32(1), 30 - it)
        ok = count_ge(pltpu.bitcast(cand, F32)) >= cap
        return jnp.where(ok, cand, bits)

    def fine(it, lohi):
        lo, hi = lohi
        mid = lo + (hi - lo) * 0.5
        ok = count_ge(mid) >= cap
        return jnp.where(ok, mid, lo), jnp.where(ok, hi, mid)

    bits = lax.fori_loop(0, 31, coarse, jnp.zeros((N_EXPERTS, 1), I32))
    thr, _ = lax.fori_loop(
        0, 30, fine, (pltpu.bitcast(bits, F32), pltpu.bitcast(bits + 1, F32)))
    gt_all = jnp.where(aff_ref[...] > thr[None], 1.0, 0.0)
    n_gt = jnp.sum(jnp.sum(gt_all, axis=0), axis=1, keepdims=True)
    need = cap - n_gt

    ii = lax.broadcasted_iota(I32, (LANES, LANES), 0)
    jj = lax.broadcasted_iota(I32, (LANES, LANES), 1)
    tri = jnp.where(ii <= jj, 1.0, 0.0).astype(BF16)
    ones = jnp.ones((LANES, LANES), BF16)
    flat = (nblk * N_EXPERTS, LANES)
    blocks = (nblk, N_EXPERTS, LANES)

    def block_scans(flags):
        f2 = flags.reshape(flat).astype(BF16)
        incl_scr[...] = jnp.dot(f2, tri, preferred_element_type=F32).reshape(blocks)
        tot_scr[...] = jnp.dot(f2, ones, preferred_element_type=F32).reshape(blocks)

    block_scans(jnp.where(aff_ref[...] == thr[None], 1.0, 0.0))

    def select_block(j, run_eq):
        blk = aff_ref[j]
        tie_rank = run_eq + incl_scr[j]
        sel_scr[j] = (jnp.where(blk > thr, 1.0, 0.0)
                      + jnp.where(blk == thr, 1.0, 0.0) * jnp.where(tie_rank <= need, 1.0, 0.0))
        return run_eq + tot_scr[j]

    zero = jnp.zeros((N_EXPERTS, LANES), F32)
    lax.fori_loop(0, nblk, select_block, zero)
    block_scans(sel_scr[...])

    def tile_body(t, run):
        tile_run = zero
        for j in range(per_tile):
            idx = t * per_tile + j
            sel = sel_scr[idx]
            loc = tile_run + incl_scr[idx] - sel
            loc_ref[idx] = jnp.where(sel > 0.5, loc, -1.0).astype(I32)
            tile_run = tile_run + tot_scr[idx]
        b_ref[t] = run[:, 0:1].astype(I32)
        c_ref[t] = tile_run[:, 0:1].astype(I32)
        return run + tile_run

    lax.fori_loop(0, ntile, tile_body, zero)


def _route(aff3, cap):
    nblk = aff3.shape[0]
    ntile = nblk * LANES // ROUTE_TILE
    full3 = lambda a: pl.BlockSpec(a, lambda: (0, 0, 0))
    return pl.pallas_call(
        functools.partial(_route_kernel, cap),
        in_specs=[full3(aff3.shape)],
        out_specs=[full3(aff3.shape), full3((ntile, N_EXPERTS, 1)),
                   full3((ntile, N_EXPERTS, 1))],
        out_shape=[jax.ShapeDtypeStruct(aff3.shape, I32),
                   jax.ShapeDtypeStruct((ntile, N_EXPERTS, 1), I32),
                   jax.ShapeDtypeStruct((ntile, N_EXPERTS, 1), I32)],
        scratch_shapes=[pltpu.VMEM(aff3.shape, F32)] * 3,
        compiler_params=pltpu.CompilerParams(vmem_limit_bytes=VMEM_LIMIT),
        name="route",
    )(aff3)


def _segment_copies(b_sm, c_sm, t, make_copy, action):
    off = 0
    for e in range(N_EXPERTS):
        c = c_sm[t * N_EXPERTS + e]
        b = b_sm[t * N_EXPERTS + e]

        @pl.when(c > 0)
        def _(e=e, b=b, c=c, off=off):
            action(make_copy(e, b, off, c))

        off = off + c
    return off


def _tile_rows(c_sm, t):
    rows = 0
    for e in range(N_EXPERTS):
        rows = rows + c_sm[t * N_EXPERTS + e]
    return rows


def _dispatch_kernel(cap, b_sm, c_sm, x_ref, loc_ref, xe_hbm, stg, sem):
    step = pl.program_id(0)
    nsteps = pl.num_programs(0)
    per_tile = ROUTE_TILE // LANES
    sub = lax.broadcasted_iota(I32, (SLOT_CHUNK, 1), 0)
    lane = lax.broadcasted_iota(I32, (1, LANES), 1)

    def copy_for(slot_):
        def make(e, dst_row, src_row, n):
            return pltpu.make_async_copy(
                stg.at[slot_, pl.ds(pl.multiple_of(src_row * SUBLANES, SUBLANES), n * SUBLANES)],
                xe_hbm.at[pl.ds(pl.multiple_of((e * cap + dst_row) * SUBLANES, SUBLANES),
                                n * SUBLANES)],
                sem.at[slot_])
        return make

    def wait_tile(t_):
        rows = _tile_rows(c_sm, t_)
        slot_ = t_ % DISPATCH_SLOTS

        @pl.when(rows > 0)
        def _():
            n = rows * SUBLANES
            pltpu.make_async_copy(stg.at[slot_, pl.ds(0, n)], xe_hbm.at[pl.ds(0, n)],
                                  sem.at[slot_]).wait()

    def one_tile(k):
        t = step * DISPATCH_TILES + k
        slot = t % DISPATCH_SLOTS
        tok = slice(k * ROUTE_TILE, (k + 1) * ROUTE_TILE)
        offs = []
        off = 0
        for e in range(N_EXPERTS):
            offs.append(off)
            off = off + c_sm[t * N_EXPERTS + e]
        offs.append(off)
        nchunk = (off + SLOT_CHUNK - 1) // SLOT_CHUNK
        seg_lo = jnp.zeros((1, LANES), I32)
        seg_hi = jnp.zeros((1, LANES), I32)
        for e in range(N_EXPERTS):
            seg_lo = jnp.where(lane == e, offs[e], seg_lo)
            seg_hi = jnp.where(lane == e, offs[e + 1], seg_hi)
        seg_lo_f = seg_lo.astype(F32)
        locf = jnp.concatenate(
            [loc_ref[k * per_tile + j] for j in range(per_tile)], axis=1).astype(F32)
        locb = jnp.concatenate(
            [locf, jnp.zeros((LANES - N_EXPERTS, ROUTE_TILE), F32)], axis=0).astype(BF16)

        def chunk_body(m, carry):
            base = m * SLOT_CHUNK
            srow = sub + base
            inseg = jnp.where(srow >= seg_lo, jnp.where(srow < seg_hi, 1.0, 0.0), 0.0)
            rank = srow.astype(F32) - jnp.sum(inseg * seg_lo_f, axis=1, keepdims=True)
            want = jnp.dot(inseg.astype(BF16), locb, preferred_element_type=F32)
            p = jnp.where(want == rank, 1.0, 0.0)
            res = jnp.dot(p.astype(BF16), x_ref[tok, :], preferred_element_type=F32)
            row0 = (base * SUBLANES if isinstance(m, int)
                    else pl.multiple_of(base * SUBLANES, SUBLANES))
            for s in range(SUBLANES):
                stg[slot, pl.ds(row0 + s, SLOT_CHUNK, stride=SUBLANES), :] = (
                    res[:, s * LANES:(s + 1) * LANES])
            return carry

        for m in range(PEELED_CHUNKS):
            chunk_body(m, 0)
        lax.fori_loop(PEELED_CHUNKS, jnp.maximum(nchunk, PEELED_CHUNKS), chunk_body, 0)

        _segment_copies(b_sm, c_sm, t, copy_for(slot), lambda cp: cp.start())

        @pl.when(t >= DISPATCH_SLOTS - 1)
        def _():
            wait_tile(t - (DISPATCH_SLOTS - 1))

    for k in range(DISPATCH_TILES):
        one_tile(k)

    @pl.when(step == nsteps - 1)
    def _():
        last = nsteps * DISPATCH_TILES - 1
        for d in range(DISPATCH_SLOTS - 2, -1, -1):
            wait_tile(last - d)


def _dispatch(b_flat, c_flat, x2b, loc3, cap):
    S, D = x2b.shape
    tok = DISPATCH_TILES * ROUTE_TILE
    assert S % tok == 0 and S // ROUTE_TILE >= DISPATCH_SLOTS
    max_rows = N_EXPERTS * ROUTE_TILE
    return pl.pallas_call(
        functools.partial(_dispatch_kernel, cap),
        grid_spec=pltpu.PrefetchScalarGridSpec(
            num_scalar_prefetch=2,
            grid=(S // tok,),
            in_specs=[pl.BlockSpec((tok, D), lambda t, b, c: (t, 0)),
                      pl.BlockSpec((tok // LANES, N_EXPERTS, LANES), lambda t, b, c: (t, 0, 0))],
            out_specs=pl.BlockSpec(memory_space=pl.ANY),
            scratch_shapes=[pltpu.VMEM((DISPATCH_SLOTS, max_rows * SUBLANES, LANES), F32),
                            pltpu.SemaphoreType.DMA((DISPATCH_SLOTS,))]),
        out_shape=jax.ShapeDtypeStruct((N_EXPERTS * cap * SUBLANES, LANES), F32),
        compiler_params=pltpu.CompilerParams(
            dimension_semantics=("arbitrary",), vmem_limit_bytes=VMEM_LIMIT),
        name="dispatch",
    )(b_flat, c_flat, x2b, loc3)


def _ffn_kernel(xe_ref, wg_ref, wu_ref, wd_ref, wr_ref, o_ref, xb, acc, gate):
    e = pl.program_id(0)
    f = pl.program_id(2)
    tm = xb.shape[0]

    @pl.when(f == 0)
    def _():
        for s in range(SUBLANES):
            xb[:, s * LANES:(s + 1) * LANES] = (
                xe_ref[pl.ds(s, tm, stride=SUBLANES), :].astype(BF16))
        lg = jnp.dot(xb[...], wr_ref[...], preferred_element_type=F32)
        lane = lax.broadcasted_iota(I32, (1, LANES), 1)
        lg = jnp.where(lane < N_EXPERTS, lg, NEG)
        p = jnp.exp(lg - jnp.max(lg, axis=-1, keepdims=True))
        mine = jnp.sum(jnp.where(lane == e, p, 0.0), axis=-1, keepdims=True)
        gate[...] = mine / jnp.sum(p, axis=-1, keepdims=True)

    def down_proj():
        x = xb[...]
        g = jnp.dot(x, wg_ref[...].astype(BF16), preferred_element_type=F32)
        u = jnp.dot(x, wu_ref[...].astype(BF16), preferred_element_type=F32)
        h = (g * jax.nn.sigmoid(g) * u).astype(BF16)
        return jnp.dot(h, wd_ref[...].astype(BF16), preferred_element_type=F32)

    last = pl.num_programs(2) - 1

    @pl.when(f == 0)
    def _():
        acc[...] = down_proj()

    @pl.when((f > 0) & (f < last))
    def _():
        acc[...] += down_proj()

    @pl.when(f == last)
    def _():
        y = (acc[...] + down_proj()) * gate[...]
        for s in range(SUBLANES):
            o_ref[pl.ds(s, tm, stride=SUBLANES), :] = y[:, s * LANES:(s + 1) * LANES]


def _ffn(xe2d, w_gate, w_up, w_down, wr_pad, cap):
    E, D, FF = w_gate.shape
    tm, tf = FFN_TM, FFN_TF
    mt = cap // tm
    rows = lambda e, m, f: (e * mt + m, 0)
    return pl.pallas_call(
        _ffn_kernel,
        grid=(E, mt, FF // tf),
        in_specs=[pl.BlockSpec((tm * SUBLANES, LANES), rows),
                  pl.BlockSpec((None, D, tf), lambda e, m, f: (e, 0, f)),
                  pl.BlockSpec((None, D, tf), lambda e, m, f: (e, 0, f)),
                  pl.BlockSpec((None, tf, D), lambda e, m, f: (e, f, 0)),
                  pl.BlockSpec(wr_pad.shape, lambda e, m, f: (0, 0))],
        out_specs=pl.BlockSpec((tm * SUBLANES, LANES), rows),
        out_shape=jax.ShapeDtypeStruct(xe2d.shape, F32),
        scratch_shapes=[pltpu.VMEM((tm, D), BF16), pltpu.VMEM((tm, D), F32),
                        pltpu.VMEM((tm, 1), F32)],
        compiler_params=pltpu.CompilerParams(
            dimension_semantics=("arbitrary", "arbitrary", "arbitrary"),
            vmem_limit_bytes=VMEM_LIMIT),
        name="ffn",
    )(xe2d, w_gate, w_up, w_down, wr_pad)


def _combine_kernel(cap, alpha, b_sm, c_sm, x_ref, loc_ref, g_ref, be_ref, ye_hbm,
                    o_ref, ybuf, sem, hl_scr, z_scr):
    step = pl.program_id(0)
    nt = pl.num_programs(0) * COMBINE_TILES
    lane = lax.broadcasted_iota(I32, (1, SLOT_CHUNK), 1)
    sub = lax.broadcasted_iota(I32, (SLOT_CHUNK, 1), 0)
    eid = lax.broadcasted_iota(I32, (N_EXPERTS, 1), 0)

    def copy_for(slot_):
        def make(e, src_row, dst_row, n):
            return pltpu.make_async_copy(
                ye_hbm.at[pl.ds(pl.multiple_of((e * cap + src_row) * SUBLANES, SUBLANES),
                                n * SUBLANES)],
                ybuf.at[slot_, pl.ds(pl.multiple_of(dst_row * SUBLANES, SUBLANES), n * SUBLANES)],
                sem.at[slot_])
        return make

    def fetch(t_):
        _segment_copies(b_sm, c_sm, t_, copy_for(t_ % 2), lambda cp: cp.start())

    @pl.when(step == 0)
    def _():
        ybuf[...] = jnp.zeros_like(ybuf)
        fetch(0)
        fetch(1)

    def one_tile(k):
        t = step * COMBINE_TILES + k
        slot = t % 2
        tok = slice(k * ROUTE_TILE, (k + 1) * ROUTE_TILE)

        rows_in = _tile_rows(c_sm, t)

        @pl.when(rows_in > 0)
        def _():
            n = rows_in * SUBLANES
            pltpu.make_async_copy(ye_hbm.at[pl.ds(0, n)], ybuf.at[slot, pl.ds(0, n)],
                                  sem.at[slot]).wait()

        offs = []
        off = 0
        for e in range(N_EXPERTS):
            offs.append(off)
            off = off + c_sm[t * N_EXPERTS + e]
        offs.append(off)
        total = off
        nchunk = (total + SLOT_CHUNK - 1) // SLOT_CHUNK
        seg_lo = jnp.zeros((N_EXPERTS, 1), I32)
        seg_hi = jnp.zeros((N_EXPERTS, 1), I32)
        for e in range(N_EXPERTS):
            seg_lo = jnp.where(eid == e, offs[e], seg_lo)
            seg_hi = jnp.where(eid == e, offs[e + 1], seg_hi)
        seg_lo_f = seg_lo.astype(F32)
        locb = loc_ref[tok, :].astype(F32).astype(BF16)

        z_scr[...] = alpha * x_ref[tok, :]

        def chunk_body(m, carry):
            base = m * SLOT_CHUNK
            srow = lane + base
            inseg = jnp.where(srow >= seg_lo, jnp.where(srow < seg_hi, 1.0, 0.0), 0.0)
            rank = srow.astype(F32) - jnp.sum(inseg * seg_lo_f, axis=0, keepdims=True)
            want = jnp.dot(locb, inseg.astype(BF16), preferred_element_type=F32)
            ptb = jnp.where(want == rank, 1.0, 0.0).astype(BF16)
            row0 = (base * SUBLANES if isinstance(m, int)
                    else pl.multiple_of(base * SUBLANES, SUBLANES))
            yc = jnp.concatenate(
                [ybuf[slot, pl.ds(row0 + s, SLOT_CHUNK, stride=SUBLANES), :]
                 for s in range(SUBLANES)], axis=1)
            yc = jnp.where(sub < total - base, yc, 0.0)
            hi = yc.astype(BF16)
            hl = hl_scr.at[m % PEELED_CHUNKS]
            hl[0:SLOT_CHUNK] = hi
            hl[SLOT_CHUNK:2 * SLOT_CHUNK] = (yc - hi.astype(F32)).astype(BF16)
            z_scr[...] += jnp.dot(jnp.concatenate([ptb, ptb], axis=1), hl[...],
                                  preferred_element_type=F32)
            return carry

        for m in range(PEELED_CHUNKS):
            chunk_body(m, 0)
        lax.fori_loop(PEELED_CHUNKS, jnp.maximum(nchunk, PEELED_CHUNKS), chunk_body, 0)

        @pl.when(t + 2 < nt)
        def _():
            fetch(t + 2)

        o_ref[tok, :] = _layer_norm(z_scr[...], g_ref[...], be_ref[...])

    for k in range(COMBINE_TILES):
        one_tile(k)


def _combine(b_flat, c_flat, x2, loc_t, g, b, ye2d, cap, alpha):
    S, D = x2.shape
    tok = COMBINE_TILES * ROUTE_TILE
    assert S % tok == 0 and S // ROUTE_TILE >= 2
    max_rows = N_EXPERTS * ROUTE_TILE
    cur = lambda t, b_, c_: (t, 0)
    const2 = lambda t, b_, c_: (0, 0)
    return pl.pallas_call(
        functools.partial(_combine_kernel, cap, alpha),
        grid_spec=pltpu.PrefetchScalarGridSpec(
            num_scalar_prefetch=2,
            grid=(S // tok,),
            in_specs=[pl.BlockSpec((tok, D), cur),
                      pl.BlockSpec((tok, N_EXPERTS), cur),
                      pl.BlockSpec((1, D), const2),
                      pl.BlockSpec((1, D), const2),
                      pl.BlockSpec(memory_space=pl.ANY)],
            out_specs=pl.BlockSpec((tok, D), cur),
            scratch_shapes=[pltpu.VMEM((2, max_rows * SUBLANES, LANES), F32),
                            pltpu.SemaphoreType.DMA((2,)),
                            pltpu.VMEM((PEELED_CHUNKS, 2 * SLOT_CHUNK, D), BF16),
                            pltpu.VMEM((ROUTE_TILE, D), F32)]),
        out_shape=jax.ShapeDtypeStruct((S, D), F32),
        compiler_params=pltpu.CompilerParams(
            dimension_semantics=("arbitrary",), vmem_limit_bytes=VMEM_LIMIT),
        name="combine",
    )(b_flat, c_flat, x2, loc_t, g, b, ye2d)


def _layer(x, mem, w_in, rpb, conv_w, w_mix_out, ln1_g, ln1_b, wq, wk, wv, wo,
           ln2_g, ln2_b, w_router, w_gate, w_up, w_down, ln3_g, ln3_b, alpha):
    S, D = x.shape
    cap = CAPACITY_FACTOR * S // N_EXPERTS
    assert S % TOK_BLOCK == 0 and S % PROJ_TM == 0 and S % XATTN_TM == 0
    assert S // GRID_W >= 3 * ROW_BLOCK
    assert cap % FFN_TM == 0 and cap <= S
    row = lambda a: a.reshape(1, -1)

    q, k, v, bg, u = _proj(x, w_in.astype(BF16))
    x1 = _mixer(x, q, k, v, bg, u, w_mix_out.astype(BF16), rpb,
                conv_w, row(ln1_g), row(ln1_b), alpha)

    wr_t = w_router.T
    wr_hi = wr_t.astype(BF16)
    wr_lo = (wr_t - wr_hi.astype(F32)).astype(BF16)
    x2, x2b, aff3 = _xattn(x1, mem, wq.astype(BF16), wk.astype(BF16), wv.astype(BF16),
                           wo.astype(BF16), row(ln2_g), row(ln2_b), wr_hi, wr_lo, alpha)

    loc3, b3, c3 = _route(aff3, cap)
    b_flat = b3.reshape(-1)
    c_flat = c3.reshape(-1)

    xe2d = _dispatch(b_flat, c_flat, x2b, loc3, cap)
    wr_pad = jnp.pad(w_router.astype(BF16), ((0, 0), (0, LANES - N_EXPERTS)))
    ye2d = _ffn(xe2d, w_gate, w_up, w_down, wr_pad, cap)
    loc_t = loc3.transpose(0, 2, 1).reshape(S, N_EXPERTS)
    return _combine(b_flat, c_flat, x2, loc_t, row(ln3_g), row(ln3_b), ye2d, cap, alpha)


def kernel(x, mem, w_in, na_rpb, conv_w, w_mix_out, ln1_g, ln1_b, w_mem_q, w_mem_k,
           w_mem_v, w_mem_out, ln2_g, ln2_b, w_router, w_exp_gate, w_exp_up,
           w_exp_down, ln3_g, ln3_b):
    depth = w_in.shape[0]
    alpha = (2 * depth) ** 0.25
    outs = []
    for bi in range(x.shape[0]):
        xb = x[bi]
        for l in range(depth):
            xb = _layer(xb, mem[bi], w_in[l], na_rpb[l], conv_w[l], w_mix_out[l],
                        ln1_g[l], ln1_b[l], w_mem_q[l], w_mem_k[l], w_mem_v[l],
                        w_mem_out[l], ln2_g[l], ln2_b[l], w_router[l], w_exp_gate[l],
                        w_exp_up[l], w_exp_down[l], ln3_g[l], ln3_b[l], alpha)
        outs.append(xb)
    return jnp.stack(outs)
```

```python
import functools

import jax
import jax.numpy as jnp
from jax import lax
from jax.experimental import pallas as pl
from jax.experimental.pallas import tpu as pltpu

F32 = jnp.float32
BF16 = jnp.bfloat16
I32 = jnp.int32

GRID_W = 64
HEAD_DIM = 64
NA_HEADS = 8
NA_WIDTH = NA_HEADS * HEAD_DIM
WIN_ROWS = 8
WIN_COLS = 16
MEM_HEADS = 4
MEM_HEAD_DIM = 128
N_EXPERTS = 16
CAPACITY_FACTOR = 2
LN_EPS = 1e-5
NEG = -1e30

LANES = 128
SUBLANES = 8
VMEM_LIMIT = 56 * 1024 * 1024

ROW_BLOCK = 8
TOK_BLOCK = ROW_BLOCK * GRID_W
PROJ_TM = 1024
XATTN_TM = 1024
ROUTE_TILE = 256
SLOT_CHUNK = 256
SM_ROWS = 16
PEELED_CHUNKS = 2
DISPATCH_TILES = 4
COMBINE_TILES = 2
DISPATCH_SLOTS = 3
FFN_TM = 1024
FFN_TF = 512


def _layer_norm(z, g, b):
    mu = jnp.mean(z, axis=-1, keepdims=True)
    zc = z - mu
    var = jnp.mean(zc * zc, axis=-1, keepdims=True)
    return zc * lax.rsqrt(var + LN_EPS) * g + b


def _dot_nt(a, b):
    return lax.dot_general(a, b, (((1,), (1,)), ((), ())),
                           preferred_element_type=F32)


def _proj_kernel(x_ref, w_ref, q_ref, k_ref, v_ref, b_ref, u_ref):
    xb = x_ref[...].astype(BF16)
    nw = NA_WIDTH

    def mm(c0):
        return jnp.dot(xb, w_ref[:, c0:c0 + nw], preferred_element_type=F32)

    q_ref[...] = (mm(0) * (HEAD_DIM ** -0.5)).astype(BF16)
    k_ref[...] = mm(nw).astype(BF16)
    v_ref[...] = mm(2 * nw).astype(BF16)
    b_ref[...] = mm(3 * nw)
    u_ref[...] = mm(4 * nw) * mm(5 * nw)


def _proj(x, w_in_b):
    S, D = x.shape
    nw = NA_WIDTH
    tm = PROJ_TM
    blk = lambda i: (i, 0)
    return pl.pallas_call(
        _proj_kernel,
        grid=(S // tm,),
        in_specs=[pl.BlockSpec((tm, D), blk),
                  pl.BlockSpec((D, 6 * nw), lambda i: (0, 0))],
        out_specs=[pl.BlockSpec((tm, nw), blk)] * 5,
        out_shape=[jax.ShapeDtypeStruct((S, nw), BF16)] * 3
        + [jax.ShapeDtypeStruct((S, nw), F32)] * 2,
        compiler_params=pltpu.CompilerParams(
            dimension_semantics=("arbitrary",), vmem_limit_bytes=VMEM_LIMIT),
        name="proj",
    )(x, w_in_b)


def _build_na_bias(rpb_ref, bias_ref):
    shape = (GRID_W, LANES)
    qc = lax.broadcasted_iota(I32, shape, 0)
    ln = lax.broadcasted_iota(I32, shape, 1)
    kc = ln % GRID_W
    c_start = jnp.clip(qc - WIN_COLS // 2, 0, GRID_W - WIN_COLS)
    base = LANES - (WIN_COLS - 1)

    def canvas(h, d):
        both = rpb_ref[h, d:d + 1, :] + pltpu.roll(rpb_ref[h, d + 1:d + 2, :], GRID_W, 1)
        t = pltpu.roll(pltpu.roll(jnp.broadcast_to(both, shape), base, 1), 0, 1,
                       stride=1, stride_axis=0)
        return jnp.where(kc >= c_start, jnp.where(kc < c_start + WIN_COLS, t, NEG), NEG)

    for par in range(2):
        for hp in range(NA_HEADS // 2):
            for m in range(WIN_ROWS):
                d = 2 * m + par
                bias_ref[par, hp, m, 0:GRID_W] = canvas(2 * hp, d)
                bias_ref[par, hp, m, GRID_W:2 * GRID_W] = canvas(2 * hp + 1, d)


def _kv_window_start(i, rows):
    return jnp.clip((i - 1) * ROW_BLOCK, 0, rows - 3 * ROW_BLOCK)


def _mixer_kernel(rows, alpha,
                  q_ref, kbuf, vbuf,
                  b_ref, u_ref, up_ref, un_ref, x_ref, wout_ref, rpb_ref,
                  cw_ref, g_ref, be_ref, o_ref, mix, bias_ref,
                  s_scr0, s_scr1, p_scr0, p_scr1):
    i = pl.program_id(0)
    nb = pl.num_programs(0)
    tb = TOK_BLOCK
    win = WIN_ROWS * GRID_W

    @pl.when(i == 0)
    def _():
        _build_na_bias(rpb_ref, bias_ref)

    lane = lax.broadcasted_iota(I32, (1, LANES), 1)
    first_head = lane < HEAD_DIM
    m_lo = jnp.where(first_head, 1.0, 0.0).astype(BF16)
    m_hi = jnp.where(first_head, 0.0, 1.0).astype(BF16)

    def window(r):
        grow = i * ROW_BLOCK + r
        start = jnp.clip(grow - WIN_ROWS // 2, 0, rows - WIN_ROWS)
        w0 = pl.multiple_of((start - _kv_window_start(i, rows)) * GRID_W, GRID_W)
        d0 = start - grow + WIN_ROWS
        return w0, d0 % 2, d0 // 2

    def scores(r, s_out):
        w0, _, _ = window(r)
        q0 = r * GRID_W if isinstance(r, int) else pl.multiple_of(r * GRID_W, GRID_W)
        for hp in range(NA_HEADS // 2):
            cs = slice(hp * LANES, (hp + 1) * LANES)
            qp = q_ref[pl.ds(q0, GRID_W), cs]
            lhs = jnp.concatenate([qp * m_lo, qp * m_hi], axis=0)
            kw = kbuf[pl.ds(w0, win), cs]
            s_out[hp] = _dot_nt(lhs, kw)

    def probs(r, s_in, p_out):
        _, par, m0 = window(r)
        for hp in range(NA_HEADS // 2):
            for rb in range(2 * GRID_W // SM_ROWS):
                rs = slice(rb * SM_ROWS, (rb + 1) * SM_ROWS)
                bias = jnp.concatenate(
                    [bias_ref[par, hp, m0 + j, rs, :] for j in range(WIN_ROWS // 2)], axis=1)
                s = s_in[hp, rs, :] + bias
                p = jnp.exp(s - jnp.max(s, axis=-1, keepdims=True))
                p = p * (1.0 / jnp.sum(p, axis=-1, keepdims=True))
                p_out[hp, rs, :] = p.astype(BF16)

    def values(r, p_in):
        w0, _, _ = window(r)
        q0 = r * GRID_W if isinstance(r, int) else pl.multiple_of(r * GRID_W, GRID_W)
        for hp in range(NA_HEADS // 2):
            cs = slice(hp * LANES, (hp + 1) * LANES)
            vw = vbuf[pl.ds(w0, win), cs]
            o2 = jnp.dot(p_in[hp], vw, preferred_element_type=F32)
            y = jnp.where(first_head, o2[0:GRID_W], o2[GRID_W:2 * GRID_W])
            mix[pl.ds(q0, GRID_W), cs] = y.astype(BF16)

    s_buf = (s_scr0, s_scr1)
    p_buf = (p_scr0, p_scr1)

    def steady(k, carry):
        for half in range(2):
            r = 1 + 2 * k + half
            values(r - 1, p_buf[half])
            probs(r, s_buf[1 - half], p_buf[1 - half])
            scores(r + 1, s_buf[half])
        return carry

    scores(0, s_buf[0])
    probs(0, s_buf[0], p_buf[0])
    scores(1, s_buf[1])
    lax.fori_loop(0, (ROW_BLOCK - 2) // 2, steady, 0)
    values(ROW_BLOCK - 2, p_buf[0])
    probs(ROW_BLOCK - 1, s_buf[1], p_buf[1])
    values(ROW_BLOCK - 1, p_buf[1])

    u = u_ref[...]
    prev_row = up_ref[SUBLANES - 1:SUBLANES, :] * jnp.where(i > 0, 1.0, 0.0)
    next_row = un_ref[0:1, :] * jnp.where(i < nb - 1, 1.0, 0.0)
    rid = lax.broadcasted_iota(I32, (tb, 1), 0)
    um1 = jnp.where(rid == 0, prev_row, pltpu.roll(u, 1, axis=0))
    up1 = jnp.where(rid == tb - 1, next_row, pltpu.roll(u, tb - 1, axis=0))
    cw = cw_ref[...]
    yc = b_ref[...] * (cw[0:1] * um1 + cw[1:2] * u + cw[2:3] * up1)
    mix[:, NA_WIDTH:] = yc.astype(BF16)

    y = jnp.dot(mix[...], wout_ref[...], preferred_element_type=F32)
    z = alpha * x_ref[...] + y
    o_ref[...] = _layer_norm(z, g_ref[...], be_ref[...])


def _mixer(x, q, k, v, bg, u, w_out_b, rpb, conv_w, g, b, alpha):
    S, D = x.shape
    rpb_pad = jnp.pad(rpb.astype(F32), ((0, 0), (1, 1), (0, LANES - rpb.shape[2])))
    nw = NA_WIDTH
    tb = TOK_BLOCK
    nb = S // tb
    rows = S // GRID_W
    cur = lambda i: (i, 0)
    kv_win = pl.BlockSpec((pl.Element(3 * tb), pl.Element(nw)),
                          lambda i: (_kv_window_start(i, rows) * GRID_W, 0))
    halo = tb // SUBLANES
    hprev = lambda i: (jnp.maximum(i * halo - 1, 0), 0)
    hnext = lambda i: (jnp.minimum((i + 1) * halo, S // SUBLANES - 1), 0)
    const2 = lambda i: (0, 0)
    kv = lambda im: pl.BlockSpec((tb, nw), im)
    return pl.pallas_call(
        functools.partial(_mixer_kernel, rows, alpha),
        grid=(nb,),
        in_specs=[kv(cur), kv_win, kv_win,
                  kv(cur), kv(cur),
                  pl.BlockSpec((SUBLANES, nw), hprev),
                  pl.BlockSpec((SUBLANES, nw), hnext),
                  pl.BlockSpec((tb, D), cur),
                  pl.BlockSpec((D, D), const2),
                  pl.BlockSpec(rpb_pad.shape, lambda i: (0, 0, 0)),
                  pl.BlockSpec(conv_w.shape, const2),
                  pl.BlockSpec((1, D), const2),
                  pl.BlockSpec((1, D), const2)],
        out_specs=pl.BlockSpec((tb, D), cur),
        out_shape=jax.ShapeDtypeStruct((S, D), F32),
        scratch_shapes=[pltpu.VMEM((tb, D), BF16),
                        pltpu.VMEM((2, NA_HEADS // 2, WIN_ROWS, 2 * GRID_W, LANES), F32),
                        pltpu.VMEM((NA_HEADS // 2, 2 * GRID_W, WIN_ROWS * GRID_W), F32),
                        pltpu.VMEM((NA_HEADS // 2, 2 * GRID_W, WIN_ROWS * GRID_W), F32),
                        pltpu.VMEM((NA_HEADS // 2, 2 * GRID_W, WIN_ROWS * GRID_W), BF16),
                        pltpu.VMEM((NA_HEADS // 2, 2 * GRID_W, WIN_ROWS * GRID_W), BF16)],
        compiler_params=pltpu.CompilerParams(
            dimension_semantics=("arbitrary",), vmem_limit_bytes=VMEM_LIMIT),
        name="mixer",
    )(q, k, v, bg, u, u, u, x, w_out_b, rpb_pad, conv_w, g, b)


def _xattn_kernel(alpha, x_ref, mem_ref, wq_ref, wk_ref, wv_ref, wo_ref,
                  g_ref, be_ref, wrh_ref, wrl_ref,
                  x2_ref, x2b_ref, aff_ref, kmem, vmem):
    i = pl.program_id(0)

    @pl.when(i == 0)
    def _():
        mb = mem_ref[...].astype(BF16)
        kmem[...] = jnp.dot(mb, wk_ref[...], preferred_element_type=F32).astype(BF16)
        vmem[...] = jnp.dot(mb, wv_ref[...], preferred_element_type=F32).astype(BF16)

    x1 = x_ref[...]
    qb = jnp.dot(x1.astype(BF16), wq_ref[...], preferred_element_type=F32).astype(BF16)
    outs = []
    for h in range(MEM_HEADS):
        cs = slice(h * MEM_HEAD_DIM, (h + 1) * MEM_HEAD_DIM)
        s = _dot_nt(qb[:, cs], kmem[:, cs]) * (MEM_HEAD_DIM ** -0.5)
        mx = jnp.max(s, axis=-1, keepdims=True)
        p = jnp.exp(s - mx)
        l = jnp.sum(p, axis=-1, keepdims=True)
        oh = jnp.dot(p.astype(BF16), vmem[:, cs], preferred_element_type=F32)
        outs.append(oh / l)
    o = jnp.concatenate(outs, axis=1).astype(BF16)
    y = jnp.dot(o, wo_ref[...], preferred_element_type=F32)
    x2 = _layer_norm(alpha * x1 + y, g_ref[...], be_ref[...])
    x2_ref[...] = x2
    hi = x2.astype(BF16)
    x2b_ref[...] = hi
    lo = (x2 - hi.astype(F32)).astype(BF16)
    wrh = wrh_ref[...]
    r1 = _dot_nt(jnp.concatenate([wrh, wrl_ref[...]], axis=0), hi)
    lg = r1[0:N_EXPERTS] + r1[N_EXPERTS:2 * N_EXPERTS] + _dot_nt(wrh, lo)
    mx = jnp.max(lg, axis=0, keepdims=True)
    p = jnp.exp(lg - mx)
    aff = p / jnp.sum(p, axis=0, keepdims=True)
    for j in range(aff_ref.shape[0]):
        aff_ref[j] = aff[:, j * LANES:(j + 1) * LANES]


def _xattn(x1, mem, wq_b, wk_b, wv_b, wo_b, g, b, wr_hi, wr_lo, alpha):
    S, D = x1.shape
    M = mem.shape[0]
    mw = MEM_HEADS * MEM_HEAD_DIM
    tm = XATTN_TM
    cur = lambda i: (i, 0)
    const2 = lambda i: (0, 0)
    full = lambda a: pl.BlockSpec(a.shape, const2)
    return pl.pallas_call(
        functools.partial(_xattn_kernel, alpha),
        grid=(S // tm,),
        in_specs=[pl.BlockSpec((tm, D), cur), full(mem), full(wq_b), full(wk_b),
                  full(wv_b), full(wo_b), full(g), full(b), full(wr_hi), full(wr_lo)],
        out_specs=[pl.BlockSpec((tm, D), cur), pl.BlockSpec((tm, D), cur),
                   pl.BlockSpec((tm // LANES, N_EXPERTS, LANES), lambda i: (i, 0, 0))],
        out_shape=[jax.ShapeDtypeStruct((S, D), F32),
                   jax.ShapeDtypeStruct((S, D), BF16),
                   jax.ShapeDtypeStruct((S // LANES, N_EXPERTS, LANES), F32)],
        scratch_shapes=[pltpu.VMEM((M, mw), BF16), pltpu.VMEM((M, mw), BF16)],
        compiler_params=pltpu.CompilerParams(
            dimension_semantics=("arbitrary",), vmem_limit_bytes=VMEM_LIMIT),
        name="xattn",
    )(x1, mem, wq_b, wk_b, wv_b, wo_b, g, b, wr_hi, wr_lo)


def _route_kernel(cap, aff_ref, loc_ref, b_ref, c_ref, incl_scr, tot_scr, sel_scr):
    nblk = aff_ref.shape[0]
    per_tile = ROUTE_TILE // LANES
    ntile = nblk // per_tile

    def count_ge(t):
        ge = jnp.where(aff_ref[...] >= t[None], 1.0, 0.0)
        return jnp.sum(jnp.sum(ge, axis=0), axis=1, keepdims=True)

    def coarse(it, bits):
        cand = bits | jnp.left_shift(jnp.int32(1), 30 - it)
        ok = count_ge(pltpu.bitcast(cand, F32)) >= cap
        return jnp.where(ok, cand, bits)

    def fine(it, lohi):
        lo, hi = lohi
        mid = lo + (hi - lo) * 0.5
        ok = count_ge(mid) >= cap
        return jnp.where(ok, mid, lo), jnp.where(ok, hi, mid)

    bits = lax.fori_loop(0, 31, coarse, jnp.zeros((N_EXPERTS, 1), I32))
    thr, _ = lax.fori_loop(
        0, 30, fine, (pltpu.bitcast(bits, F32), pltpu.bitcast(bits + 1, F32)))
    gt_all = jnp.where(aff_ref[...] > thr[None], 1.0, 0.0)
    n_gt = jnp.sum(jnp.sum(gt_all, axis=0), axis=1, keepdims=True)
    need = cap - n_gt

    ii = lax.broadcasted_iota(I32, (LANES, LANES), 0)
    jj = lax.broadcasted_iota(I32, (LANES, LANES), 1)
    tri = jnp.where(ii <= jj, 1.0, 0.0).astype(BF16)
    ones = jnp.ones((LANES, LANES), BF16)
    flat = (nblk * N_EXPERTS, LANES)
    blocks = (nblk, N_EXPERTS, LANES)

    def block_scans(flags):
        f2 = flags.reshape(flat).astype(BF16)
        incl_scr[...] = jnp.dot(f2, tri, preferred_element_type=F32).reshape(blocks)
        tot_scr[...] = jnp.dot(f2, ones, preferred_element_type=F32).reshape(blocks)

    block_scans(jnp.where(aff_ref[...] == thr[None], 1.0, 0.0))

    def select_block(j, run_eq):
        blk = aff_ref[j]
        tie_rank = run_eq + incl_scr[j]
        sel_scr[j] = (jnp.where(blk > thr, 1.0, 0.0)
                      + jnp.where(blk == thr, 1.0, 0.0) * jnp.where(tie_rank <= need, 1.0, 0.0))
        return run_eq + tot_scr[j]

    zero = jnp.zeros((N_EXPERTS, LANES), F32)
    lax.fori_loop(0, nblk, select_block, zero)
    block_scans(sel_scr[...])

    def tile_body(t, run):
        tile_run = zero
        for j in range(per_tile):
            idx = t * per_tile + j
            sel = sel_scr[idx]
            loc = tile_run + incl_scr[idx] - sel
            loc_ref[idx] = jnp.where(sel > 0.5, loc, -1.0).astype(I32)
            tile_run = tile_run + tot_scr[idx]
        b_ref[t] = run[:, 0:1].astype(I32)
        c_ref[t] = tile_run[:, 0:1].astype(I32)
        return run + tile_run

    lax.fori_loop(0, ntile, tile_body, zero)


def _route(aff3, cap):
    nblk = aff3.shape[0]
    ntile = nblk * LANES // ROUTE_TILE
    full3 = lambda a: pl.BlockSpec(a, lambda: (0, 0, 0))
    return pl.pallas_call(
        functools.partial(_route_kernel, cap),
        in_specs=[full3(aff3.shape)],
        out_specs=[full3(aff3.shape), full3((ntile, N_EXPERTS, 1)),
                   full3((ntile, N_EXPERTS, 1))],
        out_shape=[jax.ShapeDtypeStruct(aff3.shape, I32),
                   jax.ShapeDtypeStruct((ntile, N_EXPERTS, 1), I32),
                   jax.ShapeDtypeStruct((ntile, N_EXPERTS, 1), I32)],
        scratch_shapes=[pltpu.VMEM(aff3.shape, F32)] * 3,
        compiler_params=pltpu.CompilerParams(vmem_limit_bytes=VMEM_LIMIT),
        name="route",
    )(aff3)


def _segment_copies(b_sm, c_sm, t, make_copy, action):
    off = 0
    for e in range(N_EXPERTS):
        c = c_sm[t * N_EXPERTS + e]
        b = b_sm[t * N_EXPERTS + e]

        @pl.when(c > 0)
        def _(e=e, b=b, c=c, off=off):
            action(make_copy(e, b, off, c))

        off = off + c
    return off


def _tile_rows(c_sm, t):
    rows = 0
    for e in range(N_EXPERTS):
        rows = rows + c_sm[t * N_EXPERTS + e]
    return rows


def _dispatch_kernel(cap, b_sm, c_sm, x_ref, loc_ref, xe_hbm, stg, sem):
    step = pl.program_id(0)
    nsteps = pl.num_programs(0)
    per_tile = ROUTE_TILE // LANES
    sub = lax.broadcasted_iota(I32, (SLOT_CHUNK, 1), 0)
    lane = lax.broadcasted_iota(I32, (1, LANES), 1)

    def copy_for(slot_):
        def make(e, dst_row, src_row, n):
            return pltpu.make_async_copy(
                stg.at[slot_, pl.ds(pl.multiple_of(src_row * SUBLANES, SUBLANES), n * SUBLANES)],
                xe_hbm.at[pl.ds(pl.multiple_of((e * cap + dst_row) * SUBLANES, SUBLANES),
                                n * SUBLANES)],
                sem.at[slot_])
        return make

    def wait_tile(t_):
        rows = _tile_rows(c_sm, t_)
        slot_ = t_ % DISPATCH_SLOTS

        @pl.when(rows > 0)
        def _():
            n = rows * SUBLANES
            pltpu.make_async_copy(stg.at[slot_, pl.ds(0, n)], xe_hbm.at[pl.ds(0, n)],
                                  sem.at[slot_]).wait()

    def one_tile(k):
        t = step * DISPATCH_TILES + k
        slot = t % DISPATCH_SLOTS
        tok = slice(k * ROUTE_TILE, (k + 1) * ROUTE_TILE)
        offs = []
        off = 0
        for e in range(N_EXPERTS):
            offs.append(off)
            off = off + c_sm[t * N_EXPERTS + e]
        offs.append(off)
        nchunk = (off + SLOT_CHUNK - 1) // SLOT_CHUNK
        seg_lo = jnp.zeros((1, LANES), I32)
        seg_hi = jnp.zeros((1, LANES), I32)
        for e in range(N_EXPERTS):
            seg_lo = jnp.where(lane == e, offs[e], seg_lo)
            seg_hi = jnp.where(lane == e, offs[e + 1], seg_hi)
        seg_lo_f = seg_lo.astype(F32)
        locf = jnp.concatenate(
            [loc_ref[k * per_tile + j] for j in range(per_tile)], axis=1).astype(F32)
        locb = jnp.concatenate(
            [locf, jnp.zeros((LANES - N_EXPERTS, ROUTE_TILE), F32)], axis=0).astype(BF16)

        def chunk_body(m, carry):
            base = m * SLOT_CHUNK
            srow = sub + base
            inseg = jnp.where(srow >= seg_lo, jnp.where(srow < seg_hi, 1.0, 0.0), 0.0)
            rank = srow.astype(F32) - jnp.sum(inseg * seg_lo_f, axis=1, keepdims=True)
            want = jnp.dot(inseg.astype(BF16), locb, preferred_element_type=F32)
            p = jnp.where(want == rank, 1.0, 0.0)
            res = jnp.dot(p.astype(BF16), x_ref[tok, :], preferred_element_type=F32)
            row0 = (base * SUBLANES if isinstance(m, int)
                    else pl.multiple_of(base * SUBLANES, SUBLANES))
            for s in range(SUBLANES):
                stg[slot, pl.ds(row0 + s, SLOT_CHUNK, stride=SUBLANES), :] = (
                    res[:, s * LANES:(s + 1) * LANES])
            return carry

        for m in range(PEELED_CHUNKS):
            chunk_body(m, 0)
        lax.fori_loop(PEELED_CHUNKS, jnp.maximum(nchunk, PEELED_CHUNKS), chunk_body, 0)

        _segment_copies(b_sm, c_sm, t, copy_for(slot), lambda cp: cp.start())

        @pl.when(t >= DISPATCH_SLOTS - 1)
        def _():
            wait_tile(t - (DISPATCH_SLOTS - 1))

    for k in range(DISPATCH_TILES):
        one_tile(k)

    @pl.when(step == nsteps - 1)
    def _():
        last = nsteps * DISPATCH_TILES - 1
        for d in range(DISPATCH_SLOTS - 2, -1, -1):
            wait_tile(last - d)


def _dispatch(b_flat, c_flat, x2b, loc3, cap):
    S, D = x2b.shape
    tok = DISPATCH_TILES * ROUTE_TILE
    assert S % tok == 0 and S // ROUTE_TILE >= DISPATCH_SLOTS
    max_rows = N_EXPERTS * ROUTE_TILE
    return pl.pallas_call(
        functools.partial(_dispatch_kernel, cap),
        grid_spec=pltpu.PrefetchScalarGridSpec(
            num_scalar_prefetch=2,
            grid=(S // tok,),
            in_specs=[pl.BlockSpec((tok, D), lambda t, b, c: (t, 0)),
                      pl.BlockSpec((tok // LANES, N_EXPERTS, LANES), lambda t, b, c: (t, 0, 0))],
            out_specs=pl.BlockSpec(memory_space=pl.ANY),
            scratch_shapes=[pltpu.VMEM((DISPATCH_SLOTS, max_rows * SUBLANES, LANES), F32),
                            pltpu.SemaphoreType.DMA((DISPATCH_SLOTS,))]),
        out_shape=jax.ShapeDtypeStruct((N_EXPERTS * cap * SUBLANES, LANES), F32),
        compiler_params=pltpu.CompilerParams(
            dimension_semantics=("arbitrary",), vmem_limit_bytes=VMEM_LIMIT),
        name="dispatch",
    )(b_flat, c_flat, x2b, loc3)


def _ffn_kernel(xe_ref, wg_ref, wu_ref, wd_ref, wr_ref, o_ref, xb, acc, gate):
    e = pl.program_id(0)
    f = pl.program_id(2)
    tm = xb.shape[0]

    @pl.when(f == 0)
    def _():
        for s in range(SUBLANES):
            xb[:, s * LANES:(s + 1) * LANES] = (
                xe_ref[pl.ds(s, tm, stride=SUBLANES), :].astype(BF16))
        lg = jnp.dot(xb[...], wr_ref[...], preferred_element_type=F32)
        lane = lax.broadcasted_iota(I32, (1, LANES), 1)
        lg = jnp.where(lane < N_EXPERTS, lg, NEG)
        p = jnp.exp(lg - jnp.max(lg, axis=-1, keepdims=True))
        mine = jnp.sum(jnp.where(lane == e, p, 0.0), axis=-1, keepdims=True)
        gate[...] = mine / jnp.sum(p, axis=-1, keepdims=True)

    def down_proj():
        x = xb[...]
        g = jnp.dot(x, wg_ref[...].astype(BF16), preferred_element_type=F32)
        u = jnp.dot(x, wu_ref[...].astype(BF16), preferred_element_type=F32)
        h = (g * jax.nn.sigmoid(g) * u).astype(BF16)
        return jnp.dot(h, wd_ref[...].astype(BF16), preferred_element_type=F32)

    last = pl.num_programs(2) - 1

    @pl.when(f == 0)
    def _():
        acc[...] = down_proj()

    @pl.when((f > 0) & (f < last))
    def _():
        acc[...] += down_proj()

    @pl.when(f == last)
    def _():
        y = (acc[...] + down_proj()) * gate[...]
        for s in range(SUBLANES):
            o_ref[pl.ds(s, tm, stride=SUBLANES), :] = y[:, s * LANES:(s + 1) * LANES]


def _ffn(xe2d, w_gate, w_up, w_down, wr_pad, cap):
    E, D, FF = w_gate.shape
    tm, tf = FFN_TM, FFN_TF
    mt = cap // tm
    rows = lambda e, m, f: (e * mt + m, 0)
    return pl.pallas_call(
        _ffn_kernel,
        grid=(E, mt, FF // tf),
        in_specs=[pl.BlockSpec((tm * SUBLANES, LANES), rows),
                  pl.BlockSpec((None, D, tf), lambda e, m, f: (e, 0, f)),
                  pl.BlockSpec((None, D, tf), lambda e, m, f: (e, 0, f)),
                  pl.BlockSpec((None, tf, D), lambda e, m, f: (e, f, 0)),
                  pl.BlockSpec(wr_pad.shape, lambda e, m, f: (0, 0))],
        out_specs=pl.BlockSpec((tm * SUBLANES, LANES), rows),
        out_shape=jax.ShapeDtypeStruct(xe2d.shape, F32),
        scratch_shapes=[pltpu.VMEM((tm, D), BF16), pltpu.VMEM((tm, D), F32),
                        pltpu.VMEM((tm, 1), F32)],
        compiler_params=pltpu.CompilerParams(
            dimension_semantics=("arbitrary", "arbitrary", "arbitrary"),
            vmem_limit_bytes=VMEM_LIMIT),
        name="ffn",
    )(xe2d, w_gate, w_up, w_down, wr_pad)


def _combine_kernel(cap, alpha, b_sm, c_sm, x_ref, loc_ref, g_ref, be_ref, ye_hbm,
                    o_ref, ybuf, sem, hl_scr, z_scr):
    step = pl.program_id(0)
    nt = pl.num_programs(0) * COMBINE_TILES
    lane = lax.broadcasted_iota(I32, (1, SLOT_CHUNK), 1)
    sub = lax.broadcasted_iota(I32, (SLOT_CHUNK, 1), 0)
    eid = lax.broadcasted_iota(I32, (N_EXPERTS, 1), 0)

    def copy_for(slot_):
        def make(e, src_row, dst_row, n):
            return pltpu.make_async_copy(
                ye_hbm.at[pl.ds(pl.multiple_of((e * cap + src_row) * SUBLANES, SUBLANES),
                                n * SUBLANES)],
                ybuf.at[slot_, pl.ds(pl.multiple_of(dst_row * SUBLANES, SUBLANES), n * SUBLANES)],
                sem.at[slot_])
        return make

    def fetch(t_):
        _segment_copies(b_sm, c_sm, t_, copy_for(t_ % 2), lambda cp: cp.start())

    @pl.when(step == 0)
    def _():
        ybuf[...] = jnp.zeros_like(ybuf)
        fetch(0)
        fetch(1)

    def one_tile(k):
        t = step * COMBINE_TILES + k
        slot = t % 2
        tok = slice(k * ROUTE_TILE, (k + 1) * ROUTE_TILE)

        rows_in = _tile_rows(c_sm, t)

        @pl.when(rows_in > 0)
        def _():
            n = rows_in * SUBLANES
            pltpu.make_async_copy(ye_hbm.at[pl.ds(0, n)], ybuf.at[slot, pl.ds(0, n)],
                                  sem.at[slot]).wait()

        offs = []
        off = 0
        for e in range(N_EXPERTS):
            offs.append(off)
            off = off + c_sm[t * N_EXPERTS + e]
        offs.append(off)
        total = off
        nchunk = (total + SLOT_CHUNK - 1) // SLOT_CHUNK
        seg_lo = jnp.zeros((N_EXPERTS, 1), I32)
        seg_hi = jnp.zeros((N_EXPERTS, 1), I32)
        for e in range(N_EXPERTS):
            seg_lo = jnp.where(eid == e, offs[e], seg_lo)
            seg_hi = jnp.where(eid == e, offs[e + 1], seg_hi)
        seg_lo_f = seg_lo.astype(F32)
        locb = loc_ref[tok, :].astype(F32).astype(BF16)

        z_scr[...] = alpha * x_ref[tok, :]

        def chunk_body(m, carry):
            base = m * SLOT_CHUNK
            srow = lane + base
            inseg = jnp.where(srow >= seg_lo, jnp.where(srow < seg_hi, 1.0, 0.0), 0.0)
            rank = srow.astype(F32) - jnp.sum(inseg * seg_lo_f, axis=0, keepdims=True)
            want = jnp.dot(locb, inseg.astype(BF16), preferred_element_type=F32)
            ptb = jnp.where(want == rank, 1.0, 0.0).astype(BF16)
            row0 = (base * SUBLANES if isinstance(m, int)
                    else pl.multiple_of(base * SUBLANES, SUBLANES))
            yc = jnp.concatenate(
                [ybuf[slot, pl.ds(row0 + s, SLOT_CHUNK, stride=SUBLANES), :]
                 for s in range(SUBLANES)], axis=1)
            yc = jnp.where(sub < total - base, yc, 0.0)
            hi = yc.astype(BF16)
            hl = hl_scr.at[m % PEELED_CHUNKS]
            hl[0:SLOT_CHUNK] = hi
            hl[SLOT_CHUNK:2 * SLOT_CHUNK] = (yc - hi.astype(F32)).astype(BF16)
            z_scr[...] += jnp.dot(jnp.concatenate([ptb, ptb], axis=1), hl[...],
                                  preferred_element_type=F32)
            return carry

        for m in range(PEELED_CHUNKS):
            chunk_body(m, 0)
        lax.fori_loop(PEELED_CHUNKS, jnp.maximum(nchunk, PEELED_CHUNKS), chunk_body, 0)

        @pl.when(t + 2 < nt)
        def _():
            fetch(t + 2)

        o_ref[tok, :] = _layer_norm(z_scr[...], g_ref[...], be_ref[...])

    for k in range(COMBINE_TILES):
        one_tile(k)


def _combine(b_flat, c_flat, x2, loc_t, g, b, ye2d, cap, alpha):
    S, D = x2.shape
    tok = COMBINE_TILES * ROUTE_TILE
    assert S % tok == 0 and S // ROUTE_TILE >= 2
    max_rows = N_EXPERTS * ROUTE_TILE
    cur = lambda t, b_, c_: (t, 0)
    const2 = lambda t, b_, c_: (0, 0)
    return pl.pallas_call(
        functools.partial(_combine_kernel, cap, alpha),
        grid_spec=pltpu.PrefetchScalarGridSpec(
            num_scalar_prefetch=2,
            grid=(S // tok,),
            in_specs=[pl.BlockSpec((tok, D), cur),
                      pl.BlockSpec((tok, N_EXPERTS), cur),
                      pl.BlockSpec((1, D), const2),
                      pl.BlockSpec((1, D), const2),
                      pl.BlockSpec(memory_space=pl.ANY)],
            out_specs=pl.BlockSpec((tok, D), cur),
            scratch_shapes=[pltpu.VMEM((2, max_rows * SUBLANES, LANES), F32),
                            pltpu.SemaphoreType.DMA((2,)),
                            pltpu.VMEM((PEELED_CHUNKS, 2 * SLOT_CHUNK, D), BF16),
                            pltpu.VMEM((ROUTE_TILE, D), F32)]),
        out_shape=jax.ShapeDtypeStruct((S, D), F32),
        compiler_params=pltpu.CompilerParams(
            dimension_semantics=("arbitrary",), vmem_limit_bytes=VMEM_LIMIT),
        name="combine",
    )(b_flat, c_flat, x2, loc_t, g, b, ye2d)


def _layer(x, mem, w_in, rpb, conv_w, w_mix_out, ln1_g, ln1_b, wq, wk, wv, wo,
           ln2_g, ln2_b, w_router, w_gate, w_up, w_down, ln3_g, ln3_b, alpha):
    S, D = x.shape
    cap = CAPACITY_FACTOR * S // N_EXPERTS
    assert S % TOK_BLOCK == 0 and S % PROJ_TM == 0 and S % XATTN_TM == 0
    assert S // GRID_W >= 3 * ROW_BLOCK
    assert cap % FFN_TM == 0 and cap <= S
    row = lambda a: a.reshape(1, -1)

    q, k, v, bg, u = _proj(x, w_in.astype(BF16))
    x1 = _mixer(x, q, k, v, bg, u, w_mix_out.astype(BF16), rpb,
                conv_w, row(ln1_g), row(ln1_b), alpha)

    wr_t = w_router.T
    wr_hi = wr_t.astype(BF16)
    wr_lo = (wr_t - wr_hi.astype(F32)).astype(BF16)
    x2, x2b, aff3 = _xattn(x1, mem, wq.astype(BF16), wk.astype(BF16), wv.astype(BF16),
                           wo.astype(BF16), row(ln2_g), row(ln2_b), wr_hi, wr_lo, alpha)

    loc3, b3, c3 = _route(aff3, cap)
    b_flat = b3.reshape(-1)
    c_flat = c3.reshape(-1)

    xe2d = _dispatch(b_flat, c_flat, x2b, loc3, cap)
    wr_pad = jnp.pad(w_router.astype(BF16), ((0, 0), (0, LANES - N_EXPERTS)))
    ye2d = _ffn(xe2d, w_gate, w_up, w_down, wr_pad, cap)
    loc_t = loc3.transpose(0, 2, 1).reshape(S, N_EXPERTS)
    return _combine(b_flat, c_flat, x2, loc_t, row(ln3_g), row(ln3_b), ye2d, cap, alpha)


def kernel(x, mem, w_in, na_rpb, conv_w, w_mix_out, ln1_g, ln1_b, w_mem_q, w_mem_k,
           w_mem_v, w_mem_out, ln2_g, ln2_b, w_router, w_exp_gate, w_exp_up,
           w_exp_down, ln3_g, ln3_b):
    depth = w_in.shape[0]
    alpha = (2 * depth) ** 0.25
    outs = []
    for bi in range(x.shape[0]):
        xb = x[bi]
        for l in range(depth):
            xb = _layer(xb, mem[bi], w_in[l], na_rpb[l], conv_w[l], w_mix_out[l],
                        ln1_g[l], ln1_b[l], w_mem_q[l], w_mem_k[l], w_mem_v[l],
                        w_mem_out[l], ln2_g[l], ln2_b[l], w_router[l], w_exp_gate[l],
                        w_exp_up[l], w_exp_down[l], ln3_g[l], ln3_b[l], alpha)
        outs.append(xb)
    return jnp.stack(outs)
```

```python
import functools

import jax
import jax.numpy as jnp
from jax import lax
from jax.experimental import pallas as pl
from jax.experimental.pallas import tpu as pltpu

F32 = jnp.float32
BF16 = jnp.bfloat16
I32 = jnp.int32

GRID_W = 64
HEAD_DIM = 64
NA_HEADS = 8
NA_WIDTH = NA_HEADS * HEAD_DIM
WIN_ROWS = 8
WIN_COLS = 16
MEM_HEADS = 4
MEM_HEAD_DIM = 128
N_EXPERTS = 16
CAPACITY_FACTOR = 2
LN_EPS = 1e-5
NEG = -1e30

LANES = 128
SUBLANES = 8
VMEM_LIMIT = 56 * 1024 * 1024

ROW_BLOCK = 8
TOK_BLOCK = ROW_BLOCK * GRID_W
PROJ_TM = 1024
XATTN_TM = 1024
ROUTE_TILE = 256
SLOT_CHUNK = 256
SM_ROWS = 16
PEELED_CHUNKS = 2
DISPATCH_TILES = 4
COMBINE_TILES = 2
DISPATCH_SLOTS = 3
FFN_TM = 2048
FFN_TF = 256


def _layer_norm(z, g, b):
    mu = jnp.mean(z, axis=-1, keepdims=True)
    zc = z - mu
    var = jnp.mean(zc * zc, axis=-1, keepdims=True)
    return zc * lax.rsqrt(var + LN_EPS) * g + b


def _dot_nt(a, b):
    return lax.dot_general(a, b, (((1,), (1,)), ((), ())),
                           preferred_element_type=F32)


def _proj_kernel(x_ref, w_ref, q_ref, k_ref, v_ref, b_ref, u_ref):
    xb = x_ref[...].astype(BF16)
    nw = NA_WIDTH

    def mm(c0):
        return jnp.dot(xb, w_ref[:, c0:c0 + nw], preferred_element_type=F32)

    q_ref[...] = (mm(0) * (HEAD_DIM ** -0.5)).astype(BF16)
    k_ref[...] = mm(nw).astype(BF16)
    v_ref[...] = mm(2 * nw).astype(BF16)
    b_ref[...] = mm(3 * nw)
    u_ref[...] = mm(4 * nw) * mm(5 * nw)


def _proj(x, w_in_b):
    S, D = x.shape
    nw = NA_WIDTH
    tm = PROJ_TM
    blk = lambda i: (i, 0)
    return pl.pallas_call(
        _proj_kernel,
        grid=(S // tm,),
        in_specs=[pl.BlockSpec((tm, D), blk),
                  pl.BlockSpec((D, 6 * nw), lambda i: (0, 0))],
        out_specs=[pl.BlockSpec((tm, nw), blk)] * 5,
        out_shape=[jax.ShapeDtypeStruct((S, nw), BF16)] * 3
        + [jax.ShapeDtypeStruct((S, nw), F32)] * 2,
        compiler_params=pltpu.CompilerParams(
            dimension_semantics=("arbitrary",), vmem_limit_bytes=VMEM_LIMIT),
        name="proj",
    )(x, w_in_b)


def _build_na_bias(rpb_ref, bias_ref):
    shape = (GRID_W, LANES)
    qc = lax.broadcasted_iota(I32, shape, 0)
    ln = lax.broadcasted_iota(I32, shape, 1)
    kc = ln % GRID_W
    c_start = jnp.clip(qc - WIN_COLS // 2, 0, GRID_W - WIN_COLS)
    base = LANES - (WIN_COLS - 1)

    def canvas(h, d):
        both = rpb_ref[h, d:d + 1, :] + pltpu.roll(rpb_ref[h, d + 1:d + 2, :], GRID_W, 1)
        t = pltpu.roll(pltpu.roll(jnp.broadcast_to(both, shape), base, 1), 0, 1,
                       stride=1, stride_axis=0)
        return jnp.where(kc >= c_start, jnp.where(kc < c_start + WIN_COLS, t, NEG), NEG)

    for par in range(2):
        for hp in range(NA_HEADS // 2):
            for m in range(WIN_ROWS):
                d = 2 * m + par
                bias_ref[par, hp, m, 0:GRID_W] = canvas(2 * hp, d)
                bias_ref[par, hp, m, GRID_W:2 * GRID_W] = canvas(2 * hp + 1, d)


def _kv_window_start(i, rows):
    return jnp.clip((i - 1) * ROW_BLOCK, 0, rows - 3 * ROW_BLOCK)


def _mixer_kernel(rows, alpha,
                  q_ref, kbuf, vbuf,
                  b_ref, u_ref, up_ref, un_ref, x_ref, wout_ref, rpb_ref,
                  cw_ref, g_ref, be_ref, o_ref, mix, bias_ref,
                  s_scr0, s_scr1, p_scr0, p_scr1):
    i = pl.program_id(0)
    nb = pl.num_programs(0)
    tb = TOK_BLOCK
    win = WIN_ROWS * GRID_W

    @pl.when(i == 0)
    def _():
        _build_na_bias(rpb_ref, bias_ref)

    lane = lax.broadcasted_iota(I32, (1, LANES), 1)
    first_head = lane < HEAD_DIM
    m_lo = jnp.where(first_head, 1.0, 0.0).astype(BF16)
    m_hi = jnp.where(first_head, 0.0, 1.0).astype(BF16)

    def window(r):
        grow = i * ROW_BLOCK + r
        start = jnp.clip(grow - WIN_ROWS // 2, 0, rows - WIN_ROWS)
        w0 = pl.multiple_of((start - _kv_window_start(i, rows)) * GRID_W, GRID_W)
        d0 = start - grow + WIN_ROWS
        return w0, d0 % 2, d0 // 2

    def scores(r, s_out):
        w0, _, _ = window(r)
        q0 = r * GRID_W if isinstance(r, int) else pl.multiple_of(r * GRID_W, GRID_W)
        for hp in range(NA_HEADS // 2):
            cs = slice(hp * LANES, (hp + 1) * LANES)
            qp = q_ref[pl.ds(q0, GRID_W), cs]
            lhs = jnp.concatenate([qp * m_lo, qp * m_hi], axis=0)
            kw = kbuf[pl.ds(w0, win), cs]
            s_out[hp] = _dot_nt(lhs, kw)

    def probs(r, s_in, p_out):
        _, par, m0 = window(r)
        for hp in range(NA_HEADS // 2):
            for rb in range(2 * GRID_W // SM_ROWS):
                rs = slice(rb * SM_ROWS, (rb + 1) * SM_ROWS)
                bias = jnp.concatenate(
                    [bias_ref[par, hp, m0 + j, rs, :] for j in range(WIN_ROWS // 2)], axis=1)
                s = s_in[hp, rs, :] + bias
                p = jnp.exp(s - jnp.max(s, axis=-1, keepdims=True))
                p = p * (1.0 / jnp.sum(p, axis=-1, keepdims=True))
                p_out[hp, rs, :] = p.astype(BF16)

    def values(r, p_in):
        w0, _, _ = window(r)
        q0 = r * GRID_W if isinstance(r, int) else pl.multiple_of(r * GRID_W, GRID_W)
        for hp in range(NA_HEADS // 2):
            cs = slice(hp * LANES, (hp + 1) * LANES)
            vw = vbuf[pl.ds(w0, win), cs]
            o2 = jnp.dot(p_in[hp], vw, preferred_element_type=F32)
            y = jnp.where(first_head, o2[0:GRID_W], o2[GRID_W:2 * GRID_W])
            mix[pl.ds(q0, GRID_W), cs] = y.astype(BF16)

    s_buf = (s_scr0, s_scr1)
    p_buf = (p_scr0, p_scr1)

    def steady(k, carry):
        for half in range(2):
            r = 1 + 2 * k + half
            values(r - 1, p_buf[half])
            probs(r, s_buf[1 - half], p_buf[1 - half])
            scores(r + 1, s_buf[half])
        return carry

    scores(0, s_buf[0])
    probs(0, s_buf[0], p_buf[0])
    scores(1, s_buf[1])
    lax.fori_loop(0, (ROW_BLOCK - 2) // 2, steady, 0)
    values(ROW_BLOCK - 2, p_buf[0])
    probs(ROW_BLOCK - 1, s_buf[1], p_buf[1])
    values(ROW_BLOCK - 1, p_buf[1])

    u = u_ref[...]
    prev_row = up_ref[SUBLANES - 1:SUBLANES, :] * jnp.where(i > 0, 1.0, 0.0)
    next_row = un_ref[0:1, :] * jnp.where(i < nb - 1, 1.0, 0.0)
    rid = lax.broadcasted_iota(I32, (tb, 1), 0)
    um1 = jnp.where(rid == 0, prev_row, pltpu.roll(u, 1, axis=0))
    up1 = jnp.where(rid == tb - 1, next_row, pltpu.roll(u, tb - 1, axis=0))
    cw = cw_ref[...]
    yc = b_ref[...] * (cw[0:1] * um1 + cw[1:2] * u + cw[2:3] * up1)
    mix[:, NA_WIDTH:] = yc.astype(BF16)

    y = jnp.dot(mix[...], wout_ref[...], preferred_element_type=F32)
    z = alpha * x_ref[...] + y
    o_ref[...] = _layer_norm(z, g_ref[...], be_ref[...])


def _mixer(x, q, k, v, bg, u, w_out_b, rpb, conv_w, g, b, alpha):
    S, D = x.shape
    rpb_pad = jnp.pad(rpb.astype(F32), ((0, 0), (1, 1), (0, LANES - rpb.shape[2])))
    nw = NA_WIDTH
    tb = TOK_BLOCK
    nb = S // tb
    rows = S // GRID_W
    cur = lambda i: (i, 0)
    kv_win = pl.BlockSpec((pl.Element(3 * tb), pl.Element(nw)),
                          lambda i: (_kv_window_start(i, rows) * GRID_W, 0))
    halo = tb // SUBLANES
    hprev = lambda i: (jnp.maximum(i * halo - 1, 0), 0)
    hnext = lambda i: (jnp.minimum((i + 1) * halo, S // SUBLANES - 1), 0)
    const2 = lambda i: (0, 0)
    kv = lambda im: pl.BlockSpec((tb, nw), im)
    return pl.pallas_call(
        functools.partial(_mixer_kernel, rows, alpha),
        grid=(nb,),
        in_specs=[kv(cur), kv_win, kv_win,
                  kv(cur), kv(cur),
                  pl.BlockSpec((SUBLANES, nw), hprev),
                  pl.BlockSpec((SUBLANES, nw), hnext),
                  pl.BlockSpec((tb, D), cur),
                  pl.BlockSpec((D, D), const2),
                  pl.BlockSpec(rpb_pad.shape, lambda i: (0, 0, 0)),
                  pl.BlockSpec(conv_w.shape, const2),
                  pl.BlockSpec((1, D), const2),
                  pl.BlockSpec((1, D), const2)],
        out_specs=pl.BlockSpec((tb, D), cur),
        out_shape=jax.ShapeDtypeStruct((S, D), F32),
        scratch_shapes=[pltpu.VMEM((tb, D), BF16),
                        pltpu.VMEM((2, NA_HEADS // 2, WIN_ROWS, 2 * GRID_W, LANES), F32),
                        pltpu.VMEM((NA_HEADS // 2, 2 * GRID_W, WIN_ROWS * GRID_W), F32),
                        pltpu.VMEM((NA_HEADS // 2, 2 * GRID_W, WIN_ROWS * GRID_W), F32),
                        pltpu.VMEM((NA_HEADS // 2, 2 * GRID_W, WIN_ROWS * GRID_W), BF16),
                        pltpu.VMEM((NA_HEADS // 2, 2 * GRID_W, WIN_ROWS * GRID_W), BF16)],
        compiler_params=pltpu.CompilerParams(
            dimension_semantics=("arbitrary",), vmem_limit_bytes=VMEM_LIMIT),
        name="mixer",
    )(q, k, v, bg, u, u, u, x, w_out_b, rpb_pad, conv_w, g, b)


def _xattn_kernel(alpha, x_ref, mem_ref, wq_ref, wk_ref, wv_ref, wo_ref,
                  g_ref, be_ref, wrh_ref, wrl_ref,
                  x2_ref, x2b_ref, aff_ref, kmem, vmem):
    i = pl.program_id(0)

    @pl.when(i == 0)
    def _():
        mb = mem_ref[...].astype(BF16)
        kmem[...] = jnp.dot(mb, wk_ref[...], preferred_element_type=F32).astype(BF16)
        vmem[...] = jnp.dot(mb, wv_ref[...], preferred_element_type=F32).astype(BF16)

    x1 = x_ref[...]
    qb = jnp.dot(x1.astype(BF16), wq_ref[...], preferred_element_type=F32).astype(BF16)
    outs = []
    for h in range(MEM_HEADS):
        cs = slice(h * MEM_HEAD_DIM, (h + 1) * MEM_HEAD_DIM)
        s = _dot_nt(qb[:, cs], kmem[:, cs]) * (MEM_HEAD_DIM ** -0.5)
        mx = jnp.max(s, axis=-1, keepdims=True)
        p = jnp.exp(s - mx)
        l = jnp.sum(p, axis=-1, keepdims=True)
        oh = jnp.dot(p.astype(BF16), vmem[:, cs], preferred_element_type=F32)
        outs.append(oh / l)
    o = jnp.concatenate(outs, axis=1).astype(BF16)
    y = jnp.dot(o, wo_ref[...], preferred_element_type=F32)
    x2 = _layer_norm(alpha * x1 + y, g_ref[...], be_ref[...])
    x2_ref[...] = x2
    hi = x2.astype(BF16)
    x2b_ref[...] = hi
    lo = (x2 - hi.astype(F32)).astype(BF16)
    wrh = wrh_ref[...]
    r1 = _dot_nt(jnp.concatenate([wrh, wrl_ref[...]], axis=0), hi)
    lg = r1[0:N_EXPERTS] + r1[N_EXPERTS:2 * N_EXPERTS] + _dot_nt(wrh, lo)
    mx = jnp.max(lg, axis=0, keepdims=True)
    p = jnp.exp(lg - mx)
    aff = p / jnp.sum(p, axis=0, keepdims=True)
    for j in range(aff_ref.shape[0]):
        aff_ref[j] = aff[:, j * LANES:(j + 1) * LANES]


def _xattn(x1, mem, wq_b, wk_b, wv_b, wo_b, g, b, wr_hi, wr_lo, alpha):
    S, D = x1.shape
    M = mem.shape[0]
    mw = MEM_HEADS * MEM_HEAD_DIM
    tm = XATTN_TM
    cur = lambda i: (i, 0)
    const2 = lambda i: (0, 0)
    full = lambda a: pl.BlockSpec(a.shape, const2)
    return pl.pallas_call(
        functools.partial(_xattn_kernel, alpha),
        grid=(S // tm,),
        in_specs=[pl.BlockSpec((tm, D), cur), full(mem), full(wq_b), full(wk_b),
                  full(wv_b), full(wo_b), full(g), full(b), full(wr_hi), full(wr_lo)],
        out_specs=[pl.BlockSpec((tm, D), cur), pl.BlockSpec((tm, D), cur),
                   pl.BlockSpec((tm // LANES, N_EXPERTS, LANES), lambda i: (i, 0, 0))],
        out_shape=[jax.ShapeDtypeStruct((S, D), F32),
                   jax.ShapeDtypeStruct((S, D), BF16),
                   jax.ShapeDtypeStruct((S // LANES, N_EXPERTS, LANES), F32)],
        scratch_shapes=[pltpu.VMEM((M, mw), BF16), pltpu.VMEM((M, mw), BF16)],
        compiler_params=pltpu.CompilerParams(
            dimension_semantics=("arbitrary",), vmem_limit_bytes=VMEM_LIMIT),
        name="xattn",
    )(x1, mem, wq_b, wk_b, wv_b, wo_b, g, b, wr_hi, wr_lo)


def _route_kernel(cap, aff_ref, loc_ref, b_ref, c_ref, incl_scr, tot_scr, sel_scr):
    nblk = aff_ref.shape[0]
    per_tile = ROUTE_TILE // LANES
    ntile = nblk // per_tile

    def count_ge(t):
        ge = jnp.where(aff_ref[...] >= t[None], 1.0, 0.0)
        return jnp.sum(jnp.sum(ge, axis=0), axis=1, keepdims=True)

    def coarse(it, bits):
        cand = bits | jnp.left_shift(jnp.int32(1), 30 - it)
        ok = count_ge(pltpu.bitcast(cand, F32)) >= cap
        return jnp.where(ok, cand, bits)

    def fine(it, lohi):
        lo, hi = lohi
        mid = lo + (hi - lo) * 0.5
        ok = count_ge(mid) >= cap
        return jnp.where(ok, mid, lo), jnp.where(ok, hi, mid)

    bits = lax.fori_loop(0, 31, coarse, jnp.zeros((N_EXPERTS, 1), I32))
    thr, _ = lax.fori_loop(
        0, 30, fine, (pltpu.bitcast(bits, F32), pltpu.bitcast(bits + 1, F32)))
    gt_all = jnp.where(aff_ref[...] > thr[None], 1.0, 0.0)
    n_gt = jnp.sum(jnp.sum(gt_all, axis=0), axis=1, keepdims=True)
    need = cap - n_gt

    ii = lax.broadcasted_iota(I32, (LANES, LANES), 0)
    jj = lax.broadcasted_iota(I32, (LANES, LANES), 1)
    tri = jnp.where(ii <= jj, 1.0, 0.0).astype(BF16)
    ones = jnp.ones((LANES, LANES), BF16)
    flat = (nblk * N_EXPERTS, LANES)
    blocks = (nblk, N_EXPERTS, LANES)

    def block_scans(flags):
        f2 = flags.reshape(flat).astype(BF16)
        incl_scr[...] = jnp.dot(f2, tri, preferred_element_type=F32).reshape(blocks)
        tot_scr[...] = jnp.dot(f2, ones, preferred_element_type=F32).reshape(blocks)

    block_scans(jnp.where(aff_ref[...] == thr[None], 1.0, 0.0))

    def select_block(j, run_eq):
        blk = aff_ref[j]
        tie_rank = run_eq + incl_scr[j]
        sel_scr[j] = (jnp.where(blk > thr, 1.0, 0.0)
                      + jnp.where(blk == thr, 1.0, 0.0) * jnp.where(tie_rank <= need, 1.0, 0.0))
        return run_eq + tot_scr[j]

    zero = jnp.zeros((N_EXPERTS, LANES), F32)
    lax.fori_loop(0, nblk, select_block, zero)
    block_scans(sel_scr[...])

    def tile_body(t, run):
        tile_run = zero
        for j in range(per_tile):
            idx = t * per_tile + j
            sel = sel_scr[idx]
            loc = tile_run + incl_scr[idx] - sel
            loc_ref[idx] = jnp.where(sel > 0.5, loc, -1.0).astype(I32)
            tile_run = tile_run + tot_scr[idx]
        b_ref[t] = run[:, 0:1].astype(I32)
        c_ref[t] = tile_run[:, 0:1].astype(I32)
        return run + tile_run

    lax.fori_loop(0, ntile, tile_body, zero)


def _route(aff3, cap):
    nblk = aff3.shape[0]
    ntile = nblk * LANES // ROUTE_TILE
    full3 = lambda a: pl.BlockSpec(a, lambda: (0, 0, 0))
    return pl.pallas_call(
        functools.partial(_route_kernel, cap),
        in_specs=[full3(aff3.shape)],
        out_specs=[full3(aff3.shape), full3((ntile, N_EXPERTS, 1)),
                   full3((ntile, N_EXPERTS, 1))],
        out_shape=[jax.ShapeDtypeStruct(aff3.shape, I32),
                   jax.ShapeDtypeStruct((ntile, N_EXPERTS, 1), I32),
                   jax.ShapeDtypeStruct((ntile, N_EXPERTS, 1), I32)],
        scratch_shapes=[pltpu.VMEM(aff3.shape, F32)] * 3,
        compiler_params=pltpu.CompilerParams(vmem_limit_bytes=VMEM_LIMIT),
        name="route",
    )(aff3)


def _segment_copies(b_sm, c_sm, t, make_copy, action):
    off = 0
    for e in range(N_EXPERTS):
        c = c_sm[t * N_EXPERTS + e]
        b = b_sm[t * N_EXPERTS + e]

        @pl.when(c > 0)
        def _(e=e, b=b, c=c, off=off):
            action(make_copy(e, b, off, c))

        off = off + c
    return off


def _tile_rows(c_sm, t):
    rows = 0
    for e in range(N_EXPERTS):
        rows = rows + c_sm[t * N_EXPERTS + e]
    return rows


def _dispatch_kernel(cap, b_sm, c_sm, x_ref, loc_ref, xe_hbm, stg, sem):
    step = pl.program_id(0)
    nsteps = pl.num_programs(0)
    per_tile = ROUTE_TILE // LANES
    sub = lax.broadcasted_iota(I32, (SLOT_CHUNK, 1), 0)
    lane = lax.broadcasted_iota(I32, (1, LANES), 1)

    def copy_for(slot_):
        def make(e, dst_row, src_row, n):
            return pltpu.make_async_copy(
                stg.at[slot_, pl.ds(pl.multiple_of(src_row * SUBLANES, SUBLANES), n * SUBLANES)],
                xe_hbm.at[pl.ds(pl.multiple_of((e * cap + dst_row) * SUBLANES, SUBLANES),
                                n * SUBLANES)],
                sem.at[slot_])
        return make

    def wait_tile(t_):
        rows = _tile_rows(c_sm, t_)
        slot_ = t_ % DISPATCH_SLOTS

        @pl.when(rows > 0)
        def _():
            n = rows * SUBLANES
            pltpu.make_async_copy(stg.at[slot_, pl.ds(0, n)], xe_hbm.at[pl.ds(0, n)],
                                  sem.at[slot_]).wait()

    def one_tile(k):
        t = step * DISPATCH_TILES + k
        slot = t % DISPATCH_SLOTS
        tok = slice(k * ROUTE_TILE, (k + 1) * ROUTE_TILE)
        offs = []
        off = 0
        for e in range(N_EXPERTS):
            offs.append(off)
            off = off + c_sm[t * N_EXPERTS + e]
        offs.append(off)
        nchunk = (off + SLOT_CHUNK - 1) // SLOT_CHUNK
        seg_lo = jnp.zeros((1, LANES), I32)
        seg_hi = jnp.zeros((1, LANES), I32)
        for e in range(N_EXPERTS):
            seg_lo = jnp.where(lane == e, offs[e], seg_lo)
            seg_hi = jnp.where(lane == e, offs[e + 1], seg_hi)
        seg_lo_f = seg_lo.astype(F32)
        locf = jnp.concatenate(
            [loc_ref[k * per_tile + j] for j in range(per_tile)], axis=1).astype(F32)
        locb = jnp.concatenate(
            [locf, jnp.zeros((LANES - N_EXPERTS, ROUTE_TILE), F32)], axis=0).astype(BF16)

        def chunk_body(m, carry):
            base = m * SLOT_CHUNK
            srow = sub + base
            inseg = jnp.where(srow >= seg_lo, jnp.where(srow < seg_hi, 1.0, 0.0), 0.0)
            rank = srow.astype(F32) - jnp.sum(inseg * seg_lo_f, axis=1, keepdims=True)
            want = jnp.dot(inseg.astype(BF16), locb, preferred_element_type=F32)
            p = jnp.where(want == rank, 1.0, 0.0)
            res = jnp.dot(p.astype(BF16), x_ref[tok, :], preferred_element_type=F32)
            row0 = (base * SUBLANES if isinstance(m, int)
                    else pl.multiple_of(base * SUBLANES, SUBLANES))
            for s in range(SUBLANES):
                stg[slot, pl.ds(row0 + s, SLOT_CHUNK, stride=SUBLANES), :] = (
                    res[:, s * LANES:(s + 1) * LANES])
            return carry

        for m in range(PEELED_CHUNKS):
            chunk_body(m, 0)
        lax.fori_loop(PEELED_CHUNKS, jnp.maximum(nchunk, PEELED_CHUNKS), chunk_body, 0)

        _segment_copies(b_sm, c_sm, t, copy_for(slot), lambda cp: cp.start())

        @pl.when(t >= DISPATCH_SLOTS - 1)
        def _():
            wait_tile(t - (DISPATCH_SLOTS - 1))

    for k in range(DISPATCH_TILES):
        one_tile(k)

    @pl.when(step == nsteps - 1)
    def _():
        last = nsteps * DISPATCH_TILES - 1
        for d in range(DISPATCH_SLOTS - 2, -1, -1):
            wait_tile(last - d)


def _dispatch(b_flat, c_flat, x2b, loc3, cap):
    S, D = x2b.shape
    tok = DISPATCH_TILES * ROUTE_TILE
    assert S % tok == 0 and S // ROUTE_TILE >= DISPATCH_SLOTS
    max_rows = N_EXPERTS * ROUTE_TILE
    return pl.pallas_call(
        functools.partial(_dispatch_kernel, cap),
        grid_spec=pltpu.PrefetchScalarGridSpec(
            num_scalar_prefetch=2,
            grid=(S // tok,),
            in_specs=[pl.BlockSpec((tok, D), lambda t, b, c: (t, 0)),
                      pl.BlockSpec((tok // LANES, N_EXPERTS, LANES), lambda t, b, c: (t, 0, 0))],
            out_specs=pl.BlockSpec(memory_space=pl.ANY),
            scratch_shapes=[pltpu.VMEM((DISPATCH_SLOTS, max_rows * SUBLANES, LANES), F32),
                            pltpu.SemaphoreType.DMA((DISPATCH_SLOTS,))]),
        out_shape=jax.ShapeDtypeStruct((N_EXPERTS * cap * SUBLANES, LANES), F32),
        compiler_params=pltpu.CompilerParams(
            dimension_semantics=("arbitrary",), vmem_limit_bytes=VMEM_LIMIT),
        name="dispatch",
    )(b_flat, c_flat, x2b, loc3)


def _ffn_kernel(xe_ref, wg_ref, wu_ref, wd_ref, wr_ref, o_ref, xb, acc, gate):
    e = pl.program_id(0)
    f = pl.program_id(2)
    tm = xb.shape[0]

    @pl.when(f == 0)
    def _():
        for s in range(SUBLANES):
            xb[:, s * LANES:(s + 1) * LANES] = (
                xe_ref[pl.ds(s, tm, stride=SUBLANES), :].astype(BF16))
        lg = jnp.dot(xb[...], wr_ref[...], preferred_element_type=F32)
        lane = lax.broadcasted_iota(I32, (1, LANES), 1)
        lg = jnp.where(lane < N_EXPERTS, lg, NEG)
        p = jnp.exp(lg - jnp.max(lg, axis=-1, keepdims=True))
        mine = jnp.sum(jnp.where(lane == e, p, 0.0), axis=-1, keepdims=True)
        gate[...] = mine / jnp.sum(p, axis=-1, keepdims=True)

    def down_proj():
        x = xb[...]
        g = jnp.dot(x, wg_ref[...].astype(BF16), preferred_element_type=F32)
        u = jnp.dot(x, wu_ref[...].astype(BF16), preferred_element_type=F32)
        h = (g * jax.nn.sigmoid(g) * u).astype(BF16)
        return jnp.dot(h, wd_ref[...].astype(BF16), preferred_element_type=F32)

    last = pl.num_programs(2) - 1

    @pl.when(f == 0)
    def _():
        acc[...] = down_proj()

    @pl.when((f > 0) & (f < last))
    def _():
        acc[...] += down_proj()

    @pl.when(f == last)
    def _():
        y = (acc[...] + down_proj()) * gate[...]
        for s in range(SUBLANES):
            o_ref[pl.ds(s, tm, stride=SUBLANES), :] = y[:, s * LANES:(s + 1) * LANES]


def _ffn(xe2d, w_gate, w_up, w_down, wr_pad, cap):
    E, D, FF = w_gate.shape
    tm, tf = FFN_TM, FFN_TF
    mt = cap // tm
    rows = lambda e, m, f: (e * mt + m, 0)
    return pl.pallas_call(
        _ffn_kernel,
        grid=(E, mt, FF // tf),
        in_specs=[pl.BlockSpec((tm * SUBLANES, LANES), rows),
                  pl.BlockSpec((None, D, tf), lambda e, m, f: (e, 0, f)),
                  pl.BlockSpec((None, D, tf), lambda e, m, f: (e, 0, f)),
                  pl.BlockSpec((None, tf, D), lambda e, m, f: (e, f, 0)),
                  pl.BlockSpec(wr_pad.shape, lambda e, m, f: (0, 0))],
        out_specs=pl.BlockSpec((tm * SUBLANES, LANES), rows),
        out_shape=jax.ShapeDtypeStruct(xe2d.shape, F32),
        scratch_shapes=[pltpu.VMEM((tm, D), BF16), pltpu.VMEM((tm, D), F32),
                        pltpu.VMEM((tm, 1), F32)],
        compiler_params=pltpu.CompilerParams(
            dimension_semantics=("arbitrary", "arbitrary", "arbitrary"),
            vmem_limit_bytes=62 * 1024 * 1024),
        name="ffn",
    )(xe2d, w_gate, w_up, w_down, wr_pad)


def _combine_kernel(cap, alpha, b_sm, c_sm, x_ref, loc_ref, g_ref, be_ref, ye_hbm,
                    o_ref, ybuf, sem, hl_scr, z_scr):
    step = pl.program_id(0)
    nt = pl.num_programs(0) * COMBINE_TILES
    lane = lax.broadcasted_iota(I32, (1, SLOT_CHUNK), 1)
    sub = lax.broadcasted_iota(I32, (SLOT_CHUNK, 1), 0)
    eid = lax.broadcasted_iota(I32, (N_EXPERTS, 1), 0)

    def copy_for(slot_):
        def make(e, src_row, dst_row, n):
            return pltpu.make_async_copy(
                ye_hbm.at[pl.ds(pl.multiple_of((e * cap + src_row) * SUBLANES, SUBLANES),
                                n * SUBLANES)],
                ybuf.at[slot_, pl.ds(pl.multiple_of(dst_row * SUBLANES, SUBLANES), n * SUBLANES)],
                sem.at[slot_])
        return make

    def fetch(t_):
        _segment_copies(b_sm, c_sm, t_, copy_for(t_ % 2), lambda cp: cp.start())

    @pl.when(step == 0)
    def _():
        ybuf[...] = jnp.zeros_like(ybuf)
        fetch(0)
        fetch(1)

    def one_tile(k):
        t = step * COMBINE_TILES + k
        slot = t % 2
        tok = slice(k * ROUTE_TILE, (k + 1) * ROUTE_TILE)

        rows_in = _tile_rows(c_sm, t)

        @pl.when(rows_in > 0)
        def _():
            n = rows_in * SUBLANES
            pltpu.make_async_copy(ye_hbm.at[pl.ds(0, n)], ybuf.at[slot, pl.ds(0, n)],
                                  sem.at[slot]).wait()

        offs = []
        off = 0
        for e in range(N_EXPERTS):
            offs.append(off)
            off = off + c_sm[t * N_EXPERTS + e]
        offs.append(off)
        total = off
        nchunk = (total + SLOT_CHUNK - 1) // SLOT_CHUNK
        seg_lo = jnp.zeros((N_EXPERTS, 1), I32)
        seg_hi = jnp.zeros((N_EXPERTS, 1), I32)
        for e in range(N_EXPERTS):
            seg_lo = jnp.where(eid == e, offs[e], seg_lo)
            seg_hi = jnp.where(eid == e, offs[e + 1], seg_hi)
        seg_lo_f = seg_lo.astype(F32)
        locb = loc_ref[tok, :].astype(F32).astype(BF16)

        z_scr[...] = alpha * x_ref[tok, :]

        def chunk_body(m, carry):
            base = m * SLOT_CHUNK
            srow = lane + base
            inseg = jnp.where(srow >= seg_lo, jnp.where(srow < seg_hi, 1.0, 0.0), 0.0)
            rank = srow.astype(F32) - jnp.sum(inseg * seg_lo_f, axis=0, keepdims=True)
            want = jnp.dot(locb, inseg.astype(BF16), preferred_element_type=F32)
            ptb = jnp.where(want == rank, 1.0, 0.0).astype(BF16)
            row0 = (base * SUBLANES if isinstance(m, int)
                    else pl.multiple_of(base * SUBLANES, SUBLANES))
            yc = jnp.concatenate(
                [ybuf[slot, pl.ds(row0 + s, SLOT_CHUNK, stride=SUBLANES), :]
                 for s in range(SUBLANES)], axis=1)
            yc = jnp.where(sub < total - base, yc, 0.0)
            hi = yc.astype(BF16)
            hl = hl_scr.at[m % PEELED_CHUNKS]
            hl[0:SLOT_CHUNK] = hi
            hl[SLOT_CHUNK:2 * SLOT_CHUNK] = (yc - hi.astype(F32)).astype(BF16)
            z_scr[...] += jnp.dot(jnp.concatenate([ptb, ptb], axis=1), hl[...],
                                  preferred_element_type=F32)
            return carry

        for m in range(PEELED_CHUNKS):
            chunk_body(m, 0)
        lax.fori_loop(PEELED_CHUNKS, jnp.maximum(nchunk, PEELED_CHUNKS), chunk_body, 0)

        @pl.when(t + 2 < nt)
        def _():
            fetch(t + 2)

        o_ref[tok, :] = _layer_norm(z_scr[...], g_ref[...], be_ref[...])

    for k in range(COMBINE_TILES):
        one_tile(k)


def _combine(b_flat, c_flat, x2, loc_t, g, b, ye2d, cap, alpha):
    S, D = x2.shape
    tok = COMBINE_TILES * ROUTE_TILE
    assert S % tok == 0 and S // ROUTE_TILE >= 2
    max_rows = N_EXPERTS * ROUTE_TILE
    cur = lambda t, b_, c_: (t, 0)
    const2 = lambda t, b_, c_: (0, 0)
    return pl.pallas_call(
        functools.partial(_combine_kernel, cap, alpha),
        grid_spec=pltpu.PrefetchScalarGridSpec(
            num_scalar_prefetch=2,
            grid=(S // tok,),
            in_specs=[pl.BlockSpec((tok, D), cur),
                      pl.BlockSpec((tok, N_EXPERTS), cur),
                      pl.BlockSpec((1, D), const2),
                      pl.BlockSpec((1, D), const2),
                      pl.BlockSpec(memory_space=pl.ANY)],
            out_specs=pl.BlockSpec((tok, D), cur),
            scratch_shapes=[pltpu.VMEM((2, max_rows * SUBLANES, LANES), F32),
                            pltpu.SemaphoreType.DMA((2,)),
                            pltpu.VMEM((PEELED_CHUNKS, 2 * SLOT_CHUNK, D), BF16),
                            pltpu.VMEM((ROUTE_TILE, D), F32)]),
        out_shape=jax.ShapeDtypeStruct((S, D), F32),
        compiler_params=pltpu.CompilerParams(
            dimension_semantics=("arbitrary",), vmem_limit_bytes=VMEM_LIMIT),
        name="combine",
    )(b_flat, c_flat, x2, loc_t, g, b, ye2d)


def _layer(x, mem, w_in, rpb, conv_w, w_mix_out, ln1_g, ln1_b, wq, wk, wv, wo,
           ln2_g, ln2_b, w_router, w_gate, w_up, w_down, ln3_g, ln3_b, alpha):
    S, D = x.shape
    cap = CAPACITY_FACTOR * S // N_EXPERTS
    assert S % TOK_BLOCK == 0 and S % PROJ_TM == 0 and S % XATTN_TM == 0
    assert S // GRID_W >= 3 * ROW_BLOCK
    assert cap % FFN_TM == 0 and cap <= S
    row = lambda a: a.reshape(1, -1)

    q, k, v, bg, u = _proj(x, w_in.astype(BF16))
    x1 = _mixer(x, q, k, v, bg, u, w_mix_out.astype(BF16), rpb,
                conv_w, row(ln1_g), row(ln1_b), alpha)

    wr_t = w_router.T
    wr_hi = wr_t.astype(BF16)
    wr_lo = (wr_t - wr_hi.astype(F32)).astype(BF16)
    x2, x2b, aff3 = _xattn(x1, mem, wq.astype(BF16), wk.astype(BF16), wv.astype(BF16),
                           wo.astype(BF16), row(ln2_g), row(ln2_b), wr_hi, wr_lo, alpha)

    loc3, b3, c3 = _route(aff3, cap)
    b_flat = b3.reshape(-1)
    c_flat = c3.reshape(-1)

    xe2d = _dispatch(b_flat, c_flat, x2b, loc3, cap)
    wr_pad = jnp.pad(w_router.astype(BF16), ((0, 0), (0, LANES - N_EXPERTS)))
    ye2d = _ffn(xe2d, w_gate, w_up, w_down, wr_pad, cap)
    loc_t = loc3.transpose(0, 2, 1).reshape(S, N_EXPERTS)
    return _combine(b_flat, c_flat, x2, loc_t, row(ln3_g), row(ln3_b), ye2d, cap, alpha)


def kernel(x, mem, w_in, na_rpb, conv_w, w_mix_out, ln1_g, ln1_b, w_mem_q, w_mem_k,
           w_mem_v, w_mem_out, ln2_g, ln2_b, w_router, w_exp_gate, w_exp_up,
           w_exp_down, ln3_g, ln3_b):
    depth = w_in.shape[0]
    alpha = (2 * depth) ** 0.25
    outs = []
    for bi in range(x.shape[0]):
        xb = x[bi]
        for l in range(depth):
            xb = _layer(xb, mem[bi], w_in[l], na_rpb[l], conv_w[l], w_mix_out[l],
                        ln1_g[l], ln1_b[l], w_mem_q[l], w_mem_k[l], w_mem_v[l],
                        w_mem_out[l], ln2_g[l], ln2_b[l], w_router[l], w_exp_gate[l],
                        w_exp_up[l], w_exp_down[l], ln3_g[l], ln3_b[l], alpha)
        outs.append(xb)
    return jnp.stack(outs)
```

```python
import functools

import jax
import jax.numpy as jnp
from jax import lax
from jax.experimental import pallas as pl
from jax.experimental.pallas import tpu as pltpu

F32 = jnp.float32
BF16 = jnp.bfloat16
I32 = jnp.int32

GRID_W = 64
HEAD_DIM = 64
NA_HEADS = 8
NA_WIDTH = NA_HEADS * HEAD_DIM
WIN_ROWS = 8
WIN_COLS = 16
MEM_HEADS = 4
MEM_HEAD_DIM = 128
N_EXPERTS = 16
CAPACITY_FACTOR = 2
LN_EPS = 1e-5
NEG = -1e30

LANES = 128
SUBLANES = 8
VMEM_LIMIT = 56 * 1024 * 1024

ROW_BLOCK = 8
TOK_BLOCK = ROW_BLOCK * GRID_W
PROJ_TM = 1024
XATTN_TM = 1024
ROUTE_TILE = 256
SLOT_CHUNK = 256
SM_ROWS = 16
PEELED_CHUNKS = 2
DISPATCH_TILES = 4
COMBINE_TILES = 2
DISPATCH_SLOTS = 3
FFN_TM = 1024
FFN_TF = 512


def _layer_norm(z, g, b):
    mu = jnp.mean(z, axis=-1, keepdims=True)
    zc = z - mu
    var = jnp.mean(zc * zc, axis=-1, keepdims=True)
    return zc * lax.rsqrt(var + LN_EPS) * g + b


def _dot_nt(a, b):
    return lax.dot_general(a, b, (((1,), (1,)), ((), ())),
                           preferred_element_type=F32)


def _proj_kernel(x_ref, w_ref, q_ref, k_ref, v_ref, b_ref, u_ref):
    xb = x_ref[...].astype(BF16)
    nw = NA_WIDTH

    def mm(c0):
        return jnp.dot(xb, w_ref[:, c0:c0 + nw], preferred_element_type=F32)

    q_ref[...] = (mm(0) * (HEAD_DIM ** -0.5)).astype(BF16)
    k_ref[...] = mm(nw).astype(BF16)
    v_ref[...] = mm(2 * nw).astype(BF16)
    b_ref[...] = mm(3 * nw)
    u_ref[...] = mm(4 * nw) * mm(5 * nw)


def _proj(x, w_in_b):
    S, D = x.shape
    nw = NA_WIDTH
    tm = PROJ_TM
    blk = lambda i: (i, 0)
    return pl.pallas_call(
        _proj_kernel,
        grid=(S // tm,),
        in_specs=[pl.BlockSpec((tm, D), blk),
                  pl.BlockSpec((D, 6 * nw), lambda i: (0, 0))],
        out_specs=[pl.BlockSpec((tm, nw), blk)] * 5,
        out_shape=[jax.ShapeDtypeStruct((S, nw), BF16)] * 3
        + [jax.ShapeDtypeStruct((S, nw), F32)] * 2,
        compiler_params=pltpu.CompilerParams(
            dimension_semantics=("arbitrary",), vmem_limit_bytes=VMEM_LIMIT),
        name="proj",
    )(x, w_in_b)


def _build_na_bias(rpb_ref, bias_ref):
    shape = (GRID_W, LANES)
    qc = lax.broadcasted_iota(I32, shape, 0)
    ln = lax.broadcasted_iota(I32, shape, 1)
    kc = ln % GRID_W
    c_start = jnp.clip(qc - WIN_COLS // 2, 0, GRID_W - WIN_COLS)
    base = LANES - (WIN_COLS - 1)

    def canvas(h, d):
        both = rpb_ref[h, d:d + 1, :] + pltpu.roll(rpb_ref[h, d + 1:d + 2, :], GRID_W, 1)
        t = pltpu.roll(pltpu.roll(jnp.broadcast_to(both, shape), base, 1), 0, 1,
                       stride=1, stride_axis=0)
        return jnp.where(kc >= c_start, jnp.where(kc < c_start + WIN_COLS, t, NEG), NEG)

    for par in range(2):
        for hp in range(NA_HEADS // 2):
            for m in range(WIN_ROWS):
                d = 2 * m + par
                bias_ref[par, hp, m, 0:GRID_W] = canvas(2 * hp, d)
                bias_ref[par, hp, m, GRID_W:2 * GRID_W] = canvas(2 * hp + 1, d)


def _kv_window_start(i, rows):
    return jnp.clip((i - 1) * ROW_BLOCK, 0, rows - 3 * ROW_BLOCK)


def _mixer_kernel(rows, alpha,
                  q_ref, kbuf, vbuf,
                  b_ref, u_ref, up_ref, un_ref, x_ref, wout_ref, rpb_ref,
                  cw_ref, g_ref, be_ref, o_ref, mix, bias_ref,
                  s_scr0, s_scr1, p_scr0, p_scr1):
    i = pl.program_id(0)
    nb = pl.num_programs(0)
    tb = TOK_BLOCK
    win = WIN_ROWS * GRID_W

    @pl.when(i == 0)
    def _():
        _build_na_bias(rpb_ref, bias_ref)

    lane = lax.broadcasted_iota(I32, (1, LANES), 1)
    first_head = lane < HEAD_DIM
    m_lo = jnp.where(first_head, 1.0, 0.0).astype(BF16)
    m_hi = jnp.where(first_head, 0.0, 1.0).astype(BF16)

    def window(r):
        grow = i * ROW_BLOCK + r
        start = jnp.clip(grow - WIN_ROWS // 2, 0, rows - WIN_ROWS)
        w0 = pl.multiple_of((start - _kv_window_start(i, rows)) * GRID_W, GRID_W)
        d0 = start - grow + WIN_ROWS
        return w0, d0 % 2, d0 // 2

    def scores(r, s_out):
        w0, _, _ = window(r)
        q0 = r * GRID_W if isinstance(r, int) else pl.multiple_of(r * GRID_W, GRID_W)
        for hp in range(NA_HEADS // 2):
            cs = slice(hp * LANES, (hp + 1) * LANES)
            qp = q_ref[pl.ds(q0, GRID_W), cs]
            lhs = jnp.concatenate([qp * m_lo, qp * m_hi], axis=0)
            kw = kbuf[pl.ds(w0, win), cs]
            s_out[hp] = _dot_nt(lhs, kw)

    def probs(r, s_in, p_out):
        _, par, m0 = window(r)
        for hp in range(NA_HEADS // 2):
            for rb in range(2 * GRID_W // SM_ROWS):
                rs = slice(rb * SM_ROWS, (rb + 1) * SM_ROWS)
                bias = jnp.concatenate(
                    [bias_ref[par, hp, m0 + j, rs, :] for j in range(WIN_ROWS // 2)], axis=1)
                s = s_in[hp, rs, :] + bias
                p = jnp.exp(s - jnp.max(s, axis=-1, keepdims=True))
                p = p * (1.0 / jnp.sum(p, axis=-1, keepdims=True))
                p_out[hp, rs, :] = p.astype(BF16)

    def values(r, p_in):
        w0, _, _ = window(r)
        q0 = r * GRID_W if isinstance(r, int) else pl.multiple_of(r * GRID_W, GRID_W)
        for hp in range(NA_HEADS // 2):
            cs = slice(hp * LANES, (hp + 1) * LANES)
            vw = vbuf[pl.ds(w0, win), cs]
            o2 = jnp.dot(p_in[hp], vw, preferred_element_type=F32)
            y = jnp.where(first_head, o2[0:GRID_W], o2[GRID_W:2 * GRID_W])
            mix[pl.ds(q0, GRID_W), cs] = y.astype(BF16)

    s_buf = (s_scr0, s_scr1)
    p_buf = (p_scr0, p_scr1)

    def steady(k, carry):
        for half in range(2):
            r = 1 + 2 * k + half
            values(r - 1, p_buf[half])
            probs(r, s_buf[1 - half], p_buf[1 - half])
            scores(r + 1, s_buf[half])
        return carry

    scores(0, s_buf[0])
    probs(0, s_buf[0], p_buf[0])
    scores(1, s_buf[1])
    for kk in range((ROW_BLOCK - 2) // 2):
        steady(kk, 0)
    values(ROW_BLOCK - 2, p_buf[0])
    probs(ROW_BLOCK - 1, s_buf[1], p_buf[1])
    values(ROW_BLOCK - 1, p_buf[1])

    u = u_ref[...]
    prev_row = up_ref[SUBLANES - 1:SUBLANES, :] * jnp.where(i > 0, 1.0, 0.0)
    next_row = un_ref[0:1, :] * jnp.where(i < nb - 1, 1.0, 0.0)
    rid = lax.broadcasted_iota(I32, (tb, 1), 0)
    um1 = jnp.where(rid == 0, prev_row, pltpu.roll(u, 1, axis=0))
    up1 = jnp.where(rid == tb - 1, next_row, pltpu.roll(u, tb - 1, axis=0))
    cw = cw_ref[...]
    yc = b_ref[...] * (cw[0:1] * um1 + cw[1:2] * u + cw[2:3] * up1)
    mix[:, NA_WIDTH:] = yc.astype(BF16)

    y = jnp.dot(mix[...], wout_ref[...], preferred_element_type=F32)
    z = alpha * x_ref[...] + y
    o_ref[...] = _layer_norm(z, g_ref[...], be_ref[...])


def _mixer(x, q, k, v, bg, u, w_out_b, rpb, conv_w, g, b, alpha):
    S, D = x.shape
    rpb_pad = jnp.pad(rpb.astype(F32), ((0, 0), (1, 1), (0, LANES - rpb.shape[2])))
    nw = NA_WIDTH
    tb = TOK_BLOCK
    nb = S // tb
    rows = S // GRID_W
    cur = lambda i: (i, 0)
    kv_win = pl.BlockSpec((pl.Element(3 * tb), pl.Element(nw)),
                          lambda i: (_kv_window_start(i, rows) * GRID_W, 0))
    halo = tb // SUBLANES
    hprev = lambda i: (jnp.maximum(i * halo - 1, 0), 0)
    hnext = lambda i: (jnp.minimum((i + 1) * halo, S // SUBLANES - 1), 0)
    const2 = lambda i: (0, 0)
    kv = lambda im: pl.BlockSpec((tb, nw), im)
    return pl.pallas_call(
        functools.partial(_mixer_kernel, rows, alpha),
        grid=(nb,),
        in_specs=[kv(cur), kv_win, kv_win,
                  kv(cur), kv(cur),
                  pl.BlockSpec((SUBLANES, nw), hprev),
                  pl.BlockSpec((SUBLANES, nw), hnext),
                  pl.BlockSpec((tb, D), cur),
                  pl.BlockSpec((D, D), const2),
                  pl.BlockSpec(rpb_pad.shape, lambda i: (0, 0, 0)),
                  pl.BlockSpec(conv_w.shape, const2),
                  pl.BlockSpec((1, D), const2),
                  pl.BlockSpec((1, D), const2)],
        out_specs=pl.BlockSpec((tb, D), cur),
        out_shape=jax.ShapeDtypeStruct((S, D), F32),
        scratch_shapes=[pltpu.VMEM((tb, D), BF16),
                        pltpu.VMEM((2, NA_HEADS // 2, WIN_ROWS, 2 * GRID_W, LANES), F32),
                        pltpu.VMEM((NA_HEADS // 2, 2 * GRID_W, WIN_ROWS * GRID_W), F32),
                        pltpu.VMEM((NA_HEADS // 2, 2 * GRID_W, WIN_ROWS * GRID_W), F32),
                        pltpu.VMEM((NA_HEADS // 2, 2 * GRID_W, WIN_ROWS * GRID_W), BF16),
                        pltpu.VMEM((NA_HEADS // 2, 2 * GRID_W, WIN_ROWS * GRID_W), BF16)],
        compiler_params=pltpu.CompilerParams(
            dimension_semantics=("arbitrary",), vmem_limit_bytes=VMEM_LIMIT),
        name="mixer",
    )(q, k, v, bg, u, u, u, x, w_out_b, rpb_pad, conv_w, g, b)


def _xattn_kernel(alpha, x_ref, mem_ref, wq_ref, wk_ref, wv_ref, wo_ref,
                  g_ref, be_ref, wrh_ref, wrl_ref,
                  x2_ref, x2b_ref, aff_ref, kmem, vmem):
    i = pl.program_id(0)

    @pl.when(i == 0)
    def _():
        mb = mem_ref[...].astype(BF16)
        kmem[...] = jnp.dot(mb, wk_ref[...], preferred_element_type=F32).astype(BF16)
        vmem[...] = jnp.dot(mb, wv_ref[...], preferred_element_type=F32).astype(BF16)

    x1 = x_ref[...]
    qb = jnp.dot(x1.astype(BF16), wq_ref[...], preferred_element_type=F32).astype(BF16)
    outs = []
    for h in range(MEM_HEADS):
        cs = slice(h * MEM_HEAD_DIM, (h + 1) * MEM_HEAD_DIM)
        s = _dot_nt(qb[:, cs], kmem[:, cs]) * (MEM_HEAD_DIM ** -0.5)
        mx = jnp.max(s, axis=-1, keepdims=True)
        p = jnp.exp(s - mx)
        l = jnp.sum(p, axis=-1, keepdims=True)
        oh = jnp.dot(p.astype(BF16), vmem[:, cs], preferred_element_type=F32)
        outs.append(oh / l)
    o = jnp.concatenate(outs, axis=1).astype(BF16)
    y = jnp.dot(o, wo_ref[...], preferred_element_type=F32)
    x2 = _layer_norm(alpha * x1 + y, g_ref[...], be_ref[...])
    x2_ref[...] = x2
    hi = x2.astype(BF16)
    x2b_ref[...] = hi
    lo = (x2 - hi.astype(F32)).astype(BF16)
    wrh = wrh_ref[...]
    r1 = _dot_nt(jnp.concatenate([wrh, wrl_ref[...]], axis=0), hi)
    lg = r1[0:N_EXPERTS] + r1[N_EXPERTS:2 * N_EXPERTS] + _dot_nt(wrh, lo)
    mx = jnp.max(lg, axis=0, keepdims=True)
    p = jnp.exp(lg - mx)
    aff = p / jnp.sum(p, axis=0, keepdims=True)
    for j in range(aff_ref.shape[0]):
        aff_ref[j] = aff[:, j * LANES:(j + 1) * LANES]


def _xattn(x1, mem, wq_b, wk_b, wv_b, wo_b, g, b, wr_hi, wr_lo, alpha):
    S, D = x1.shape
    M = mem.shape[0]
    mw = MEM_HEADS * MEM_HEAD_DIM
    tm = XATTN_TM
    cur = lambda i: (i, 0)
    const2 = lambda i: (0, 0)
    full = lambda a: pl.BlockSpec(a.shape, const2)
    return pl.pallas_call(
        functools.partial(_xattn_kernel, alpha),
        grid=(S // tm,),
        in_specs=[pl.BlockSpec((tm, D), cur), full(mem), full(wq_b), full(wk_b),
                  full(wv_b), full(wo_b), full(g), full(b), full(wr_hi), full(wr_lo)],
        out_specs=[pl.BlockSpec((tm, D), cur), pl.BlockSpec((tm, D), cur),
                   pl.BlockSpec((tm // LANES, N_EXPERTS, LANES), lambda i: (i, 0, 0))],
        out_shape=[jax.ShapeDtypeStruct((S, D), F32),
                   jax.ShapeDtypeStruct((S, D), BF16),
                   jax.ShapeDtypeStruct((S // LANES, N_EXPERTS, LANES), F32)],
        scratch_shapes=[pltpu.VMEM((M, mw), BF16), pltpu.VMEM((M, mw), BF16)],
        compiler_params=pltpu.CompilerParams(
            dimension_semantics=("arbitrary",), vmem_limit_bytes=VMEM_LIMIT),
        name="xattn",
    )(x1, mem, wq_b, wk_b, wv_b, wo_b, g, b, wr_hi, wr_lo)


def _route_kernel(cap, aff_ref, loc_ref, b_ref, c_ref, incl_scr, tot_scr, sel_scr):
    nblk = aff_ref.shape[0]
    per_tile = ROUTE_TILE // LANES
    ntile = nblk // per_tile

    def count_ge(t):
        ge = jnp.where(aff_ref[...] >= t[None], 1.0, 0.0)
        return jnp.sum(jnp.sum(ge, axis=0), axis=1, keepdims=True)

    def coarse(it, bits):
        cand = bits | jnp.left_shift(jnp.int32(1), 30 - it)
        ok = count_ge(pltpu.bitcast(cand, F32)) >= cap
        return jnp.where(ok, cand, bits)

    def fine(it, lohi):
        lo, hi = lohi
        mid = lo + (hi - lo) * 0.5
        ok = count_ge(mid) >= cap
        return jnp.where(ok, mid, lo), jnp.where(ok, hi, mid)

    bits = lax.fori_loop(0, 31, coarse, jnp.zeros((N_EXPERTS, 1), I32))
    thr, _ = lax.fori_loop(
        0, 30, fine, (pltpu.bitcast(bits, F32), pltpu.bitcast(bits + 1, F32)))
    gt_all = jnp.where(aff_ref[...] > thr[None], 1.0, 0.0)
    n_gt = jnp.sum(jnp.sum(gt_all, axis=0), axis=1, keepdims=True)
    need = cap - n_gt

    ii = lax.broadcasted_iota(I32, (LANES, LANES), 0)
    jj = lax.broadcasted_iota(I32, (LANES, LANES), 1)
    tri = jnp.where(ii <= jj, 1.0, 0.0).astype(BF16)
    ones = jnp.ones((LANES, LANES), BF16)
    flat = (nblk * N_EXPERTS, LANES)
    blocks = (nblk, N_EXPERTS, LANES)

    def block_scans(flags):
        f2 = flags.reshape(flat).astype(BF16)
        incl_scr[...] = jnp.dot(f2, tri, preferred_element_type=F32).reshape(blocks)
        tot_scr[...] = jnp.dot(f2, ones, preferred_element_type=F32).reshape(blocks)

    block_scans(jnp.where(aff_ref[...] == thr[None], 1.0, 0.0))

    def select_block(j, run_eq):
        blk = aff_ref[j]
        tie_rank = run_eq + incl_scr[j]
        sel_scr[j] = (jnp.where(blk > thr, 1.0, 0.0)
                      + jnp.where(blk == thr, 1.0, 0.0) * jnp.where(tie_rank <= need, 1.0, 0.0))
        return run_eq + tot_scr[j]

    zero = jnp.zeros((N_EXPERTS, LANES), F32)
    lax.fori_loop(0, nblk, select_block, zero)
    block_scans(sel_scr[...])

    def tile_body(t, run):
        tile_run = zero
        for j in range(per_tile):
            idx = t * per_tile + j
            sel = sel_scr[idx]
            loc = tile_run + incl_scr[idx] - sel
            loc_ref[idx] = jnp.where(sel > 0.5, loc, -1.0).astype(I32)
            tile_run = tile_run + tot_scr[idx]
        b_ref[t] = run[:, 0:1].astype(I32)
        c_ref[t] = tile_run[:, 0:1].astype(I32)
        return run + tile_run

    lax.fori_loop(0, ntile, tile_body, zero)


def _route(aff3, cap):
    nblk = aff3.shape[0]
    ntile = nblk * LANES // ROUTE_TILE
    full3 = lambda a: pl.BlockSpec(a, lambda: (0, 0, 0))
    return pl.pallas_call(
        functools.partial(_route_kernel, cap),
        in_specs=[full3(aff3.shape)],
        out_specs=[full3(aff3.shape), full3((ntile, N_EXPERTS, 1)),
                   full3((ntile, N_EXPERTS, 1))],
        out_shape=[jax.ShapeDtypeStruct(aff3.shape, I32),
                   jax.ShapeDtypeStruct((ntile, N_EXPERTS, 1), I32),
                   jax.ShapeDtypeStruct((ntile, N_EXPERTS, 1), I32)],
        scratch_shapes=[pltpu.VMEM(aff3.shape, F32)] * 3,
        compiler_params=pltpu.CompilerParams(vmem_limit_bytes=VMEM_LIMIT),
        name="route",
    )(aff3)


def _segment_copies(b_sm, c_sm, t, make_copy, action):
    off = 0
    for e in range(N_EXPERTS):
        c = c_sm[t * N_EXPERTS + e]
        b = b_sm[t * N_EXPERTS + e]

        @pl.when(c > 0)
        def _(e=e, b=b, c=c, off=off):
            action(make_copy(e, b, off, c))

        off = off + c
    return off


def _tile_rows(c_sm, t):
    rows = 0
    for e in range(N_EXPERTS):
        rows = rows + c_sm[t * N_EXPERTS + e]
    return rows


def _dispatch_kernel(cap, b_sm, c_sm, x_ref, loc_ref, xe_hbm, stg, sem):
    step = pl.program_id(0)
    nsteps = pl.num_programs(0)
    per_tile = ROUTE_TILE // LANES
    sub = lax.broadcasted_iota(I32, (SLOT_CHUNK, 1), 0)
    lane = lax.broadcasted_iota(I32, (1, LANES), 1)

    def copy_for(slot_):
        def make(e, dst_row, src_row, n):
            return pltpu.make_async_copy(
                stg.at[slot_, pl.ds(pl.multiple_of(src_row * SUBLANES, SUBLANES), n * SUBLANES)],
                xe_hbm.at[pl.ds(pl.multiple_of((e * cap + dst_row) * SUBLANES, SUBLANES),
                                n * SUBLANES)],
                sem.at[slot_])
        return make

    def wait_tile(t_):
        rows = _tile_rows(c_sm, t_)
        slot_ = t_ % DISPATCH_SLOTS

        @pl.when(rows > 0)
        def _():
            n = rows * SUBLANES
            pltpu.make_async_copy(stg.at[slot_, pl.ds(0, n)], xe_hbm.at[pl.ds(0, n)],
                                  sem.at[slot_]).wait()

    def one_tile(k):
        t = step * DISPATCH_TILES + k
        slot = t % DISPATCH_SLOTS
        tok = slice(k * ROUTE_TILE, (k + 1) * ROUTE_TILE)
        offs = []
        off = 0
        for e in range(N_EXPERTS):
            offs.append(off)
            off = off + c_sm[t * N_EXPERTS + e]
        offs.append(off)
        nchunk = (off + SLOT_CHUNK - 1) // SLOT_CHUNK
        seg_lo = jnp.zeros((1, LANES), I32)
        seg_hi = jnp.zeros((1, LANES), I32)
        for e in range(N_EXPERTS):
            seg_lo = jnp.where(lane == e, offs[e], seg_lo)
            seg_hi = jnp.where(lane == e, offs[e + 1], seg_hi)
        seg_lo_f = seg_lo.astype(F32)
        locf = jnp.concatenate(
            [loc_ref[k * per_tile + j] for j in range(per_tile)], axis=1).astype(F32)
        locb = jnp.concatenate(
            [locf, jnp.zeros((LANES - N_EXPERTS, ROUTE_TILE), F32)], axis=0).astype(BF16)

        def chunk_body(m, carry):
            base = m * SLOT_CHUNK
            srow = sub + base
            inseg = jnp.where(srow >= seg_lo, jnp.where(srow < seg_hi, 1.0, 0.0), 0.0)
            rank = srow.astype(F32) - jnp.sum(inseg * seg_lo_f, axis=1, keepdims=True)
            want = jnp.dot(inseg.astype(BF16), locb, preferred_element_type=F32)
            p = jnp.where(want == rank, 1.0, 0.0)
            res = jnp.dot(p.astype(BF16), x_ref[tok, :], preferred_element_type=F32)
            row0 = (base * SUBLANES if isinstance(m, int)
                    else pl.multiple_of(base * SUBLANES, SUBLANES))
            for s in range(SUBLANES):
                stg[slot, pl.ds(row0 + s, SLOT_CHUNK, stride=SUBLANES), :] = (
                    res[:, s * LANES:(s + 1) * LANES])
            return carry

        for m in range(PEELED_CHUNKS):
            chunk_body(m, 0)
        lax.fori_loop(PEELED_CHUNKS, jnp.maximum(nchunk, PEELED_CHUNKS), chunk_body, 0)

        _segment_copies(b_sm, c_sm, t, copy_for(slot), lambda cp: cp.start())

        @pl.when(t >= DISPATCH_SLOTS - 1)
        def _():
            wait_tile(t - (DISPATCH_SLOTS - 1))

    for k in range(DISPATCH_TILES):
        one_tile(k)

    @pl.when(step == nsteps - 1)
    def _():
        last = nsteps * DISPATCH_TILES - 1
        for d in range(DISPATCH_SLOTS - 2, -1, -1):
            wait_tile(last - d)


def _dispatch(b_flat, c_flat, x2b, loc3, cap):
    S, D = x2b.shape
    tok = DISPATCH_TILES * ROUTE_TILE
    assert S % tok == 0 and S // ROUTE_TILE >= DISPATCH_SLOTS
    max_rows = N_EXPERTS * ROUTE_TILE
    return pl.pallas_call(
        functools.partial(_dispatch_kernel, cap),
        grid_spec=pltpu.PrefetchScalarGridSpec(
            num_scalar_prefetch=2,
            grid=(S // tok,),
            in_specs=[pl.BlockSpec((tok, D), lambda t, b, c: (t, 0)),
                      pl.BlockSpec((tok // LANES, N_EXPERTS, LANES), lambda t, b, c: (t, 0, 0))],
            out_specs=pl.BlockSpec(memory_space=pl.ANY),
            scratch_shapes=[pltpu.VMEM((DISPATCH_SLOTS, max_rows * SUBLANES, LANES), F32),
                            pltpu.SemaphoreType.DMA((DISPATCH_SLOTS,))]),
        out_shape=jax.ShapeDtypeStruct((N_EXPERTS * cap * SUBLANES, LANES), F32),
        compiler_params=pltpu.CompilerParams(
            dimension_semantics=("arbitrary",), vmem_limit_bytes=VMEM_LIMIT),
        name="dispatch",
    )(b_flat, c_flat, x2b, loc3)


def _ffn_kernel(xe_ref, wg_ref, wu_ref, wd_ref, wr_ref, o_ref, xb, acc, gate):
    e = pl.program_id(0)
    f = pl.program_id(2)
    tm = xb.shape[0]

    @pl.when(f == 0)
    def _():
        for s in range(SUBLANES):
            xb[:, s * LANES:(s + 1) * LANES] = (
                xe_ref[pl.ds(s, tm, stride=SUBLANES), :].astype(BF16))
        lg = jnp.dot(xb[...], wr_ref[...], preferred_element_type=F32)
        lane = lax.broadcasted_iota(I32, (1, LANES), 1)
        lg = jnp.where(lane < N_EXPERTS, lg, NEG)
        p = jnp.exp(lg - jnp.max(lg, axis=-1, keepdims=True))
        mine = jnp.sum(jnp.where(lane == e, p, 0.0), axis=-1, keepdims=True)
        gate[...] = mine / jnp.sum(p, axis=-1, keepdims=True)

    def down_proj():
        x = xb[...]
        g = jnp.dot(x, wg_ref[...].astype(BF16), preferred_element_type=F32)
        u = jnp.dot(x, wu_ref[...].astype(BF16), preferred_element_type=F32)
        h = (g * jax.nn.sigmoid(g) * u).astype(BF16)
        return jnp.dot(h, wd_ref[...].astype(BF16), preferred_element_type=F32)

    last = pl.num_programs(2) - 1

    @pl.when(f == 0)
    def _():
        acc[...] = down_proj()

    @pl.when((f > 0) & (f < last))
    def _():
        acc[...] += down_proj()

    @pl.when(f == last)
    def _():
        y = (acc[...] + down_proj()) * gate[...]
        for s in range(SUBLANES):
            o_ref[pl.ds(s, tm, stride=SUBLANES), :] = y[:, s * LANES:(s + 1) * LANES]


def _ffn(xe2d, w_gate, w_up, w_down, wr_pad, cap):
    E, D, FF = w_gate.shape
    tm, tf = FFN_TM, FFN_TF
    mt = cap // tm
    rows = lambda e, m, f: (e * mt + m, 0)
    return pl.pallas_call(
        _ffn_kernel,
        grid=(E, mt, FF // tf),
        in_specs=[pl.BlockSpec((tm * SUBLANES, LANES), rows),
                  pl.BlockSpec((None, D, tf), lambda e, m, f: (e, 0, f)),
                  pl.BlockSpec((None, D, tf), lambda e, m, f: (e, 0, f)),
                  pl.BlockSpec((None, tf, D), lambda e, m, f: (e, f, 0)),
                  pl.BlockSpec(wr_pad.shape, lambda e, m, f: (0, 0))],
        out_specs=pl.BlockSpec((tm * SUBLANES, LANES), rows),
        out_shape=jax.ShapeDtypeStruct(xe2d.shape, F32),
        scratch_shapes=[pltpu.VMEM((tm, D), BF16), pltpu.VMEM((tm, D), F32),
                        pltpu.VMEM((tm, 1), F32)],
        compiler_params=pltpu.CompilerParams(
            dimension_semantics=("arbitrary", "arbitrary", "arbitrary"),
            vmem_limit_bytes=VMEM_LIMIT),
        name="ffn",
    )(xe2d, w_gate, w_up, w_down, wr_pad)


def _combine_kernel(cap, alpha, b_sm, c_sm, x_ref, loc_ref, g_ref, be_ref, ye_hbm,
                    o_ref, ybuf, sem, hl_scr, z_scr):
    step = pl.program_id(0)
    nt = pl.num_programs(0) * COMBINE_TILES
    lane = lax.broadcasted_iota(I32, (1, SLOT_CHUNK), 1)
    sub = lax.broadcasted_iota(I32, (SLOT_CHUNK, 1), 0)
    eid = lax.broadcasted_iota(I32, (N_EXPERTS, 1), 0)

    def copy_for(slot_):
        def make(e, src_row, dst_row, n):
            return pltpu.make_async_copy(
                ye_hbm.at[pl.ds(pl.multiple_of((e * cap + src_row) * SUBLANES, SUBLANES),
                                n * SUBLANES)],
                ybuf.at[slot_, pl.ds(pl.multiple_of(dst_row * SUBLANES, SUBLANES), n * SUBLANES)],
                sem.at[slot_])
        return make

    def fetch(t_):
        _segment_copies(b_sm, c_sm, t_, copy_for(t_ % 2), lambda cp: cp.start())

    @pl.when(step == 0)
    def _():
        ybuf[...] = jnp.zeros_like(ybuf)
        fetch(0)
        fetch(1)

    def one_tile(k):
        t = step * COMBINE_TILES + k
        slot = t % 2
        tok = slice(k * ROUTE_TILE, (k + 1) * ROUTE_TILE)

        rows_in = _tile_rows(c_sm, t)

        @pl.when(rows_in > 0)
        def _():
            n = rows_in * SUBLANES
            pltpu.make_async_copy(ye_hbm.at[pl.ds(0, n)], ybuf.at[slot, pl.ds(0, n)],
                                  sem.at[slot]).wait()

        offs = []
        off = 0
        for e in range(N_EXPERTS):
            offs.append(off)
            off = off + c_sm[t * N_EXPERTS + e]
        offs.append(off)
        total = off
        nchunk = (total + SLOT_CHUNK - 1) // SLOT_CHUNK
        seg_lo = jnp.zeros((N_EXPERTS, 1), I32)
        seg_hi = jnp.zeros((N_EXPERTS, 1), I32)
        for e in range(N_EXPERTS):
            seg_lo = jnp.where(eid == e, offs[e], seg_lo)
            seg_hi = jnp.where(eid == e, offs[e + 1], seg_hi)
        seg_lo_f = seg_lo.astype(F32)
        locb = loc_ref[tok, :].astype(F32).astype(BF16)

        z_scr[...] = alpha * x_ref[tok, :]

        def chunk_body(m, carry):
            base = m * SLOT_CHUNK
            srow = lane + base
            inseg = jnp.where(srow >= seg_lo, jnp.where(srow < seg_hi, 1.0, 0.0), 0.0)
            rank = srow.astype(F32) - jnp.sum(inseg * seg_lo_f, axis=0, keepdims=True)
            want = jnp.dot(locb, inseg.astype(BF16), preferred_element_type=F32)
            ptb = jnp.where(want == rank, 1.0, 0.0).astype(BF16)
            row0 = (base * SUBLANES if isinstance(m, int)
                    else pl.multiple_of(base * SUBLANES, SUBLANES))
            yc = jnp.concatenate(
                [ybuf[slot, pl.ds(row0 + s, SLOT_CHUNK, stride=SUBLANES), :]
                 for s in range(SUBLANES)], axis=1)
            yc = jnp.where(sub < total - base, yc, 0.0)
            hi = yc.astype(BF16)
            hl = hl_scr.at[m % PEELED_CHUNKS]
            hl[0:SLOT_CHUNK] = hi
            hl[SLOT_CHUNK:2 * SLOT_CHUNK] = (yc - hi.astype(F32)).astype(BF16)
            z_scr[...] += jnp.dot(jnp.concatenate([ptb, ptb], axis=1), hl[...],
                                  preferred_element_type=F32)
            return carry

        for m in range(PEELED_CHUNKS):
            chunk_body(m, 0)
        lax.fori_loop(PEELED_CHUNKS, jnp.maximum(nchunk, PEELED_CHUNKS), chunk_body, 0)

        @pl.when(t + 2 < nt)
        def _():
            fetch(t + 2)

        o_ref[tok, :] = _layer_norm(z_scr[...], g_ref[...], be_ref[...])

    for k in range(COMBINE_TILES):
        one_tile(k)


def _combine(b_flat, c_flat, x2, loc_t, g, b, ye2d, cap, alpha):
    S, D = x2.shape
    tok = COMBINE_TILES * ROUTE_TILE
    assert S % tok == 0 and S // ROUTE_TILE >= 2
    max_rows = N_EXPERTS * ROUTE_TILE
    cur = lambda t, b_, c_: (t, 0)
    const2 = lambda t, b_, c_: (0, 0)
    return pl.pallas_call(
        functools.partial(_combine_kernel, cap, alpha),
        grid_spec=pltpu.PrefetchScalarGridSpec(
            num_scalar_prefetch=2,
            grid=(S // tok,),
            in_specs=[pl.BlockSpec((tok, D), cur),
                      pl.BlockSpec((tok, N_EXPERTS), cur),
                      pl.BlockSpec((1, D), const2),
                      pl.BlockSpec((1, D), const2),
                      pl.BlockSpec(memory_space=pl.ANY)],
            out_specs=pl.BlockSpec((tok, D), cur),
            scratch_shapes=[pltpu.VMEM((2, max_rows * SUBLANES, LANES), F32),
                            pltpu.SemaphoreType.DMA((2,)),
                            pltpu.VMEM((PEELED_CHUNKS, 2 * SLOT_CHUNK, D), BF16),
                            pltpu.VMEM((ROUTE_TILE, D), F32)]),
        out_shape=jax.ShapeDtypeStruct((S, D), F32),
        compiler_params=pltpu.CompilerParams(
            dimension_semantics=("arbitrary",), vmem_limit_bytes=VMEM_LIMIT),
        name="combine",
    )(b_flat, c_flat, x2, loc_t, g, b, ye2d)


def _layer(x, mem, w_in, rpb, conv_w, w_mix_out, ln1_g, ln1_b, wq, wk, wv, wo,
           ln2_g, ln2_b, w_router, w_gate, w_up, w_down, ln3_g, ln3_b, alpha):
    S, D = x.shape
    cap = CAPACITY_FACTOR * S // N_EXPERTS
    assert S % TOK_BLOCK == 0 and S % PROJ_TM == 0 and S % XATTN_TM == 0
    assert S // GRID_W >= 3 * ROW_BLOCK
    assert cap % FFN_TM == 0 and cap <= S
    row = lambda a: a.reshape(1, -1)

    q, k, v, bg, u = _proj(x, w_in.astype(BF16))
    x1 = _mixer(x, q, k, v, bg, u, w_mix_out.astype(BF16), rpb,
                conv_w, row(ln1_g), row(ln1_b), alpha)

    wr_t = w_router.T
    wr_hi = wr_t.astype(BF16)
    wr_lo = (wr_t - wr_hi.astype(F32)).astype(BF16)
    x2, x2b, aff3 = _xattn(x1, mem, wq.astype(BF16), wk.astype(BF16), wv.astype(BF16),
                           wo.astype(BF16), row(ln2_g), row(ln2_b), wr_hi, wr_lo, alpha)

    loc3, b3, c3 = _route(aff3, cap)
    b_flat = b3.reshape(-1)
    c_flat = c3.reshape(-1)

    xe2d = _dispatch(b_flat, c_flat, x2b, loc3, cap)
    wr_pad = jnp.pad(w_router.astype(BF16), ((0, 0), (0, LANES - N_EXPERTS)))
    ye2d = _ffn(xe2d, w_gate, w_up, w_down, wr_pad, cap)
    loc_t = loc3.transpose(0, 2, 1).reshape(S, N_EXPERTS)
    return _combine(b_flat, c_flat, x2, loc_t, row(ln3_g), row(ln3_b), ye2d, cap, alpha)


def kernel(x, mem, w_in, na_rpb, conv_w, w_mix_out, ln1_g, ln1_b, w_mem_q, w_mem_k,
           w_mem_v, w_mem_out, ln2_g, ln2_b, w_router, w_exp_gate, w_exp_up,
           w_exp_down, ln3_g, ln3_b):
    depth = w_in.shape[0]
    alpha = (2 * depth) ** 0.25
    outs = []
    for bi in range(x.shape[0]):
        xb = x[bi]
        for l in range(depth):
            xb = _layer(xb, mem[bi], w_in[l], na_rpb[l], conv_w[l], w_mix_out[l],
                        ln1_g[l], ln1_b[l], w_mem_q[l], w_mem_k[l], w_mem_v[l],
                        w_mem_out[l], ln2_g[l], ln2_b[l], w_router[l], w_exp_gate[l],
                        w_exp_up[l], w_exp_down[l], ln3_g[l], ln3_b[l], alpha)
        outs.append(xb)
    return jnp.stack(outs)
```

```python
import functools

import jax
import jax.numpy as jnp
from jax import lax
from jax.experimental import pallas as pl
from jax.experimental.pallas import tpu as pltpu

F32 = jnp.float32
BF16 = jnp.bfloat16
I32 = jnp.int32

GRID_W = 64
HEAD_DIM = 64
NA_HEADS = 8
NA_WIDTH = NA_HEADS * HEAD_DIM
WIN_ROWS = 8
WIN_COLS = 16
MEM_HEADS = 4
MEM_HEAD_DIM = 128
N_EXPERTS = 16
CAPACITY_FACTOR = 2
LN_EPS = 1e-5
NEG = -1e30

LANES = 128
SUBLANES = 8
VMEM_LIMIT = 56 * 1024 * 1024

ROW_BLOCK = 8
TOK_BLOCK = ROW_BLOCK * GRID_W
PROJ_TM = 1024
XATTN_TM = 1024
ROUTE_TILE = 256
SLOT_CHUNK = 384
SM_ROWS = 16
PEELED_CHUNKS = 2
DISPATCH_TILES = 4
COMBINE_TILES = 2
DISPATCH_SLOTS = 3
FFN_TM = 1024
FFN_TF = 512


def _layer_norm(z, g, b):
    mu = jnp.mean(z, axis=-1, keepdims=True)
    zc = z - mu
    var = jnp.mean(zc * zc, axis=-1, keepdims=True)
    return zc * lax.rsqrt(var + LN_EPS) * g + b


def _dot_nt(a, b):
    return lax.dot_general(a, b, (((1,), (1,)), ((), ())),
                           preferred_element_type=F32)


def _proj_kernel(x_ref, w_ref, q_ref, k_ref, v_ref, b_ref, u_ref):
    xb = x_ref[...].astype(BF16)
    nw = NA_WIDTH

    def mm(c0):
        return jnp.dot(xb, w_ref[:, c0:c0 + nw], preferred_element_type=F32)

    q_ref[...] = (mm(0) * (HEAD_DIM ** -0.5)).astype(BF16)
    k_ref[...] = mm(nw).astype(BF16)
    v_ref[...] = mm(2 * nw).astype(BF16)
    b_ref[...] = mm(3 * nw)
    u_ref[...] = mm(4 * nw) * mm(5 * nw)


def _proj(x, w_in_b):
    S, D = x.shape
    nw = NA_WIDTH
    tm = PROJ_TM
    blk = lambda i: (i, 0)
    return pl.pallas_call(
        _proj_kernel,
        grid=(S // tm,),
        in_specs=[pl.BlockSpec((tm, D), blk),
                  pl.BlockSpec((D, 6 * nw), lambda i: (0, 0))],
        out_specs=[pl.BlockSpec((tm, nw), blk)] * 5,
        out_shape=[jax.ShapeDtypeStruct((S, nw), BF16)] * 3
        + [jax.ShapeDtypeStruct((S, nw), F32)] * 2,
        compiler_params=pltpu.CompilerParams(
            dimension_semantics=("arbitrary",), vmem_limit_bytes=VMEM_LIMIT),
        name="proj",
    )(x, w_in_b)


def _build_na_bias(rpb_ref, bias_ref):
    shape = (GRID_W, LANES)
    qc = lax.broadcasted_iota(I32, shape, 0)
    ln = lax.broadcasted_iota(I32, shape, 1)
    kc = ln % GRID_W
    c_start = jnp.clip(qc - WIN_COLS // 2, 0, GRID_W - WIN_COLS)
    base = LANES - (WIN_COLS - 1)

    def canvas(h, d):
        both = rpb_ref[h, d:d + 1, :] + pltpu.roll(rpb_ref[h, d + 1:d + 2, :], GRID_W, 1)
        t = pltpu.roll(pltpu.roll(jnp.broadcast_to(both, shape), base, 1), 0, 1,
                       stride=1, stride_axis=0)
        return jnp.where(kc >= c_start, jnp.where(kc < c_start + WIN_COLS, t, NEG), NEG)

    for par in range(2):
        for hp in range(NA_HEADS // 2):
            for m in range(WIN_ROWS):
                d = 2 * m + par
                bias_ref[par, hp, m, 0:GRID_W] = canvas(2 * hp, d)
                bias_ref[par, hp, m, GRID_W:2 * GRID_W] = canvas(2 * hp + 1, d)


def _kv_window_start(i, rows):
    return jnp.clip((i - 1) * ROW_BLOCK, 0, rows - 3 * ROW_BLOCK)


def _mixer_kernel(rows, alpha,
                  q_ref, kbuf, vbuf,
                  b_ref, u_ref, up_ref, un_ref, x_ref, wout_ref, rpb_ref,
                  cw_ref, g_ref, be_ref, o_ref, mix, bias_ref,
                  s_scr0, s_scr1, p_scr0, p_scr1):
    i = pl.program_id(0)
    nb = pl.num_programs(0)
    tb = TOK_BLOCK
    win = WIN_ROWS * GRID_W

    @pl.when(i == 0)
    def _():
        _build_na_bias(rpb_ref, bias_ref)

    lane = lax.broadcasted_iota(I32, (1, LANES), 1)
    first_head = lane < HEAD_DIM
    m_lo = jnp.where(first_head, 1.0, 0.0).astype(BF16)
    m_hi = jnp.where(first_head, 0.0, 1.0).astype(BF16)

    def window(r):
        grow = i * ROW_BLOCK + r
        start = jnp.clip(grow - WIN_ROWS // 2, 0, rows - WIN_ROWS)
        w0 = pl.multiple_of((start - _kv_window_start(i, rows)) * GRID_W, GRID_W)
        d0 = start - grow + WIN_ROWS
        return w0, d0 % 2, d0 // 2

    def scores(r, s_out):
        w0, _, _ = window(r)
        q0 = r * GRID_W if isinstance(r, int) else pl.multiple_of(r * GRID_W, GRID_W)
        for hp in range(NA_HEADS // 2):
            cs = slice(hp * LANES, (hp + 1) * LANES)
            qp = q_ref[pl.ds(q0, GRID_W), cs]
            lhs = jnp.concatenate([qp * m_lo, qp * m_hi], axis=0)
            kw = kbuf[pl.ds(w0, win), cs]
            s_out[hp] = _dot_nt(lhs, kw)

    def probs(r, s_in, p_out):
        _, par, m0 = window(r)
        for hp in range(NA_HEADS // 2):
            for rb in range(2 * GRID_W // SM_ROWS):
                rs = slice(rb * SM_ROWS, (rb + 1) * SM_ROWS)
                bias = jnp.concatenate(
                    [bias_ref[par, hp, m0 + j, rs, :] for j in range(WIN_ROWS // 2)], axis=1)
                s = s_in[hp, rs, :] + bias
                p = jnp.exp(s - jnp.max(s, axis=-1, keepdims=True))
                p = p * (1.0 / jnp.sum(p, axis=-1, keepdims=True))
                p_out[hp, rs, :] = p.astype(BF16)

    def values(r, p_in):
        w0, _, _ = window(r)
        q0 = r * GRID_W if isinstance(r, int) else pl.multiple_of(r * GRID_W, GRID_W)
        for hp in range(NA_HEADS // 2):
            cs = slice(hp * LANES, (hp + 1) * LANES)
            vw = vbuf[pl.ds(w0, win), cs]
            o2 = jnp.dot(p_in[hp], vw, preferred_element_type=F32)
            y = jnp.where(first_head, o2[0:GRID_W], o2[GRID_W:2 * GRID_W])
            mix[pl.ds(q0, GRID_W), cs] = y.astype(BF16)

    s_buf = (s_scr0, s_scr1)
    p_buf = (p_scr0, p_scr1)

    def steady(k, carry):
        for half in range(2):
            r = 1 + 2 * k + half
            values(r - 1, p_buf[half])
            probs(r, s_buf[1 - half], p_buf[1 - half])
            scores(r + 1, s_buf[half])
        return carry

    scores(0, s_buf[0])
    probs(0, s_buf[0], p_buf[0])
    scores(1, s_buf[1])
    for kk in range((ROW_BLOCK - 2) // 2):
        steady(kk, 0)
    values(ROW_BLOCK - 2, p_buf[0])
    probs(ROW_BLOCK - 1, s_buf[1], p_buf[1])
    values(ROW_BLOCK - 1, p_buf[1])

    u = u_ref[...]
    prev_row = up_ref[SUBLANES - 1:SUBLANES, :] * jnp.where(i > 0, 1.0, 0.0)
    next_row = un_ref[0:1, :] * jnp.where(i < nb - 1, 1.0, 0.0)
    rid = lax.broadcasted_iota(I32, (tb, 1), 0)
    um1 = jnp.where(rid == 0, prev_row, pltpu.roll(u, 1, axis=0))
    up1 = jnp.where(rid == tb - 1, next_row, pltpu.roll(u, tb - 1, axis=0))
    cw = cw_ref[...]
    yc = b_ref[...] * (cw[0:1] * um1 + cw[1:2] * u + cw[2:3] * up1)
    mix[:, NA_WIDTH:] = yc.astype(BF16)

    y = jnp.dot(mix[...], wout_ref[...], preferred_element_type=F32)
    z = alpha * x_ref[...] + y
    o_ref[...] = _layer_norm(z, g_ref[...], be_ref[...])


def _mixer(x, q, k, v, bg, u, w_out_b, rpb, conv_w, g, b, alpha):
    S, D = x.shape
    rpb_pad = jnp.pad(rpb.astype(F32), ((0, 0), (1, 1), (0, LANES - rpb.shape[2])))
    nw = NA_WIDTH
    tb = TOK_BLOCK
    nb = S // tb
    rows = S // GRID_W
    cur = lambda i: (i, 0)
    kv_win = pl.BlockSpec((pl.Element(3 * tb), pl.Element(nw)),
                          lambda i: (_kv_window_start(i, rows) * GRID_W, 0))
    halo = tb // SUBLANES
    hprev = lambda i: (jnp.maximum(i * halo - 1, 0), 0)
    hnext = lambda i: (jnp.minimum((i + 1) * halo, S // SUBLANES - 1), 0)
    const2 = lambda i: (0, 0)
    kv = lambda im: pl.BlockSpec((tb, nw), im)
    return pl.pallas_call(
        functools.partial(_mixer_kernel, rows, alpha),
        grid=(nb,),
        in_specs=[kv(cur), kv_win, kv_win,
                  kv(cur), kv(cur),
                  pl.BlockSpec((SUBLANES, nw), hprev),
                  pl.BlockSpec((SUBLANES, nw), hnext),
                  pl.BlockSpec((tb, D), cur),
                  pl.BlockSpec((D, D), const2),
                  pl.BlockSpec(rpb_pad.shape, lambda i: (0, 0, 0)),
                  pl.BlockSpec(conv_w.shape, const2),
                  pl.BlockSpec((1, D), const2),
                  pl.BlockSpec((1, D), const2)],
        out_specs=pl.BlockSpec((tb, D), cur),
        out_shape=jax.ShapeDtypeStruct((S, D), F32),
        scratch_shapes=[pltpu.VMEM((tb, D), BF16),
                        pltpu.VMEM((2, NA_HEADS // 2, WIN_ROWS, 2 * GRID_W, LANES), F32),
                        pltpu.VMEM((NA_HEADS // 2, 2 * GRID_W, WIN_ROWS * GRID_W), F32),
                        pltpu.VMEM((NA_HEADS // 2, 2 * GRID_W, WIN_ROWS * GRID_W), F32),
                        pltpu.VMEM((NA_HEADS // 2, 2 * GRID_W, WIN_ROWS * GRID_W), BF16),
                        pltpu.VMEM((NA_HEADS // 2, 2 * GRID_W, WIN_ROWS * GRID_W), BF16)],
        compiler_params=pltpu.CompilerParams(
            dimension_semantics=("arbitrary",), vmem_limit_bytes=VMEM_LIMIT),
        name="mixer",
    )(q, k, v, bg, u, u, u, x, w_out_b, rpb_pad, conv_w, g, b)


def _xattn_kernel(alpha, x_ref, mem_ref, wq_ref, wk_ref, wv_ref, wo_ref,
                  g_ref, be_ref, wrh_ref, wrl_ref,
                  x2_ref, x2b_ref, aff_ref, kmem, vmem):
    i = pl.program_id(0)

    @pl.when(i == 0)
    def _():
        mb = mem_ref[...].astype(BF16)
        kmem[...] = jnp.dot(mb, wk_ref[...], preferred_element_type=F32).astype(BF16)
        vmem[...] = jnp.dot(mb, wv_ref[...], preferred_element_type=F32).astype(BF16)

    x1 = x_ref[...]
    qb = jnp.dot(x1.astype(BF16), wq_ref[...], preferred_element_type=F32).astype(BF16)
    outs = []
    for h in range(MEM_HEADS):
        cs = slice(h * MEM_HEAD_DIM, (h + 1) * MEM_HEAD_DIM)
        s = _dot_nt(qb[:, cs], kmem[:, cs]) * (MEM_HEAD_DIM ** -0.5)
        mx = jnp.max(s, axis=-1, keepdims=True)
        p = jnp.exp(s - mx)
        l = jnp.sum(p, axis=-1, keepdims=True)
        oh = jnp.dot(p.astype(BF16), vmem[:, cs], preferred_element_type=F32)
        outs.append(oh / l)
    o = jnp.concatenate(outs, axis=1).astype(BF16)
    y = jnp.dot(o, wo_ref[...], preferred_element_type=F32)
    x2 = _layer_norm(alpha * x1 + y, g_ref[...], be_ref[...])
    x2_ref[...] = x2
    hi = x2.astype(BF16)
    x2b_ref[...] = hi
    lo = (x2 - hi.astype(F32)).astype(BF16)
    wrh = wrh_ref[...]
    r1 = _dot_nt(jnp.concatenate([wrh, wrl_ref[...]], axis=0), hi)
    lg = r1[0:N_EXPERTS] + r1[N_EXPERTS:2 * N_EXPERTS] + _dot_nt(wrh, lo)
    mx = jnp.max(lg, axis=0, keepdims=True)
    p = jnp.exp(lg - mx)
    aff = p / jnp.sum(p, axis=0, keepdims=True)
    for j in range(aff_ref.shape[0]):
        aff_ref[j] = aff[:, j * LANES:(j + 1) * LANES]


def _xattn(x1, mem, wq_b, wk_b, wv_b, wo_b, g, b, wr_hi, wr_lo, alpha):
    S, D = x1.shape
    M = mem.shape[0]
    mw = MEM_HEADS * MEM_HEAD_DIM
    tm = XATTN_TM
    cur = lambda i: (i, 0)
    const2 = lambda i: (0, 0)
    full = lambda a: pl.BlockSpec(a.shape, const2)
    return pl.pallas_call(
        functools.partial(_xattn_kernel, alpha),
        grid=(S // tm,),
        in_specs=[pl.BlockSpec((tm, D), cur), full(mem), full(wq_b), full(wk_b),
                  full(wv_b), full(wo_b), full(g), full(b), full(wr_hi), full(wr_lo)],
        out_specs=[pl.BlockSpec((tm, D), cur), pl.BlockSpec((tm, D), cur),
                   pl.BlockSpec((tm // LANES, N_EXPERTS, LANES), lambda i: (i, 0, 0))],
        out_shape=[jax.ShapeDtypeStruct((S, D), F32),
                   jax.ShapeDtypeStruct((S, D), BF16),
                   jax.ShapeDtypeStruct((S // LANES, N_EXPERTS, LANES), F32)],
        scratch_shapes=[pltpu.VMEM((M, mw), BF16), pltpu.VMEM((M, mw), BF16)],
        compiler_params=pltpu.CompilerParams(
            dimension_semantics=("arbitrary",), vmem_limit_bytes=VMEM_LIMIT),
        name="xattn",
    )(x1, mem, wq_b, wk_b, wv_b, wo_b, g, b, wr_hi, wr_lo)


def _route_kernel(cap, aff_ref, loc_ref, b_ref, c_ref, incl_scr, tot_scr, sel_scr):
    nblk = aff_ref.shape[0]
    per_tile = ROUTE_TILE // LANES
    ntile = nblk // per_tile

    def count_ge(t):
        ge = jnp.where(aff_ref[...] >= t[None], 1.0, 0.0)
        return jnp.sum(jnp.sum(ge, axis=0), axis=1, keepdims=True)

    def coarse(it, bits):
        cand = bits | jnp.left_shift(jnp.int32(1), 30 - it)
        ok = count_ge(pltpu.bitcast(cand, F32)) >= cap
        return jnp.where(ok, cand, bits)

    def fine(it, lohi):
        lo, hi = lohi
        mid = lo + (hi - lo) * 0.5
        ok = count_ge(mid) >= cap
        return jnp.where(ok, mid, lo), jnp.where(ok, hi, mid)

    bits = lax.fori_loop(0, 31, coarse, jnp.zeros((N_EXPERTS, 1), I32))
    thr, _ = lax.fori_loop(
        0, 30, fine, (pltpu.bitcast(bits, F32), pltpu.bitcast(bits + 1, F32)))
    gt_all = jnp.where(aff_ref[...] > thr[None], 1.0, 0.0)
    n_gt = jnp.sum(jnp.sum(gt_all, axis=0), axis=1, keepdims=True)
    need = cap - n_gt

    ii = lax.broadcasted_iota(I32, (LANES, LANES), 0)
    jj = lax.broadcasted_iota(I32, (LANES, LANES), 1)
    tri = jnp.where(ii <= jj, 1.0, 0.0).astype(BF16)
    ones = jnp.ones((LANES, LANES), BF16)
    flat = (nblk * N_EXPERTS, LANES)
    blocks = (nblk, N_EXPERTS, LANES)

    def block_scans(flags):
        f2 = flags.reshape(flat).astype(BF16)
        incl_scr[...] = jnp.dot(f2, tri, preferred_element_type=F32).reshape(blocks)
        tot_scr[...] = jnp.dot(f2, ones, preferred_element_type=F32).reshape(blocks)

    block_scans(jnp.where(aff_ref[...] == thr[None], 1.0, 0.0))

    def select_block(j, run_eq):
        blk = aff_ref[j]
        tie_rank = run_eq + incl_scr[j]
        sel_scr[j] = (jnp.where(blk > thr, 1.0, 0.0)
                      + jnp.where(blk == thr, 1.0, 0.0) * jnp.where(tie_rank <= need, 1.0, 0.0))
        return run_eq + tot_scr[j]

    zero = jnp.zeros((N_EXPERTS, LANES), F32)
    lax.fori_loop(0, nblk, select_block, zero)
    block_scans(sel_scr[...])

    def tile_body(t, run):
        tile_run = zero
        for j in range(per_tile):
            idx = t * per_tile + j
            sel = sel_scr[idx]
            loc = tile_run + incl_scr[idx] - sel
            loc_ref[idx] = jnp.where(sel > 0.5, loc, -1.0).astype(I32)
            tile_run = tile_run + tot_scr[idx]
        b_ref[t] = run[:, 0:1].astype(I32)
        c_ref[t] = tile_run[:, 0:1].astype(I32)
        return run + tile_run

    lax.fori_loop(0, ntile, tile_body, zero)


def _route(aff3, cap):
    nblk = aff3.shape[0]
    ntile = nblk * LANES // ROUTE_TILE
    full3 = lambda a: pl.BlockSpec(a, lambda: (0, 0, 0))
    return pl.pallas_call(
        functools.partial(_route_kernel, cap),
        in_specs=[full3(aff3.shape)],
        out_specs=[full3(aff3.shape), full3((ntile, N_EXPERTS, 1)),
                   full3((ntile, N_EXPERTS, 1))],
        out_shape=[jax.ShapeDtypeStruct(aff3.shape, I32),
                   jax.ShapeDtypeStruct((ntile, N_EXPERTS, 1), I32),
                   jax.ShapeDtypeStruct((ntile, N_EXPERTS, 1), I32)],
        scratch_shapes=[pltpu.VMEM(aff3.shape, F32)] * 3,
        compiler_params=pltpu.CompilerParams(vmem_limit_bytes=VMEM_LIMIT),
        name="route",
    )(aff3)


def _segment_copies(b_sm, c_sm, t, make_copy, action):
    off = 0
    for e in range(N_EXPERTS):
        c = c_sm[t * N_EXPERTS + e]
        b = b_sm[t * N_EXPERTS + e]

        @pl.when(c > 0)
        def _(e=e, b=b, c=c, off=off):
            action(make_copy(e, b, off, c))

        off = off + c
    return off


def _tile_rows(c_sm, t):
    rows = 0
    for e in range(N_EXPERTS):
        rows = rows + c_sm[t * N_EXPERTS + e]
    return rows


def _dispatch_kernel(cap, b_sm, c_sm, x_ref, loc_ref, xe_hbm, stg, sem):
    step = pl.program_id(0)
    nsteps = pl.num_programs(0)
    per_tile = ROUTE_TILE // LANES
    sub = lax.broadcasted_iota(I32, (SLOT_CHUNK, 1), 0)
    lane = lax.broadcasted_iota(I32, (1, LANES), 1)

    def copy_for(slot_):
        def make(e, dst_row, src_row, n):
            return pltpu.make_async_copy(
                stg.at[slot_, pl.ds(pl.multiple_of(src_row * SUBLANES, SUBLANES), n * SUBLANES)],
                xe_hbm.at[pl.ds(pl.multiple_of((e * cap + dst_row) * SUBLANES, SUBLANES),
                                n * SUBLANES)],
                sem.at[slot_])
        return make

    def wait_tile(t_):
        rows = _tile_rows(c_sm, t_)
        slot_ = t_ % DISPATCH_SLOTS

        @pl.when(rows > 0)
        def _():
            n = rows * SUBLANES
            pltpu.make_async_copy(stg.at[slot_, pl.ds(0, n)], xe_hbm.at[pl.ds(0, n)],
                                  sem.at[slot_]).wait()

    def one_tile(k):
        t = step * DISPATCH_TILES + k
        slot = t % DISPATCH_SLOTS
        tok = slice(k * ROUTE_TILE, (k + 1) * ROUTE_TILE)
        offs = []
        off = 0
        for e in range(N_EXPERTS):
            offs.append(off)
            off = off + c_sm[t * N_EXPERTS + e]
        offs.append(off)
        nchunk = (off + SLOT_CHUNK - 1) // SLOT_CHUNK
        seg_lo = jnp.zeros((1, LANES), I32)
        seg_hi = jnp.zeros((1, LANES), I32)
        for e in range(N_EXPERTS):
            seg_lo = jnp.where(lane == e, offs[e], seg_lo)
            seg_hi = jnp.where(lane == e, offs[e + 1], seg_hi)
        seg_lo_f = seg_lo.astype(F32)
        locf = jnp.concatenate(
            [loc_ref[k * per_tile + j] for j in range(per_tile)], axis=1).astype(F32)
        locb = jnp.concatenate(
            [locf, jnp.zeros((LANES - N_EXPERTS, ROUTE_TILE), F32)], axis=0).astype(BF16)

        def chunk_body(m, carry):
            base = m * SLOT_CHUNK
            srow = sub + base
            inseg = jnp.where(srow >= seg_lo, jnp.where(srow < seg_hi, 1.0, 0.0), 0.0)
            rank = srow.astype(F32) - jnp.sum(inseg * seg_lo_f, axis=1, keepdims=True)
            want = jnp.dot(inseg.astype(BF16), locb, preferred_element_type=F32)
            p = jnp.where(want == rank, 1.0, 0.0)
            res = jnp.dot(p.astype(BF16), x_ref[tok, :], preferred_element_type=F32)
            row0 = (base * SUBLANES if isinstance(m, int)
                    else pl.multiple_of(base * SUBLANES, SUBLANES))
            for s in range(SUBLANES):
                stg[slot, pl.ds(row0 + s, SLOT_CHUNK, stride=SUBLANES), :] = (
                    res[:, s * LANES:(s + 1) * LANES])
            return carry

        for m in range(PEELED_CHUNKS):
            chunk_body(m, 0)
        lax.fori_loop(PEELED_CHUNKS, jnp.maximum(nchunk, PEELED_CHUNKS), chunk_body, 0)

        _segment_copies(b_sm, c_sm, t, copy_for(slot), lambda cp: cp.start())

        @pl.when(t >= DISPATCH_SLOTS - 1)
        def _():
            wait_tile(t - (DISPATCH_SLOTS - 1))

    for k in range(DISPATCH_TILES):
        one_tile(k)

    @pl.when(step == nsteps - 1)
    def _():
        last = nsteps * DISPATCH_TILES - 1
        for d in range(DISPATCH_SLOTS - 2, -1, -1):
            wait_tile(last - d)


def _dispatch(b_flat, c_flat, x2b, loc3, cap):
    S, D = x2b.shape
    tok = DISPATCH_TILES * ROUTE_TILE
    assert S % tok == 0 and S // ROUTE_TILE >= DISPATCH_SLOTS
    max_rows = pl.cdiv(N_EXPERTS * ROUTE_TILE, SLOT_CHUNK) * SLOT_CHUNK
    return pl.pallas_call(
        functools.partial(_dispatch_kernel, cap),
        grid_spec=pltpu.PrefetchScalarGridSpec(
            num_scalar_prefetch=2,
            grid=(S // tok,),
            in_specs=[pl.BlockSpec((tok, D), lambda t, b, c: (t, 0)),
                      pl.BlockSpec((tok // LANES, N_EXPERTS, LANES), lambda t, b, c: (t, 0, 0))],
            out_specs=pl.BlockSpec(memory_space=pl.ANY),
            scratch_shapes=[pltpu.VMEM((DISPATCH_SLOTS, max_rows * SUBLANES, LANES), F32),
                            pltpu.SemaphoreType.DMA((DISPATCH_SLOTS,))]),
        out_shape=jax.ShapeDtypeStruct((N_EXPERTS * cap * SUBLANES, LANES), F32),
        compiler_params=pltpu.CompilerParams(
            dimension_semantics=("arbitrary",), vmem_limit_bytes=VMEM_LIMIT),
        name="dispatch",
    )(b_flat, c_flat, x2b, loc3)


def _ffn_kernel(xe_ref, wg_ref, wu_ref, wd_ref, wr_ref, o_ref, xb, acc, gate):
    e = pl.program_id(0)
    f = pl.program_id(2)
    tm = xb.shape[0]

    @pl.when(f == 0)
    def _():
        for s in range(SUBLANES):
            xb[:, s * LANES:(s + 1) * LANES] = (
                xe_ref[pl.ds(s, tm, stride=SUBLANES), :].astype(BF16))
        lg = jnp.dot(xb[...], wr_ref[...], preferred_element_type=F32)
        lane = lax.broadcasted_iota(I32, (1, LANES), 1)
        lg = jnp.where(lane < N_EXPERTS, lg, NEG)
        p = jnp.exp(lg - jnp.max(lg, axis=-1, keepdims=True))
        mine = jnp.sum(jnp.where(lane == e, p, 0.0), axis=-1, keepdims=True)
        gate[...] = mine / jnp.sum(p, axis=-1, keepdims=True)

    def down_proj():
        x = xb[...]
        g = jnp.dot(x, wg_ref[...].astype(BF16), preferred_element_type=F32)
        u = jnp.dot(x, wu_ref[...].astype(BF16), preferred_element_type=F32)
        h = (g * jax.nn.sigmoid(g) * u).astype(BF16)
        return jnp.dot(h, wd_ref[...].astype(BF16), preferred_element_type=F32)

    last = pl.num_programs(2) - 1

    @pl.when(f == 0)
    def _():
        acc[...] = down_proj()

    @pl.when((f > 0) & (f < last))
    def _():
        acc[...] += down_proj()

    @pl.when(f == last)
    def _():
        y = (acc[...] + down_proj()) * gate[...]
        for s in range(SUBLANES):
            o_ref[pl.ds(s, tm, stride=SUBLANES), :] = y[:, s * LANES:(s + 1) * LANES]


def _ffn(xe2d, w_gate, w_up, w_down, wr_pad, cap):
    E, D, FF = w_gate.shape
    tm, tf = FFN_TM, FFN_TF
    mt = cap // tm
    rows = lambda e, m, f: (e * mt + m, 0)
    return pl.pallas_call(
        _ffn_kernel,
        grid=(E, mt, FF // tf),
        in_specs=[pl.BlockSpec((tm * SUBLANES, LANES), rows),
                  pl.BlockSpec((None, D, tf), lambda e, m, f: (e, 0, f)),
                  pl.BlockSpec((None, D, tf), lambda e, m, f: (e, 0, f)),
                  pl.BlockSpec((None, tf, D), lambda e, m, f: (e, f, 0)),
                  pl.BlockSpec(wr_pad.shape, lambda e, m, f: (0, 0))],
        out_specs=pl.BlockSpec((tm * SUBLANES, LANES), rows),
        out_shape=jax.ShapeDtypeStruct(xe2d.shape, F32),
        scratch_shapes=[pltpu.VMEM((tm, D), BF16), pltpu.VMEM((tm, D), F32),
                        pltpu.VMEM((tm, 1), F32)],
        compiler_params=pltpu.CompilerParams(
            dimension_semantics=("arbitrary", "arbitrary", "arbitrary"),
            vmem_limit_bytes=VMEM_LIMIT),
        name="ffn",
    )(xe2d, w_gate, w_up, w_down, wr_pad)


def _combine_kernel(cap, alpha, b_sm, c_sm, x_ref, loc_ref, g_ref, be_ref, ye_hbm,
                    o_ref, ybuf, sem, hl_scr, z_scr):
    step = pl.program_id(0)
    nt = pl.num_programs(0) * COMBINE_TILES
    lane = lax.broadcasted_iota(I32, (1, SLOT_CHUNK), 1)
    sub = lax.broadcasted_iota(I32, (SLOT_CHUNK, 1), 0)
    eid = lax.broadcasted_iota(I32, (N_EXPERTS, 1), 0)

    def copy_for(slot_):
        def make(e, src_row, dst_row, n):
            return pltpu.make_async_copy(
                ye_hbm.at[pl.ds(pl.multiple_of((e * cap + src_row) * SUBLANES, SUBLANES),
                                n * SUBLANES)],
                ybuf.at[slot_, pl.ds(pl.multiple_of(dst_row * SUBLANES, SUBLANES), n * SUBLANES)],
                sem.at[slot_])
        return make

    def fetch(t_):
        _segment_copies(b_sm, c_sm, t_, copy_for(t_ % 2), lambda cp: cp.start())

    @pl.when(step == 0)
    def _():
        ybuf[...] = jnp.zeros_like(ybuf)
        fetch(0)
        fetch(1)

    def one_tile(k):
        t = step * COMBINE_TILES + k
        slot = t % 2
        tok = slice(k * ROUTE_TILE, (k + 1) * ROUTE_TILE)

        rows_in = _tile_rows(c_sm, t)

        @pl.when(rows_in > 0)
        def _():
            n = rows_in * SUBLANES
            pltpu.make_async_copy(ye_hbm.at[pl.ds(0, n)], ybuf.at[slot, pl.ds(0, n)],
                                  sem.at[slot]).wait()

        offs = []
        off = 0
        for e in range(N_EXPERTS):
            offs.append(off)
            off = off + c_sm[t * N_EXPERTS + e]
        offs.append(off)
        total = off
        nchunk = (total + SLOT_CHUNK - 1) // SLOT_CHUNK
        seg_lo = jnp.zeros((N_EXPERTS, 1), I32)
        seg_hi = jnp.zeros((N_EXPERTS, 1), I32)
        for e in range(N_EXPERTS):
            seg_lo = jnp.where(eid == e, offs[e], seg_lo)
            seg_hi = jnp.where(eid == e, offs[e + 1], seg_hi)
        seg_lo_f = seg_lo.astype(F32)
        locb = loc_ref[tok, :].astype(F32).astype(BF16)

        z_scr[...] = alpha * x_ref[tok, :]

        def chunk_body(m, carry):
            base = m * SLOT_CHUNK
            srow = lane + base
            inseg = jnp.where(srow >= seg_lo, jnp.where(srow < seg_hi, 1.0, 0.0), 0.0)
            rank = srow.astype(F32) - jnp.sum(inseg * seg_lo_f, axis=0, keepdims=True)
            want = jnp.dot(locb, inseg.astype(BF16), preferred_element_type=F32)
            ptb = jnp.where(want == rank, 1.0, 0.0).astype(BF16)
            row0 = (base * SUBLANES if isinstance(m, int)
                    else pl.multiple_of(base * SUBLANES, SUBLANES))
            yc = jnp.concatenate(
                [ybuf[slot, pl.ds(row0 + s, SLOT_CHUNK, stride=SUBLANES), :]
                 for s in range(SUBLANES)], axis=1)
            yc = jnp.where(sub < total - base, yc, 0.0)
            hi = yc.astype(BF16)
            hl = hl_scr.at[m % PEELED_CHUNKS]
            hl[0:SLOT_CHUNK] = hi
            hl[SLOT_CHUNK:2 * SLOT_CHUNK] = (yc - hi.astype(F32)).astype(BF16)
            z_scr[...] += jnp.dot(jnp.concatenate([ptb, ptb], axis=1), hl[...],
                                  preferred_element_type=F32)
            return carry

        for m in range(PEELED_CHUNKS):
            chunk_body(m, 0)
        lax.fori_loop(PEELED_CHUNKS, jnp.maximum(nchunk, PEELED_CHUNKS), chunk_body, 0)

        @pl.when(t + 2 < nt)
        def _():
            fetch(t + 2)

        o_ref[tok, :] = _layer_norm(z_scr[...], g_ref[...], be_ref[...])

    for k in range(COMBINE_TILES):
        one_tile(k)


def _combine(b_flat, c_flat, x2, loc_t, g, b, ye2d, cap, alpha):
    S, D = x2.shape
    tok = COMBINE_TILES * ROUTE_TILE
    assert S % tok == 0 and S // ROUTE_TILE >= 2
    max_rows = pl.cdiv(N_EXPERTS * ROUTE_TILE, SLOT_CHUNK) * SLOT_CHUNK
    cur = lambda t, b_, c_: (t, 0)
    const2 = lambda t, b_, c_: (0, 0)
    return pl.pallas_call(
        functools.partial(_combine_kernel, cap, alpha),
        grid_spec=pltpu.PrefetchScalarGridSpec(
            num_scalar_prefetch=2,
            grid=(S // tok,),
            in_specs=[pl.BlockSpec((tok, D), cur),
                      pl.BlockSpec((tok, N_EXPERTS), cur),
                      pl.BlockSpec((1, D), const2),
                      pl.BlockSpec((1, D), const2),
                      pl.BlockSpec(memory_space=pl.ANY)],
            out_specs=pl.BlockSpec((tok, D), cur),
            scratch_shapes=[pltpu.VMEM((2, max_rows * SUBLANES, LANES), F32),
                            pltpu.SemaphoreType.DMA((2,)),
                            pltpu.VMEM((PEELED_CHUNKS, 2 * SLOT_CHUNK, D), BF16),
                            pltpu.VMEM((ROUTE_TILE, D), F32)]),
        out_shape=jax.ShapeDtypeStruct((S, D), F32),
        compiler_params=pltpu.CompilerParams(
            dimension_semantics=("arbitrary",), vmem_limit_bytes=VMEM_LIMIT),
        name="combine",
    )(b_flat, c_flat, x2, loc_t, g, b, ye2d)


def _layer(x, mem, w_in, rpb, conv_w, w_mix_out, ln1_g, ln1_b, wq, wk, wv, wo,
           ln2_g, ln2_b, w_router, w_gate, w_up, w_down, ln3_g, ln3_b, alpha):
    S, D = x.shape
    cap = CAPACITY_FACTOR * S // N_EXPERTS
    assert S % TOK_BLOCK == 0 and S % PROJ_TM == 0 and S % XATTN_TM == 0
    assert S // GRID_W >= 3 * ROW_BLOCK
    assert cap % FFN_TM == 0 and cap <= S
    row = lambda a: a.reshape(1, -1)

    q, k, v, bg, u = _proj(x, w_in.astype(BF16))
    x1 = _mixer(x, q, k, v, bg, u, w_mix_out.astype(BF16), rpb,
                conv_w, row(ln1_g), row(ln1_b), alpha)

    wr_t = w_router.T
    wr_hi = wr_t.astype(BF16)
    wr_lo = (wr_t - wr_hi.astype(F32)).astype(BF16)
    x2, x2b, aff3 = _xattn(x1, mem, wq.astype(BF16), wk.astype(BF16), wv.astype(BF16),
                           wo.astype(BF16), row(ln2_g), row(ln2_b), wr_hi, wr_lo, alpha)

    loc3, b3, c3 = _route(aff3, cap)
    b_flat = b3.reshape(-1)
    c_flat = c3.reshape(-1)

    xe2d = _dispatch(b_flat, c_flat, x2b, loc3, cap)
    wr_pad = jnp.pad(w_router.astype(BF16), ((0, 0), (0, LANES - N_EXPERTS)))
    ye2d = _ffn(xe2d, w_gate, w_up, w_down, wr_pad, cap)
    loc_t = loc3.transpose(0, 2, 1).reshape(S, N_EXPERTS)
    return _combine(b_flat, c_flat, x2, loc_t, row(ln3_g), row(ln3_b), ye2d, cap, alpha)


def kernel(x, mem, w_in, na_rpb, conv_w, w_mix_out, ln1_g, ln1_b, w_mem_q, w_mem_k,
           w_mem_v, w_mem_out, ln2_g, ln2_b, w_router, w_exp_gate, w_exp_up,
           w_exp_down, ln3_g, ln3_b):
    depth = w_in.shape[0]
    alpha = (2 * depth) ** 0.25
    outs = []
    for bi in range(x.shape[0]):
        xb = x[bi]
        for l in range(depth):
            xb = _layer(xb, mem[bi], w_in[l], na_rpb[l], conv_w[l], w_mix_out[l],
                        ln1_g[l], ln1_b[l], w_mem_q[l], w_mem_k[l], w_mem_v[l],
                        w_mem_out[l], ln2_g[l], ln2_b[l], w_router[l], w_exp_gate[l],
                        w_exp_up[l], w_exp_down[l], ln3_g[l], ln3_b[l], alpha)
        outs.append(xb)
    return jnp.stack(outs)
```

```python
import functools

import jax
import jax.numpy as jnp
from jax import lax
from jax.experimental import pallas as pl
from jax.experimental.pallas import tpu as pltpu

F32 = jnp.float32
BF16 = jnp.bfloat16
I32 = jnp.int32

GRID_W = 64
HEAD_DIM = 64
NA_HEADS = 8
NA_WIDTH = NA_HEADS * HEAD_DIM
WIN_ROWS = 8
WIN_COLS = 16
MEM_HEADS = 4
MEM_HEAD_DIM = 128
N_EXPERTS = 16
CAPACITY_FACTOR = 2
LN_EPS = 1e-5
NEG = -1e30

LANES = 128
SUBLANES = 8
VMEM_LIMIT = 56 * 1024 * 1024

ROW_BLOCK = 8
TOK_BLOCK = ROW_BLOCK * GRID_W
PROJ_TM = 1024
XATTN_TM = 1024
ROUTE_TILE = 256
DISPATCH_CHUNK = 256
DISPATCH_PEELED = 2
SLOT_CHUNK = 128
SM_ROWS = 16
PEELED_CHUNKS = 4
DISPATCH_TILES = 4
COMBINE_TILES = 2
DISPATCH_SLOTS = 3
FFN_TM = 1024
FFN_TF = 512


def _layer_norm(z, g, b):
    mu = jnp.mean(z, axis=-1, keepdims=True)
    zc = z - mu
    var = jnp.mean(zc * zc, axis=-1, keepdims=True)
    return zc * lax.rsqrt(var + LN_EPS) * g + b


def _dot_nt(a, b):
    return lax.dot_general(a, b, (((1,), (1,)), ((), ())),
                           preferred_element_type=F32)


def _proj_kernel(x_ref, w_ref, q_ref, k_ref, v_ref, b_ref, u_ref):
    xb = x_ref[...].astype(BF16)
    nw = NA_WIDTH

    def mm(c0):
        return jnp.dot(xb, w_ref[:, c0:c0 + nw], preferred_element_type=F32)

    q_ref[...] = (mm(0) * (HEAD_DIM ** -0.5)).astype(BF16)
    k_ref[...] = mm(nw).astype(BF16)
    v_ref[...] = mm(2 * nw).astype(BF16)
    b_ref[...] = mm(3 * nw)
    u_ref[...] = mm(4 * nw) * mm(5 * nw)


def _proj(x, w_in_b):
    S, D = x.shape
    nw = NA_WIDTH
    tm = PROJ_TM
    blk = lambda i: (i, 0)
    return pl.pallas_call(
        _proj_kernel,
        grid=(S // tm,),
        in_specs=[pl.BlockSpec((tm, D), blk),
                  pl.BlockSpec((D, 6 * nw), lambda i: (0, 0))],
        out_specs=[pl.BlockSpec((tm, nw), blk)] * 5,
        out_shape=[jax.ShapeDtypeStruct((S, nw), BF16)] * 3
        + [jax.ShapeDtypeStruct((S, nw), F32)] * 2,
        compiler_params=pltpu.CompilerParams(
            dimension_semantics=("arbitrary",), vmem_limit_bytes=VMEM_LIMIT),
        name="proj",
    )(x, w_in_b)


def _build_na_bias(rpb_ref, bias_ref):
    shape = (GRID_W, LANES)
    qc = lax.broadcasted_iota(I32, shape, 0)
    ln = lax.broadcasted_iota(I32, shape, 1)
    kc = ln % GRID_W
    c_start = jnp.clip(qc - WIN_COLS // 2, 0, GRID_W - WIN_COLS)
    base = LANES - (WIN_COLS - 1)

    def canvas(h, d):
        both = rpb_ref[h, d:d + 1, :] + pltpu.roll(rpb_ref[h, d + 1:d + 2, :], GRID_W, 1)
        t = pltpu.roll(pltpu.roll(jnp.broadcast_to(both, shape), base, 1), 0, 1,
                       stride=1, stride_axis=0)
        return jnp.where(kc >= c_start, jnp.where(kc < c_start + WIN_COLS, t, NEG), NEG)

    for par in range(2):
        for hp in range(NA_HEADS // 2):
            for m in range(WIN_ROWS):
                d = 2 * m + par
                bias_ref[par, hp, m, 0:GRID_W] = canvas(2 * hp, d)
                bias_ref[par, hp, m, GRID_W:2 * GRID_W] = canvas(2 * hp + 1, d)


def _kv_window_start(i, rows):
    return jnp.clip((i - 1) * ROW_BLOCK, 0, rows - 3 * ROW_BLOCK)


def _mixer_kernel(rows, alpha,
                  q_ref, kbuf, vbuf,
                  b_ref, u_ref, up_ref, un_ref, x_ref, wout_ref, rpb_ref,
                  cw_ref, g_ref, be_ref, o_ref, mix, bias_ref,
                  s_scr0, s_scr1, p_scr0, p_scr1):
    i = pl.program_id(0)
    nb = pl.num_programs(0)
    tb = TOK_BLOCK
    win = WIN_ROWS * GRID_W

    @pl.when(i == 0)
    def _():
        _build_na_bias(rpb_ref, bias_ref)

    lane = lax.broadcasted_iota(I32, (1, LANES), 1)
    first_head = lane < HEAD_DIM
    m_lo = jnp.where(first_head, 1.0, 0.0).astype(BF16)
    m_hi = jnp.where(first_head, 0.0, 1.0).astype(BF16)

    def window(r):
        grow = i * ROW_BLOCK + r
        start = jnp.clip(grow - WIN_ROWS // 2, 0, rows - WIN_ROWS)
        w0 = pl.multiple_of((start - _kv_window_start(i, rows)) * GRID_W, GRID_W)
        d0 = start - grow + WIN_ROWS
        return w0, d0 % 2, d0 // 2

    def scores(r, s_out):
        w0, _, _ = window(r)
        q0 = r * GRID_W if isinstance(r, int) else pl.multiple_of(r * GRID_W, GRID_W)
        for hp in range(NA_HEADS // 2):
            cs = slice(hp * LANES, (hp + 1) * LANES)
            qp = q_ref[pl.ds(q0, GRID_W), cs]
            lhs = jnp.concatenate([qp * m_lo, qp * m_hi], axis=0)
            kw = kbuf[pl.ds(w0, win), cs]
            s_out[hp] = _dot_nt(lhs, kw)

    def probs(r, s_in, p_out):
        _, par, m0 = window(r)
        for hp in range(NA_HEADS // 2):
            for rb in range(2 * GRID_W // SM_ROWS):
                rs = slice(rb * SM_ROWS, (rb + 1) * SM_ROWS)
                bias = jnp.concatenate(
                    [bias_ref[par, hp, m0 + j, rs, :] for j in range(WIN_ROWS // 2)], axis=1)
                s = s_in[hp, rs, :] + bias
                p = jnp.exp(s - jnp.max(s, axis=-1, keepdims=True))
                p = p * (1.0 / jnp.sum(p, axis=-1, keepdims=True))
                p_out[hp, rs, :] = p.astype(BF16)

    def values(r, p_in):
        w0, _, _ = window(r)
        q0 = r * GRID_W if isinstance(r, int) else pl.multiple_of(r * GRID_W, GRID_W)
        for hp in range(NA_HEADS // 2):
            cs = slice(hp * LANES, (hp + 1) * LANES)
            vw = vbuf[pl.ds(w0, win), cs]
            o2 = jnp.dot(p_in[hp], vw, preferred_element_type=F32)
            y = jnp.where(first_head, o2[0:GRID_W], o2[GRID_W:2 * GRID_W])
            mix[pl.ds(q0, GRID_W), cs] = y.astype(BF16)

    s_buf = (s_scr0, s_scr1)
    p_buf = (p_scr0, p_scr1)

    def steady(k, carry):
        for half in range(2):
            r = 1 + 2 * k + half
            values(r - 1, p_buf[half])
            probs(r, s_buf[1 - half], p_buf[1 - half])
            scores(r + 1, s_buf[half])
        return carry

    scores(0, s_buf[0])
    probs(0, s_buf[0], p_buf[0])
    scores(1, s_buf[1])
    for kk in range((ROW_BLOCK - 2) // 2):
        steady(kk, 0)
    values(ROW_BLOCK - 2, p_buf[0])
    probs(ROW_BLOCK - 1, s_buf[1], p_buf[1])
    values(ROW_BLOCK - 1, p_buf[1])

    u = u_ref[...]
    prev_row = up_ref[SUBLANES - 1:SUBLANES, :] * jnp.where(i > 0, 1.0, 0.0)
    next_row = un_ref[0:1, :] * jnp.where(i < nb - 1, 1.0, 0.0)
    rid = lax.broadcasted_iota(I32, (tb, 1), 0)
    um1 = jnp.where(rid == 0, prev_row, pltpu.roll(u, 1, axis=0))
    up1 = jnp.where(rid == tb - 1, next_row, pltpu.roll(u, tb - 1, axis=0))
    cw = cw_ref[...]
    yc = b_ref[...] * (cw[0:1] * um1 + cw[1:2] * u + cw[2:3] * up1)
    mix[:, NA_WIDTH:] = yc.astype(BF16)

    y = jnp.dot(mix[...], wout_ref[...], preferred_element_type=F32)
    z = alpha * x_ref[...] + y
    o_ref[...] = _layer_norm(z, g_ref[...], be_ref[...])


def _mixer(x, q, k, v, bg, u, w_out_b, rpb, conv_w, g, b, alpha):
    S, D = x.shape
    rpb_pad = jnp.pad(rpb.astype(F32), ((0, 0), (1, 1), (0, LANES - rpb.shape[2])))
    nw = NA_WIDTH
    tb = TOK_BLOCK
    nb = S // tb
    rows = S // GRID_W
    cur = lambda i: (i, 0)
    kv_win = pl.BlockSpec((pl.Element(3 * tb), pl.Element(nw)),
                          lambda i: (_kv_window_start(i, rows) * GRID_W, 0))
    halo = tb // SUBLANES
    hprev = lambda i: (jnp.maximum(i * halo - 1, 0), 0)
    hnext = lambda i: (jnp.minimum((i + 1) * halo, S // SUBLANES - 1), 0)
    const2 = lambda i: (0, 0)
    kv = lambda im: pl.BlockSpec((tb, nw), im)
    return pl.pallas_call(
        functools.partial(_mixer_kernel, rows, alpha),
        grid=(nb,),
        in_specs=[kv(cur), kv_win, kv_win,
                  kv(cur), kv(cur),
                  pl.BlockSpec((SUBLANES, nw), hprev),
                  pl.BlockSpec((SUBLANES, nw), hnext),
                  pl.BlockSpec((tb, D), cur),
                  pl.BlockSpec((D, D), const2),
                  pl.BlockSpec(rpb_pad.shape, lambda i: (0, 0, 0)),
                  pl.BlockSpec(conv_w.shape, const2),
                  pl.BlockSpec((1, D), const2),
                  pl.BlockSpec((1, D), const2)],
        out_specs=pl.BlockSpec((tb, D), cur),
        out_shape=jax.ShapeDtypeStruct((S, D), F32),
        scratch_shapes=[pltpu.VMEM((tb, D), BF16),
                        pltpu.VMEM((2, NA_HEADS // 2, WIN_ROWS, 2 * GRID_W, LANES), F32),
                        pltpu.VMEM((NA_HEADS // 2, 2 * GRID_W, WIN_ROWS * GRID_W), F32),
                        pltpu.VMEM((NA_HEADS // 2, 2 * GRID_W, WIN_ROWS * GRID_W), F32),
                        pltpu.VMEM((NA_HEADS // 2, 2 * GRID_W, WIN_ROWS * GRID_W), BF16),
                        pltpu.VMEM((NA_HEADS // 2, 2 * GRID_W, WIN_ROWS * GRID_W), BF16)],
        compiler_params=pltpu.CompilerParams(
            dimension_semantics=("arbitrary",), vmem_limit_bytes=VMEM_LIMIT),
        name="mixer",
    )(q, k, v, bg, u, u, u, x, w_out_b, rpb_pad, conv_w, g, b)


def _xattn_kernel(alpha, x_ref, mem_ref, wq_ref, wk_ref, wv_ref, wo_ref,
                  g_ref, be_ref, wrh_ref, wrl_ref,
                  x2_ref, x2b_ref, aff_ref, kmem, vmem):
    i = pl.program_id(0)

    @pl.when(i == 0)
    def _():
        mb = mem_ref[...].astype(BF16)
        kmem[...] = jnp.dot(mb, wk_ref[...], preferred_element_type=F32).astype(BF16)
        vmem[...] = jnp.dot(mb, wv_ref[...], preferred_element_type=F32).astype(BF16)

    x1 = x_ref[...]
    qb = jnp.dot(x1.astype(BF16), wq_ref[...], preferred_element_type=F32).astype(BF16)
    outs = []
    for h in range(MEM_HEADS):
        cs = slice(h * MEM_HEAD_DIM, (h + 1) * MEM_HEAD_DIM)
        s = _dot_nt(qb[:, cs], kmem[:, cs]) * (MEM_HEAD_DIM ** -0.5)
        mx = jnp.max(s, axis=-1, keepdims=True)
        p = jnp.exp(s - mx)
        l = jnp.sum(p, axis=-1, keepdims=True)
        oh = jnp.dot(p.astype(BF16), vmem[:, cs], preferred_element_type=F32)
        outs.append(oh / l)
    o = jnp.concatenate(outs, axis=1).astype(BF16)
    y = jnp.dot(o, wo_ref[...], preferred_element_type=F32)
    x2 = _layer_norm(alpha * x1 + y, g_ref[...], be_ref[...])
    x2_ref[...] = x2
    hi = x2.astype(BF16)
    x2b_ref[...] = hi
    lo = (x2 - hi.astype(F32)).astype(BF16)
    wrh = wrh_ref[...]
    r1 = _dot_nt(jnp.concatenate([wrh, wrl_ref[...]], axis=0), hi)
    lg = r1[0:N_EXPERTS] + r1[N_EXPERTS:2 * N_EXPERTS] + _dot_nt(wrh, lo)
    mx = jnp.max(lg, axis=0, keepdims=True)
    p = jnp.exp(lg - mx)
    aff = p / jnp.sum(p, axis=0, keepdims=True)
    for j in range(aff_ref.shape[0]):
        aff_ref[j] = aff[:, j * LANES:(j + 1) * LANES]


def _xattn(x1, mem, wq_b, wk_b, wv_b, wo_b, g, b, wr_hi, wr_lo, alpha):
    S, D = x1.shape
    M = mem.shape[0]
    mw = MEM_HEADS * MEM_HEAD_DIM
    tm = XATTN_TM
    cur = lambda i: (i, 0)
    const2 = lambda i: (0, 0)
    full = lambda a: pl.BlockSpec(a.shape, const2)
    return pl.pallas_call(
        functools.partial(_xattn_kernel, alpha),
        grid=(S // tm,),
        in_specs=[pl.BlockSpec((tm, D), cur), full(mem), full(wq_b), full(wk_b),
                  full(wv_b), full(wo_b), full(g), full(b), full(wr_hi), full(wr_lo)],
        out_specs=[pl.BlockSpec((tm, D), cur), pl.BlockSpec((tm, D), cur),
                   pl.BlockSpec((tm // LANES, N_EXPERTS, LANES), lambda i: (i, 0, 0))],
        out_shape=[jax.ShapeDtypeStruct((S, D), F32),
                   jax.ShapeDtypeStruct((S, D), BF16),
                   jax.ShapeDtypeStruct((S // LANES, N_EXPERTS, LANES), F32)],
        scratch_shapes=[pltpu.VMEM((M, mw), BF16), pltpu.VMEM((M, mw), BF16)],
        compiler_params=pltpu.CompilerParams(
            dimension_semantics=("arbitrary",), vmem_limit_bytes=VMEM_LIMIT),
        name="xattn",
    )(x1, mem, wq_b, wk_b, wv_b, wo_b, g, b, wr_hi, wr_lo)


def _route_kernel(cap, aff_ref, loc_ref, b_ref, c_ref, incl_scr, tot_scr, sel_scr):
    nblk = aff_ref.shape[0]
    per_tile = ROUTE_TILE // LANES
    ntile = nblk // per_tile

    def count_ge(t):
        ge = jnp.where(aff_ref[...] >= t[None], 1.0, 0.0)
        return jnp.sum(jnp.sum(ge, axis=0), axis=1, keepdims=True)

    def coarse(it, bits):
        cand = bits | jnp.left_shift(jnp.int32(1), 30 - it)
        ok = count_ge(pltpu.bitcast(cand, F32)) >= cap
        return jnp.where(ok, cand, bits)

    def fine(it, lohi):
        lo, hi = lohi
        mid = lo + (hi - lo) * 0.5
        ok = count_ge(mid) >= cap
        return jnp.where(ok, mid, lo), jnp.where(ok, hi, mid)

    bits = lax.fori_loop(0, 31, coarse, jnp.zeros((N_EXPERTS, 1), I32))
    thr, _ = lax.fori_loop(
        0, 30, fine, (pltpu.bitcast(bits, F32), pltpu.bitcast(bits + 1, F32)))
    gt_all = jnp.where(aff_ref[...] > thr[None], 1.0, 0.0)
    n_gt = jnp.sum(jnp.sum(gt_all, axis=0), axis=1, keepdims=True)
    need = cap - n_gt

    ii = lax.broadcasted_iota(I32, (LANES, LANES), 0)
    jj = lax.broadcasted_iota(I32, (LANES, LANES), 1)
    tri = jnp.where(ii <= jj, 1.0, 0.0).astype(BF16)
    ones = jnp.ones((LANES, LANES), BF16)
    flat = (nblk * N_EXPERTS, LANES)
    blocks = (nblk, N_EXPERTS, LANES)

    def block_scans(flags):
        f2 = flags.reshape(flat).astype(BF16)
        incl_scr[...] = jnp.dot(f2, tri, preferred_element_type=F32).reshape(blocks)
        tot_scr[...] = jnp.dot(f2, ones, preferred_element_type=F32).reshape(blocks)

    block_scans(jnp.where(aff_ref[...] == thr[None], 1.0, 0.0))

    def select_block(j, run_eq):
        blk = aff_ref[j]
        tie_rank = run_eq + incl_scr[j]
        sel_scr[j] = (jnp.where(blk > thr, 1.0, 0.0)
                      + jnp.where(blk == thr, 1.0, 0.0) * jnp.where(tie_rank <= need, 1.0, 0.0))
        return run_eq + tot_scr[j]

    zero = jnp.zeros((N_EXPERTS, LANES), F32)
    lax.fori_loop(0, nblk, select_block, zero)
    block_scans(sel_scr[...])

    def tile_body(t, run):
        tile_run = zero
        for j in range(per_tile):
            idx = t * per_tile + j
            sel = sel_scr[idx]
            loc = tile_run + incl_scr[idx] - sel
            loc_ref[idx] = jnp.where(sel > 0.5, loc, -1.0).astype(I32)
            tile_run = tile_run + tot_scr[idx]
        b_ref[t] = run[:, 0:1].astype(I32)
        c_ref[t] = tile_run[:, 0:1].astype(I32)
        return run + tile_run

    lax.fori_loop(0, ntile, tile_body, zero)


def _route(aff3, cap):
    nblk = aff3.shape[0]
    ntile = nblk * LANES // ROUTE_TILE
    full3 = lambda a: pl.BlockSpec(a, lambda: (0, 0, 0))
    return pl.pallas_call(
        functools.partial(_route_kernel, cap),
        in_specs=[full3(aff3.shape)],
        out_specs=[full3(aff3.shape), full3((ntile, N_EXPERTS, 1)),
                   full3((ntile, N_EXPERTS, 1))],
        out_shape=[jax.ShapeDtypeStruct(aff3.shape, I32),
                   jax.ShapeDtypeStruct((ntile, N_EXPERTS, 1), I32),
                   jax.ShapeDtypeStruct((ntile, N_EXPERTS, 1), I32)],
        scratch_shapes=[pltpu.VMEM(aff3.shape, F32)] * 3,
        compiler_params=pltpu.CompilerParams(vmem_limit_bytes=VMEM_LIMIT),
        name="route",
    )(aff3)


def _segment_copies(b_sm, c_sm, t, make_copy, action):
    off = 0
    for e in range(N_EXPERTS):
        c = c_sm[t * N_EXPERTS + e]
        b = b_sm[t * N_EXPERTS + e]

        @pl.when(c > 0)
        def _(e=e, b=b, c=c, off=off):
            action(make_copy(e, b, off, c))

        off = off + c
    return off


def _tile_rows(c_sm, t):
    rows = 0
    for e in range(N_EXPERTS):
        rows = rows + c_sm[t * N_EXPERTS + e]
    return rows


def _dispatch_kernel(cap, b_sm, c_sm, x_ref, loc_ref, xe_hbm, stg, sem):
    step = pl.program_id(0)
    nsteps = pl.num_programs(0)
    per_tile = ROUTE_TILE // LANES
    chunk_rows = DISPATCH_CHUNK
    sub = lax.broadcasted_iota(I32, (chunk_rows, 1), 0)
    lane = lax.broadcasted_iota(I32, (1, LANES), 1)

    def copy_for(slot_):
        def make(e, dst_row, src_row, n):
            return pltpu.make_async_copy(
                stg.at[slot_, pl.ds(pl.multiple_of(src_row * SUBLANES, SUBLANES), n * SUBLANES)],
                xe_hbm.at[pl.ds(pl.multiple_of((e * cap + dst_row) * SUBLANES, SUBLANES),
                                n * SUBLANES)],
                sem.at[slot_])
        return make

    def wait_tile(t_):
        rows = _tile_rows(c_sm, t_)
        slot_ = t_ % DISPATCH_SLOTS

        @pl.when(rows > 0)
        def _():
            n = rows * SUBLANES
            pltpu.make_async_copy(stg.at[slot_, pl.ds(0, n)], xe_hbm.at[pl.ds(0, n)],
                                  sem.at[slot_]).wait()

    def one_tile(k):
        t = step * DISPATCH_TILES + k
        slot = t % DISPATCH_SLOTS
        tok = slice(k * ROUTE_TILE, (k + 1) * ROUTE_TILE)
        offs = []
        off = 0
        for e in range(N_EXPERTS):
            offs.append(off)
            off = off + c_sm[t * N_EXPERTS + e]
        offs.append(off)
        nchunk = (off + chunk_rows - 1) // chunk_rows
        seg_lo = jnp.zeros((1, LANES), I32)
        seg_hi = jnp.zeros((1, LANES), I32)
        for e in range(N_EXPERTS):
            seg_lo = jnp.where(lane == e, offs[e], seg_lo)
            seg_hi = jnp.where(lane == e, offs[e + 1], seg_hi)
        seg_lo_f = seg_lo.astype(F32)
        locf = jnp.concatenate(
            [loc_ref[k * per_tile + j] for j in range(per_tile)], axis=1).astype(F32)
        locb = jnp.concatenate(
            [locf, jnp.zeros((LANES - N_EXPERTS, ROUTE_TILE), F32)], axis=0).astype(BF16)

        def chunk_body(m, carry):
            base = m * chunk_rows
            srow = sub + base
            inseg = jnp.where(srow >= seg_lo, jnp.where(srow < seg_hi, 1.0, 0.0), 0.0)
            rank = srow.astype(F32) - jnp.sum(inseg * seg_lo_f, axis=1, keepdims=True)
            want = jnp.dot(inseg.astype(BF16), locb, preferred_element_type=F32)
            p = jnp.where(want == rank, 1.0, 0.0)
            res = jnp.dot(p.astype(BF16), x_ref[tok, :], preferred_element_type=F32)
            row0 = (base * SUBLANES if isinstance(m, int)
                    else pl.multiple_of(base * SUBLANES, SUBLANES))
            for s in range(SUBLANES):
                stg[slot, pl.ds(row0 + s, chunk_rows, stride=SUBLANES), :] = (
                    res[:, s * LANES:(s + 1) * LANES])
            return carry

        for m in range(DISPATCH_PEELED):
            chunk_body(m, 0)
        lax.fori_loop(DISPATCH_PEELED, jnp.maximum(nchunk, DISPATCH_PEELED), chunk_body, 0)

        _segment_copies(b_sm, c_sm, t, copy_for(slot), lambda cp: cp.start())

        @pl.when(t >= DISPATCH_SLOTS - 1)
        def _():
            wait_tile(t - (DISPATCH_SLOTS - 1))

    for k in range(DISPATCH_TILES):
        one_tile(k)

    @pl.when(step == nsteps - 1)
    def _():
        last = nsteps * DISPATCH_TILES - 1
        for d in range(DISPATCH_SLOTS - 2, -1, -1):
            wait_tile(last - d)


def _dispatch(b_flat, c_flat, x2b, loc3, cap):
    S, D = x2b.shape
    tok = DISPATCH_TILES * ROUTE_TILE
    assert S % tok == 0 and S // ROUTE_TILE >= DISPATCH_SLOTS
    max_rows = N_EXPERTS * ROUTE_TILE
    return pl.pallas_call(
        functools.partial(_dispatch_kernel, cap),
        grid_spec=pltpu.PrefetchScalarGridSpec(
            num_scalar_prefetch=2,
            grid=(S // tok,),
            in_specs=[pl.BlockSpec((tok, D), lambda t, b, c: (t, 0)),
                      pl.BlockSpec((tok // LANES, N_EXPERTS, LANES), lambda t, b, c: (t, 0, 0))],
            out_specs=pl.BlockSpec(memory_space=pl.ANY),
            scratch_shapes=[pltpu.VMEM((DISPATCH_SLOTS, max_rows * SUBLANES, LANES), F32),
                            pltpu.SemaphoreType.DMA((DISPATCH_SLOTS,))]),
        out_shape=jax.ShapeDtypeStruct((N_EXPERTS * cap * SUBLANES, LANES), F32),
        compiler_params=pltpu.CompilerParams(
            dimension_semantics=("arbitrary",), vmem_limit_bytes=VMEM_LIMIT),
        name="dispatch",
    )(b_flat, c_flat, x2b, loc3)


def _ffn_kernel(xe_ref, wg_ref, wu_ref, wd_ref, wr_ref, o_ref, xb, acc, gate):
    e = pl.program_id(0)
    f = pl.program_id(2)
    tm = xb.shape[0]

    @pl.when(f == 0)
    def _():
        for s in range(SUBLANES):
            xb[:, s * LANES:(s + 1) * LANES] = (
                xe_ref[pl.ds(s, tm, stride=SUBLANES), :].astype(BF16))
        lg = jnp.dot(xb[...], wr_ref[...], preferred_element_type=F32)
        lane = lax.broadcasted_iota(I32, (1, LANES), 1)
        lg = jnp.where(lane < N_EXPERTS, lg, NEG)
        p = jnp.exp(lg - jnp.max(lg, axis=-1, keepdims=True))
        mine = jnp.sum(jnp.where(lane == e, p, 0.0), axis=-1, keepdims=True)
        gate[...] = mine / jnp.sum(p, axis=-1, keepdims=True)

    def down_proj():
        x = xb[...]
        g = jnp.dot(x, wg_ref[...].astype(BF16), preferred_element_type=F32)
        u = jnp.dot(x, wu_ref[...].astype(BF16), preferred_element_type=F32)
        h = (g * jax.nn.sigmoid(g) * u).astype(BF16)
        return jnp.dot(h, wd_ref[...].astype(BF16), preferred_element_type=F32)

    last = pl.num_programs(2) - 1

    @pl.when(f == 0)
    def _():
        acc[...] = down_proj()

    @pl.when((f > 0) & (f < last))
    def _():
        acc[...] += down_proj()

    @pl.when(f == last)
    def _():
        y = (acc[...] + down_proj()) * gate[...]
        for s in range(SUBLANES):
            o_ref[pl.ds(s, tm, stride=SUBLANES), :] = y[:, s * LANES:(s + 1) * LANES]


def _ffn(xe2d, w_gate, w_up, w_down, wr_pad, cap):
    E, D, FF = w_gate.shape
    tm, tf = FFN_TM, FFN_TF
    mt = cap // tm
    rows = lambda e, m, f: (e * mt + m, 0)
    return pl.pallas_call(
        _ffn_kernel,
        grid=(E, mt, FF // tf),
        in_specs=[pl.BlockSpec((tm * SUBLANES, LANES), rows),
                  pl.BlockSpec((None, D, tf), lambda e, m, f: (e, 0, f)),
                  pl.BlockSpec((None, D, tf), lambda e, m, f: (e, 0, f)),
                  pl.BlockSpec((None, tf, D), lambda e, m, f: (e, f, 0)),
                  pl.BlockSpec(wr_pad.shape, lambda e, m, f: (0, 0))],
        out_specs=pl.BlockSpec((tm * SUBLANES, LANES), rows),
        out_shape=jax.ShapeDtypeStruct(xe2d.shape, F32),
        scratch_shapes=[pltpu.VMEM((tm, D), BF16), pltpu.VMEM((tm, D), F32),
                        pltpu.VMEM((tm, 1), F32)],
        compiler_params=pltpu.CompilerParams(
            dimension_semantics=("arbitrary", "arbitrary", "arbitrary"),
            vmem_limit_bytes=VMEM_LIMIT),
        name="ffn",
    )(xe2d, w_gate, w_up, w_down, wr_pad)


def _combine_kernel(cap, alpha, b_sm, c_sm, x_ref, loc_ref, g_ref, be_ref, ye_hbm,
                    o_ref, ybuf, sem, hl_scr, z_scr):
    step = pl.program_id(0)
    nt = pl.num_programs(0) * COMBINE_TILES
    lane = lax.broadcasted_iota(I32, (1, SLOT_CHUNK), 1)
    sub = lax.broadcasted_iota(I32, (SLOT_CHUNK, 1), 0)
    eid = lax.broadcasted_iota(I32, (N_EXPERTS, 1), 0)

    def copy_for(slot_):
        def make(e, src_row, dst_row, n):
            return pltpu.make_async_copy(
                ye_hbm.at[pl.ds(pl.multiple_of((e * cap + src_row) * SUBLANES, SUBLANES),
                                n * SUBLANES)],
                ybuf.at[slot_, pl.ds(pl.multiple_of(dst_row * SUBLANES, SUBLANES), n * SUBLANES)],
                sem.at[slot_])
        return make

    def fetch(t_):
        _segment_copies(b_sm, c_sm, t_, copy_for(t_ % 2), lambda cp: cp.start())

    @pl.when(step == 0)
    def _():
        ybuf[...] = jnp.zeros_like(ybuf)
        fetch(0)
        fetch(1)

    def one_tile(k):
        t = step * COMBINE_TILES + k
        slot = t % 2
        tok = slice(k * ROUTE_TILE, (k + 1) * ROUTE_TILE)

        rows_in = _tile_rows(c_sm, t)

        @pl.when(rows_in > 0)
        def _():
            n = rows_in * SUBLANES
            pltpu.make_async_copy(ye_hbm.at[pl.ds(0, n)], ybuf.at[slot, pl.ds(0, n)],
                                  sem.at[slot]).wait()

        offs = []
        off = 0
        for e in range(N_EXPERTS):
            offs.append(off)
            off = off + c_sm[t * N_EXPERTS + e]
        offs.append(off)
        total = off
        nchunk = (total + SLOT_CHUNK - 1) // SLOT_CHUNK
        seg_lo = jnp.zeros((N_EXPERTS, 1), I32)
        seg_hi = jnp.zeros((N_EXPERTS, 1), I32)
        for e in range(N_EXPERTS):
            seg_lo = jnp.where(eid == e, offs[e], seg_lo)
            seg_hi = jnp.where(eid == e, offs[e + 1], seg_hi)
        seg_lo_f = seg_lo.astype(F32)
        locb = loc_ref[tok, :].astype(F32).astype(BF16)

        z_scr[...] = alpha * x_ref[tok, :]

        def chunk_body(m, carry):
            base = m * SLOT_CHUNK
            srow = lane + base
            inseg = jnp.where(srow >= seg_lo, jnp.where(srow < seg_hi, 1.0, 0.0), 0.0)
            rank = srow.astype(F32) - jnp.sum(inseg * seg_lo_f, axis=0, keepdims=True)
            want = jnp.dot(locb, inseg.astype(BF16), preferred_element_type=F32)
            ptb = jnp.where(want == rank, 1.0, 0.0).astype(BF16)
            row0 = (base * SUBLANES if isinstance(m, int)
                    else pl.multiple_of(base * SUBLANES, SUBLANES))
            yc = jnp.concatenate(
                [ybuf[slot, pl.ds(row0 + s, SLOT_CHUNK, stride=SUBLANES), :]
                 for s in range(SUBLANES)], axis=1)
            yc = jnp.where(sub < total - base, yc, 0.0)
            hi = yc.astype(BF16)
            hl = hl_scr.at[m % PEELED_CHUNKS]
            hl[0:SLOT_CHUNK] = hi
            hl[SLOT_CHUNK:2 * SLOT_CHUNK] = (yc - hi.astype(F32)).astype(BF16)
            z_scr[...] += jnp.dot(jnp.concatenate([ptb, ptb], axis=1), hl[...],
                                  preferred_element_type=F32)
            return carry

        for m in range(PEELED_CHUNKS):
            chunk_body(m, 0)
        lax.fori_loop(PEELED_CHUNKS, jnp.maximum(nchunk, PEELED_CHUNKS), chunk_body, 0)

        @pl.when(t + 2 < nt)
        def _():
            fetch(t + 2)

        o_ref[tok, :] = _layer_norm(z_scr[...], g_ref[...], be_ref[...])

    for k in range(COMBINE_TILES):
        one_tile(k)


def _combine(b_flat, c_flat, x2, loc_t, g, b, ye2d, cap, alpha):
    S, D = x2.shape
    tok = COMBINE_TILES * ROUTE_TILE
    assert S % tok == 0 and S // ROUTE_TILE >= 2
    max_rows = N_EXPERTS * ROUTE_TILE
    cur = lambda t, b_, c_: (t, 0)
    const2 = lambda t, b_, c_: (0, 0)
    return pl.pallas_call(
        functools.partial(_combine_kernel, cap, alpha),
        grid_spec=pltpu.PrefetchScalarGridSpec(
            num_scalar_prefetch=2,
            grid=(S // tok,),
            in_specs=[pl.BlockSpec((tok, D), cur),
                      pl.BlockSpec((tok, N_EXPERTS), cur),
                      pl.BlockSpec((1, D), const2),
                      pl.BlockSpec((1, D), const2),
                      pl.BlockSpec(memory_space=pl.ANY)],
            out_specs=pl.BlockSpec((tok, D), cur),
            scratch_shapes=[pltpu.VMEM((2, max_rows * SUBLANES, LANES), F32),
                            pltpu.SemaphoreType.DMA((2,)),
                            pltpu.VMEM((PEELED_CHUNKS, 2 * SLOT_CHUNK, D), BF16),
                            pltpu.VMEM((ROUTE_TILE, D), F32)]),
        out_shape=jax.ShapeDtypeStruct((S, D), F32),
        compiler_params=pltpu.CompilerParams(
            dimension_semantics=("arbitrary",), vmem_limit_bytes=VMEM_LIMIT),
        name="combine",
    )(b_flat, c_flat, x2, loc_t, g, b, ye2d)


def _layer(x, mem, w_in, rpb, conv_w, w_mix_out, ln1_g, ln1_b, wq, wk, wv, wo,
           ln2_g, ln2_b, w_router, w_gate, w_up, w_down, ln3_g, ln3_b, alpha):
    S, D = x.shape
    cap = CAPACITY_FACTOR * S // N_EXPERTS
    assert S % TOK_BLOCK == 0 and S % PROJ_TM == 0 and S % XATTN_TM == 0
    assert S // GRID_W >= 3 * ROW_BLOCK
    assert cap % FFN_TM == 0 and cap <= S
    row = lambda a: a.reshape(1, -1)

    q, k, v, bg, u = _proj(x, w_in.astype(BF16))
    x1 = _mixer(x, q, k, v, bg, u, w_mix_out.astype(BF16), rpb,
                conv_w, row(ln1_g), row(ln1_b), alpha)

    wr_t = w_router.T
    wr_hi = wr_t.astype(BF16)
    wr_lo = (wr_t - wr_hi.astype(F32)).astype(BF16)
    x2, x2b, aff3 = _xattn(x1, mem, wq.astype(BF16), wk.astype(BF16), wv.astype(BF16),
                           wo.astype(BF16), row(ln2_g), row(ln2_b), wr_hi, wr_lo, alpha)

    loc3, b3, c3 = _route(aff3, cap)
    b_flat = b3.reshape(-1)
    c_flat = c3.reshape(-1)

    xe2d = _dispatch(b_flat, c_flat, x2b, loc3, cap)
    wr_pad = jnp.pad(w_router.astype(BF16), ((0, 0), (0, LANES - N_EXPERTS)))
    ye2d = _ffn(xe2d, w_gate, w_up, w_down, wr_pad, cap)
    loc_t = loc3.transpose(0, 2, 1).reshape(S, N_EXPERTS)
    return _combine(b_flat, c_flat, x2, loc_t, row(ln3_g), row(ln3_b), ye2d, cap, alpha)


def kernel(x, mem, w_in, na_rpb, conv_w, w_mix_out, ln1_g, ln1_b, w_mem_q, w_mem_k,
           w_mem_v, w_mem_out, ln2_g, ln2_b, w_router, w_exp_gate, w_exp_up,
           w_exp_down, ln3_g, ln3_b):
    depth = w_in.shape[0]
    alpha = (2 * depth) ** 0.25
    outs = []
    for bi in range(x.shape[0]):
        xb = x[bi]
        for l in range(depth):
            xb = _layer(xb, mem[bi], w_in[l], na_rpb[l], conv_w[l], w_mix_out[l],
                        ln1_g[l], ln1_b[l], w_mem_q[l], w_mem_k[l], w_mem_v[l],
                        w_mem_out[l], ln2_g[l], ln2_b[l], w_router[l], w_exp_gate[l],
                        w_exp_up[l], w_exp_down[l], ln3_g[l], ln3_b[l], alpha)
        outs.append(xb)
    return jnp.stack(outs)
```

```python
import functools

import jax
import jax.numpy as jnp
from jax import lax
from jax.experimental import pallas as pl
from jax.experimental.pallas import tpu as pltpu

F32 = jnp.float32
BF16 = jnp.bfloat16
I32 = jnp.int32

GRID_W = 64
HEAD_DIM = 64
NA_HEADS = 8
NA_WIDTH = NA_HEADS * HEAD_DIM
WIN_ROWS = 8
WIN_COLS = 16
MEM_HEADS = 4
MEM_HEAD_DIM = 128
N_EXPERTS = 16
CAPACITY_FACTOR = 2
LN_EPS = 1e-5
NEG = -1e30

LANES = 128
SUBLANES = 8
VMEM_LIMIT = 56 * 1024 * 1024

ROW_BLOCK = 8
TOK_BLOCK = ROW_BLOCK * GRID_W
PROJ_TM = 1024
XATTN_TM = 1024
ROUTE_TILE = 256
SLOT_CHUNK = 256
SM_ROWS = 16
PEELED_CHUNKS = 2
DISPATCH_TILES = 4
COMBINE_TILES = 2
DISPATCH_SLOTS = 3
FFN_TM = 1024
FFN_TF = 512


def _layer_norm(z, g, b):
    mu = jnp.mean(z, axis=-1, keepdims=True)
    zc = z - mu
    var = jnp.mean(zc * zc, axis=-1, keepdims=True)
    return zc * lax.rsqrt(var + LN_EPS) * g + b


def _dot_nt(a, b):
    return lax.dot_general(a, b, (((1,), (1,)), ((), ())),
                           preferred_element_type=F32)


def _proj_kernel(x_ref, w_ref, q_ref, k_ref, v_ref, b_ref, u_ref):
    xb = x_ref[...].astype(BF16)
    nw = NA_WIDTH

    def mm(c0):
        return jnp.dot(xb, w_ref[:, c0:c0 + nw].astype(BF16), preferred_element_type=F32)

    q_ref[...] = (mm(0) * (HEAD_DIM ** -0.5)).astype(BF16)
    k_ref[...] = mm(nw).astype(BF16)
    v_ref[...] = mm(2 * nw).astype(BF16)
    b_ref[...] = mm(3 * nw)
    u_ref[...] = mm(4 * nw) * mm(5 * nw)


def _proj(x, w_in_b):
    S, D = x.shape
    nw = NA_WIDTH
    tm = PROJ_TM
    blk = lambda i: (i, 0)
    return pl.pallas_call(
        _proj_kernel,
        grid=(S // tm,),
        in_specs=[pl.BlockSpec((tm, D), blk),
                  pl.BlockSpec((D, 6 * nw), lambda i: (0, 0))],
        out_specs=[pl.BlockSpec((tm, nw), blk)] * 5,
        out_shape=[jax.ShapeDtypeStruct((S, nw), BF16)] * 3
        + [jax.ShapeDtypeStruct((S, nw), F32)] * 2,
        compiler_params=pltpu.CompilerParams(
            dimension_semantics=("arbitrary",), vmem_limit_bytes=VMEM_LIMIT),
        name="proj",
    )(x, w_in_b)


def _build_na_bias(rpb_ref, bias_ref):
    shape = (GRID_W, LANES)
    qc = lax.broadcasted_iota(I32, shape, 0)
    ln = lax.broadcasted_iota(I32, shape, 1)
    kc = ln % GRID_W
    c_start = jnp.clip(qc - WIN_COLS // 2, 0, GRID_W - WIN_COLS)
    base = LANES - (WIN_COLS - 1)

    def canvas(h, d):
        both = rpb_ref[h, d:d + 1, :] + pltpu.roll(rpb_ref[h, d + 1:d + 2, :], GRID_W, 1)
        t = pltpu.roll(pltpu.roll(jnp.broadcast_to(both, shape), base, 1), 0, 1,
                       stride=1, stride_axis=0)
        return jnp.where(kc >= c_start, jnp.where(kc < c_start + WIN_COLS, t, NEG), NEG)

    for par in range(2):
        for hp in range(NA_HEADS // 2):
            for m in range(WIN_ROWS):
                d = 2 * m + par
                bias_ref[par, hp, m, 0:GRID_W] = canvas(2 * hp, d)
                bias_ref[par, hp, m, GRID_W:2 * GRID_W] = canvas(2 * hp + 1, d)


def _kv_window_start(i, rows):
    return jnp.clip((i - 1) * ROW_BLOCK, 0, rows - 3 * ROW_BLOCK)


def _mixer_kernel(rows, alpha,
                  q_ref, kbuf, vbuf,
                  b_ref, u_ref, up_ref, un_ref, x_ref, wout_ref, rpb_ref,
                  cw_ref, g_ref, be_ref, o_ref, mix, bias_ref,
                  s_scr0, s_scr1, p_scr0, p_scr1):
    i = pl.program_id(0)
    nb = pl.num_programs(0)
    tb = TOK_BLOCK
    win = WIN_ROWS * GRID_W

    @pl.when(i == 0)
    def _():
        _build_na_bias(rpb_ref, bias_ref)

    lane = lax.broadcasted_iota(I32, (1, LANES), 1)
    first_head = lane < HEAD_DIM
    m_lo = jnp.where(first_head, 1.0, 0.0).astype(BF16)
    m_hi = jnp.where(first_head, 0.0, 1.0).astype(BF16)

    def window(r):
        grow = i * ROW_BLOCK + r
        start = jnp.clip(grow - WIN_ROWS // 2, 0, rows - WIN_ROWS)
        w0 = pl.multiple_of((start - _kv_window_start(i, rows)) * GRID_W, GRID_W)
        d0 = start - grow + WIN_ROWS
        return w0, d0 % 2, d0 // 2

    def scores(r, s_out):
        w0, _, _ = window(r)
        q0 = r * GRID_W if isinstance(r, int) else pl.multiple_of(r * GRID_W, GRID_W)
        for hp in range(NA_HEADS // 2):
            cs = slice(hp * LANES, (hp + 1) * LANES)
            qp = q_ref[pl.ds(q0, GRID_W), cs]
            lhs = jnp.concatenate([qp * m_lo, qp * m_hi], axis=0)
            kw = kbuf[pl.ds(w0, win), cs]
            s_out[hp] = _dot_nt(lhs, kw)

    def probs(r, s_in, p_out):
        _, par, m0 = window(r)
        for hp in range(NA_HEADS // 2):
            for rb in range(2 * GRID_W // SM_ROWS):
                rs = slice(rb * SM_ROWS, (rb + 1) * SM_ROWS)
                bias = jnp.concatenate(
                    [bias_ref[par, hp, m0 + j, rs, :] for j in range(WIN_ROWS // 2)], axis=1)
                s = s_in[hp, rs, :] + bias
                p = jnp.exp(s - jnp.max(s, axis=-1, keepdims=True))
                p = p * (1.0 / jnp.sum(p, axis=-1, keepdims=True))
                p_out[hp, rs, :] = p.astype(BF16)

    def values(r, p_in):
        w0, _, _ = window(r)
        q0 = r * GRID_W if isinstance(r, int) else pl.multiple_of(r * GRID_W, GRID_W)
        for hp in range(NA_HEADS // 2):
            cs = slice(hp * LANES, (hp + 1) * LANES)
            vw = vbuf[pl.ds(w0, win), cs]
            o2 = jnp.dot(p_in[hp], vw, preferred_element_type=F32)
            y = jnp.where(first_head, o2[0:GRID_W], o2[GRID_W:2 * GRID_W])
            mix[pl.ds(q0, GRID_W), cs] = y.astype(BF16)

    s_buf = (s_scr0, s_scr1)
    p_buf = (p_scr0, p_scr1)

    def steady(k, carry):
        for half in range(2):
            r = 1 + 2 * k + half
            values(r - 1, p_buf[half])
            probs(r, s_buf[1 - half], p_buf[1 - half])
            scores(r + 1, s_buf[half])
        return carry

    scores(0, s_buf[0])
    probs(0, s_buf[0], p_buf[0])
    scores(1, s_buf[1])
    for kk in range((ROW_BLOCK - 2) // 2):
        steady(kk, 0)
    values(ROW_BLOCK - 2, p_buf[0])
    probs(ROW_BLOCK - 1, s_buf[1], p_buf[1])
    values(ROW_BLOCK - 1, p_buf[1])

    u = u_ref[...]
    prev_row = up_ref[SUBLANES - 1:SUBLANES, :] * jnp.where(i > 0, 1.0, 0.0)
    next_row = un_ref[0:1, :] * jnp.where(i < nb - 1, 1.0, 0.0)
    rid = lax.broadcasted_iota(I32, (tb, 1), 0)
    um1 = jnp.where(rid == 0, prev_row, pltpu.roll(u, 1, axis=0))
    up1 = jnp.where(rid == tb - 1, next_row, pltpu.roll(u, tb - 1, axis=0))
    cw = cw_ref[...]
    yc = b_ref[...] * (cw[0:1] * um1 + cw[1:2] * u + cw[2:3] * up1)
    mix[:, NA_WIDTH:] = yc.astype(BF16)

    y = jnp.dot(mix[...], wout_ref[...].astype(BF16), preferred_element_type=F32)
    z = alpha * x_ref[...] + y
    o_ref[...] = _layer_norm(z, g_ref[...], be_ref[...])


def _mixer(x, q, k, v, bg, u, w_out_b, rpb, conv_w, g, b, alpha):
    S, D = x.shape
    rpb_pad = jnp.pad(rpb.astype(F32), ((0, 0), (1, 1), (0, LANES - rpb.shape[2])))
    nw = NA_WIDTH
    tb = TOK_BLOCK
    nb = S // tb
    rows = S // GRID_W
    cur = lambda i: (i, 0)
    kv_win = pl.BlockSpec((pl.Element(3 * tb), pl.Element(nw)),
                          lambda i: (_kv_window_start(i, rows) * GRID_W, 0))
    halo = tb // SUBLANES
    hprev = lambda i: (jnp.maximum(i * halo - 1, 0), 0)
    hnext = lambda i: (jnp.minimum((i + 1) * halo, S // SUBLANES - 1), 0)
    const2 = lambda i: (0, 0)
    kv = lambda im: pl.BlockSpec((tb, nw), im)
    return pl.pallas_call(
        functools.partial(_mixer_kernel, rows, alpha),
        grid=(nb,),
        in_specs=[kv(cur), kv_win, kv_win,
                  kv(cur), kv(cur),
                  pl.BlockSpec((SUBLANES, nw), hprev),
                  pl.BlockSpec((SUBLANES, nw), hnext),
                  pl.BlockSpec((tb, D), cur),
                  pl.BlockSpec((D, D), const2),
                  pl.BlockSpec(rpb_pad.shape, lambda i: (0, 0, 0)),
                  pl.BlockSpec(conv_w.shape, const2),
                  pl.BlockSpec((1, D), const2),
                  pl.BlockSpec((1, D), const2)],
        out_specs=pl.BlockSpec((tb, D), cur),
        out_shape=jax.ShapeDtypeStruct((S, D), F32),
        scratch_shapes=[pltpu.VMEM((tb, D), BF16),
                        pltpu.VMEM((2, NA_HEADS // 2, WIN_ROWS, 2 * GRID_W, LANES), F32),
                        pltpu.VMEM((NA_HEADS // 2, 2 * GRID_W, WIN_ROWS * GRID_W), F32),
                        pltpu.VMEM((NA_HEADS // 2, 2 * GRID_W, WIN_ROWS * GRID_W), F32),
                        pltpu.VMEM((NA_HEADS // 2, 2 * GRID_W, WIN_ROWS * GRID_W), BF16),
                        pltpu.VMEM((NA_HEADS // 2, 2 * GRID_W, WIN_ROWS * GRID_W), BF16)],
        compiler_params=pltpu.CompilerParams(
            dimension_semantics=("arbitrary",), vmem_limit_bytes=VMEM_LIMIT),
        name="mixer",
    )(q, k, v, bg, u, u, u, x, w_out_b, rpb_pad, conv_w, g, b)


def _xattn_kernel(alpha, x_ref, mem_ref, wq_ref, wk_ref, wv_ref, wo_ref,
                  g_ref, be_ref, wrh_ref, wrl_ref,
                  x2_ref, x2b_ref, aff_ref, kmem, vmem):
    i = pl.program_id(0)

    @pl.when(i == 0)
    def _():
        mb = mem_ref[...].astype(BF16)
        kmem[...] = jnp.dot(mb, wk_ref[...].astype(BF16),
                            preferred_element_type=F32).astype(BF16)
        vmem[...] = jnp.dot(mb, wv_ref[...].astype(BF16),
                            preferred_element_type=F32).astype(BF16)

    x1 = x_ref[...]
    qb = jnp.dot(x1.astype(BF16), wq_ref[...].astype(BF16),
                 preferred_element_type=F32).astype(BF16)
    outs = []
    for h in range(MEM_HEADS):
        cs = slice(h * MEM_HEAD_DIM, (h + 1) * MEM_HEAD_DIM)
        s = _dot_nt(qb[:, cs], kmem[:, cs]) * (MEM_HEAD_DIM ** -0.5)
        mx = jnp.max(s, axis=-1, keepdims=True)
        p = jnp.exp(s - mx)
        l = jnp.sum(p, axis=-1, keepdims=True)
        oh = jnp.dot(p.astype(BF16), vmem[:, cs], preferred_element_type=F32)
        outs.append(oh / l)
    o = jnp.concatenate(outs, axis=1).astype(BF16)
    y = jnp.dot(o, wo_ref[...].astype(BF16), preferred_element_type=F32)
    x2 = _layer_norm(alpha * x1 + y, g_ref[...], be_ref[...])
    x2_ref[...] = x2
    hi = x2.astype(BF16)
    x2b_ref[...] = hi
    lo = (x2 - hi.astype(F32)).astype(BF16)
    wrh = wrh_ref[...]
    r1 = _dot_nt(jnp.concatenate([wrh, wrl_ref[...]], axis=0), hi)
    lg = r1[0:N_EXPERTS] + r1[N_EXPERTS:2 * N_EXPERTS] + _dot_nt(wrh, lo)
    mx = jnp.max(lg, axis=0, keepdims=True)
    p = jnp.exp(lg - mx)
    aff = p / jnp.sum(p, axis=0, keepdims=True)
    for j in range(aff_ref.shape[0]):
        aff_ref[j] = aff[:, j * LANES:(j + 1) * LANES]


def _xattn(x1, mem, wq_b, wk_b, wv_b, wo_b, g, b, wr_hi, wr_lo, alpha):
    S, D = x1.shape
    M = mem.shape[0]
    mw = MEM_HEADS * MEM_HEAD_DIM
    tm = XATTN_TM
    cur = lambda i: (i, 0)
    const2 = lambda i: (0, 0)
    full = lambda a: pl.BlockSpec(a.shape, const2)
    return pl.pallas_call(
        functools.partial(_xattn_kernel, alpha),
        grid=(S // tm,),
        in_specs=[pl.BlockSpec((tm, D), cur), full(mem), full(wq_b), full(wk_b),
                  full(wv_b), full(wo_b), full(g), full(b), full(wr_hi), full(wr_lo)],
        out_specs=[pl.BlockSpec((tm, D), cur), pl.BlockSpec((tm, D), cur),
                   pl.BlockSpec((tm // LANES, N_EXPERTS, LANES), lambda i: (i, 0, 0))],
        out_shape=[jax.ShapeDtypeStruct((S, D), F32),
                   jax.ShapeDtypeStruct((S, D), BF16),
                   jax.ShapeDtypeStruct((S // LANES, N_EXPERTS, LANES), F32)],
        scratch_shapes=[pltpu.VMEM((M, mw), BF16), pltpu.VMEM((M, mw), BF16)],
        compiler_params=pltpu.CompilerParams(
            dimension_semantics=("arbitrary",), vmem_limit_bytes=VMEM_LIMIT),
        name="xattn",
    )(x1, mem, wq_b, wk_b, wv_b, wo_b, g, b, wr_hi, wr_lo)


def _route_kernel(cap, aff_ref, loc_ref, b_ref, c_ref, incl_scr, tot_scr, sel_scr):
    nblk = aff_ref.shape[0]
    per_tile = ROUTE_TILE // LANES
    ntile = nblk // per_tile

    def count_ge(t):
        ge = jnp.where(aff_ref[...] >= t[None], 1.0, 0.0)
        return jnp.sum(jnp.sum(ge, axis=0), axis=1, keepdims=True)

    def coarse(it, bits):
        cand = bits | jnp.left_shift(jnp.int32(1), 30 - it)
        ok = count_ge(pltpu.bitcast(cand, F32)) >= cap
        return jnp.where(ok, cand, bits)

    def fine(it, lohi):
        lo, hi = lohi
        mid = lo + (hi - lo) * 0.5
        ok = count_ge(mid) >= cap
        return jnp.where(ok, mid, lo), jnp.where(ok, hi, mid)

    bits = lax.fori_loop(0, 31, coarse, jnp.zeros((N_EXPERTS, 1), I32))
    thr, _ = lax.fori_loop(
        0, 30, fine, (pltpu.bitcast(bits, F32), pltpu.bitcast(bits + 1, F32)))
    gt_all = jnp.where(aff_ref[...] > thr[None], 1.0, 0.0)
    n_gt = jnp.sum(jnp.sum(gt_all, axis=0), axis=1, keepdims=True)
    need = cap - n_gt

    ii = lax.broadcasted_iota(I32, (LANES, LANES), 0)
    jj = lax.broadcasted_iota(I32, (LANES, LANES), 1)
    tri = jnp.where(ii <= jj, 1.0, 0.0).astype(BF16)
    ones = jnp.ones((LANES, LANES), BF16)
    flat = (nblk * N_EXPERTS, LANES)
    blocks = (nblk, N_EXPERTS, LANES)

    def block_scans(flags):
        f2 = flags.reshape(flat).astype(BF16)
        incl_scr[...] = jnp.dot(f2, tri, preferred_element_type=F32).reshape(blocks)
        tot_scr[...] = jnp.dot(f2, ones, preferred_element_type=F32).reshape(blocks)

    block_scans(jnp.where(aff_ref[...] == thr[None], 1.0, 0.0))

    def select_block(j, run_eq):
        blk = aff_ref[j]
        tie_rank = run_eq + incl_scr[j]
        sel_scr[j] = (jnp.where(blk > thr, 1.0, 0.0)
                      + jnp.where(blk == thr, 1.0, 0.0) * jnp.where(tie_rank <= need, 1.0, 0.0))
        return run_eq + tot_scr[j]

    zero = jnp.zeros((N_EXPERTS, LANES), F32)
    lax.fori_loop(0, nblk, select_block, zero)
    block_scans(sel_scr[...])

    def tile_body(t, run):
        tile_run = zero
        for j in range(per_tile):
            idx = t * per_tile + j
            sel = sel_scr[idx]
            loc = tile_run + incl_scr[idx] - sel
            loc_ref[idx] = jnp.where(sel > 0.5, loc, -1.0).astype(I32)
            tile_run = tile_run + tot_scr[idx]
        b_ref[t] = run[:, 0:1].astype(I32)
        c_ref[t] = tile_run[:, 0:1].astype(I32)
        return run + tile_run

    lax.fori_loop(0, ntile, tile_body, zero)


def _route(aff3, cap):
    nblk = aff3.shape[0]
    ntile = nblk * LANES // ROUTE_TILE
    full3 = lambda a: pl.BlockSpec(a, lambda: (0, 0, 0))
    return pl.pallas_call(
        functools.partial(_route_kernel, cap),
        in_specs=[full3(aff3.shape)],
        out_specs=[full3(aff3.shape), full3((ntile, N_EXPERTS, 1)),
                   full3((ntile, N_EXPERTS, 1))],
        out_shape=[jax.ShapeDtypeStruct(aff3.shape, I32),
                   jax.ShapeDtypeStruct((ntile, N_EXPERTS, 1), I32),
                   jax.ShapeDtypeStruct((ntile, N_EXPERTS, 1), I32)],
        scratch_shapes=[pltpu.VMEM(aff3.shape, F32)] * 3,
        compiler_params=pltpu.CompilerParams(vmem_limit_bytes=VMEM_LIMIT),
        name="route",
    )(aff3)


def _segment_copies(b_sm, c_sm, t, make_copy, action):
    off = 0
    for e in range(N_EXPERTS):
        c = c_sm[t * N_EXPERTS + e]
        b = b_sm[t * N_EXPERTS + e]

        @pl.when(c > 0)
        def _(e=e, b=b, c=c, off=off):
            action(make_copy(e, b, off, c))

        off = off + c
    return off


def _tile_rows(c_sm, t):
    rows = 0
    for e in range(N_EXPERTS):
        rows = rows + c_sm[t * N_EXPERTS + e]
    return rows


def _dispatch_kernel(cap, b_sm, c_sm, x_ref, loc_ref, xe_hbm, stg, sem):
    step = pl.program_id(0)
    nsteps = pl.num_programs(0)
    per_tile = ROUTE_TILE // LANES
    sub = lax.broadcasted_iota(I32, (SLOT_CHUNK, 1), 0)
    lane = lax.broadcasted_iota(I32, (1, LANES), 1)

    def copy_for(slot_):
        def make(e, dst_row, src_row, n):
            return pltpu.make_async_copy(
                stg.at[slot_, pl.ds(pl.multiple_of(src_row * SUBLANES, SUBLANES), n * SUBLANES)],
                xe_hbm.at[pl.ds(pl.multiple_of((e * cap + dst_row) * SUBLANES, SUBLANES),
                                n * SUBLANES)],
                sem.at[slot_])
        return make

    def wait_tile(t_):
        rows = _tile_rows(c_sm, t_)
        slot_ = t_ % DISPATCH_SLOTS

        @pl.when(rows > 0)
        def _():
            n = rows * SUBLANES
            pltpu.make_async_copy(stg.at[slot_, pl.ds(0, n)], xe_hbm.at[pl.ds(0, n)],
                                  sem.at[slot_]).wait()

    def one_tile(k):
        t = step * DISPATCH_TILES + k
        slot = t % DISPATCH_SLOTS
        tok = slice(k * ROUTE_TILE, (k + 1) * ROUTE_TILE)
        offs = []
        off = 0
        for e in range(N_EXPERTS):
            offs.append(off)
            off = off + c_sm[t * N_EXPERTS + e]
        offs.append(off)
        nchunk = (off + SLOT_CHUNK - 1) // SLOT_CHUNK
        seg_lo = jnp.zeros((1, LANES), I32)
        seg_hi = jnp.zeros((1, LANES), I32)
        for e in range(N_EXPERTS):
            seg_lo = jnp.where(lane == e, offs[e], seg_lo)
            seg_hi = jnp.where(lane == e, offs[e + 1], seg_hi)
        seg_lo_f = seg_lo.astype(F32)
        locf = jnp.concatenate(
            [loc_ref[k * per_tile + j] for j in range(per_tile)], axis=1).astype(F32)
        locb = jnp.concatenate(
            [locf, jnp.zeros((LANES - N_EXPERTS, ROUTE_TILE), F32)], axis=0).astype(BF16)

        def chunk_body(m, carry):
            base = m * SLOT_CHUNK
            srow = sub + base
            inseg = jnp.where(srow >= seg_lo, jnp.where(srow < seg_hi, 1.0, 0.0), 0.0)
            rank = srow.astype(F32) - jnp.sum(inseg * seg_lo_f, axis=1, keepdims=True)
            want = jnp.dot(inseg.astype(BF16), locb, preferred_element_type=F32)
            p = jnp.where(want == rank, 1.0, 0.0)
            res = jnp.dot(p.astype(BF16), x_ref[tok, :], preferred_element_type=F32)
            row0 = (base * SUBLANES if isinstance(m, int)
                    else pl.multiple_of(base * SUBLANES, SUBLANES))
            for s in range(SUBLANES):
                stg[slot, pl.ds(row0 + s, SLOT_CHUNK, stride=SUBLANES), :] = (
                    res[:, s * LANES:(s + 1) * LANES])
            return carry

        for m in range(PEELED_CHUNKS):
            chunk_body(m, 0)
        lax.fori_loop(PEELED_CHUNKS, jnp.maximum(nchunk, PEELED_CHUNKS), chunk_body, 0)

        _segment_copies(b_sm, c_sm, t, copy_for(slot), lambda cp: cp.start())

        @pl.when(t >= DISPATCH_SLOTS - 1)
        def _():
            wait_tile(t - (DISPATCH_SLOTS - 1))

    for k in range(DISPATCH_TILES):
        one_tile(k)

    @pl.when(step == nsteps - 1)
    def _():
        last = nsteps * DISPATCH_TILES - 1
        for d in range(DISPATCH_SLOTS - 2, -1, -1):
            wait_tile(last - d)


def _dispatch(b_flat, c_flat, x2b, loc3, cap):
    S, D = x2b.shape
    tok = DISPATCH_TILES * ROUTE_TILE
    assert S % tok == 0 and S // ROUTE_TILE >= DISPATCH_SLOTS
    max_rows = N_EXPERTS * ROUTE_TILE
    return pl.pallas_call(
        functools.partial(_dispatch_kernel, cap),
        grid_spec=pltpu.PrefetchScalarGridSpec(
            num_scalar_prefetch=2,
            grid=(S // tok,),
            in_specs=[pl.BlockSpec((tok, D), lambda t, b, c: (t, 0)),
                      pl.BlockSpec((tok // LANES, N_EXPERTS, LANES), lambda t, b, c: (t, 0, 0))],
            out_specs=pl.BlockSpec(memory_space=pl.ANY),
            scratch_shapes=[pltpu.VMEM((DISPATCH_SLOTS, max_rows * SUBLANES, LANES), F32),
                            pltpu.SemaphoreType.DMA((DISPATCH_SLOTS,))]),
        out_shape=jax.ShapeDtypeStruct((N_EXPERTS * cap * SUBLANES, LANES), F32),
        compiler_params=pltpu.CompilerParams(
            dimension_semantics=("arbitrary",), vmem_limit_bytes=VMEM_LIMIT),
        name="dispatch",
    )(b_flat, c_flat, x2b, loc3)


def _ffn_kernel(xe_ref, wg_ref, wu_ref, wd_ref, wr_ref, o_ref, xb, acc, gate):
    e = pl.program_id(0)
    f = pl.program_id(2)
    tm = xb.shape[0]

    @pl.when(f == 0)
    def _():
        for s in range(SUBLANES):
            xb[:, s * LANES:(s + 1) * LANES] = (
                xe_ref[pl.ds(s, tm, stride=SUBLANES), :].astype(BF16))
        lg = jnp.dot(xb[...], wr_ref[...], preferred_element_type=F32)
        lane = lax.broadcasted_iota(I32, (1, LANES), 1)
        lg = jnp.where(lane < N_EXPERTS, lg, NEG)
        p = jnp.exp(lg - jnp.max(lg, axis=-1, keepdims=True))
        mine = jnp.sum(jnp.where(lane == e, p, 0.0), axis=-1, keepdims=True)
        gate[...] = mine / jnp.sum(p, axis=-1, keepdims=True)

    def down_proj():
        x = xb[...]
        g = jnp.dot(x, wg_ref[...].astype(BF16), preferred_element_type=F32)
        u = jnp.dot(x, wu_ref[...].astype(BF16), preferred_element_type=F32)
        h = (g * jax.nn.sigmoid(g) * u).astype(BF16)
        return jnp.dot(h, wd_ref[...].astype(BF16), preferred_element_type=F32)

    last = pl.num_programs(2) - 1

    @pl.when(f == 0)
    def _():
        acc[...] = down_proj()

    @pl.when((f > 0) & (f < last))
    def _():
        acc[...] += down_proj()

    @pl.when(f == last)
    def _():
        y = (acc[...] + down_proj()) * gate[...]
        for s in range(SUBLANES):
            o_ref[pl.ds(s, tm, stride=SUBLANES), :] = y[:, s * LANES:(s + 1) * LANES]


def _ffn(xe2d, w_gate, w_up, w_down, wr_pad, cap):
    E, D, FF = w_gate.shape
    tm, tf = FFN_TM, FFN_TF
    mt = cap // tm
    rows = lambda e, m, f: (e * mt + m, 0)
    return pl.pallas_call(
        _ffn_kernel,
        grid=(E, mt, FF // tf),
        in_specs=[pl.BlockSpec((tm * SUBLANES, LANES), rows),
                  pl.BlockSpec((None, D, tf), lambda e, m, f: (e, 0, f)),
                  pl.BlockSpec((None, D, tf), lambda e, m, f: (e, 0, f)),
                  pl.BlockSpec((None, tf, D), lambda e, m, f: (e, f, 0)),
                  pl.BlockSpec(wr_pad.shape, lambda e, m, f: (0, 0))],
        out_specs=pl.BlockSpec((tm * SUBLANES, LANES), rows),
        out_shape=jax.ShapeDtypeStruct(xe2d.shape, F32),
        scratch_shapes=[pltpu.VMEM((tm, D), BF16), pltpu.VMEM((tm, D), F32),
                        pltpu.VMEM((tm, 1), F32)],
        compiler_params=pltpu.CompilerParams(
            dimension_semantics=("arbitrary", "arbitrary", "arbitrary"),
            vmem_limit_bytes=VMEM_LIMIT),
        name="ffn",
    )(xe2d, w_gate, w_up, w_down, wr_pad)


def _combine_kernel(cap, alpha, b_sm, c_sm, x_ref, loc_ref, g_ref, be_ref, ye_hbm,
                    o_ref, ybuf, sem, hl_scr, z_scr):
    step = pl.program_id(0)
    nt = pl.num_programs(0) * COMBINE_TILES
    lane = lax.broadcasted_iota(I32, (1, SLOT_CHUNK), 1)
    sub = lax.broadcasted_iota(I32, (SLOT_CHUNK, 1), 0)
    eid = lax.broadcasted_iota(I32, (N_EXPERTS, 1), 0)

    def copy_for(slot_):
        def make(e, src_row, dst_row, n):
            return pltpu.make_async_copy(
                ye_hbm.at[pl.ds(pl.multiple_of((e * cap + src_row) * SUBLANES, SUBLANES),
                                n * SUBLANES)],
                ybuf.at[slot_, pl.ds(pl.multiple_of(dst_row * SUBLANES, SUBLANES), n * SUBLANES)],
                sem.at[slot_])
        return make

    def fetch(t_):
        _segment_copies(b_sm, c_sm, t_, copy_for(t_ % 2), lambda cp: cp.start())

    @pl.when(step == 0)
    def _():
        ybuf[...] = jnp.zeros_like(ybuf)
        fetch(0)
        fetch(1)

    def one_tile(k):
        t = step * COMBINE_TILES + k
        slot = t % 2
        tok = slice(k * ROUTE_TILE, (k + 1) * ROUTE_TILE)

        rows_in = _tile_rows(c_sm, t)

        @pl.when(rows_in > 0)
        def _():
            n = rows_in * SUBLANES
            pltpu.make_async_copy(ye_hbm.at[pl.ds(0, n)], ybuf.at[slot, pl.ds(0, n)],
                                  sem.at[slot]).wait()

        offs = []
        off = 0
        for e in range(N_EXPERTS):
            offs.append(off)
            off = off + c_sm[t * N_EXPERTS + e]
        offs.append(off)
        total = off
        nchunk = (total + SLOT_CHUNK - 1) // SLOT_CHUNK
        seg_lo = jnp.zeros((N_EXPERTS, 1), I32)
        seg_hi = jnp.zeros((N_EXPERTS, 1), I32)
        for e in range(N_EXPERTS):
            seg_lo = jnp.where(eid == e, offs[e], seg_lo)
            seg_hi = jnp.where(eid == e, offs[e + 1], seg_hi)
        seg_lo_f = seg_lo.astype(F32)
        locb = loc_ref[tok, :].astype(F32).astype(BF16)

        z_scr[...] = alpha * x_ref[tok, :]

        def chunk_body(m, carry):
            base = m * SLOT_CHUNK
            srow = lane + base
            inseg = jnp.where(srow >= seg_lo, jnp.where(srow < seg_hi, 1.0, 0.0), 0.0)
            rank = srow.astype(F32) - jnp.sum(inseg * seg_lo_f, axis=0, keepdims=True)
            want = jnp.dot(locb, inseg.astype(BF16), preferred_element_type=F32)
            ptb = jnp.where(want == rank, 1.0, 0.0).astype(BF16)
            row0 = (base * SUBLANES if isinstance(m, int)
                    else pl.multiple_of(base * SUBLANES, SUBLANES))
            yc = jnp.concatenate(
                [ybuf[slot, pl.ds(row0 + s, SLOT_CHUNK, stride=SUBLANES), :]
                 for s in range(SUBLANES)], axis=1)
            yc = jnp.where(sub < total - base, yc, 0.0)
            hi = yc.astype(BF16)
            hl = hl_scr.at[m % PEELED_CHUNKS]
            hl[0:SLOT_CHUNK] = hi
            hl[SLOT_CHUNK:2 * SLOT_CHUNK] = (yc - hi.astype(F32)).astype(BF16)
            z_scr[...] += jnp.dot(jnp.concatenate([ptb, ptb], axis=1), hl[...],
                                  preferred_element_type=F32)
            return carry

        for m in range(PEELED_CHUNKS):
            chunk_body(m, 0)
        lax.fori_loop(PEELED_CHUNKS, jnp.maximum(nchunk, PEELED_CHUNKS), chunk_body, 0)

        @pl.when(t + 2 < nt)
        def _():
            fetch(t + 2)

        o_ref[tok, :] = _layer_norm(z_scr[...], g_ref[...], be_ref[...])

    for k in range(COMBINE_TILES):
        one_tile(k)


def _combine(b_flat, c_flat, x2, loc_t, g, b, ye2d, cap, alpha):
    S, D = x2.shape
    tok = COMBINE_TILES * ROUTE_TILE
    assert S % tok == 0 and S // ROUTE_TILE >= 2
    max_rows = N_EXPERTS * ROUTE_TILE
    cur = lambda t, b_, c_: (t, 0)
    const2 = lambda t, b_, c_: (0, 0)
    return pl.pallas_call(
        functools.partial(_combine_kernel, cap, alpha),
        grid_spec=pltpu.PrefetchScalarGridSpec(
            num_scalar_prefetch=2,
            grid=(S // tok,),
            in_specs=[pl.BlockSpec((tok, D), cur),
                      pl.BlockSpec((tok, N_EXPERTS), cur),
                      pl.BlockSpec((1, D), const2),
                      pl.BlockSpec((1, D), const2),
                      pl.BlockSpec(memory_space=pl.ANY)],
            out_specs=pl.BlockSpec((tok, D), cur),
            scratch_shapes=[pltpu.VMEM((2, max_rows * SUBLANES, LANES), F32),
                            pltpu.SemaphoreType.DMA((2,)),
                            pltpu.VMEM((PEELED_CHUNKS, 2 * SLOT_CHUNK, D), BF16),
                            pltpu.VMEM((ROUTE_TILE, D), F32)]),
        out_shape=jax.ShapeDtypeStruct((S, D), F32),
        compiler_params=pltpu.CompilerParams(
            dimension_semantics=("arbitrary",), vmem_limit_bytes=VMEM_LIMIT),
        name="combine",
    )(b_flat, c_flat, x2, loc_t, g, b, ye2d)


def _layer(x, mem, w_in, rpb, conv_w, w_mix_out, ln1_g, ln1_b, wq, wk, wv, wo,
           ln2_g, ln2_b, w_router, w_gate, w_up, w_down, ln3_g, ln3_b, alpha):
    S, D = x.shape
    cap = CAPACITY_FACTOR * S // N_EXPERTS
    assert S % TOK_BLOCK == 0 and S % PROJ_TM == 0 and S % XATTN_TM == 0
    assert S // GRID_W >= 3 * ROW_BLOCK
    assert cap % FFN_TM == 0 and cap <= S
    row = lambda a: a.reshape(1, -1)

    q, k, v, bg, u = _proj(x, w_in)
    x1 = _mixer(x, q, k, v, bg, u, w_mix_out, rpb,
                conv_w, row(ln1_g), row(ln1_b), alpha)

    wr_t = w_router.T
    wr_hi = wr_t.astype(BF16)
    wr_lo = (wr_t - wr_hi.astype(F32)).astype(BF16)
    x2, x2b, aff3 = _xattn(x1, mem, wq, wk, wv, wo, row(ln2_g), row(ln2_b),
                           wr_hi, wr_lo, alpha)

    loc3, b3, c3 = _route(aff3, cap)
    b_flat = b3.reshape(-1)
    c_flat = c3.reshape(-1)

    xe2d = _dispatch(b_flat, c_flat, x2b, loc3, cap)
    wr_pad = jnp.pad(w_router.astype(BF16), ((0, 0), (0, LANES - N_EXPERTS)))
    ye2d = _ffn(xe2d, w_gate, w_up, w_down, wr_pad, cap)
    loc_t = loc3.transpose(0, 2, 1).reshape(S, N_EXPERTS)
    return _combine(b_flat, c_flat, x2, loc_t, row(ln3_g), row(ln3_b), ye2d, cap, alpha)


def kernel(x, mem, w_in, na_rpb, conv_w, w_mix_out, ln1_g, ln1_b, w_mem_q, w_mem_k,
           w_mem_v, w_mem_out, ln2_g, ln2_b, w_router, w_exp_gate, w_exp_up,
           w_exp_down, ln3_g, ln3_b):
    depth = w_in.shape[0]
    alpha = (2 * depth) ** 0.25
    outs = []
    for bi in range(x.shape[0]):
        xb = x[bi]
        for l in range(depth):
            xb = _layer(xb, mem[bi], w_in[l], na_rpb[l], conv_w[l], w_mix_out[l],
                        ln1_g[l], ln1_b[l], w_mem_q[l], w_mem_k[l], w_mem_v[l],
                        w_mem_out[l], ln2_g[l], ln2_b[l], w_router[l], w_exp_gate[l],
                        w_exp_up[l], w_exp_down[l], ln3_g[l], ln3_b[l], alpha)
        outs.append(xb)
    return jnp.stack(outs)
```

```python
import functools

import jax
import jax.numpy as jnp
from jax import lax
from jax.experimental import pallas as pl
from jax.experimental.pallas import tpu as pltpu

F32 = jnp.float32
BF16 = jnp.bfloat16
I32 = jnp.int32

GRID_W = 64
HEAD_DIM = 64
NA_HEADS = 8
NA_WIDTH = NA_HEADS * HEAD_DIM
WIN_ROWS = 8
WIN_COLS = 16
MEM_HEADS = 4
MEM_HEAD_DIM = 128
N_EXPERTS = 16
CAPACITY_FACTOR = 2
LN_EPS = 1e-5
NEG = -1e30

LANES = 128
SUBLANES = 8
VMEM_LIMIT = 56 * 1024 * 1024

ROW_BLOCK = 8
TOK_BLOCK = ROW_BLOCK * GRID_W
PROJ_TM = 1024
XATTN_TM = 1024
ROUTE_TILE = 256
SLOT_CHUNK = 256
SM_ROWS = 16
PEELED_CHUNKS = 2
DISPATCH_TILES = 4
COMBINE_TILES = 2
DISPATCH_SLOTS = 3
FFN_TM = 1024
FFN_TF = 512


def _layer_norm(z, g, b):
    mu = jnp.mean(z, axis=-1, keepdims=True)
    zc = z - mu
    var = jnp.mean(zc * zc, axis=-1, keepdims=True)
    return zc * lax.rsqrt(var + LN_EPS) * g + b


def _dot_nt(a, b):
    return lax.dot_general(a, b, (((1,), (1,)), ((), ())),
                           preferred_element_type=F32)


def _proj_kernel(x_ref, w_ref, q_ref, k_ref, v_ref, b_ref, u_ref):
    xb = x_ref[...].astype(BF16)
    nw = NA_WIDTH

    def mm(c0):
        return jnp.dot(xb, w_ref[:, c0:c0 + nw].astype(BF16), preferred_element_type=F32)

    q_ref[...] = (mm(0) * (HEAD_DIM ** -0.5)).astype(BF16)
    k_ref[...] = mm(nw).astype(BF16)
    v_ref[...] = mm(2 * nw).astype(BF16)
    b_ref[...] = mm(3 * nw)
    u_ref[...] = mm(4 * nw) * mm(5 * nw)


def _proj(x, w_in_b):
    S, D = x.shape
    nw = NA_WIDTH
    tm = PROJ_TM
    blk = lambda i: (i, 0)
    return pl.pallas_call(
        _proj_kernel,
        grid=(S // tm,),
        in_specs=[pl.BlockSpec((tm, D), blk),
                  pl.BlockSpec((D, 6 * nw), lambda i: (0, 0))],
        out_specs=[pl.BlockSpec((tm, nw), blk)] * 5,
        out_shape=[jax.ShapeDtypeStruct((S, nw), BF16)] * 3
        + [jax.ShapeDtypeStruct((S, nw), F32)] * 2,
        compiler_params=pltpu.CompilerParams(
            dimension_semantics=("arbitrary",), vmem_limit_bytes=VMEM_LIMIT),
        name="proj",
    )(x, w_in_b)


def _build_na_bias(rpb_ref, bias_ref):
    shape = (GRID_W, LANES)
    qc = lax.broadcasted_iota(I32, shape, 0)
    ln = lax.broadcasted_iota(I32, shape, 1)
    kc = ln % GRID_W
    c_start = jnp.clip(qc - WIN_COLS // 2, 0, GRID_W - WIN_COLS)
    base = LANES - (WIN_COLS - 1)

    def canvas(h, d):
        both = rpb_ref[h, d:d + 1, :] + pltpu.roll(rpb_ref[h, d + 1:d + 2, :], GRID_W, 1)
        t = pltpu.roll(pltpu.roll(jnp.broadcast_to(both, shape), base, 1), 0, 1,
                       stride=1, stride_axis=0)
        return jnp.where(kc >= c_start, jnp.where(kc < c_start + WIN_COLS, t, NEG), NEG)

    for par in range(2):
        for hp in range(NA_HEADS // 2):
            for m in range(WIN_ROWS):
                d = 2 * m + par
                bias_ref[par, hp, m, 0:GRID_W] = canvas(2 * hp, d)
                bias_ref[par, hp, m, GRID_W:2 * GRID_W] = canvas(2 * hp + 1, d)


KV_WIN_ROWS = ROW_BLOCK + WIN_ROWS


def _kv_window_start(i, rows):
    return jnp.clip(i * ROW_BLOCK - WIN_ROWS // 2, 0, rows - KV_WIN_ROWS)


def _mixer_kernel(rows, alpha,
                  q_ref, kbuf, vbuf,
                  b_ref, u_ref, up_ref, un_ref, x_ref, wout_ref, rpb_ref,
                  cw_ref, g_ref, be_ref, o_ref, mix, bias_ref,
                  s_scr0, s_scr1, p_scr0, p_scr1):
    i = pl.program_id(0)
    nb = pl.num_programs(0)
    tb = TOK_BLOCK
    win = WIN_ROWS * GRID_W

    @pl.when(i == 0)
    def _():
        _build_na_bias(rpb_ref, bias_ref)

    lane = lax.broadcasted_iota(I32, (1, LANES), 1)
    first_head = lane < HEAD_DIM
    m_lo = jnp.where(first_head, 1.0, 0.0).astype(BF16)
    m_hi = jnp.where(first_head, 0.0, 1.0).astype(BF16)

    def window(r):
        grow = i * ROW_BLOCK + r
        start = jnp.clip(grow - WIN_ROWS // 2, 0, rows - WIN_ROWS)
        w0 = pl.multiple_of((start - _kv_window_start(i, rows)) * GRID_W, GRID_W)
        d0 = start - grow + WIN_ROWS
        return w0, d0 % 2, d0 // 2

    def scores(r, s_out):
        w0, _, _ = window(r)
        q0 = r * GRID_W if isinstance(r, int) else pl.multiple_of(r * GRID_W, GRID_W)
        for hp in range(NA_HEADS // 2):
            cs = slice(hp * LANES, (hp + 1) * LANES)
            qp = q_ref[pl.ds(q0, GRID_W), cs]
            lhs = jnp.concatenate([qp * m_lo, qp * m_hi], axis=0)
            kw = kbuf[pl.ds(w0, win), cs]
            s_out[hp] = _dot_nt(lhs, kw)

    def probs(r, s_in, p_out):
        _, par, m0 = window(r)
        for hp in range(NA_HEADS // 2):
            for rb in range(2 * GRID_W // SM_ROWS):
                rs = slice(rb * SM_ROWS, (rb + 1) * SM_ROWS)
                bias = jnp.concatenate(
                    [bias_ref[par, hp, m0 + j, rs, :] for j in range(WIN_ROWS // 2)], axis=1)
                s = s_in[hp, rs, :] + bias
                p = jnp.exp(s - jnp.max(s, axis=-1, keepdims=True))
                p = p * (1.0 / jnp.sum(p, axis=-1, keepdims=True))
                p_out[hp, rs, :] = p.astype(BF16)

    def values(r, p_in):
        w0, _, _ = window(r)
        q0 = r * GRID_W if isinstance(r, int) else pl.multiple_of(r * GRID_W, GRID_W)
        for hp in range(NA_HEADS // 2):
            cs = slice(hp * LANES, (hp + 1) * LANES)
            vw = vbuf[pl.ds(w0, win), cs]
            o2 = jnp.dot(p_in[hp], vw, preferred_element_type=F32)
            y = jnp.where(first_head, o2[0:GRID_W], o2[GRID_W:2 * GRID_W])
            mix[pl.ds(q0, GRID_W), cs] = y.astype(BF16)

    s_buf = (s_scr0, s_scr1)
    p_buf = (p_scr0, p_scr1)

    def steady(k, carry):
        for half in range(2):
            r = 1 + 2 * k + half
            values(r - 1, p_buf[half])
            probs(r, s_buf[1 - half], p_buf[1 - half])
            scores(r + 1, s_buf[half])
        return carry

    scores(0, s_buf[0])
    probs(0, s_buf[0], p_buf[0])
    scores(1, s_buf[1])
    for kk in range((ROW_BLOCK - 2) // 2):
        steady(kk, 0)
    values(ROW_BLOCK - 2, p_buf[0])
    probs(ROW_BLOCK - 1, s_buf[1], p_buf[1])
    values(ROW_BLOCK - 1, p_buf[1])

    u = u_ref[...]
    prev_row = up_ref[SUBLANES - 1:SUBLANES, :] * jnp.where(i > 0, 1.0, 0.0)
    next_row = un_ref[0:1, :] * jnp.where(i < nb - 1, 1.0, 0.0)
    rid = lax.broadcasted_iota(I32, (tb, 1), 0)
    um1 = jnp.where(rid == 0, prev_row, pltpu.roll(u, 1, axis=0))
    up1 = jnp.where(rid == tb - 1, next_row, pltpu.roll(u, tb - 1, axis=0))
    cw = cw_ref[...]
    yc = b_ref[...] * (cw[0:1] * um1 + cw[1:2] * u + cw[2:3] * up1)
    mix[:, NA_WIDTH:] = yc.astype(BF16)

    y = jnp.dot(mix[...], wout_ref[...].astype(BF16), preferred_element_type=F32)
    z = alpha * x_ref[...] + y
    o_ref[...] = _layer_norm(z, g_ref[...], be_ref[...])


def _mixer(x, q, k, v, bg, u, w_out_b, rpb, conv_w, g, b, alpha):
    S, D = x.shape
    rpb_pad = jnp.pad(rpb.astype(F32), ((0, 0), (1, 1), (0, LANES - rpb.shape[2])))
    nw = NA_WIDTH
    tb = TOK_BLOCK
    nb = S // tb
    rows = S // GRID_W
    cur = lambda i: (i, 0)
    kv_win = pl.BlockSpec((pl.Element(KV_WIN_ROWS * GRID_W), pl.Element(nw)),
                          lambda i: (_kv_window_start(i, rows) * GRID_W, 0))
    halo = tb // SUBLANES
    hprev = lambda i: (jnp.maximum(i * halo - 1, 0), 0)
    hnext = lambda i: (jnp.minimum((i + 1) * halo, S // SUBLANES - 1), 0)
    const2 = lambda i: (0, 0)
    kv = lambda im: pl.BlockSpec((tb, nw), im)
    return pl.pallas_call(
        functools.partial(_mixer_kernel, rows, alpha),
        grid=(nb,),
        in_specs=[kv(cur), kv_win, kv_win,
                  kv(cur), kv(cur),
                  pl.BlockSpec((SUBLANES, nw), hprev),
                  pl.BlockSpec((SUBLANES, nw), hnext),
                  pl.BlockSpec((tb, D), cur),
                  pl.BlockSpec((D, D), const2),
                  pl.BlockSpec(rpb_pad.shape, lambda i: (0, 0, 0)),
                  pl.BlockSpec(conv_w.shape, const2),
                  pl.BlockSpec((1, D), const2),
                  pl.BlockSpec((1, D), const2)],
        out_specs=pl.BlockSpec((tb, D), cur),
        out_shape=jax.ShapeDtypeStruct((S, D), F32),
        scratch_shapes=[pltpu.VMEM((tb, D), BF16),
                        pltpu.VMEM((2, NA_HEADS // 2, WIN_ROWS, 2 * GRID_W, LANES), F32),
                        pltpu.VMEM((NA_HEADS // 2, 2 * GRID_W, WIN_ROWS * GRID_W), F32),
                        pltpu.VMEM((NA_HEADS // 2, 2 * GRID_W, WIN_ROWS * GRID_W), F32),
                        pltpu.VMEM((NA_HEADS // 2, 2 * GRID_W, WIN_ROWS * GRID_W), BF16),
                        pltpu.VMEM((NA_HEADS // 2, 2 * GRID_W, WIN_ROWS * GRID_W), BF16)],
        compiler_params=pltpu.CompilerParams(
            dimension_semantics=("arbitrary",), vmem_limit_bytes=VMEM_LIMIT),
        name="mixer",
    )(q, k, v, bg, u, u, u, x, w_out_b, rpb_pad, conv_w, g, b)


def _xattn_kernel(alpha, x_ref, mem_ref, wq_ref, wk_ref, wv_ref, wo_ref,
                  g_ref, be_ref, wrh_ref, wrl_ref,
                  x2_ref, x2b_ref, aff_ref, kmem, vmem):
    i = pl.program_id(0)

    @pl.when(i == 0)
    def _():
        mb = mem_ref[...].astype(BF16)
        kmem[...] = jnp.dot(mb, wk_ref[...].astype(BF16),
                            preferred_element_type=F32).astype(BF16)
        vmem[...] = jnp.dot(mb, wv_ref[...].astype(BF16),
                            preferred_element_type=F32).astype(BF16)

    x1 = x_ref[...]
    qb = jnp.dot(x1.astype(BF16), wq_ref[...].astype(BF16),
                 preferred_element_type=F32).astype(BF16)
    outs = []
    for h in range(MEM_HEADS):
        cs = slice(h * MEM_HEAD_DIM, (h + 1) * MEM_HEAD_DIM)
        s = _dot_nt(qb[:, cs], kmem[:, cs]) * (MEM_HEAD_DIM ** -0.5)
        mx = jnp.max(s, axis=-1, keepdims=True)
        p = jnp.exp(s - mx)
        l = jnp.sum(p, axis=-1, keepdims=True)
        oh = jnp.dot(p.astype(BF16), vmem[:, cs], preferred_element_type=F32)
        outs.append(oh / l)
    o = jnp.concatenate(outs, axis=1).astype(BF16)
    y = jnp.dot(o, wo_ref[...].astype(BF16), preferred_element_type=F32)
    x2 = _layer_norm(alpha * x1 + y, g_ref[...], be_ref[...])
    x2_ref[...] = x2
    hi = x2.astype(BF16)
    x2b_ref[...] = hi
    lo = (x2 - hi.astype(F32)).astype(BF16)
    wrh = wrh_ref[...]
    r1 = _dot_nt(jnp.concatenate([wrh, wrl_ref[...]], axis=0), hi)
    lg = r1[0:N_EXPERTS] + r1[N_EXPERTS:2 * N_EXPERTS] + _dot_nt(wrh, lo)
    mx = jnp.max(lg, axis=0, keepdims=True)
    p = jnp.exp(lg - mx)
    aff = p / jnp.sum(p, axis=0, keepdims=True)
    for j in range(aff_ref.shape[0]):
        aff_ref[j] = aff[:, j * LANES:(j + 1) * LANES]


def _xattn(x1, mem, wq_b, wk_b, wv_b, wo_b, g, b, wr_hi, wr_lo, alpha):
    S, D = x1.shape
    M = mem.shape[0]
    mw = MEM_HEADS * MEM_HEAD_DIM
    tm = XATTN_TM
    cur = lambda i: (i, 0)
    const2 = lambda i: (0, 0)
    full = lambda a: pl.BlockSpec(a.shape, const2)
    return pl.pallas_call(
        functools.partial(_xattn_kernel, alpha),
        grid=(S // tm,),
        in_specs=[pl.BlockSpec((tm, D), cur), full(mem), full(wq_b), full(wk_b),
                  full(wv_b), full(wo_b), full(g), full(b), full(wr_hi), full(wr_lo)],
        out_specs=[pl.BlockSpec((tm, D), cur), pl.BlockSpec((tm, D), cur),
                   pl.BlockSpec((tm // LANES, N_EXPERTS, LANES), lambda i: (i, 0, 0))],
        out_shape=[jax.ShapeDtypeStruct((S, D), F32),
                   jax.ShapeDtypeStruct((S, D), BF16),
                   jax.ShapeDtypeStruct((S // LANES, N_EXPERTS, LANES), F32)],
        scratch_shapes=[pltpu.VMEM((M, mw), BF16), pltpu.VMEM((M, mw), BF16)],
        compiler_params=pltpu.CompilerParams(
            dimension_semantics=("arbitrary",), vmem_limit_bytes=VMEM_LIMIT),
        name="xattn",
    )(x1, mem, wq_b, wk_b, wv_b, wo_b, g, b, wr_hi, wr_lo)


def _route_kernel(cap, aff_ref, loc_ref, b_ref, c_ref, incl_scr, tot_scr, sel_scr):
    nblk = aff_ref.shape[0]
    per_tile = ROUTE_TILE // LANES
    ntile = nblk // per_tile

    def count_ge(t):
        ge = jnp.where(aff_ref[...] >= t[None], 1.0, 0.0)
        return jnp.sum(jnp.sum(ge, axis=0), axis=1, keepdims=True)

    def coarse(it, bits):
        cand = bits | jnp.left_shift(jnp.int32(1), 30 - it)
        ok = count_ge(pltpu.bitcast(cand, F32)) >= cap
        return jnp.where(ok, cand, bits)

    def fine(it, lohi):
        lo, hi = lohi
        mid = lo + (hi - lo) * 0.5
        ok = count_ge(mid) >= cap
        return jnp.where(ok, mid, lo), jnp.where(ok, hi, mid)

    bits = lax.fori_loop(0, 31, coarse, jnp.zeros((N_EXPERTS, 1), I32))
    thr, _ = lax.fori_loop(
        0, 30, fine, (pltpu.bitcast(bits, F32), pltpu.bitcast(bits + 1, F32)))
    gt_all = jnp.where(aff_ref[...] > thr[None], 1.0, 0.0)
    n_gt = jnp.sum(jnp.sum(gt_all, axis=0), axis=1, keepdims=True)
    need = cap - n_gt

    ii = lax.broadcasted_iota(I32, (LANES, LANES), 0)
    jj = lax.broadcasted_iota(I32, (LANES, LANES), 1)
    tri = jnp.where(ii <= jj, 1.0, 0.0).astype(BF16)
    ones = jnp.ones((LANES, LANES), BF16)
    flat = (nblk * N_EXPERTS, LANES)
    blocks = (nblk, N_EXPERTS, LANES)

    def block_scans(flags):
        f2 = flags.reshape(flat).astype(BF16)
        incl_scr[...] = jnp.dot(f2, tri, preferred_element_type=F32).reshape(blocks)
        tot_scr[...] = jnp.dot(f2, ones, preferred_element_type=F32).reshape(blocks)

    block_scans(jnp.where(aff_ref[...] == thr[None], 1.0, 0.0))

    def select_block(j, run_eq):
        blk = aff_ref[j]
        tie_rank = run_eq + incl_scr[j]
        sel_scr[j] = (jnp.where(blk > thr, 1.0, 0.0)
                      + jnp.where(blk == thr, 1.0, 0.0) * jnp.where(tie_rank <= need, 1.0, 0.0))
        return run_eq + tot_scr[j]

    zero = jnp.zeros((N_EXPERTS, LANES), F32)
    lax.fori_loop(0, nblk, select_block, zero)
    block_scans(sel_scr[...])

    def tile_body(t, run):
        tile_run = zero
        for j in range(per_tile):
            idx = t * per_tile + j
            sel = sel_scr[idx]
            loc = tile_run + incl_scr[idx] - sel
            loc_ref[idx] = jnp.where(sel > 0.5, loc, -1.0).astype(I32)
            tile_run = tile_run + tot_scr[idx]
        b_ref[t] = run[:, 0:1].astype(I32)
        c_ref[t] = tile_run[:, 0:1].astype(I32)
        return run + tile_run

    lax.fori_loop(0, ntile, tile_body, zero)


def _route(aff3, cap):
    nblk = aff3.shape[0]
    ntile = nblk * LANES // ROUTE_TILE
    full3 = lambda a: pl.BlockSpec(a, lambda: (0, 0, 0))
    return pl.pallas_call(
        functools.partial(_route_kernel, cap),
        in_specs=[full3(aff3.shape)],
        out_specs=[full3(aff3.shape), full3((ntile, N_EXPERTS, 1)),
                   full3((ntile, N_EXPERTS, 1))],
        out_shape=[jax.ShapeDtypeStruct(aff3.shape, I32),
                   jax.ShapeDtypeStruct((ntile, N_EXPERTS, 1), I32),
                   jax.ShapeDtypeStruct((ntile, N_EXPERTS, 1), I32)],
        scratch_shapes=[pltpu.VMEM(aff3.shape, F32)] * 3,
        compiler_params=pltpu.CompilerParams(vmem_limit_bytes=VMEM_LIMIT),
        name="route",
    )(aff3)


def _segment_copies(b_sm, c_sm, t, make_copy, action):
    off = 0
    for e in range(N_EXPERTS):
        c = c_sm[t * N_EXPERTS + e]
        b = b_sm[t * N_EXPERTS + e]

        @pl.when(c > 0)
        def _(e=e, b=b, c=c, off=off):
            action(make_copy(e, b, off, c))

        off = off + c
    return off


def _tile_rows(c_sm, t):
    rows = 0
    for e in range(N_EXPERTS):
        rows = rows + c_sm[t * N_EXPERTS + e]
    return rows


def _dispatch_kernel(cap, b_sm, c_sm, x_ref, loc_ref, xe_hbm, stg, sem):
    step = pl.program_id(0)
    nsteps = pl.num_programs(0)
    per_tile = ROUTE_TILE // LANES
    sub = lax.broadcasted_iota(I32, (SLOT_CHUNK, 1), 0)
    lane = lax.broadcasted_iota(I32, (1, LANES), 1)

    def copy_for(slot_):
        def make(e, dst_row, src_row, n):
            return pltpu.make_async_copy(
                stg.at[slot_, pl.ds(pl.multiple_of(src_row * SUBLANES, SUBLANES), n * SUBLANES)],
                xe_hbm.at[pl.ds(pl.multiple_of((e * cap + dst_row) * SUBLANES, SUBLANES),
                                n * SUBLANES)],
                sem.at[slot_])
        return make

    def wait_tile(t_):
        rows = _tile_rows(c_sm, t_)
        slot_ = t_ % DISPATCH_SLOTS

        @pl.when(rows > 0)
        def _():
            n = rows * SUBLANES
            pltpu.make_async_copy(stg.at[slot_, pl.ds(0, n)], xe_hbm.at[pl.ds(0, n)],
                                  sem.at[slot_]).wait()

    def one_tile(k):
        t = step * DISPATCH_TILES + k
        slot = t % DISPATCH_SLOTS
        tok = slice(k * ROUTE_TILE, (k + 1) * ROUTE_TILE)
        offs = []
        off = 0
        for e in range(N_EXPERTS):
            offs.append(off)
            off = off + c_sm[t * N_EXPERTS + e]
        offs.append(off)
        nchunk = (off + SLOT_CHUNK - 1) // SLOT_CHUNK
        seg_lo = jnp.zeros((1, LANES), I32)
        seg_hi = jnp.zeros((1, LANES), I32)
        for e in range(N_EXPERTS):
            seg_lo = jnp.where(lane == e, offs[e], seg_lo)
            seg_hi = jnp.where(lane == e, offs[e + 1], seg_hi)
        seg_lo_f = seg_lo.astype(F32)
        locf = jnp.concatenate(
            [loc_ref[k * per_tile + j] for j in range(per_tile)], axis=1).astype(F32)
        locb = jnp.concatenate(
            [locf, jnp.zeros((LANES - N_EXPERTS, ROUTE_TILE), F32)], axis=0).astype(BF16)

        def chunk_body(m, carry):
            base = m * SLOT_CHUNK
            srow = sub + base
            inseg = jnp.where(srow >= seg_lo, jnp.where(srow < seg_hi, 1.0, 0.0), 0.0)
            rank = srow.astype(F32) - jnp.sum(inseg * seg_lo_f, axis=1, keepdims=True)
            want = jnp.dot(inseg.astype(BF16), locb, preferred_element_type=F32)
            p = jnp.where(want == rank, 1.0, 0.0)
            res = jnp.dot(p.astype(BF16), x_ref[tok, :], preferred_element_type=F32)
            row0 = (base * SUBLANES if isinstance(m, int)
                    else pl.multiple_of(base * SUBLANES, SUBLANES))
            for s in range(SUBLANES):
                stg[slot, pl.ds(row0 + s, SLOT_CHUNK, stride=SUBLANES), :] = (
                    res[:, s * LANES:(s + 1) * LANES])
            return carry

        for m in range(PEELED_CHUNKS):
            chunk_body(m, 0)
        lax.fori_loop(PEELED_CHUNKS, jnp.maximum(nchunk, PEELED_CHUNKS), chunk_body, 0)

        _segment_copies(b_sm, c_sm, t, copy_for(slot), lambda cp: cp.start())

        @pl.when(t >= DISPATCH_SLOTS - 1)
        def _():
            wait_tile(t - (DISPATCH_SLOTS - 1))

    for k in range(DISPATCH_TILES):
        one_tile(k)

    @pl.when(step == nsteps - 1)
    def _():
        last = nsteps * DISPATCH_TILES - 1
        for d in range(DISPATCH_SLOTS - 2, -1, -1):
            wait_tile(last - d)


def _dispatch(b_flat, c_flat, x2b, loc3, cap):
    S, D = x2b.shape
    tok = DISPATCH_TILES * ROUTE_TILE
    assert S % tok == 0 and S // ROUTE_TILE >= DISPATCH_SLOTS
    max_rows = N_EXPERTS * ROUTE_TILE
    return pl.pallas_call(
        functools.partial(_dispatch_kernel, cap),
        grid_spec=pltpu.PrefetchScalarGridSpec(
            num_scalar_prefetch=2,
            grid=(S // tok,),
            in_specs=[pl.BlockSpec((tok, D), lambda t, b, c: (t, 0)),
                      pl.BlockSpec((tok // LANES, N_EXPERTS, LANES), lambda t, b, c: (t, 0, 0))],
            out_specs=pl.BlockSpec(memory_space=pl.ANY),
            scratch_shapes=[pltpu.VMEM((DISPATCH_SLOTS, max_rows * SUBLANES, LANES), F32),
                            pltpu.SemaphoreType.DMA((DISPATCH_SLOTS,))]),
        out_shape=jax.ShapeDtypeStruct((N_EXPERTS * cap * SUBLANES, LANES), F32),
        compiler_params=pltpu.CompilerParams(
            dimension_semantics=("arbitrary",), vmem_limit_bytes=VMEM_LIMIT),
        name="dispatch",
    )(b_flat, c_flat, x2b, loc3)


def _ffn_kernel(xe_ref, wg_ref, wu_ref, wd_ref, wr_ref, o_ref, xb, acc, gate):
    e = pl.program_id(0)
    f = pl.program_id(2)
    tm = xb.shape[0]

    @pl.when(f == 0)
    def _():
        for s in range(SUBLANES):
            xb[:, s * LANES:(s + 1) * LANES] = (
                xe_ref[pl.ds(s, tm, stride=SUBLANES), :].astype(BF16))
        lg = jnp.dot(xb[...], wr_ref[...], preferred_element_type=F32)
        lane = lax.broadcasted_iota(I32, (1, LANES), 1)
        lg = jnp.where(lane < N_EXPERTS, lg, NEG)
        p = jnp.exp(lg - jnp.max(lg, axis=-1, keepdims=True))
        mine = jnp.sum(jnp.where(lane == e, p, 0.0), axis=-1, keepdims=True)
        gate[...] = mine / jnp.sum(p, axis=-1, keepdims=True)

    def down_proj():
        x = xb[...]
        g = jnp.dot(x, wg_ref[...].astype(BF16), preferred_element_type=F32)
        u = jnp.dot(x, wu_ref[...].astype(BF16), preferred_element_type=F32)
        h = (g * jax.nn.sigmoid(g) * u).astype(BF16)
        return jnp.dot(h, wd_ref[...].astype(BF16), preferred_element_type=F32)

    last = pl.num_programs(2) - 1

    @pl.when(f == 0)
    def _():
        acc[...] = down_proj()

    @pl.when((f > 0) & (f < last))
    def _():
        acc[...] += down_proj()

    @pl.when(f == last)
    def _():
        y = (acc[...] + down_proj()) * gate[...]
        for s in range(SUBLANES):
            o_ref[pl.ds(s, tm, stride=SUBLANES), :] = y[:, s * LANES:(s + 1) * LANES]


def _ffn(xe2d, w_gate, w_up, w_down, wr_pad, cap):
    E, D, FF = w_gate.shape
    tm, tf = FFN_TM, FFN_TF
    mt = cap // tm
    rows = lambda e, m, f: (e * mt + m, 0)
    return pl.pallas_call(
        _ffn_kernel,
        grid=(E, mt, FF // tf),
        in_specs=[pl.BlockSpec((tm * SUBLANES, LANES), rows),
                  pl.BlockSpec((None, D, tf), lambda e, m, f: (e, 0, f)),
                  pl.BlockSpec((None, D, tf), lambda e, m, f: (e, 0, f)),
                  pl.BlockSpec((None, tf, D), lambda e, m, f: (e, f, 0)),
                  pl.BlockSpec(wr_pad.shape, lambda e, m, f: (0, 0))],
        out_specs=pl.BlockSpec((tm * SUBLANES, LANES), rows),
        out_shape=jax.ShapeDtypeStruct(xe2d.shape, F32),
        scratch_shapes=[pltpu.VMEM((tm, D), BF16), pltpu.VMEM((tm, D), F32),
                        pltpu.VMEM((tm, 1), F32)],
        compiler_params=pltpu.CompilerParams(
            dimension_semantics=("arbitrary", "arbitrary", "arbitrary"),
            vmem_limit_bytes=VMEM_LIMIT),
        name="ffn",
    )(xe2d, w_gate, w_up, w_down, wr_pad)


def _combine_kernel(cap, alpha, b_sm, c_sm, x_ref, loc_ref, g_ref, be_ref, ye_hbm,
                    o_ref, ybuf, sem, hl_scr, z_scr):
    step = pl.program_id(0)
    nt = pl.num_programs(0) * COMBINE_TILES
    lane = lax.broadcasted_iota(I32, (1, SLOT_CHUNK), 1)
    sub = lax.broadcasted_iota(I32, (SLOT_CHUNK, 1), 0)
    eid = lax.broadcasted_iota(I32, (N_EXPERTS, 1), 0)

    def copy_for(slot_):
        def make(e, src_row, dst_row, n):
            return pltpu.make_async_copy(
                ye_hbm.at[pl.ds(pl.multiple_of((e * cap + src_row) * SUBLANES, SUBLANES),
                                n * SUBLANES)],
                ybuf.at[slot_, pl.ds(pl.multiple_of(dst_row * SUBLANES, SUBLANES), n * SUBLANES)],
                sem.at[slot_])
        return make

    def fetch(t_):
        _segment_copies(b_sm, c_sm, t_, copy_for(t_ % 2), lambda cp: cp.start())

    @pl.when(step == 0)
    def _():
        ybuf[...] = jnp.zeros_like(ybuf)
        fetch(0)
        fetch(1)

    def one_tile(k):
        t = step * COMBINE_TILES + k
        slot = t % 2
        tok = slice(k * ROUTE_TILE, (k + 1) * ROUTE_TILE)

        rows_in = _tile_rows(c_sm, t)

        @pl.when(rows_in > 0)
        def _():
            n = rows_in * SUBLANES
            pltpu.make_async_copy(ye_hbm.at[pl.ds(0, n)], ybuf.at[slot, pl.ds(0, n)],
                                  sem.at[slot]).wait()

        offs = []
        off = 0
        for e in range(N_EXPERTS):
            offs.append(off)
            off = off + c_sm[t * N_EXPERTS + e]
        offs.append(off)
        total = off
        nchunk = (total + SLOT_CHUNK - 1) // SLOT_CHUNK
        seg_lo = jnp.zeros((N_EXPERTS, 1), I32)
        seg_hi = jnp.zeros((N_EXPERTS, 1), I32)
        for e in range(N_EXPERTS):
            seg_lo = jnp.where(eid == e, offs[e], seg_lo)
            seg_hi = jnp.where(eid == e, offs[e + 1], seg_hi)
        seg_lo_f = seg_lo.astype(F32)
        locb = loc_ref[tok, :].astype(F32).astype(BF16)

        z_scr[...] = alpha * x_ref[tok, :]

        def chunk_body(m, carry):
            base = m * SLOT_CHUNK
            srow = lane + base
            inseg = jnp.where(srow >= seg_lo, jnp.where(srow < seg_hi, 1.0, 0.0), 0.0)
            rank = srow.astype(F32) - jnp.sum(inseg * seg_lo_f, axis=0, keepdims=True)
            want = jnp.dot(locb, inseg.astype(BF16), preferred_element_type=F32)
            ptb = jnp.where(want == rank, 1.0, 0.0).astype(BF16)
            row0 = (base * SUBLANES if isinstance(m, int)
                    else pl.multiple_of(base * SUBLANES, SUBLANES))
            yc = jnp.concatenate(
                [ybuf[slot, pl.ds(row0 + s, SLOT_CHUNK, stride=SUBLANES), :]
                 for s in range(SUBLANES)], axis=1)
            yc = jnp.where(sub < total - base, yc, 0.0)
            hi = yc.astype(BF16)
            hl = hl_scr.at[m % PEELED_CHUNKS]
            hl[0:SLOT_CHUNK] = hi
            hl[SLOT_CHUNK:2 * SLOT_CHUNK] = (yc - hi.astype(F32)).astype(BF16)
            z_scr[...] += jnp.dot(jnp.concatenate([ptb, ptb], axis=1), hl[...],
                                  preferred_element_type=F32)
            return carry

        for m in range(PEELED_CHUNKS):
            chunk_body(m, 0)
        lax.fori_loop(PEELED_CHUNKS, jnp.maximum(nchunk, PEELED_CHUNKS), chunk_body, 0)

        @pl.when(t + 2 < nt)
        def _():
            fetch(t + 2)

        o_ref[tok, :] = _layer_norm(z_scr[...], g_ref[...], be_ref[...])

    for k in range(COMBINE_TILES):
        one_tile(k)


def _combine(b_flat, c_flat, x2, loc_t, g, b, ye2d, cap, alpha):
    S, D = x2.shape
    tok = COMBINE_TILES * ROUTE_TILE
    assert S % tok == 0 and S // ROUTE_TILE >= 2
    max_rows = N_EXPERTS * ROUTE_TILE
    cur = lambda t, b_, c_: (t, 0)
    const2 = lambda t, b_, c_: (0, 0)
    return pl.pallas_call(
        functools.partial(_combine_kernel, cap, alpha),
        grid_spec=pltpu.PrefetchScalarGridSpec(
            num_scalar_prefetch=2,
            grid=(S // tok,),
            in_specs=[pl.BlockSpec((tok, D), cur),
                      pl.BlockSpec((tok, N_EXPERTS), cur),
                      pl.BlockSpec((1, D), const2),
                      pl.BlockSpec((1, D), const2),
                      pl.BlockSpec(memory_space=pl.ANY)],
            out_specs=pl.BlockSpec((tok, D), cur),
            scratch_shapes=[pltpu.VMEM((2, max_rows * SUBLANES, LANES), F32),
                            pltpu.SemaphoreType.DMA((2,)),
                            pltpu.VMEM((PEELED_CHUNKS, 2 * SLOT_CHUNK, D), BF16),
                            pltpu.VMEM((ROUTE_TILE, D), F32)]),
        out_shape=jax.ShapeDtypeStruct((S, D), F32),
        compiler_params=pltpu.CompilerParams(
            dimension_semantics=("arbitrary",), vmem_limit_bytes=VMEM_LIMIT),
        name="combine",
    )(b_flat, c_flat, x2, loc_t, g, b, ye2d)


def _layer(x, mem, w_in, rpb, conv_w, w_mix_out, ln1_g, ln1_b, wq, wk, wv, wo,
           ln2_g, ln2_b, w_router, w_gate, w_up, w_down, ln3_g, ln3_b, alpha):
    S, D = x.shape
    cap = CAPACITY_FACTOR * S // N_EXPERTS
    assert S % TOK_BLOCK == 0 and S % PROJ_TM == 0 and S % XATTN_TM == 0
    assert S // GRID_W >= KV_WIN_ROWS
    assert cap % FFN_TM == 0 and cap <= S
    row = lambda a: a.reshape(1, -1)

    q, k, v, bg, u = _proj(x, w_in)
    x1 = _mixer(x, q, k, v, bg, u, w_mix_out, rpb,
                conv_w, row(ln1_g), row(ln1_b), alpha)

    wr_t = w_router.T
    wr_hi = wr_t.astype(BF16)
    wr_lo = (wr_t - wr_hi.astype(F32)).astype(BF16)
    x2, x2b, aff3 = _xattn(x1, mem, wq, wk, wv, wo, row(ln2_g), row(ln2_b),
                           wr_hi, wr_lo, alpha)

    loc3, b3, c3 = _route(aff3, cap)
    b_flat = b3.reshape(-1)
    c_flat = c3.reshape(-1)

    xe2d = _dispatch(b_flat, c_flat, x2b, loc3, cap)
    wr_pad = jnp.pad(w_router.astype(BF16), ((0, 0), (0, LANES - N_EXPERTS)))
    ye2d = _ffn(xe2d, w_gate, w_up, w_down, wr_pad, cap)
    loc_t = loc3.transpose(0, 2, 1).reshape(S, N_EXPERTS)
    return _combine(b_flat, c_flat, x2, loc_t, row(ln3_g), row(ln3_b), ye2d, cap, alpha)


def kernel(x, mem, w_in, na_rpb, conv_w, w_mix_out, ln1_g, ln1_b, w_mem_q, w_mem_k,
           w_mem_v, w_mem_out, ln2_g, ln2_b, w_router, w_exp_gate, w_exp_up,
           w_exp_down, ln3_g, ln3_b):
    depth = w_in.shape[0]
    alpha = (2 * depth) ** 0.25
    outs = []
    for bi in range(x.shape[0]):
        xb = x[bi]
        for l in range(depth):
            xb = _layer(xb, mem[bi], w_in[l], na_rpb[l], conv_w[l], w_mix_out[l],
                        ln1_g[l], ln1_b[l], w_mem_q[l], w_mem_k[l], w_mem_v[l],
                        w_mem_out[l], ln2_g[l], ln2_b[l], w_router[l], w_exp_gate[l],
                        w_exp_up[l], w_exp_down[l], ln3_g[l], ln3_b[l], alpha)
        outs.append(xb)
    return jnp.stack(outs)
```
